```python
import math
import jax, jax.numpy as jnp
from jax import lax
import numpy as np

D_MODEL = 1024
BATCH = 8
SEQ = 2048
DEPTH = 4
DEC_BATCH = 128
DEC_SEQ = 8
PAST_LEN = 8192
PAGE_SIZE = 128

HEAD_DIM = 64
RET_HEADS = D_MODEL // (2 * HEAD_DIM)
RET_DK = HEAD_DIM
RET_DV = HEAD_DIM
RET_QK = RET_HEADS * RET_DK
RET_WIDTH = RET_HEADS * RET_DV
RET_CHUNK = 128
SWA_HEADS = D_MODEL // (2 * HEAD_DIM)
SWA_KV_HEADS = 2
SWA_GROUP = SWA_HEADS // SWA_KV_HEADS
SWA_WIDTH = SWA_HEADS * HEAD_DIM
SWA_KV = SWA_KV_HEADS * HEAD_DIM
WINDOW = 128
SWA_BLOCK = 128
NUM_BUCKETS = 32
MAX_DISTANCE = WINDOW
MIX_WIDTH = RET_WIDTH + SWA_WIDTH
IN_COLS = 2 * RET_QK + 2 * RET_WIDTH + SWA_WIDTH + 2 * SWA_KV
SPLITS = (RET_QK, 2 * RET_QK, 2 * RET_QK + RET_WIDTH, 2 * RET_QK + 2 * RET_WIDTH,
          2 * RET_QK + 2 * RET_WIDTH + SWA_WIDTH, 2 * RET_QK + 2 * RET_WIDTH + SWA_WIDTH + SWA_KV)
D_FF = ((8 * D_MODEL // 3 + 127) // 128) * 128
CONV_W = 3
ROPE_BASE = 10000.0
DEEPNORM_ALPHA = (2.0 * DEPTH) ** 0.25
DEEPNORM_BETA = (8.0 * DEPTH) ** -0.25
LN_EPS = 1e-5
NEG_INF = -1e30

kernel_name = "hybrid_retention_swa_convffn_step"


def layer_norm(x, g, b):
    xf = x.astype(jnp.float32)
    mu = jnp.mean(xf, axis=-1, keepdims=True)
    var = jnp.mean(jnp.square(xf - mu), axis=-1, keepdims=True)
    return ((xf - mu) * lax.rsqrt(var + LN_EPS) * g.astype(jnp.float32) + b.astype(jnp.float32)).astype(x.dtype)


def head_norm(o):
    mu = jnp.mean(o, axis=-1, keepdims=True)
    var = jnp.mean(jnp.square(o - mu), axis=-1, keepdims=True)
    return (o - mu) * lax.rsqrt(var + LN_EPS)


def rotary(x, pos):
    half = x.shape[-1] // 2
    inv = 1.0 / (ROPE_BASE ** (jnp.arange(half, dtype=jnp.float32) / half))
    ang = pos.astype(jnp.float32)[:, None] * inv[None, :]
    cos = jnp.cos(ang)[None, :, None, :]
    sin = jnp.sin(ang)[None, :, None, :]
    x1, x2 = x[..., :half], x[..., half:]
    return jnp.concatenate([x1 * cos - x2 * sin, x1 * sin + x2 * cos], axis=-1)


def retention_chunkwise(q, k, v, s0):
    B, L, H, dk = q.shape
    dv = v.shape[-1]
    chunk = L if L <= RET_CHUNK else RET_CHUNK
    nc = L // chunk
    log_g = jnp.log(1.0 - 2.0 ** (-5.0 - jnp.arange(H, dtype=jnp.float32)))
    idx = jnp.arange(chunk)
    diff = idx[:, None] - idx[None, :]
    decay_mat = jnp.where(diff[None] >= 0,
                          jnp.exp(log_g[:, None, None] * jnp.maximum(diff, 0)[None].astype(jnp.float32)),
                          0.0)
    q_dec = jnp.exp(log_g[None, :] * (idx[:, None] + 1).astype(jnp.float32))
    k_dec = jnp.exp(log_g[None, :] * (chunk - 1 - idx[:, None]).astype(jnp.float32))
    chunk_dec = jnp.exp(log_g * chunk)

    qc = q.reshape(B, nc, chunk, H, dk).swapaxes(0, 1)
    kc = k.reshape(B, nc, chunk, H, dk).swapaxes(0, 1)
    vc = v.reshape(B, nc, chunk, H, dv).swapaxes(0, 1)

    def step(S, inp):
        qi, ki, vi = inp
        scores = jnp.einsum('bihd,bjhd->bhij', qi, ki) * decay_mat[None]
        o = (jnp.einsum('bhij,bjhe->bihe', scores, vi)
             + jnp.einsum('bihd,bhde->bihe', qi, S) * q_dec[None, :, :, None])
        S = (S * chunk_dec[None, :, None, None]
             + jnp.einsum('bjhd,bjhe->bhde', ki * k_dec[None, :, :, None], vi))
        return S, o

    S, o = lax.scan(step, s0, (qc, kc, vc))
    o = o.swapaxes(0, 1).reshape(B, L, H, dv)
    return o, S


def rel_bucket(dist):
    n = jnp.maximum(dist, 0)
    max_exact = NUM_BUCKETS // 2
    nf = jnp.maximum(n, max_exact).astype(jnp.float32)
    large = max_exact + (jnp.log(nf / max_exact) / math.log(MAX_DISTANCE / max_exact)
                         * (NUM_BUCKETS - max_exact)).astype(jnp.int32)
    large = jnp.minimum(large, NUM_BUCKETS - 1)
    return jnp.where(n < max_exact, n, large)


def swa_band(q, k_all, v_all, sinks, rel_bias, pos0):
    B, L = q.shape[0], q.shape[1]
    qb = L if L <= SWA_BLOCK else SWA_BLOCK
    nb = L // qb
    kw = WINDOW + qb
    kidx = jnp.arange(nb)[:, None] * qb + jnp.arange(kw)[None, :]
    kb = k_all[:, kidx].astype(jnp.float32)
    vb = v_all[:, kidx].astype(jnp.float32)
    qg = q.astype(jnp.float32).reshape(B, nb, qb, SWA_KV_HEADS, SWA_GROUP, HEAD_DIM)
    q_pos = pos0 + jnp.arange(L).reshape(nb, qb)
    k_pos = pos0 - WINDOW + kidx
    dist = q_pos[:, :, None] - k_pos[:, None, :]
    valid = (k_pos[:, None, :] >= 0) & (dist >= 0) & (dist < WINDOW)
    bias = rel_bias.astype(jnp.float32)[rel_bucket(dist)]
    bias = bias.transpose(0, 3, 1, 2).reshape(nb, SWA_KV_HEADS, SWA_GROUP, qb, kw)
    logits = jnp.einsum('bnqkgd,bnjkd->bnkgqj', qg, kb) * (HEAD_DIM ** -0.5) + bias[None]
    logits = jnp.where(valid[None, :, None, None], logits, NEG_INF)
    sink = sinks.astype(jnp.float32).reshape(SWA_KV_HEADS, SWA_GROUP)[None, None, :, :, None, None]
    m = jnp.maximum(jnp.max(logits, axis=-1, keepdims=True), sink)
    p = jnp.exp(logits - m)
    w = p / (jnp.sum(p, axis=-1, keepdims=True) + jnp.exp(sink - m))
    out = jnp.einsum('bnkgqj,bnjkd->bnqkgd', w, vb)
    return out.reshape(B, L, SWA_WIDTH)


def hybrid_layer(x, c, ret_s0, swa_k_prev, swa_v_prev, conv_prev, pos0, rel_bias,
                 w_ada, b_ada, w_in, swa_sinks, w_out, ln1_g, ln1_b,
                 w_up, conv_w, conv_b, w_down, ln2_g, ln2_b):
    B, L, _ = x.shape
    f32 = jnp.float32
    mod = jax.nn.silu(c) @ w_ada + b_ada
    sh1, sc1, g1, sh2, sc2, g2 = [m[:, None, :] for m in jnp.split(mod, 6, axis=-1)]

    h = x * (1.0 + sc1) + sh1
    proj = h @ w_in
    rq, rk, rv, rg, sq, sk, sv = jnp.split(proj, SPLITS, axis=-1)
    pos = pos0 + jnp.arange(L, dtype=jnp.int32)

    rq = rotary(rq.reshape(B, L, RET_HEADS, RET_DK).astype(f32), pos)
    rk = rotary(rk.reshape(B, L, RET_HEADS, RET_DK).astype(f32), pos) * (RET_DK ** -0.5)
    rv = rv.reshape(B, L, RET_HEADS, RET_DV).astype(f32)
    ret_o, ret_s = retention_chunkwise(rq, rk, rv, ret_s0.astype(f32))
    ret_o = head_norm(ret_o).reshape(B, L, RET_WIDTH) * jax.nn.silu(rg.astype(f32))

    q = sq.reshape(B, L, SWA_HEADS, HEAD_DIM)
    k_all = jnp.concatenate([swa_k_prev, sk.reshape(B, L, SWA_KV_HEADS, HEAD_DIM).astype(swa_k_prev.dtype)], axis=1)
    v_all = jnp.concatenate([swa_v_prev, sv.reshape(B, L, SWA_KV_HEADS, HEAD_DIM).astype(swa_v_prev.dtype)], axis=1)
    swa_o = swa_band(q, k_all, v_all, swa_sinks, rel_bias, pos0)

    mix = jnp.concatenate([ret_o.astype(x.dtype), swa_o.astype(x.dtype)], axis=-1) @ w_out
    x = layer_norm(DEEPNORM_ALPHA * x + g1 * mix, ln1_g, ln1_b)

    h2 = x * (1.0 + sc2) + sh2
    up = h2 @ w_up
    up_all = jnp.concatenate([conv_prev.astype(up.dtype), up], axis=1)
    conv = conv_b + up_all[:, 0:L] * conv_w[0]
    for tap in range(1, CONV_W):
        conv = conv + up_all[:, tap:tap + L] * conv_w[tap]
    ua, ub = jnp.split(conv, 2, axis=-1)
    f = (jax.nn.silu(ua) * ub) @ w_down
    x = layer_norm(DEEPNORM_ALPHA * x + g2 * f, ln2_g, ln2_b)

    return (x, ret_s.astype(ret_s0.dtype), k_all[:, -WINDOW:], v_all[:, -WINDOW:],
            up_all[:, -(CONV_W - 1):])


def setup_inputs(seed: int = 0) -> dict:
    key = jax.random.key(seed)
    ks = jax.random.split(key, 24)
    f32 = jnp.float32

    def nrm(k, shape, scale):
        return jax.random.normal(k, shape, f32) * scale

    return {
        "x_prompt": nrm(ks[0], (BATCH, SEQ, D_MODEL), 1.0),
        "x_sample": nrm(ks[1], (DEC_BATCH, DEC_SEQ, D_MODEL), 1.0),
        "c_prompt": nrm(ks[2], (BATCH, D_MODEL), 1.0),
        "c_sample": nrm(ks[3], (DEC_BATCH, D_MODEL), 1.0),
        "state_ret": nrm(ks[4], (DEPTH, DEC_BATCH, RET_HEADS, RET_DK, RET_DV), 1.0),
        "cache_swa_k": nrm(ks[5], (DEPTH, DEC_BATCH, WINDOW, SWA_KV_HEADS, HEAD_DIM), 1.0),
        "cache_swa_v": nrm(ks[6], (DEPTH, DEC_BATCH, WINDOW, SWA_KV_HEADS, HEAD_DIM), 1.0),
        "state_conv": nrm(ks[7], (DEPTH, DEC_BATCH, CONV_W - 1, 2 * D_FF), 1.0),
        "rel_bias": nrm(ks[8], (NUM_BUCKETS, SWA_HEADS), 0.5),
        "w_ada": nrm(ks[9], (DEPTH, D_MODEL, 6 * D_MODEL), 0.5 * D_MODEL ** -0.5),
        "b_ada": nrm(ks[10], (DEPTH, 6 * D_MODEL), 0.01),
        "w_in": nrm(ks[11], (DEPTH, D_MODEL, IN_COLS), D_MODEL ** -0.5),
        "swa_sinks": nrm(ks[12], (DEPTH, SWA_HEADS), 1.0),
        "w_out": nrm(ks[13], (DEPTH, MIX_WIDTH, D_MODEL), DEEPNORM_BETA * MIX_WIDTH ** -0.5),
        "ln1_g": 1.0 + nrm(ks[14], (DEPTH, D_MODEL), 0.05),
        "ln1_b": nrm(ks[15], (DEPTH, D_MODEL), 0.02),
        "w_up": nrm(ks[16], (DEPTH, D_MODEL, 2 * D_FF), D_MODEL ** -0.5),
        "conv_w": nrm(ks[17], (DEPTH, CONV_W, 2 * D_FF), CONV_W ** -0.5),
        "conv_b": nrm(ks[18], (DEPTH, 2 * D_FF), 0.01),
        "w_down": nrm(ks[19], (DEPTH, D_FF, D_MODEL), DEEPNORM_BETA * D_FF ** -0.5),
        "ln2_g": 1.0 + nrm(ks[20], (DEPTH, D_MODEL), 0.05),
        "ln2_b": nrm(ks[21], (DEPTH, D_MODEL), 0.02),
    }


def reference(x_prompt, x_sample, c_prompt, c_sample, state_ret, cache_swa_k, cache_swa_v, state_conv,
              rel_bias, w_ada, b_ada, w_in, swa_sinks, w_out, ln1_g, ln1_b,
              w_up, conv_w, conv_b, w_down, ln2_g, ln2_b):
    B = x_prompt.shape[0]
    xp = x_prompt
    xs = x_sample
    p_ret, p_k, p_v, p_conv = [], [], [], []
    s_ret, s_k, s_v, s_conv = [], [], [], []
    for l in range(DEPTH):
        params = (w_ada[l], b_ada[l], w_in[l], swa_sinks[l], w_out[l], ln1_g[l], ln1_b[l],
                  w_up[l], conv_w[l], conv_b[l], w_down[l], ln2_g[l], ln2_b[l])
        ret0 = jnp.zeros((B, RET_HEADS, RET_DK, RET_DV), state_ret.dtype)
        kv0 = jnp.zeros((B, WINDOW, SWA_KV_HEADS, HEAD_DIM), cache_swa_k.dtype)
        conv0 = jnp.zeros((B, CONV_W - 1, 2 * D_FF), state_conv.dtype)
        xp, r_p, k_p, v_p, c_p = hybrid_layer(xp, c_prompt, ret0, kv0, kv0, conv0, 0, rel_bias, *params)
        p_ret.append(r_p); p_k.append(k_p); p_v.append(v_p); p_conv.append(c_p)
        xs, r_s, k_s, v_s, c_s = hybrid_layer(xs, c_sample, state_ret[l], cache_swa_k[l], cache_swa_v[l],
                                              state_conv[l], PAST_LEN, rel_bias, *params)
        s_ret.append(r_s); s_k.append(k_s); s_v.append(v_s); s_conv.append(c_s)
    return (xp, xs,
            jnp.stack(p_ret), jnp.stack(p_k), jnp.stack(p_v), jnp.stack(p_conv),
            jnp.stack(s_ret), jnp.stack(s_k), jnp.stack(s_v), jnp.stack(s_conv))
```

```python
import functools
import math

import jax
import jax.numpy as jnp
import numpy as np
from jax import lax
from jax.experimental import pallas as pl
from jax.experimental.pallas import tpu as pltpu

F32 = jnp.float32
BF16 = jnp.bfloat16

D_MODEL = 1024
HEAD_DIM = 64
N_HEADS = 8
RET_W = N_HEADS * HEAD_DIM
KV_HEADS = 2
KV_W = KV_HEADS * HEAD_DIM
D_FF = 2816
UP_W = 2 * D_FF
IN_COLS = 4 * RET_W + RET_W + 2 * KV_W
QKVG_W = 5 * RET_W
WINDOW = 128
CHUNK = 128
NUM_BUCKETS = 32
PAST_LEN = 8192
ROPE_BASE = 10000.0
LN_EPS = 1e-5
NEG_INF = -1e30
LANES = 128
FF_CHUNK = 256
VMEM_LIMIT = 56 * 1024 * 1024


def _cparams(n_axes):
    return pltpu.CompilerParams(dimension_semantics=("arbitrary",) * n_axes,
                                vmem_limit_bytes=VMEM_LIMIT)


def _const_spec(shape):
    nd = len(shape)
    return pl.BlockSpec(shape, lambda *_: (0,) * nd, pipeline_mode=pl.Buffered(1))


def _rope_tables(pos):
    half = HEAD_DIM // 2
    inv = 1.0 / (ROPE_BASE ** (jnp.arange(half, dtype=F32) / half))
    ang = pos.astype(F32)[:, None] * inv[None, :]
    cos = jnp.cos(ang)
    sin = jnp.sin(ang)
    cos_h = jnp.concatenate([cos, cos], axis=-1)
    sin_h = jnp.concatenate([-sin, sin], axis=-1)
    return jnp.tile(cos_h, (1, N_HEADS)), jnp.tile(sin_h, (1, N_HEADS))


def _decay_tables(chunk):
    log_g = jnp.log(1.0 - 2.0 ** (-5.0 - jnp.arange(N_HEADS, dtype=F32)))
    idx = jnp.arange(chunk)
    diff = idx[:, None] - idx[None, :]
    decay_mat = jnp.where(diff[None] >= 0,
                          jnp.exp(log_g[:, None, None] * jnp.maximum(diff, 0)[None].astype(F32)), 0.0)
    q_dec = jnp.exp(log_g[None, :] * (idx[:, None] + 1).astype(F32))
    k_dec = jnp.exp(log_g[None, :] * (chunk - 1 - idx[:, None]).astype(F32))
    chunk_dec = jnp.exp(log_g * chunk)
    return decay_mat, q_dec, k_dec, chunk_dec


def _bucket_table(n_q, n_k):
    i = np.arange(n_q)[:, None]
    j = np.arange(n_k)[None, :]
    dist = i - j + WINDOW
    n = np.maximum(dist, 0)
    max_exact = NUM_BUCKETS // 2
    nf = np.maximum(n, max_exact).astype(np.float64)
    large = max_exact + (np.log(nf / max_exact) / math.log(WINDOW / max_exact)
                         * (NUM_BUCKETS - max_exact)).astype(np.int32)
    large = np.minimum(large, NUM_BUCKETS - 1)
    bucket = np.where(n < max_exact, n, large).astype(np.int32)
    valid = ((dist >= 0) & (dist < WINDOW)).astype(np.int32)
    return bucket, valid


def _ada_kernel(c_ref, w_ref, b_ref, o_ref):
    c = c_ref[...]
    s = (c * jax.nn.sigmoid(c)).astype(BF16)
    o_ref[0] = jnp.dot(s, w_ref[0].astype(BF16), preferred_element_type=F32) + b_ref[0]


def _ada_call(c_all, w_ada, b_ada):
    depth = w_ada.shape[0]
    n_rows = c_all.shape[0]
    tn = 1536
    return pl.pallas_call(
        _ada_kernel,
        grid=(depth, 6 * D_MODEL // tn),
        in_specs=[pl.BlockSpec((n_rows, D_MODEL), lambda l, n: (0, 0)),
                  pl.BlockSpec((1, D_MODEL, tn), lambda l, n: (l, 0, n)),
                  pl.BlockSpec((1, 1, tn), lambda l, n: (l, 0, n))],
        out_specs=pl.BlockSpec((1, n_rows, tn), lambda l, n: (l, 0, n)),
        out_shape=jax.ShapeDtypeStruct((depth, n_rows, 6 * D_MODEL), F32),
        compiler_params=_cparams(2),
        name="ada",
    )(c_all, w_ada, b_ada.reshape(depth, 1, 6 * D_MODEL))


def _bias_kernel(rb_ref, bucket_ref, valid_ref, o_ref):
    bucket = bucket_ref[...]
    valid = valid_ref[...] > 0
    first_ok = lax.broadcasted_iota(jnp.int32, bucket.shape, 1) >= WINDOW
    for h in range(N_HEADS):
        acc = jnp.zeros(bucket.shape, F32)
        for b in range(NUM_BUCKETS):
            acc = jnp.where(bucket == b, rb_ref[b, h], acc)
        o_ref[0, h] = jnp.where(valid, acc, NEG_INF)
        o_ref[1, h] = jnp.where(valid & first_ok, acc, NEG_INF)


def _bias_call(rel_bias):
    bucket, valid = _bucket_table(CHUNK, 2 * CHUNK)
    return pl.pallas_call(
        _bias_kernel,
        in_specs=[pl.BlockSpec(memory_space=pltpu.SMEM),
                  pl.BlockSpec((CHUNK, 2 * CHUNK), lambda: (0, 0)),
                  pl.BlockSpec((CHUNK, 2 * CHUNK), lambda: (0, 0))],
        out_specs=pl.BlockSpec((2, N_HEADS, CHUNK, 2 * CHUNK), lambda: (0, 0, 0, 0)),
        out_shape=jax.ShapeDtypeStruct((2, N_HEADS, CHUNK, 2 * CHUNK), F32),
        name="swa_bias",
    )(rel_bias, jnp.asarray(bucket), jnp.asarray(valid))


def _swap_halves(x):
    lane = lax.broadcasted_iota(jnp.int32, (1, LANES), 1)
    first = (lane % HEAD_DIM) < (HEAD_DIM // 2)
    cols = []
    for c in range(x.shape[1] // LANES):
        xc = x[:, c * LANES:(c + 1) * LANES]
        cols.append(jnp.where(first, pltpu.roll(xc, LANES - HEAD_DIM // 2, 1), pltpu.roll(xc, HEAD_DIM // 2, 1)))
    return jnp.concatenate(cols, axis=1)


def _proj_kernel(x_ref, sc_ref, sh_ref, w_ref, cos_ref, sin_ref, qkvg_ref, kv_ref):
    tb, tl, _ = x_ref.shape
    tm = tb * tl
    h = x_ref[...] * (1.0 + sc_ref[...]) + sh_ref[...]
    h = h.reshape(tm, D_MODEL).astype(BF16)
    proj = jnp.dot(h, w_ref[...], preferred_element_type=F32)
    cos = cos_ref[...]
    sin = sin_ref[...]
    rq = proj[:, 0:RET_W]
    rk = proj[:, RET_W:2 * RET_W]
    rq = rq * cos + _swap_halves(rq) * sin
    rk = (rk * cos + _swap_halves(rk) * sin) * (HEAD_DIM ** -0.5)
    rg = proj[:, 3 * RET_W:4 * RET_W]
    qkvg_ref[:, 0:RET_W] = rq.astype(BF16)
    qkvg_ref[:, RET_W:2 * RET_W] = rk.astype(BF16)
    qkvg_ref[:, 2 * RET_W:3 * RET_W] = proj[:, 2 * RET_W:3 * RET_W].astype(BF16)
    qkvg_ref[:, 3 * RET_W:4 * RET_W] = (rg * jax.nn.sigmoid(rg)).astype(BF16)
    qkvg_ref[:, 4 * RET_W:5 * RET_W] = (proj[:, 4 * RET_W:5 * RET_W] * (HEAD_DIM ** -0.5)).astype(BF16)
    kv_ref[...] = proj[:, 5 * RET_W:IN_COLS]


def _proj_call(x, mod, w_in, cos, sin, tb, tl):
    b, l, _ = x.shape
    nj = l // tl
    tm = tb * tl
    tab_idx = (lambda i, j: (j, 0)) if cos.shape[0] == l and nj > 1 else (lambda i, j: (0, 0))
    return pl.pallas_call(
        _proj_kernel,
        grid=(b // tb, nj),
        in_specs=[pl.BlockSpec((tb, tl, D_MODEL), lambda i, j: (i, j, 0)),
                  pl.BlockSpec((tb, 1, D_MODEL), lambda i, j: (i, 0, 1)),
                  pl.BlockSpec((tb, 1, D_MODEL), lambda i, j: (i, 0, 0)),
                  _const_spec((D_MODEL, IN_COLS)),
                  pl.BlockSpec((tm, RET_W), tab_idx),
                  pl.BlockSpec((tm, RET_W), tab_idx)],
        out_specs=[pl.BlockSpec((tm, QKVG_W), lambda i, j: (i * nj + j, 0)),
                   pl.BlockSpec((tm, 2 * KV_W), lambda i, j: (i * nj + j, 0))],
        out_shape=[jax.ShapeDtypeStruct((b * l, QKVG_W), BF16),
                   jax.ShapeDtypeStruct((b * l, 2 * KV_W), F32)],
        compiler_params=_cparams(2),
        name="proj",
    )(x, mod, mod, w_in, cos, sin)


def _lane_mask_low():
    return lax.broadcasted_iota(jnp.int32, (1, LANES), 1) < HEAD_DIM


def _head_norm(o, ones_bd):
    outs = []
    gw = ones_bd.shape[0]
    for g in range(RET_W // gw):
        og = o[:, g * gw:(g + 1) * gw]
        mu = jnp.dot(og.astype(BF16), ones_bd, preferred_element_type=F32)
        d = og - mu
        var = jnp.dot((d * d).astype(BF16), ones_bd, preferred_element_type=F32)
        outs.append(d * lax.rsqrt(var + LN_EPS))
    return jnp.concatenate(outs, axis=1)


def _swa_block(sq, k_all, v_all, bias_ref, sink_ref):
    rows = sq.shape[0]
    low = _lane_mask_low()
    zero = jnp.zeros_like(k_all)
    k_rot = pltpu.roll(k_all, HEAD_DIM, 1)
    v_rot = pltpu.roll(v_all, HEAD_DIM, 1)
    outs = []
    for g in range(KV_HEADS):
        k_lo, k_hi = (k_all, k_rot) if g == 0 else (k_rot, k_all)
        v_lo, v_hi = (v_all, v_rot) if g == 0 else (v_rot, v_all)
        kc = jnp.concatenate([jnp.where(low, k_lo, zero), jnp.where(low, zero, k_hi)], axis=0)
        vc = jnp.concatenate([jnp.where(low, v_lo, zero), jnp.where(low, zero, v_hi)], axis=0)
        for pp in range(N_HEADS // KV_HEADS // 2):
            p = g * (N_HEADS // KV_HEADS // 2) + pp
            qp = sq[:, p * LANES:(p + 1) * LANES]
            lg = lax.dot_general(qp, kc, (((1,), (1,)), ((), ())), preferred_element_type=F32)
            probs, dens = [], []
            for s in range(2):
                hh = 2 * p + s
                lo = lg[:, s * 2 * CHUNK:(s + 1) * 2 * CHUNK] + bias_ref[hh, 0:rows, :]
                sink = sink_ref[hh]
                m = jnp.maximum(jnp.max(lo, axis=-1, keepdims=True), sink)
                pe = jnp.exp(lo - m)
                dens.append(jnp.sum(pe, axis=-1, keepdims=True) + jnp.exp(sink - m))
                probs.append(pe.astype(BF16))
            oo = jnp.dot(jnp.concatenate(probs, axis=1), vc, preferred_element_type=F32)
            outs.append(oo / jnp.where(low, dens[0], dens[1]))
    return jnp.concatenate(outs, axis=1)


def _attn_prompt_kernel(sink_ref, qkvg_ref, kv_ref, bias_ref, dm_ref, qd_ref, kd_ref, cd_ref, ones_ref,
                        mix_ref, s_out_ref, s_scr, pk_scr, pv_scr):
    i = pl.program_id(1)

    @pl.when(i == 0)
    def _():
        s_scr[...] = jnp.zeros_like(s_scr)
        pk_scr[...] = jnp.zeros_like(pk_scr)
        pv_scr[...] = jnp.zeros_like(pv_scr)

    q = qkvg_ref[:, 0:RET_W]
    k = qkvg_ref[:, RET_W:2 * RET_W]
    v = qkvg_ref[:, 2 * RET_W:3 * RET_W]
    gate = qkvg_ref[:, 3 * RET_W:4 * RET_W]
    sq = qkvg_ref[:, 4 * RET_W:5 * RET_W]
    low = _lane_mask_low()
    zero = jnp.zeros((CHUNK, LANES), BF16)

    intra = []
    for p in range(N_HEADS // 2):
        kp = k[:, p * LANES:(p + 1) * LANES]
        vp = v[:, p * LANES:(p + 1) * LANES]
        kc = jnp.concatenate([jnp.where(low, kp, zero), jnp.where(low, zero, kp)], axis=0)
        vc = jnp.concatenate([jnp.where(low, vp, zero), jnp.where(low, zero, vp)], axis=0)
        s = lax.dot_general(q[:, p * LANES:(p + 1) * LANES], kc, (((1,), (1,)), ((), ())),
                            preferred_element_type=F32)
        s = (s * dm_ref[p]).astype(BF16)
        intra.append(jnp.dot(s, vc, preferred_element_type=F32))
    o = jnp.concatenate(intra, axis=1)

    kd = (k.astype(F32) * kd_ref[...]).astype(BF16)
    gw = cd_ref.shape[1]
    cross = []
    for g in range(RET_W // gw):
        sg = s_scr[g]
        cross.append(jnp.dot(q[:, g * gw:(g + 1) * gw], sg.astype(BF16), preferred_element_type=F32))
        upd = lax.dot_general(kd[:, g * gw:(g + 1) * gw], v[:, g * gw:(g + 1) * gw], (((0,), (0,)), ((), ())),
                              preferred_element_type=F32)
        cd = cd_ref[g]
        s_scr[g] = sg * cd + jnp.where(cd > 0.0, upd, 0.0)
    o = o + jnp.concatenate(cross, axis=1) * qd_ref[...]
    ret = _head_norm(o, ones_ref[...]) * gate.astype(F32)

    k_new = kv_ref[:, 0:KV_W].astype(BF16)
    v_new = kv_ref[:, KV_W:2 * KV_W].astype(BF16)
    k_all = jnp.concatenate([pk_scr[...], k_new], axis=0)
    v_all = jnp.concatenate([pv_scr[...], v_new], axis=0)
    swa = _swa_block(sq, k_all, v_all, bias_ref.at[0], sink_ref)
    pk_scr[...] = k_new
    pv_scr[...] = v_new

    mix_ref[:, 0:RET_W] = ret.astype(BF16)
    mix_ref[:, RET_W:2 * RET_W] = swa.astype(BF16)

    @pl.when(i == pl.num_programs(1) - 1)
    def _():
        for h in range(N_HEADS):
            g, hl = divmod(h, gw // HEAD_DIM)
            s_out_ref[0, h] = s_scr[g, hl * HEAD_DIM:(hl + 1) * HEAD_DIM, hl * HEAD_DIM:(hl + 1) * HEAD_DIM]


def _prompt_tables():
    decay_mat, q_dec, k_dec, chunk_dec = _decay_tables(CHUNK)
    dm = jnp.concatenate([decay_mat[0::2], decay_mat[1::2]], axis=2)
    qd = jnp.repeat(q_dec, HEAD_DIM, axis=1)
    kd = jnp.repeat(k_dec, HEAD_DIM, axis=1)
    gw = 4 * HEAD_DIM
    blk = np.kron(np.eye(4, dtype=np.float32), np.ones((HEAD_DIM, HEAD_DIM), np.float32))
    cd = jnp.repeat(chunk_dec.reshape(2, 4), HEAD_DIM, axis=1)[:, :, None] * jnp.asarray(blk)[None]
    ones_bd = jnp.asarray(blk / HEAD_DIM, BF16)
    assert cd.shape == (2, gw, gw)
    return dm, qd, kd, cd, ones_bd


def _attn_prompt_call(qkvg, kv, bias, sinks, tables, b, l):
    dm, qd, kd, cd, ones_bd = tables
    nc = l // CHUNK
    gw = cd.shape[1]
    return pl.pallas_call(
        _attn_prompt_kernel,
        grid=(b, nc),
        in_specs=[pl.BlockSpec(memory_space=pltpu.SMEM),
                  pl.BlockSpec((CHUNK, QKVG_W), lambda bi, i: (bi * nc + i, 0)),
                  pl.BlockSpec((CHUNK, 2 * KV_W), lambda bi, i: (bi * nc + i, 0)),
                  pl.BlockSpec((1, N_HEADS, CHUNK, 2 * CHUNK), lambda bi, i: (jnp.where(i == 0, 1, 0), 0, 0, 0)),
                  _const_spec(dm.shape), _const_spec(qd.shape), _const_spec(kd.shape),
                  _const_spec(cd.shape), _const_spec(ones_bd.shape)],
        out_specs=[pl.BlockSpec((CHUNK, 2 * RET_W), lambda bi, i: (bi * nc + i, 0)),
                   pl.BlockSpec((1, N_HEADS, HEAD_DIM, HEAD_DIM), lambda bi, i: (bi, 0, 0, 0))],
        out_shape=[jax.ShapeDtypeStruct((b * l, 2 * RET_W), BF16),
                   jax.ShapeDtypeStruct((b, N_HEADS, HEAD_DIM, HEAD_DIM), F32)],
        scratch_shapes=[pltpu.VMEM((RET_W // gw, gw, gw), F32),
                        pltpu.VMEM((CHUNK, KV_W), BF16),
                        pltpu.VMEM((CHUNK, KV_W), BF16)],
        compiler_params=_cparams(2),
        name="attn_prompt",
    )(sinks, qkvg, kv, bias, dm, qd, kd, cd, ones_bd)


def _attn_sample_kernel(sink_ref, qkvg_ref, kv_ref, s_in_ref, ck_ref, cv_ref, bias_ref, ds_ref, qd_ref, kd_ref,
                        cdr_ref, hm_ref, ones_ref, mix_ref, s_out_ref, ck_out_ref, cv_out_ref, k_scr, v_scr):
    tb = s_in_ref.shape[0]
    tl = qkvg_ref.shape[0] // tb
    low = _lane_mask_low()
    hm = hm_ref[...]
    k_scr[...] = jnp.zeros_like(k_scr)
    v_scr[...] = jnp.zeros_like(v_scr)
    for e in range(tb):
        rows = slice(e * tl, (e + 1) * tl)
        q = qkvg_ref[rows, 0:RET_W]
        k32 = qkvg_ref[rows, RET_W:2 * RET_W].astype(F32)
        v32 = qkvg_ref[rows, 2 * RET_W:3 * RET_W].astype(F32)
        gate = qkvg_ref[rows, 3 * RET_W:4 * RET_W].astype(F32)
        sq = qkvg_ref[rows, 4 * RET_W:5 * RET_W]

        k_big = (jnp.concatenate([k32] * N_HEADS, axis=0) * hm).astype(BF16)
        v_big = (jnp.concatenate([v32] * N_HEADS, axis=0) * hm).astype(BF16)
        q_big = (jnp.concatenate([q.astype(F32)] * N_HEADS, axis=0) * hm).astype(BF16)
        kd_big = (jnp.concatenate([k32 * kd_ref[...]] * N_HEADS, axis=0) * hm).astype(BF16)
        s = lax.dot_general(q, k_big, (((1,), (1,)), ((), ())), preferred_element_type=F32)
        s = (s * ds_ref[...]).astype(BF16)
        o = jnp.dot(s, v_big, preferred_element_type=F32)
        s_prev = s_in_ref[e]
        oc = jnp.dot(q_big, s_prev.astype(BF16), preferred_element_type=F32)
        v_stack = jnp.concatenate([v32[:, h * HEAD_DIM:(h + 1) * HEAD_DIM] for h in range(N_HEADS)], axis=0)
        upd = lax.dot_general(kd_big, v_stack.astype(BF16), (((0,), (0,)), ((), ())),
                              preferred_element_type=F32)
        s_out_ref[e] = s_prev * cdr_ref[...] + upd
        oc_nat = jnp.concatenate([oc[h * tl:(h + 1) * tl, :] for h in range(N_HEADS)], axis=1)
        o = o + oc_nat * qd_ref[...]
        ret = _head_norm(o, ones_ref[...]) * gate

        k_new = kv_ref[rows, 0:KV_W]
        v_new = kv_ref[rows, KV_W:2 * KV_W]
        k_scr[0:WINDOW, :] = ck_ref[e].astype(BF16)
        v_scr[0:WINDOW, :] = cv_ref[e].astype(BF16)
        k_scr[WINDOW:WINDOW + tl, :] = k_new.astype(BF16)
        v_scr[WINDOW:WINDOW + tl, :] = v_new.astype(BF16)
        swa = _swa_block(sq, k_scr[...], v_scr[...], bias_ref.at[0], sink_ref)

        mix_ref[rows, 0:RET_W] = ret.astype(BF16)
        mix_ref[rows, RET_W:2 * RET_W] = swa.astype(BF16)
        ck_out_ref[e, 0:WINDOW - tl, :] = ck_ref[e, tl:WINDOW, :]
        cv_out_ref[e, 0:WINDOW - tl, :] = cv_ref[e, tl:WINDOW, :]
        ck_out_ref[e, WINDOW - tl:WINDOW, :] = k_new
        cv_out_ref[e, WINDOW - tl:WINDOW, :] = v_new


def _sample_tables(tl):
    decay_mat, q_dec, k_dec, chunk_dec = _decay_tables(tl)
    ds = jnp.transpose(decay_mat, (1, 0, 2)).reshape(tl, N_HEADS * tl)
    qd = jnp.repeat(q_dec, HEAD_DIM, axis=1)
    kd = jnp.repeat(k_dec, HEAD_DIM, axis=1)
    cdr = jnp.broadcast_to(jnp.repeat(chunk_dec, HEAD_DIM)[:, None], (RET_W, HEAD_DIM))
    hm = np.kron(np.eye(N_HEADS, dtype=np.float32), np.ones((tl, HEAD_DIM), np.float32))
    blk = np.kron(np.eye(4, dtype=np.float32), np.ones((HEAD_DIM, HEAD_DIM), np.float32))
    return ds, qd, kd, cdr, jnp.asarray(hm), jnp.asarray(blk / HEAD_DIM, BF16)


def _attn_sample_call(qkvg, kv, s_in, ck, cv, bias, sinks, tables, b, tl, tb):
    ds, qd, kd, cdr, hm, ones_bd = tables
    s_rows = s_in.reshape(b, RET_W, HEAD_DIM)
    ck2 = ck.reshape(b, WINDOW, KV_W)
    cv2 = cv.reshape(b, WINDOW, KV_W)
    batch3 = lambda i: (i, 0, 0)
    mix, s_out, ck_out, cv_out = pl.pallas_call(
        _attn_sample_kernel,
        grid=(b // tb,),
        in_specs=[pl.BlockSpec(memory_space=pltpu.SMEM),
                  pl.BlockSpec((tb * tl, QKVG_W), lambda i: (i, 0)),
                  pl.BlockSpec((tb * tl, 2 * KV_W), lambda i: (i, 0)),
                  pl.BlockSpec((tb, RET_W, HEAD_DIM), batch3),
                  pl.BlockSpec((tb, WINDOW, KV_W), batch3),
                  pl.BlockSpec((tb, WINDOW, KV_W), batch3),
                  pl.BlockSpec((1, N_HEADS, CHUNK, 2 * CHUNK), lambda i: (0, 0, 0, 0)),
                  _const_spec(ds.shape), _const_spec(qd.shape), _const_spec(kd.shape),
                  _const_spec(cdr.shape), _const_spec(hm.shape), _const_spec(ones_bd.shape)],
        out_specs=[pl.BlockSpec((tb * tl, 2 * RET_W), lambda i: (i, 0)),
                   pl.BlockSpec((tb, RET_W, HEAD_DIM), batch3),
                   pl.BlockSpec((tb, WINDOW, KV_W), batch3),
                   pl.BlockSpec((tb, WINDOW, KV_W), batch3)],
        out_shape=[jax.ShapeDtypeStruct((b * tl, 2 * RET_W), BF16),
                   jax.ShapeDtypeStruct((b, RET_W, HEAD_DIM), F32),
                   jax.ShapeDtypeStruct((b, WINDOW, KV_W), F32),
                   jax.ShapeDtypeStruct((b, WINDOW, KV_W), F32)],
        scratch_shapes=[pltpu.VMEM((2 * CHUNK, KV_W), BF16),
                        pltpu.VMEM((2 * CHUNK, KV_W), BF16)],
        compiler_params=_cparams(1),
        name="attn_sample",
    )(sinks, qkvg, kv, s_rows, ck2, cv2, bias, ds, qd, kd, cdr, hm, ones_bd)
    return (mix, s_out.reshape(b, N_HEADS, HEAD_DIM, HEAD_DIM),
            ck_out.reshape(b, WINDOW, KV_HEADS, HEAD_DIM), cv_out.reshape(b, WINDOW, KV_HEADS, HEAD_DIM))


def _layer_norm(x, g, b):
    mu = jnp.mean(x, axis=-1, keepdims=True)
    xc = x - mu
    var = jnp.mean(xc * xc, axis=-1, keepdims=True)
    return xc * lax.rsqrt(var + LN_EPS) * g + b


def _ffn_kernel(alpha, carry_rows, mix_ref, x_ref, g1_ref, sh2_ref, sc2_ref, g2_ref, wout_ref, ln1g_ref, ln1b_ref,
                wup_ref, cw_ref, cb_ref, prev_ref, wdn_ref, ln2g_ref, ln2b_ref, xo_ref, tail_ref, carry_scr):
    tb, tl, _ = x_ref.shape
    tm = tb * tl
    j = pl.program_id(1)

    y = jnp.dot(mix_ref[...], wout_ref[...], preferred_element_type=F32)
    x1 = alpha * x_ref[...] + g1_ref[...] * y.reshape(tb, tl, D_MODEL)
    x1 = _layer_norm(x1, ln1g_ref[...], ln1b_ref[...])
    h2 = (x1 * (1.0 + sc2_ref[...]) + sh2_ref[...]).reshape(tm, D_MODEL).astype(BF16)

    if carry_rows:
        @pl.when(j == 0)
        def _():
            carry_scr[carry_rows - 2:carry_rows, :] = prev_ref[0]

    t_idx = lax.broadcasted_iota(jnp.int32, (tm, 1), 0) % tl
    is_t0 = t_idx == 0
    is_t1 = t_idx == 1
    acc = jnp.zeros((tm, D_MODEL), F32)
    n_chunks = D_FF // FF_CHUNK
    for c in range(n_chunks):
        halves = []
        for half in range(2):
            cols = slice(half * D_FF + c * FF_CHUNK, half * D_FF + (c + 1) * FF_CHUNK)
            up = jnp.dot(h2, wup_ref[:, cols], preferred_element_type=F32)
            if carry_rows:
                p0 = carry_scr[carry_rows - 2:carry_rows - 1, cols]
                p1 = carry_scr[carry_rows - 1:carry_rows, cols]
                tail_ref[:, cols] = up[tm - carry_rows:tm, :]
            else:
                p0 = jnp.broadcast_to(prev_ref[:, 0:1, cols], (tb, tl, FF_CHUNK)).reshape(tm, FF_CHUNK)
                p1 = jnp.broadcast_to(prev_ref[:, 1:2, cols], (tb, tl, FF_CHUNK)).reshape(tm, FF_CHUNK)
                tail_ref[:, cols] = up
            s1 = jnp.where(is_t0, p1, pltpu.roll(up, 1, 0))
            s2 = jnp.where(is_t0, p0, jnp.where(is_t1, p1, pltpu.roll(up, 2, 0)))
            halves.append(cb_ref[:, cols] + s2 * cw_ref[0:1, cols] + s1 * cw_ref[1:2, cols] + up * cw_ref[2:3, cols])
        ua, ub = halves
        gated = (ua * jax.nn.sigmoid(ua) * ub).astype(BF16)
        acc = acc + jnp.dot(gated, wdn_ref[c * FF_CHUNK:(c + 1) * FF_CHUNK, :], preferred_element_type=F32)

    if carry_rows:
        carry_scr[...] = tail_ref[...]
    x2 = alpha * x1 + g2_ref[...] * acc.reshape(tb, tl, D_MODEL)
    xo_ref[...] = _layer_norm(x2, ln2g_ref[...], ln2b_ref[...])


def _ffn_call(mix, x, mod, conv_prev, w_out, ln1_g, ln1_b, w_up, conv_w, conv_b, w_down, ln2_g, ln2_b,
              alpha, tb, tl):
    b, l, _ = x.shape
    nj = l // tl
    tm = tb * tl
    carry_rows = 8 if tb == 1 else 0
    assert carry_rows or (nj == 1 and tl == 8)
    mod_spec = lambda k: pl.BlockSpec((tb, 1, D_MODEL), lambda i, j, k=k: (i, 0, k))
    row = lambda a: a.reshape(1, -1)
    tail_rows = 8 if carry_rows else tm
    kern = functools.partial(_ffn_kernel, alpha, carry_rows)
    return pl.pallas_call(
        kern,
        grid=(b // tb, nj),
        in_specs=[pl.BlockSpec((tm, 2 * RET_W), lambda i, j: (i * nj + j, 0)),
                  pl.BlockSpec((tb, tl, D_MODEL), lambda i, j: (i, j, 0)),
                  mod_spec(2), mod_spec(3), mod_spec(4), mod_spec(5),
                  _const_spec((2 * RET_W, D_MODEL)),
                  _const_spec((1, D_MODEL)), _const_spec((1, D_MODEL)),
                  _const_spec((D_MODEL, UP_W)),
                  _const_spec((3, UP_W)), _const_spec((1, UP_W)),
                  pl.BlockSpec((tb, 2, UP_W), lambda i, j: (i, 0, 0)),
                  _const_spec((D_FF, D_MODEL)),
                  _const_spec((1, D_MODEL)), _const_spec((1, D_MODEL))],
        out_specs=[pl.BlockSpec((tb, tl, D_MODEL), lambda i, j: (i, j, 0)),
                   pl.BlockSpec((tail_rows, UP_W), lambda i, j: (i, 0))],
        out_shape=[jax.ShapeDtypeStruct((b, l, D_MODEL), F32),
                   jax.ShapeDtypeStruct((b * 8, UP_W), F32)],
        scratch_shapes=[pltpu.VMEM((8, UP_W), F32)],
        compiler_params=_cparams(2),
        name="ffn",
    )(mix, x, mod, mod, mod, mod, w_out, row(ln1_g), row(ln1_b), w_up, conv_w, row(conv_b), conv_prev, w_down,
      row(ln2_g), row(ln2_b))


def kernel(x_prompt, x_sample, c_prompt, c_sample, state_ret, cache_swa_k, cache_swa_v, state_conv, rel_bias, w_ada, b_ada, w_in, swa_sinks, w_out, ln1_g, ln1_b, w_up, conv_w, conv_b, w_down, ln2_g, ln2_b):
    depth = w_ada.shape[0]
    bp, lp, _ = x_prompt.shape
    bs, ls, _ = x_sample.shape
    alpha = (2.0 * depth) ** 0.25
    tl_p = 256
    tb_s = 256 // ls
    tb_attn = 8

    c_all = jnp.concatenate([c_prompt, c_sample], axis=0)
    mod_all = _ada_call(c_all, w_ada, b_ada)
    bias = _bias_call(rel_bias)

    cos_p, sin_p = _rope_tables(jnp.arange(lp, dtype=jnp.int32))
    cos_s, sin_s = _rope_tables(PAST_LEN + jnp.arange(ls, dtype=jnp.int32))
    cos_s, sin_s = jnp.tile(cos_s, (tb_s, 1)), jnp.tile(sin_s, (tb_s, 1))
    tab_p = _prompt_tables()
    tab_s = _sample_tables(ls)
    conv0 = jnp.zeros((bp, 2, UP_W), F32)

    w_in_b = w_in.astype(BF16)
    w_out_b = w_out.astype(BF16)
    w_up_b = w_up.astype(BF16)
    w_down_b = w_down.astype(BF16)

    xp, xs = x_prompt, x_sample
    p_ret, p_k, p_v, p_conv = [], [], [], []
    s_ret, s_k, s_v, s_conv = [], [], [], []
    for l in range(depth):
        mod_p = mod_all[l, :bp].reshape(bp, 1, 6 * D_MODEL)
        mod_s = mod_all[l, bp:].reshape(bs, 1, 6 * D_MODEL)
        ffn_w = (w_out_b[l], ln1_g[l], ln1_b[l], w_up_b[l], conv_w[l], conv_b[l], w_down_b[l], ln2_g[l], ln2_b[l])

        qkvg, kv = _proj_call(xp, mod_p, w_in_b[l], cos_p, sin_p, 1, tl_p)
        mix, r_p = _attn_prompt_call(qkvg, kv, bias, swa_sinks[l], tab_p, bp, lp)
        xp, tail = _ffn_call(mix, xp, mod_p, conv0, *ffn_w, alpha, 1, tl_p)
        kv3 = kv.reshape(bp, lp, 2, KV_HEADS, HEAD_DIM)
        p_ret.append(r_p)
        p_k.append(kv3[:, lp - WINDOW:, 0])
        p_v.append(kv3[:, lp - WINDOW:, 1])
        p_conv.append(tail.reshape(bp, 8, UP_W)[:, 6:8])

        qkvg, kv = _proj_call(xs, mod_s, w_in_b[l], cos_s, sin_s, tb_s, ls)
        mix, r_s, k_s, v_s = _attn_sample_call(qkvg, kv, state_ret[l], cache_swa_k[l], cache_swa_v[l], bias,
                                               swa_sinks[l], tab_s, bs, ls, tb_attn)
        xs, tail = _ffn_call(mix, xs, mod_s, state_conv[l], *ffn_w, alpha, tb_s, ls)
        s_ret.append(r_s)
        s_k.append(k_s)
        s_v.append(v_s)
        s_conv.append(tail.reshape(bs, ls, UP_W)[:, ls - 2:])

    return (xp, xs,
            jnp.stack(p_ret), jnp.stack(p_k), jnp.stack(p_v), jnp.stack(p_conv),
            jnp.stack(s_ret), jnp.stack(s_k), jnp.stack(s_v), jnp.stack(s_conv))
```

```python
import functools
import math

import jax
import jax.numpy as jnp
import numpy as np
from jax import lax
from jax.experimental import pallas as pl
from jax.experimental.pallas import tpu as pltpu

F32 = jnp.float32
BF16 = jnp.bfloat16

D_MODEL = 1024
HEAD_DIM = 64
N_HEADS = 8
RET_W = N_HEADS * HEAD_DIM
KV_HEADS = 2
KV_W = KV_HEADS * HEAD_DIM
D_FF = 2816
UP_W = 2 * D_FF
IN_COLS = 4 * RET_W + RET_W + 2 * KV_W
QKVG_W = 5 * RET_W
WINDOW = 128
CHUNK = 128
NUM_BUCKETS = 32
PAST_LEN = 8192
ROPE_BASE = 10000.0
LN_EPS = 1e-5
NEG_INF = -1e30
LANES = 128
FF_CHUNK = 256
VMEM_LIMIT = 56 * 1024 * 1024


def _cparams(n_axes):
    return pltpu.CompilerParams(dimension_semantics=("arbitrary",) * n_axes,
                                vmem_limit_bytes=VMEM_LIMIT)


def _const_spec(shape):
    nd = len(shape)
    return pl.BlockSpec(shape, lambda *_: (0,) * nd, pipeline_mode=pl.Buffered(1))


def _rope_tables(pos):
    half = HEAD_DIM // 2
    inv = 1.0 / (ROPE_BASE ** (jnp.arange(half, dtype=F32) / half))
    ang = pos.astype(F32)[:, None] * inv[None, :]
    cos = jnp.cos(ang)
    sin = jnp.sin(ang)
    cos_h = jnp.concatenate([cos, cos], axis=-1)
    sin_h = jnp.concatenate([-sin, sin], axis=-1)
    return jnp.tile(cos_h, (1, N_HEADS)), jnp.tile(sin_h, (1, N_HEADS))


def _decay_tables(chunk):
    log_g = jnp.log(1.0 - 2.0 ** (-5.0 - jnp.arange(N_HEADS, dtype=F32)))
    idx = jnp.arange(chunk)
    diff = idx[:, None] - idx[None, :]
    decay_mat = jnp.where(diff[None] >= 0,
                          jnp.exp(log_g[:, None, None] * jnp.maximum(diff, 0)[None].astype(F32)), 0.0)
    q_dec = jnp.exp(log_g[None, :] * (idx[:, None] + 1).astype(F32))
    k_dec = jnp.exp(log_g[None, :] * (chunk - 1 - idx[:, None]).astype(F32))
    chunk_dec = jnp.exp(log_g * chunk)
    return decay_mat, q_dec, k_dec, chunk_dec


def _bucket_table(n_q, n_k):
    i = np.arange(n_q)[:, None]
    j = np.arange(n_k)[None, :]
    dist = i - j + WINDOW
    n = np.maximum(dist, 0)
    max_exact = NUM_BUCKETS // 2
    nf = np.maximum(n, max_exact).astype(np.float64)
    large = max_exact + (np.log(nf / max_exact) / math.log(WINDOW / max_exact)
                         * (NUM_BUCKETS - max_exact)).astype(np.int32)
    large = np.minimum(large, NUM_BUCKETS - 1)
    bucket = np.where(n < max_exact, n, large).astype(np.int32)
    valid = ((dist >= 0) & (dist < WINDOW)).astype(np.int32)
    return bucket, valid


def _ada_kernel(c_ref, w_ref, b_ref, o_ref):
    c = c_ref[...]
    s = (c * jax.nn.sigmoid(c)).astype(BF16)
    o_ref[0] = jnp.dot(s, w_ref[0].astype(BF16), preferred_element_type=F32) + b_ref[0]


def _ada_call(c_all, w_ada, b_ada):
    depth = w_ada.shape[0]
    n_rows = c_all.shape[0]
    tn = 1536
    return pl.pallas_call(
        _ada_kernel,
        grid=(depth, 6 * D_MODEL // tn),
        in_specs=[pl.BlockSpec((n_rows, D_MODEL), lambda l, n: (0, 0)),
                  pl.BlockSpec((1, D_MODEL, tn), lambda l, n: (l, 0, n)),
                  pl.BlockSpec((1, 1, tn), lambda l, n: (l, 0, n))],
        out_specs=pl.BlockSpec((1, n_rows, tn), lambda l, n: (l, 0, n)),
        out_shape=jax.ShapeDtypeStruct((depth, n_rows, 6 * D_MODEL), F32),
        compiler_params=_cparams(2),
        name="ada",
    )(c_all, w_ada, b_ada.reshape(depth, 1, 6 * D_MODEL))


def _bias_kernel(rb_ref, bucket_ref, valid_ref, o_ref):
    bucket = bucket_ref[...]
    valid = valid_ref[...] > 0
    first_ok = lax.broadcasted_iota(jnp.int32, bucket.shape, 1) >= WINDOW
    for h in range(N_HEADS):
        acc = jnp.zeros(bucket.shape, F32)
        for b in range(NUM_BUCKETS):
            acc = jnp.where(bucket == b, rb_ref[b, h], acc)
        o_ref[0, h] = jnp.where(valid, acc, NEG_INF)
        o_ref[1, h] = jnp.where(valid & first_ok, acc, NEG_INF)


def _bias_call(rel_bias):
    bucket, valid = _bucket_table(CHUNK, 2 * CHUNK)
    return pl.pallas_call(
        _bias_kernel,
        in_specs=[pl.BlockSpec(memory_space=pltpu.SMEM),
                  pl.BlockSpec((CHUNK, 2 * CHUNK), lambda: (0, 0)),
                  pl.BlockSpec((CHUNK, 2 * CHUNK), lambda: (0, 0))],
        out_specs=pl.BlockSpec((2, N_HEADS, CHUNK, 2 * CHUNK), lambda: (0, 0, 0, 0)),
        out_shape=jax.ShapeDtypeStruct((2, N_HEADS, CHUNK, 2 * CHUNK), F32),
        name="swa_bias",
    )(rel_bias, jnp.asarray(bucket), jnp.asarray(valid))


def _swap_halves(x):
    lane = lax.broadcasted_iota(jnp.int32, (1, LANES), 1)
    first = (lane % HEAD_DIM) < (HEAD_DIM // 2)
    cols = []
    for c in range(x.shape[1] // LANES):
        xc = x[:, c * LANES:(c + 1) * LANES]
        cols.append(jnp.where(first, pltpu.roll(xc, LANES - HEAD_DIM // 2, 1), pltpu.roll(xc, HEAD_DIM // 2, 1)))
    return jnp.concatenate(cols, axis=1)


def _proj_kernel(x_ref, sc_ref, sh_ref, w_ref, cos_ref, sin_ref, qkvg_ref, kv_ref):
    tb, tl, _ = x_ref.shape
    tm = tb * tl
    h = x_ref[...] * (1.0 + sc_ref[...]) + sh_ref[...]
    h = h.reshape(tm, D_MODEL).astype(BF16)
    proj = jnp.dot(h, w_ref[...], preferred_element_type=F32)
    cos = cos_ref[...]
    sin = sin_ref[...]
    rq = proj[:, 0:RET_W]
    rk = proj[:, RET_W:2 * RET_W]
    rq = rq * cos + _swap_halves(rq) * sin
    rk = (rk * cos + _swap_halves(rk) * sin) * (HEAD_DIM ** -0.5)
    rg = proj[:, 3 * RET_W:4 * RET_W]
    qkvg_ref[:, 0:RET_W] = rq.astype(BF16)
    qkvg_ref[:, RET_W:2 * RET_W] = rk.astype(BF16)
    qkvg_ref[:, 2 * RET_W:3 * RET_W] = proj[:, 2 * RET_W:3 * RET_W].astype(BF16)
    qkvg_ref[:, 3 * RET_W:4 * RET_W] = (rg * jax.nn.sigmoid(rg)).astype(BF16)
    qkvg_ref[:, 4 * RET_W:5 * RET_W] = (proj[:, 4 * RET_W:5 * RET_W] * (HEAD_DIM ** -0.5)).astype(BF16)
    kv_ref[...] = proj[:, 5 * RET_W:IN_COLS]


def _layer_spec(shape, layer):
    nd = len(shape)
    return pl.BlockSpec((None,) + tuple(shape), lambda *_: (layer,) + (0,) * nd, pipeline_mode=pl.Buffered(1))


def _mod_spec(tb, layer, k):
    return pl.BlockSpec((None, tb, 1, D_MODEL), lambda i, j: (layer, i, 0, k))


def _proj_call(x, mod, w_in, layer, cos, sin, tb, tl):
    b, l, _ = x.shape
    nj = l // tl
    tm = tb * tl
    tab_idx = (lambda i, j: (j, 0)) if cos.shape[0] == l and nj > 1 else (lambda i, j: (0, 0))
    return pl.pallas_call(
        _proj_kernel,
        grid=(b // tb, nj),
        in_specs=[pl.BlockSpec((tb, tl, D_MODEL), lambda i, j: (i, j, 0)),
                  _mod_spec(tb, layer, 1),
                  _mod_spec(tb, layer, 0),
                  _layer_spec((D_MODEL, IN_COLS), layer),
                  pl.BlockSpec((tm, RET_W), tab_idx),
                  pl.BlockSpec((tm, RET_W), tab_idx)],
        out_specs=[pl.BlockSpec((tm, QKVG_W), lambda i, j: (i * nj + j, 0)),
                   pl.BlockSpec((tm, 2 * KV_W), lambda i, j: (i * nj + j, 0))],
        out_shape=[jax.ShapeDtypeStruct((b * l, QKVG_W), BF16),
                   jax.ShapeDtypeStruct((b * l, 2 * KV_W), F32)],
        compiler_params=_cparams(2),
        name="proj",
    )(x, mod, mod, w_in, cos, sin)


def _lane_mask_low():
    return lax.broadcasted_iota(jnp.int32, (1, LANES), 1) < HEAD_DIM


def _head_norm(o, ones_bd):
    outs = []
    gw = ones_bd.shape[0]
    for g in range(RET_W // gw):
        og = o[:, g * gw:(g + 1) * gw]
        mu = jnp.dot(og.astype(BF16), ones_bd, preferred_element_type=F32)
        d = og - mu
        var = jnp.dot((d * d).astype(BF16), ones_bd, preferred_element_type=F32)
        outs.append(d * lax.rsqrt(var + LN_EPS))
    return jnp.concatenate(outs, axis=1)


def _swa_operands(k_all, v_all):
    low = _lane_mask_low()
    zero = jnp.zeros_like(k_all)
    k_rot = pltpu.roll(k_all, HEAD_DIM, 1)
    v_rot = pltpu.roll(v_all, HEAD_DIM, 1)
    ops = []
    for g in range(KV_HEADS):
        k_lo, k_hi = (k_all, k_rot) if g == 0 else (k_rot, k_all)
        v_lo, v_hi = (v_all, v_rot) if g == 0 else (v_rot, v_all)
        kc = jnp.concatenate([jnp.where(low, k_lo, zero), jnp.where(low, zero, k_hi)], axis=0)
        vc = jnp.concatenate([jnp.where(low, v_lo, zero), jnp.where(low, zero, v_hi)], axis=0)
        ops.append((kc, vc))
    return ops


def _swa_softmax(lg, p, bias_ref, sink_ref):
    rows = lg.shape[0]
    low = _lane_mask_low()
    probs, dens = [], []
    for s in range(2):
        hh = 2 * p + s
        lo = lg[:, s * 2 * CHUNK:(s + 1) * 2 * CHUNK] + bias_ref[hh, 0:rows, :]
        sink = sink_ref[hh]
        m = jnp.maximum(jnp.max(lo, axis=-1, keepdims=True), sink)
        pe = jnp.exp(lo - m)
        dens.append(jnp.sum(pe, axis=-1, keepdims=True) + jnp.exp(sink - m))
        probs.append(pe)
    return jnp.concatenate(probs, axis=1), jnp.where(low, dens[0], dens[1])


def _attn_prompt_kernel(sink_ref, qkvg_ref, kv_ref, bias_ref, dm_ref, qd_ref, kd_ref, cd_ref, ones_ref,
                        mix_ref, s_out_ref, s_scr, pk_scr, pv_scr):
    i = pl.program_id(1)

    @pl.when(i == 0)
    def _():
        s_scr[...] = jnp.zeros_like(s_scr)
        pk_scr[...] = jnp.zeros_like(pk_scr)
        pv_scr[...] = jnp.zeros_like(pv_scr)

    q = qkvg_ref[:, 0:RET_W]
    k = qkvg_ref[:, RET_W:2 * RET_W]
    v = qkvg_ref[:, 2 * RET_W:3 * RET_W]
    gate = qkvg_ref[:, 3 * RET_W:4 * RET_W]
    sq = qkvg_ref[:, 4 * RET_W:5 * RET_W]
    low = _lane_mask_low()
    zero = jnp.zeros((CHUNK, LANES), BF16)
    n_pairs = N_HEADS // 2
    pairs_per_kv = n_pairs // KV_HEADS
    gw = cd_ref.shape[1]
    n_groups = RET_W // gw
    nt = (((1,), (1,)), ((), ()))
    tn = (((0,), (0,)), ((), ()))

    kd = (k.astype(F32) * kd_ref[...]).astype(BF16)
    ret_ops = []
    for p in range(n_pairs):
        kp = k[:, p * LANES:(p + 1) * LANES]
        vp = v[:, p * LANES:(p + 1) * LANES]
        ret_ops.append((jnp.concatenate([jnp.where(low, kp, zero), jnp.where(low, zero, kp)], axis=0),
                        jnp.concatenate([jnp.where(low, vp, zero), jnp.where(low, zero, vp)], axis=0)))
    k_new = kv_ref[:, 0:KV_W].astype(BF16)
    v_new = kv_ref[:, KV_W:2 * KV_W].astype(BF16)
    swa_ops = _swa_operands(jnp.concatenate([pk_scr[...], k_new], axis=0),
                            jnp.concatenate([pv_scr[...], v_new], axis=0))
    pk_scr[...] = k_new
    pv_scr[...] = v_new

    scores = [lax.dot_general(q[:, p * LANES:(p + 1) * LANES], ret_ops[p][0], nt, preferred_element_type=F32)
              for p in range(n_pairs)]
    s_prev = [s_scr[g] for g in range(n_groups)]
    cross = [jnp.dot(q[:, g * gw:(g + 1) * gw], s_prev[g].astype(BF16), preferred_element_type=F32)
             for g in range(n_groups)]
    upd = [lax.dot_general(kd[:, g * gw:(g + 1) * gw], v[:, g * gw:(g + 1) * gw], tn, preferred_element_type=F32)
           for g in range(n_groups)]
    logits = [lax.dot_general(sq[:, p * LANES:(p + 1) * LANES], swa_ops[p // pairs_per_kv][0], nt,
                              preferred_element_type=F32) for p in range(n_pairs)]

    scores = [(scores[p] * dm_ref[p]).astype(BF16) for p in range(n_pairs)]
    for g in range(n_groups):
        cd = cd_ref[g]
        s_scr[g] = s_prev[g] * cd + jnp.where(cd > 0.0, upd[g], 0.0)
    soft = [_swa_softmax(logits[p], p, bias_ref.at[0], sink_ref) for p in range(n_pairs)]

    intra = [jnp.dot(scores[p], ret_ops[p][1], preferred_element_type=F32) for p in range(n_pairs)]
    swa = [jnp.dot(soft[p][0].astype(BF16), swa_ops[p // pairs_per_kv][1], preferred_element_type=F32) / soft[p][1]
           for p in range(n_pairs)]

    o = jnp.concatenate(intra, axis=1) + jnp.concatenate(cross, axis=1) * qd_ref[...]
    ret = _head_norm(o, ones_ref[...]) * gate.astype(F32)
    mix_ref[:, 0:RET_W] = ret.astype(BF16)
    mix_ref[:, RET_W:2 * RET_W] = jnp.concatenate(swa, axis=1).astype(BF16)

    @pl.when(i == pl.num_programs(1) - 1)
    def _():
        for h in range(N_HEADS):
            g, hl = divmod(h, gw // HEAD_DIM)
            s_out_ref[0, h] = s_scr[g, hl * HEAD_DIM:(hl + 1) * HEAD_DIM, hl * HEAD_DIM:(hl + 1) * HEAD_DIM]


def _prompt_tables():
    decay_mat, q_dec, k_dec, chunk_dec = _decay_tables(CHUNK)
    dm = jnp.concatenate([decay_mat[0::2], decay_mat[1::2]], axis=2)
    qd = jnp.repeat(q_dec, HEAD_DIM, axis=1)
    kd = jnp.repeat(k_dec, HEAD_DIM, axis=1)
    gw = 4 * HEAD_DIM
    blk = np.kron(np.eye(4, dtype=np.float32), np.ones((HEAD_DIM, HEAD_DIM), np.float32))
    cd = jnp.repeat(chunk_dec.reshape(2, 4), HEAD_DIM, axis=1)[:, :, None] * jnp.asarray(blk)[None]
    ones_bd = jnp.asarray(blk / HEAD_DIM, BF16)
    assert cd.shape == (2, gw, gw)
    return dm, qd, kd, cd, ones_bd


def _attn_prompt_call(qkvg, kv, bias, sinks, tables, b, l):
    dm, qd, kd, cd, ones_bd = tables
    nc = l // CHUNK
    gw = cd.shape[1]
    return pl.pallas_call(
        _attn_prompt_kernel,
        grid=(b, nc),
        in_specs=[pl.BlockSpec(memory_space=pltpu.SMEM),
                  pl.BlockSpec((CHUNK, QKVG_W), lambda bi, i: (bi * nc + i, 0)),
                  pl.BlockSpec((CHUNK, 2 * KV_W), lambda bi, i: (bi * nc + i, 0)),
                  pl.BlockSpec((1, N_HEADS, CHUNK, 2 * CHUNK), lambda bi, i: (jnp.where(i == 0, 1, 0), 0, 0, 0)),
                  _const_spec(dm.shape), _const_spec(qd.shape), _const_spec(kd.shape),
                  _const_spec(cd.shape), _const_spec(ones_bd.shape)],
        out_specs=[pl.BlockSpec((CHUNK, 2 * RET_W), lambda bi, i: (bi * nc + i, 0)),
                   pl.BlockSpec((1, N_HEADS, HEAD_DIM, HEAD_DIM), lambda bi, i: (bi, 0, 0, 0))],
        out_shape=[jax.ShapeDtypeStruct((b * l, 2 * RET_W), BF16),
                   jax.ShapeDtypeStruct((b, N_HEADS, HEAD_DIM, HEAD_DIM), F32)],
        scratch_shapes=[pltpu.VMEM((RET_W // gw, gw, gw), F32),
                        pltpu.VMEM((CHUNK, KV_W), BF16),
                        pltpu.VMEM((CHUNK, KV_W), BF16)],
        compiler_params=_cparams(2),
        name="attn_prompt",
    )(sinks, qkvg, kv, bias, dm, qd, kd, cd, ones_bd)


def _attn_sample_kernel(sink_ref, qkvg_ref, kv_ref, s_in_ref, ck_ref, cv_ref, bias_ref, ds_ref, qd_ref, kd_ref,
                        cdr_ref, hm_ref, ones_ref, mix_ref, s_out_ref, ck_out_ref, cv_out_ref):
    tb = s_in_ref.shape[0]
    tl = qkvg_ref.shape[0] // tb
    hm = hm_ref[...]
    nt = (((1,), (1,)), ((), ()))
    tn = (((0,), (0,)), ((), ()))
    pairs_per_kv = N_HEADS // KV_HEADS // 2
    pad = jnp.zeros((CHUNK - tl, KV_W), F32)

    def stack_heads(x):
        return (jnp.concatenate([x] * N_HEADS, axis=0) * hm).astype(BF16)

    wave1 = []
    for e in range(tb):
        rows = slice(e * tl, (e + 1) * tl)
        q = qkvg_ref[rows, 0:RET_W]
        k32 = qkvg_ref[rows, RET_W:2 * RET_W].astype(F32)
        v32 = qkvg_ref[rows, 2 * RET_W:3 * RET_W].astype(F32)
        sq32 = qkvg_ref[rows, 4 * RET_W:5 * RET_W].astype(F32)
        v_big = stack_heads(v32)
        s = lax.dot_general(q, stack_heads(k32), nt, preferred_element_type=F32)
        s_prev = s_in_ref[e]
        oc = jnp.dot(stack_heads(q.astype(F32)), s_prev.astype(BF16), preferred_element_type=F32)
        v_stack = jnp.concatenate([v32[:, h * HEAD_DIM:(h + 1) * HEAD_DIM] for h in range(N_HEADS)], axis=0)
        upd = lax.dot_general(stack_heads(k32 * kd_ref[...]), v_stack.astype(BF16), tn,
                              preferred_element_type=F32)
        s_out_ref[e] = s_prev * cdr_ref[...] + upd

        k_new = kv_ref[rows, 0:KV_W]
        v_new = kv_ref[rows, KV_W:2 * KV_W]
        k_all = jnp.concatenate([ck_ref[e].astype(BF16), jnp.concatenate([k_new, pad], axis=0).astype(BF16)], axis=0)
        v_all = jnp.concatenate([cv_ref[e].astype(BF16), jnp.concatenate([v_new, pad], axis=0).astype(BF16)], axis=0)
        swa_ops = _swa_operands(k_all, v_all)
        logits = []
        for g in range(KV_HEADS):
            lhs = jnp.concatenate([sq32[:, (g * pairs_per_kv + pp) * LANES:(g * pairs_per_kv + pp + 1) * LANES]
                                   for pp in range(pairs_per_kv)], axis=0).astype(BF16)
            logits.append(lax.dot_general(lhs, swa_ops[g][0], nt, preferred_element_type=F32))
        ck_out_ref[e, 0:WINDOW - tl, :] = ck_ref[e, tl:WINDOW, :]
        cv_out_ref[e, 0:WINDOW - tl, :] = cv_ref[e, tl:WINDOW, :]
        ck_out_ref[e, WINDOW - tl:WINDOW, :] = k_new
        cv_out_ref[e, WINDOW - tl:WINDOW, :] = v_new
        wave1.append((s, oc, v_big, swa_ops, logits))

    mid = []
    for e in range(tb):
        s, oc, v_big, swa_ops, logits = wave1[e]
        s = (s * ds_ref[...]).astype(BF16)
        soft = []
        for g in range(KV_HEADS):
            per_pair = [_swa_softmax(logits[g][pp * tl:(pp + 1) * tl, :], g * pairs_per_kv + pp, bias_ref.at[0],
                                     sink_ref) for pp in range(pairs_per_kv)]
            soft.append((jnp.concatenate([x[0] for x in per_pair], axis=0).astype(BF16),
                         jnp.concatenate([x[1] for x in per_pair], axis=0)))
        mid.append((s, soft))
    o_rows, swa_rows = [], []
    for e in range(tb):
        s, soft = mid[e]
        _, oc, v_big, swa_ops, _ = wave1[e]
        o = jnp.dot(s, v_big, preferred_element_type=F32)
        oc_nat = jnp.concatenate([oc[h * tl:(h + 1) * tl, :] for h in range(N_HEADS)], axis=1)
        o_rows.append(o + oc_nat * qd_ref[...])
        outs = []
        for g in range(KV_HEADS):
            oo = jnp.dot(soft[g][0], swa_ops[g][1], preferred_element_type=F32) / soft[g][1]
            outs += [oo[pp * tl:(pp + 1) * tl, :] for pp in range(pairs_per_kv)]
        swa_rows.append(jnp.concatenate(outs, axis=1))

    gate = qkvg_ref[:, 3 * RET_W:4 * RET_W].astype(F32)
    ret = _head_norm(jnp.concatenate(o_rows, axis=0), ones_ref[...]) * gate
    mix_ref[:, 0:RET_W] = ret.astype(BF16)
    mix_ref[:, RET_W:2 * RET_W] = jnp.concatenate(swa_rows, axis=0).astype(BF16)


def _sample_tables(tl):
    decay_mat, q_dec, k_dec, chunk_dec = _decay_tables(tl)
    ds = jnp.transpose(decay_mat, (1, 0, 2)).reshape(tl, N_HEADS * tl)
    qd = jnp.repeat(q_dec, HEAD_DIM, axis=1)
    kd = jnp.repeat(k_dec, HEAD_DIM, axis=1)
    cdr = jnp.broadcast_to(jnp.repeat(chunk_dec, HEAD_DIM)[:, None], (RET_W, HEAD_DIM))
    hm = np.kron(np.eye(N_HEADS, dtype=np.float32), np.ones((tl, HEAD_DIM), np.float32))
    blk = np.kron(np.eye(4, dtype=np.float32), np.ones((HEAD_DIM, HEAD_DIM), np.float32))
    return ds, qd, kd, cdr, jnp.asarray(hm), jnp.asarray(blk / HEAD_DIM, BF16)


def _attn_sample_call(qkvg, kv, s_rows, ck2, cv2, layer, bias, sinks, tables, b, tl, tb):
    ds, qd, kd, cdr, hm, ones_bd = tables
    batch3 = lambda i: (i, 0, 0)
    layer4 = lambda i: (layer, i, 0, 0)
    mix, s_out, ck_out, cv_out = pl.pallas_call(
        _attn_sample_kernel,
        grid=(b // tb,),
        in_specs=[pl.BlockSpec(memory_space=pltpu.SMEM),
                  pl.BlockSpec((tb * tl, QKVG_W), lambda i: (i, 0)),
                  pl.BlockSpec((tb * tl, 2 * KV_W), lambda i: (i, 0)),
                  pl.BlockSpec((None, tb, RET_W, HEAD_DIM), layer4),
                  pl.BlockSpec((None, tb, WINDOW, KV_W), layer4),
                  pl.BlockSpec((None, tb, WINDOW, KV_W), layer4),
                  pl.BlockSpec((1, N_HEADS, CHUNK, 2 * CHUNK), lambda i: (0, 0, 0, 0)),
                  _const_spec(ds.shape), _const_spec(qd.shape), _const_spec(kd.shape),
                  _const_spec(cdr.shape), _const_spec(hm.shape), _const_spec(ones_bd.shape)],
        out_specs=[pl.BlockSpec((tb * tl, 2 * RET_W), lambda i: (i, 0)),
                   pl.BlockSpec((tb, RET_W, HEAD_DIM), batch3),
                   pl.BlockSpec((tb, WINDOW, KV_W), batch3),
                   pl.BlockSpec((tb, WINDOW, KV_W), batch3)],
        out_shape=[jax.ShapeDtypeStruct((b * tl, 2 * RET_W), BF16),
                   jax.ShapeDtypeStruct((b, RET_W, HEAD_DIM), F32),
                   jax.ShapeDtypeStruct((b, WINDOW, KV_W), F32),
                   jax.ShapeDtypeStruct((b, WINDOW, KV_W), F32)],
        compiler_params=_cparams(1),
        name="attn_sample",
    )(sinks, qkvg, kv, s_rows, ck2, cv2, bias, ds, qd, kd, cdr, hm, ones_bd)
    return (mix, s_out.reshape(b, N_HEADS, HEAD_DIM, HEAD_DIM),
            ck_out.reshape(b, WINDOW, KV_HEADS, HEAD_DIM), cv_out.reshape(b, WINDOW, KV_HEADS, HEAD_DIM))


def _layer_norm(x, g, b):
    mu = jnp.mean(x, axis=-1, keepdims=True)
    xc = x - mu
    var = jnp.mean(xc * xc, axis=-1, keepdims=True)
    return xc * lax.rsqrt(var + LN_EPS) * g + b


def _ffn_kernel(alpha, carry_rows, mix_ref, x_ref, g1_ref, sh2_ref, sc2_ref, g2_ref, wout_ref, ln1g_ref, ln1b_ref,
                wup_ref, cw_ref, cb_ref, prev_ref, wdn_ref, ln2g_ref, ln2b_ref, xo_ref, tail_ref, carry_scr):
    tb, tl, _ = x_ref.shape
    tm = tb * tl
    j = pl.program_id(1)

    y = jnp.dot(mix_ref[...], wout_ref[...], preferred_element_type=F32)
    x1 = alpha * x_ref[...] + g1_ref[...] * y.reshape(tb, tl, D_MODEL)
    x1 = _layer_norm(x1, ln1g_ref[...], ln1b_ref[...])
    h2 = (x1 * (1.0 + sc2_ref[...]) + sh2_ref[...]).reshape(tm, D_MODEL).astype(BF16)

    if carry_rows:
        @pl.when(j == 0)
        def _():
            carry_scr[carry_rows - 2:carry_rows, :] = prev_ref[0]

    fix_rows = 8 if carry_rows else tm
    t_idx = lax.broadcasted_iota(jnp.int32, (fix_rows, 1), 0) % tl
    is_t0 = t_idx == 0
    is_t1 = t_idx == 1

    def up_cols(c, half):
        return slice(half * D_FF + c * FF_CHUNK, half * D_FF + (c + 1) * FF_CHUNK)

    def up_dots(c):
        return [jnp.dot(h2, wup_ref[:, up_cols(c, half)], preferred_element_type=F32) for half in range(2)]

    def conv(up, cols):
        if carry_rows:
            p0 = carry_scr[carry_rows - 2:carry_rows - 1, cols]
            p1 = carry_scr[carry_rows - 1:carry_rows, cols]
            tail_ref[:, cols] = up[tm - carry_rows:tm, :]
        else:
            p0 = jnp.broadcast_to(prev_ref[:, 0:1, cols], (tb, tl, FF_CHUNK)).reshape(tm, FF_CHUNK)
            p1 = jnp.broadcast_to(prev_ref[:, 1:2, cols], (tb, tl, FF_CHUNK)).reshape(tm, FF_CHUNK)
            tail_ref[:, cols] = up
        r1 = pltpu.roll(up, 1, 0)
        r2 = pltpu.roll(up, 2, 0)
        s1 = jnp.where(is_t0, p1, r1[0:fix_rows])
        s2 = jnp.where(is_t0, p0, jnp.where(is_t1, p1, r2[0:fix_rows]))
        if fix_rows < tm:
            s1 = jnp.concatenate([s1, r1[fix_rows:]], axis=0)
            s2 = jnp.concatenate([s2, r2[fix_rows:]], axis=0)
        return cb_ref[:, cols] + s2 * cw_ref[0:1, cols] + s1 * cw_ref[1:2, cols] + up * cw_ref[2:3, cols]

    acc = jnp.zeros((tm, D_MODEL), F32)
    n_chunks = D_FF // FF_CHUNK
    ups = up_dots(0)
    for c in range(n_chunks):
        nxt = up_dots(c + 1) if c + 1 < n_chunks else None
        ua = conv(ups[0], up_cols(c, 0))
        ub = conv(ups[1], up_cols(c, 1))
        gated = (ua * jax.nn.sigmoid(ua) * ub).astype(BF16)
        acc = acc + jnp.dot(gated, wdn_ref[c * FF_CHUNK:(c + 1) * FF_CHUNK, :], preferred_element_type=F32)
        ups = nxt

    if carry_rows:
        carry_scr[...] = tail_ref[...]
    x2 = alpha * x1 + g2_ref[...] * acc.reshape(tb, tl, D_MODEL)
    xo_ref[...] = _layer_norm(x2, ln2g_ref[...], ln2b_ref[...])


def _ffn_call(mix, x, mod, conv_prev, w_out, ln1_g, ln1_b, w_up, conv_w, conv_b, w_down, ln2_g, ln2_b,
              layer, alpha, tb, tl):
    prev_layer = layer if conv_prev.shape[0] > 1 else 0
    b, l, _ = x.shape
    nj = l // tl
    tm = tb * tl
    carry_rows = 8 if tb == 1 else 0
    assert carry_rows or (nj == 1 and tl == 8)
    tail_rows = 8 if carry_rows else tm
    kern = functools.partial(_ffn_kernel, alpha, carry_rows)
    return pl.pallas_call(
        kern,
        grid=(b // tb, nj),
        in_specs=[pl.BlockSpec((tm, 2 * RET_W), lambda i, j: (i * nj + j, 0)),
                  pl.BlockSpec((tb, tl, D_MODEL), lambda i, j: (i, j, 0)),
                  _mod_spec(tb, layer, 2), _mod_spec(tb, layer, 3), _mod_spec(tb, layer, 4), _mod_spec(tb, layer, 5),
                  _layer_spec((2 * RET_W, D_MODEL), layer),
                  _layer_spec((1, D_MODEL), layer), _layer_spec((1, D_MODEL), layer),
                  _layer_spec((D_MODEL, UP_W), layer),
                  _layer_spec((3, UP_W), layer), _layer_spec((1, UP_W), layer),
                  pl.BlockSpec((None, tb, 2, UP_W), lambda i, j: (prev_layer, i, 0, 0)),
                  _layer_spec((D_FF, D_MODEL), layer),
                  _layer_spec((1, D_MODEL), layer), _layer_spec((1, D_MODEL), layer)],
        out_specs=[pl.BlockSpec((tb, tl, D_MODEL), lambda i, j: (i, j, 0)),
                   pl.BlockSpec((tail_rows, UP_W), lambda i, j: (i, 0))],
        out_shape=[jax.ShapeDtypeStruct((b, l, D_MODEL), F32),
                   jax.ShapeDtypeStruct((b * 8, UP_W), F32)],
        scratch_shapes=[pltpu.VMEM((8, UP_W), F32)],
        compiler_params=_cparams(2),
        name="ffn",
    )(mix, x, mod, mod, mod, mod, w_out, ln1_g, ln1_b, w_up, conv_w, conv_b, conv_prev, w_down, ln2_g, ln2_b)


def kernel(x_prompt, x_sample, c_prompt, c_sample, state_ret, cache_swa_k, cache_swa_v, state_conv, rel_bias, w_ada, b_ada, w_in, swa_sinks, w_out, ln1_g, ln1_b, w_up, conv_w, conv_b, w_down, ln2_g, ln2_b):
    depth = w_ada.shape[0]
    bp, lp, _ = x_prompt.shape
    bs, ls, _ = x_sample.shape
    alpha = (2.0 * depth) ** 0.25
    tl_proj = 512
    tl_p = 256
    tb_s = 256 // ls
    tb_attn = 8

    c_all = jnp.concatenate([c_prompt, c_sample], axis=0)
    mod_all = _ada_call(c_all, w_ada, b_ada)
    bias = _bias_call(rel_bias)

    cos_p, sin_p = _rope_tables(jnp.arange(lp, dtype=jnp.int32))
    cos_s, sin_s = _rope_tables(PAST_LEN + jnp.arange(ls, dtype=jnp.int32))
    cos_s, sin_s = jnp.tile(cos_s, (tb_s, 1)), jnp.tile(sin_s, (tb_s, 1))
    tab_p = _prompt_tables()
    tab_s = _sample_tables(ls)
    conv0 = jnp.zeros((1, bp, 2, UP_W), F32)

    w_in_b = w_in.astype(BF16)
    vec = lambda a: a.reshape(depth, 1, a.shape[-1])
    ffn_w = (w_out.astype(BF16), vec(ln1_g), vec(ln1_b), w_up.astype(BF16), conv_w, vec(conv_b),
             w_down.astype(BF16), vec(ln2_g), vec(ln2_b))
    mod_p = mod_all[:, :bp].reshape(depth, bp, 1, 6 * D_MODEL)
    mod_s = mod_all[:, bp:].reshape(depth, bs, 1, 6 * D_MODEL)
    s_rows = state_ret.reshape(depth, bs, RET_W, HEAD_DIM)
    ck2 = cache_swa_k.reshape(depth, bs, WINDOW, KV_W)
    cv2 = cache_swa_v.reshape(depth, bs, WINDOW, KV_W)

    xp, xs = x_prompt, x_sample
    p_ret, p_k, p_v, p_conv = [], [], [], []
    s_ret, s_k, s_v, s_conv = [], [], [], []
    for l in range(depth):
        qkvg, kv = _proj_call(xp, mod_p, w_in_b, l, cos_p, sin_p, 1, tl_proj)
        mix, r_p = _attn_prompt_call(qkvg, kv, bias, swa_sinks[l], tab_p, bp, lp)
        xp, tail = _ffn_call(mix, xp, mod_p, conv0, *ffn_w, l, alpha, 1, tl_p)
        kv3 = kv.reshape(bp, lp, 2, KV_HEADS, HEAD_DIM)
        p_ret.append(r_p)
        p_k.append(kv3[:, lp - WINDOW:, 0])
        p_v.append(kv3[:, lp - WINDOW:, 1])
        p_conv.append(tail.reshape(bp, 8, UP_W)[:, 6:8])

        qkvg, kv = _proj_call(xs, mod_s, w_in_b, l, cos_s, sin_s, tb_s, ls)
        mix, r_s, k_s, v_s = _attn_sample_call(qkvg, kv, s_rows, ck2, cv2, l, bias, swa_sinks[l], tab_s,
                                               bs, ls, tb_attn)
        xs, tail = _ffn_call(mix, xs, mod_s, state_conv, *ffn_w, l, alpha, tb_s, ls)
        s_ret.append(r_s)
        s_k.append(k_s)
        s_v.append(v_s)
        s_conv.append(tail.reshape(bs, ls, UP_W)[:, ls - 2:])

    return (xp, xs,
            jnp.stack(p_ret), jnp.stack(p_k), jnp.stack(p_v), jnp.stack(p_conv),
            jnp.stack(s_ret), jnp.stack(s_k), jnp.stack(s_v), jnp.stack(s_conv))
```

```python
import functools
import math

import jax
import jax.numpy as jnp
import numpy as np
from jax import lax
from jax.experimental import pallas as pl
from jax.experimental.pallas import tpu as pltpu

F32 = jnp.float32
BF16 = jnp.bfloat16

D_MODEL = 1024
HEAD_DIM = 64
N_HEADS = 8
RET_W = N_HEADS * HEAD_DIM
KV_HEADS = 2
KV_W = KV_HEADS * HEAD_DIM
D_FF = 2816
UP_W = 2 * D_FF
IN_COLS = 4 * RET_W + RET_W + 2 * KV_W
QKVG_W = 5 * RET_W
WINDOW = 128
CHUNK = 128
NUM_BUCKETS = 32
PAST_LEN = 8192
ROPE_BASE = 10000.0
LN_EPS = 1e-5
NEG_INF = -1e30
LANES = 128
FF_CHUNK = 256
FF_LOOKAHEAD = 2
FF_ROWS = 256
VMEM_LIMIT = 56 * 1024 * 1024


def _cparams(n_axes):
    return pltpu.CompilerParams(dimension_semantics=("arbitrary",) * n_axes,
                                vmem_limit_bytes=VMEM_LIMIT)


def _const_spec(shape):
    nd = len(shape)
    return pl.BlockSpec(shape, lambda *_: (0,) * nd, pipeline_mode=pl.Buffered(1))


def _rope_tables(pos):
    half = HEAD_DIM // 2
    inv = 1.0 / (ROPE_BASE ** (jnp.arange(half, dtype=F32) / half))
    ang = pos.astype(F32)[:, None] * inv[None, :]
    cos = jnp.cos(ang)
    sin = jnp.sin(ang)
    cos_h = jnp.concatenate([cos, cos], axis=-1)
    sin_h = jnp.concatenate([-sin, sin], axis=-1)
    return jnp.tile(cos_h, (1, N_HEADS)), jnp.tile(sin_h, (1, N_HEADS))


def _decay_tables(chunk):
    log_g = jnp.log(1.0 - 2.0 ** (-5.0 - jnp.arange(N_HEADS, dtype=F32)))
    idx = jnp.arange(chunk)
    diff = idx[:, None] - idx[None, :]
    decay_mat = jnp.where(diff[None] >= 0,
                          jnp.exp(log_g[:, None, None] * jnp.maximum(diff, 0)[None].astype(F32)), 0.0)
    q_dec = jnp.exp(log_g[None, :] * (idx[:, None] + 1).astype(F32))
    k_dec = jnp.exp(log_g[None, :] * (chunk - 1 - idx[:, None]).astype(F32))
    chunk_dec = jnp.exp(log_g * chunk)
    return decay_mat, q_dec, k_dec, chunk_dec


def _bucket_table(n_q, n_k):
    i = np.arange(n_q)[:, None]
    j = np.arange(n_k)[None, :]
    dist = i - j + WINDOW
    n = np.maximum(dist, 0)
    max_exact = NUM_BUCKETS // 2
    nf = np.maximum(n, max_exact).astype(np.float64)
    large = max_exact + (np.log(nf / max_exact) / math.log(WINDOW / max_exact)
                         * (NUM_BUCKETS - max_exact)).astype(np.int32)
    large = np.minimum(large, NUM_BUCKETS - 1)
    bucket = np.where(n < max_exact, n, large).astype(np.int32)
    valid = ((dist >= 0) & (dist < WINDOW)).astype(np.int32)
    return bucket, valid


def _ada_kernel(c_ref, w_ref, b_ref, o_ref):
    c = c_ref[...]
    s = (c * jax.nn.sigmoid(c)).astype(BF16)
    o_ref[0] = jnp.dot(s, w_ref[0].astype(BF16), preferred_element_type=F32) + b_ref[0]


def _ada_call(c_all, w_ada, b_ada):
    depth = w_ada.shape[0]
    n_rows = c_all.shape[0]
    tn = 1536
    return pl.pallas_call(
        _ada_kernel,
        grid=(depth, 6 * D_MODEL // tn),
        in_specs=[pl.BlockSpec((n_rows, D_MODEL), lambda l, n: (0, 0)),
                  pl.BlockSpec((1, D_MODEL, tn), lambda l, n: (l, 0, n)),
                  pl.BlockSpec((1, 1, tn), lambda l, n: (l, 0, n))],
        out_specs=pl.BlockSpec((1, n_rows, tn), lambda l, n: (l, 0, n)),
        out_shape=jax.ShapeDtypeStruct((depth, n_rows, 6 * D_MODEL), F32),
        compiler_params=_cparams(2),
        name="ada",
    )(c_all, w_ada, b_ada.reshape(depth, 1, 6 * D_MODEL))


def _bias_kernel(rb_ref, bucket_ref, valid_ref, o_ref):
    bucket = bucket_ref[...]
    valid = valid_ref[...] > 0
    first_ok = lax.broadcasted_iota(jnp.int32, bucket.shape, 1) >= WINDOW
    for h in range(N_HEADS):
        acc = jnp.zeros(bucket.shape, F32)
        for b in range(NUM_BUCKETS):
            acc = jnp.where(bucket == b, rb_ref[b, h], acc)
        o_ref[0, h] = jnp.where(valid, acc, NEG_INF)
        o_ref[1, h] = jnp.where(valid & first_ok, acc, NEG_INF)


def _bias_call(rel_bias):
    bucket, valid = _bucket_table(CHUNK, 2 * CHUNK)
    return pl.pallas_call(
        _bias_kernel,
        in_specs=[pl.BlockSpec(memory_space=pltpu.SMEM),
                  pl.BlockSpec((CHUNK, 2 * CHUNK), lambda: (0, 0)),
                  pl.BlockSpec((CHUNK, 2 * CHUNK), lambda: (0, 0))],
        out_specs=pl.BlockSpec((2, N_HEADS, CHUNK, 2 * CHUNK), lambda: (0, 0, 0, 0)),
        out_shape=jax.ShapeDtypeStruct((2, N_HEADS, CHUNK, 2 * CHUNK), F32),
        name="swa_bias",
    )(rel_bias, jnp.asarray(bucket), jnp.asarray(valid))


def _swap_halves(x):
    lane = lax.broadcasted_iota(jnp.int32, (1, LANES), 1)
    first = (lane % HEAD_DIM) < (HEAD_DIM // 2)
    cols = []
    for c in range(x.shape[1] // LANES):
        xc = x[:, c * LANES:(c + 1) * LANES]
        cols.append(jnp.where(first, pltpu.roll(xc, LANES - HEAD_DIM // 2, 1), pltpu.roll(xc, HEAD_DIM // 2, 1)))
    return jnp.concatenate(cols, axis=1)


def _proj_kernel(x_ref, sc_ref, sh_ref, w_ref, cos_ref, sin_ref, qkvg_ref, kv_ref):
    tb, tl, _ = x_ref.shape
    tm = tb * tl
    h = x_ref[...] * (1.0 + sc_ref[...]) + sh_ref[...]
    h = h.reshape(tm, D_MODEL).astype(BF16)
    proj = jnp.dot(h, w_ref[...], preferred_element_type=F32)
    cos = cos_ref[...]
    sin = sin_ref[...]
    rq = proj[:, 0:RET_W]
    rk = proj[:, RET_W:2 * RET_W]
    rq = rq * cos + _swap_halves(rq) * sin
    rk = (rk * cos + _swap_halves(rk) * sin) * (HEAD_DIM ** -0.5)
    rg = proj[:, 3 * RET_W:4 * RET_W]
    qkvg_ref[:, 0:RET_W] = rq.astype(BF16)
    qkvg_ref[:, RET_W:2 * RET_W] = rk.astype(BF16)
    qkvg_ref[:, 2 * RET_W:3 * RET_W] = proj[:, 2 * RET_W:3 * RET_W].astype(BF16)
    qkvg_ref[:, 3 * RET_W:4 * RET_W] = (rg * jax.nn.sigmoid(rg)).astype(BF16)
    qkvg_ref[:, 4 * RET_W:5 * RET_W] = (proj[:, 4 * RET_W:5 * RET_W] * (HEAD_DIM ** -0.5)).astype(BF16)
    kv_ref[...] = proj[:, 5 * RET_W:IN_COLS]


def _layer_spec(shape, layer):
    nd = len(shape)
    return pl.BlockSpec((None,) + tuple(shape), lambda *_: (layer,) + (0,) * nd, pipeline_mode=pl.Buffered(1))


def _mod_spec(tb, layer, k):
    return pl.BlockSpec((None, tb, 1, D_MODEL), lambda i, j: (layer, i, 0, k))


def _proj_call(x, mod, w_in, layer, cos, sin, tb, tl):
    b, l, _ = x.shape
    nj = l // tl
    tm = tb * tl
    tab_idx = (lambda i, j: (j, 0)) if cos.shape[0] == l and nj > 1 else (lambda i, j: (0, 0))
    return pl.pallas_call(
        _proj_kernel,
        grid=(b // tb, nj),
        in_specs=[pl.BlockSpec((tb, tl, D_MODEL), lambda i, j: (i, j, 0)),
                  _mod_spec(tb, layer, 1),
                  _mod_spec(tb, layer, 0),
                  _layer_spec((D_MODEL, IN_COLS), layer),
                  pl.BlockSpec((tm, RET_W), tab_idx),
                  pl.BlockSpec((tm, RET_W), tab_idx)],
        out_specs=[pl.BlockSpec((tm, QKVG_W), lambda i, j: (i * nj + j, 0)),
                   pl.BlockSpec((tm, 2 * KV_W), lambda i, j: (i * nj + j, 0))],
        out_shape=[jax.ShapeDtypeStruct((b * l, QKVG_W), BF16),
                   jax.ShapeDtypeStruct((b * l, 2 * KV_W), F32)],
        compiler_params=_cparams(2),
        name="proj",
    )(x, mod, mod, w_in, cos, sin)


def _lane_mask_low():
    return lax.broadcasted_iota(jnp.int32, (1, LANES), 1) < HEAD_DIM


def _head_norm(o, ones_bd):
    outs = []
    gw = ones_bd.shape[0]
    for g in range(RET_W // gw):
        og = o[:, g * gw:(g + 1) * gw]
        mu = jnp.dot(og.astype(BF16), ones_bd, preferred_element_type=F32)
        d = og - mu
        var = jnp.dot((d * d).astype(BF16), ones_bd, preferred_element_type=F32)
        outs.append(d * lax.rsqrt(var + LN_EPS))
    return jnp.concatenate(outs, axis=1)


def _swa_operands(k_all, v_all):
    low = _lane_mask_low()
    zero = jnp.zeros_like(k_all)
    k_rot = pltpu.roll(k_all, HEAD_DIM, 1)
    v_rot = pltpu.roll(v_all, HEAD_DIM, 1)
    ops = []
    for g in range(KV_HEADS):
        k_lo, k_hi = (k_all, k_rot) if g == 0 else (k_rot, k_all)
        v_lo, v_hi = (v_all, v_rot) if g == 0 else (v_rot, v_all)
        kc = jnp.concatenate([jnp.where(low, k_lo, zero), jnp.where(low, zero, k_hi)], axis=0)
        vc = jnp.concatenate([jnp.where(low, v_lo, zero), jnp.where(low, zero, v_hi)], axis=0)
        ops.append((kc, vc))
    return ops


def _swa_softmax(lg, p, bias_ref, sink_ref):
    rows = lg.shape[0]
    low = _lane_mask_low()
    probs, dens = [], []
    for s in range(2):
        hh = 2 * p + s
        lo = lg[:, s * 2 * CHUNK:(s + 1) * 2 * CHUNK] + bias_ref[hh, 0:rows, :]
        sink = sink_ref[hh]
        m = jnp.maximum(jnp.max(lo, axis=-1, keepdims=True), sink)
        pe = jnp.exp(lo - m)
        dens.append(jnp.sum(pe, axis=-1, keepdims=True) + jnp.exp(sink - m))
        probs.append(pe)
    return jnp.concatenate(probs, axis=1), jnp.where(low, dens[0], dens[1])


def _attn_prompt_kernel(sink_ref, qkvg_ref, kv_ref, bias_ref, bias_all_ref, dm_ref, qd_ref, kd_ref, cd_ref, ones_ref,
                        mix_ref, s_out_ref, s_scr, pk_scr, pv_scr):
    i = pl.program_id(1)

    @pl.when(i == 0)
    def _():
        s_scr[...] = jnp.zeros_like(s_scr)
        pk_scr[...] = jnp.zeros_like(pk_scr)
        pv_scr[...] = jnp.zeros_like(pv_scr)

    low = _lane_mask_low()
    zero = jnp.zeros((CHUNK, LANES), BF16)
    n_pairs = N_HEADS // 2
    pairs_per_kv = n_pairs // KV_HEADS
    gw = cd_ref.shape[1]
    n_groups = RET_W // gw
    n_blk = qkvg_ref.shape[0] // CHUNK
    nt = (((1,), (1,)), ((), ()))
    tn = (((0,), (0,)), ((), ()))
    ones_bd = ones_ref[...]

    s_cur = [s_scr[g] for g in range(n_groups)]
    k_prev, v_prev = pk_scr[...], pv_scr[...]
    wave1, mid = [], []
    for c in range(n_blk):
        rows = slice(c * CHUNK, (c + 1) * CHUNK)
        q = qkvg_ref[rows, 0:RET_W]
        k = qkvg_ref[rows, RET_W:2 * RET_W]
        v = qkvg_ref[rows, 2 * RET_W:3 * RET_W]
        sq = qkvg_ref[rows, 4 * RET_W:5 * RET_W]
        kd = (k.astype(F32) * kd_ref[...]).astype(BF16)
        ret_ops = []
        for p in range(n_pairs):
            kp = k[:, p * LANES:(p + 1) * LANES]
            vp = v[:, p * LANES:(p + 1) * LANES]
            ret_ops.append((jnp.concatenate([jnp.where(low, kp, zero), jnp.where(low, zero, kp)], axis=0),
                            jnp.concatenate([jnp.where(low, vp, zero), jnp.where(low, zero, vp)], axis=0)))
        k_new = kv_ref[rows, 0:KV_W].astype(BF16)
        v_new = kv_ref[rows, KV_W:2 * KV_W].astype(BF16)
        swa_ops = _swa_operands(jnp.concatenate([k_prev, k_new], axis=0), jnp.concatenate([v_prev, v_new], axis=0))
        k_prev, v_prev = k_new, v_new

        scores = [lax.dot_general(q[:, p * LANES:(p + 1) * LANES], ret_ops[p][0], nt, preferred_element_type=F32)
                  for p in range(n_pairs)]
        cross = [jnp.dot(q[:, g * gw:(g + 1) * gw], s_cur[g].astype(BF16), preferred_element_type=F32)
                 for g in range(n_groups)]
        upd = [lax.dot_general(kd[:, g * gw:(g + 1) * gw], v[:, g * gw:(g + 1) * gw], tn,
                               preferred_element_type=F32) for g in range(n_groups)]
        logits = [lax.dot_general(sq[:, p * LANES:(p + 1) * LANES], swa_ops[p // pairs_per_kv][0], nt,
                                  preferred_element_type=F32) for p in range(n_pairs)]
        s_cur = [s_cur[g] * cd_ref[g] + jnp.where(cd_ref[g] > 0.0, upd[g], 0.0) for g in range(n_groups)]
        wave1.append((scores, cross, logits, ret_ops, swa_ops))
    for g in range(n_groups):
        s_scr[g] = s_cur[g]
    pk_scr[...] = k_prev
    pv_scr[...] = v_prev

    for c in range(n_blk):
        scores, cross, logits, ret_ops, swa_ops = wave1[c]
        bias_c = bias_ref.at[0] if c == 0 else bias_all_ref.at[0]
        scores = [(scores[p] * dm_ref[p]).astype(BF16) for p in range(n_pairs)]
        soft = [_swa_softmax(logits[p], p, bias_c, sink_ref) for p in range(n_pairs)]
        intra = [jnp.dot(scores[p], ret_ops[p][1], preferred_element_type=F32) for p in range(n_pairs)]
        swa = [jnp.dot(soft[p][0].astype(BF16), swa_ops[p // pairs_per_kv][1], preferred_element_type=F32)
               / soft[p][1] for p in range(n_pairs)]
        o = jnp.concatenate(intra, axis=1) + jnp.concatenate(cross, axis=1) * qd_ref[...]
        mid.append((o, swa))

    o = jnp.concatenate([m[0] for m in mid], axis=0)
    ret = _head_norm(o, ones_bd) * qkvg_ref[:, 3 * RET_W:4 * RET_W].astype(F32)
    mix_ref[:, 0:RET_W] = ret.astype(BF16)
    for c in range(n_blk):
        mix_ref[c * CHUNK:(c + 1) * CHUNK, RET_W:2 * RET_W] = jnp.concatenate(mid[c][1], axis=1).astype(BF16)

    @pl.when(i == pl.num_programs(1) - 1)
    def _():
        for h in range(N_HEADS):
            g, hl = divmod(h, gw // HEAD_DIM)
            s_out_ref[0, h] = s_scr[g, hl * HEAD_DIM:(hl + 1) * HEAD_DIM, hl * HEAD_DIM:(hl + 1) * HEAD_DIM]


def _prompt_tables():
    decay_mat, q_dec, k_dec, chunk_dec = _decay_tables(CHUNK)
    dm = jnp.concatenate([decay_mat[0::2], decay_mat[1::2]], axis=2)
    qd = jnp.repeat(q_dec, HEAD_DIM, axis=1)
    kd = jnp.repeat(k_dec, HEAD_DIM, axis=1)
    gw = 4 * HEAD_DIM
    blk = np.kron(np.eye(4, dtype=np.float32), np.ones((HEAD_DIM, HEAD_DIM), np.float32))
    cd = jnp.repeat(chunk_dec.reshape(2, 4), HEAD_DIM, axis=1)[:, :, None] * jnp.asarray(blk)[None]
    ones_bd = jnp.asarray(blk / HEAD_DIM, BF16)
    assert cd.shape == (2, gw, gw)
    return dm, qd, kd, cd, ones_bd


def _attn_prompt_call(qkvg, kv, bias, sinks, tables, b, l, n_blk):
    dm, qd, kd, cd, ones_bd = tables
    rows = n_blk * CHUNK
    nc = l // rows
    gw = cd.shape[1]
    bias_spec = lambda idx: pl.BlockSpec((1, N_HEADS, CHUNK, 2 * CHUNK), idx)
    return pl.pallas_call(
        _attn_prompt_kernel,
        grid=(b, nc),
        in_specs=[pl.BlockSpec(memory_space=pltpu.SMEM),
                  pl.BlockSpec((rows, QKVG_W), lambda bi, i: (bi * nc + i, 0)),
                  pl.BlockSpec((rows, 2 * KV_W), lambda bi, i: (bi * nc + i, 0)),
                  bias_spec(lambda bi, i: (jnp.where(i == 0, 1, 0), 0, 0, 0)),
                  bias_spec(lambda bi, i: (0, 0, 0, 0)),
                  _const_spec(dm.shape), _const_spec(qd.shape), _const_spec(kd.shape),
                  _const_spec(cd.shape), _const_spec(ones_bd.shape)],
        out_specs=[pl.BlockSpec((rows, 2 * RET_W), lambda bi, i: (bi * nc + i, 0)),
                   pl.BlockSpec((1, N_HEADS, HEAD_DIM, HEAD_DIM), lambda bi, i: (bi, 0, 0, 0))],
        out_shape=[jax.ShapeDtypeStruct((b * l, 2 * RET_W), BF16),
                   jax.ShapeDtypeStruct((b, N_HEADS, HEAD_DIM, HEAD_DIM), F32)],
        scratch_shapes=[pltpu.VMEM((RET_W // gw, gw, gw), F32),
                        pltpu.VMEM((CHUNK, KV_W), BF16),
                        pltpu.VMEM((CHUNK, KV_W), BF16)],
        compiler_params=_cparams(2),
        name="attn_prompt",
    )(sinks, qkvg, kv, bias, bias, dm, qd, kd, cd, ones_bd)


def _attn_sample_kernel(sink_ref, qkvg_ref, kv_ref, s_in_ref, ck_ref, cv_ref, bias_ref, ds_ref, qd_ref, kd_ref,
                        cdr_ref, hm_ref, ones_ref, mix_ref, s_out_ref, ck_out_ref, cv_out_ref):
    tb = s_in_ref.shape[0]
    tl = qkvg_ref.shape[0] // tb
    hm = hm_ref[...]
    nt = (((1,), (1,)), ((), ()))
    tn = (((0,), (0,)), ((), ()))
    pairs_per_kv = N_HEADS // KV_HEADS // 2
    pad = jnp.zeros((CHUNK - tl, KV_W), F32)

    def stack_heads(x):
        return (jnp.concatenate([x] * N_HEADS, axis=0) * hm).astype(BF16)

    wave1 = []
    for e in range(tb):
        rows = slice(e * tl, (e + 1) * tl)
        q = qkvg_ref[rows, 0:RET_W]
        k32 = qkvg_ref[rows, RET_W:2 * RET_W].astype(F32)
        v32 = qkvg_ref[rows, 2 * RET_W:3 * RET_W].astype(F32)
        sq32 = qkvg_ref[rows, 4 * RET_W:5 * RET_W].astype(F32)
        v_big = stack_heads(v32)
        s = lax.dot_general(q, stack_heads(k32), nt, preferred_element_type=F32)
        s_prev = s_in_ref[e].reshape(RET_W, HEAD_DIM)
        oc = jnp.dot(stack_heads(q.astype(F32)), s_prev.astype(BF16), preferred_element_type=F32)
        v_stack = jnp.concatenate([v32[:, h * HEAD_DIM:(h + 1) * HEAD_DIM] for h in range(N_HEADS)], axis=0)
        upd = lax.dot_general(stack_heads(k32 * kd_ref[...]), v_stack.astype(BF16), tn,
                              preferred_element_type=F32)
        s_out_ref[e] = (s_prev * cdr_ref[...] + upd).reshape(N_HEADS, HEAD_DIM, HEAD_DIM)

        k_new = kv_ref[rows, 0:KV_W]
        v_new = kv_ref[rows, KV_W:2 * KV_W]
        k_all = jnp.concatenate([ck_ref[e].astype(BF16), jnp.concatenate([k_new, pad], axis=0).astype(BF16)], axis=0)
        v_all = jnp.concatenate([cv_ref[e].astype(BF16), jnp.concatenate([v_new, pad], axis=0).astype(BF16)], axis=0)
        swa_ops = _swa_operands(k_all, v_all)
        logits = []
        for g in range(KV_HEADS):
            lhs = jnp.concatenate([sq32[:, (g * pairs_per_kv + pp) * LANES:(g * pairs_per_kv + pp + 1) * LANES]
                                   for pp in range(pairs_per_kv)], axis=0).astype(BF16)
            logits.append(lax.dot_general(lhs, swa_ops[g][0], nt, preferred_element_type=F32))
        ck_out_ref[e, 0:WINDOW - tl, :] = ck_ref[e, tl:WINDOW, :]
        cv_out_ref[e, 0:WINDOW - tl, :] = cv_ref[e, tl:WINDOW, :]
        ck_out_ref[e, WINDOW - tl:WINDOW, :] = k_new
        cv_out_ref[e, WINDOW - tl:WINDOW, :] = v_new
        wave1.append((s, oc, v_big, swa_ops, logits))

    mid = []
    for e in range(tb):
        s, oc, v_big, swa_ops, logits = wave1[e]
        s = (s * ds_ref[...]).astype(BF16)
        soft = []
        for g in range(KV_HEADS):
            per_pair = [_swa_softmax(logits[g][pp * tl:(pp + 1) * tl, :], g * pairs_per_kv + pp, bias_ref.at[0],
                                     sink_ref) for pp in range(pairs_per_kv)]
            soft.append((jnp.concatenate([x[0] for x in per_pair], axis=0).astype(BF16),
                         jnp.concatenate([x[1] for x in per_pair], axis=0)))
        mid.append((s, soft))
    o_rows, swa_rows = [], []
    for e in range(tb):
        s, soft = mid[e]
        _, oc, v_big, swa_ops, _ = wave1[e]
        o = jnp.dot(s, v_big, preferred_element_type=F32)
        oc_nat = jnp.concatenate([oc[h * tl:(h + 1) * tl, :] for h in range(N_HEADS)], axis=1)
        o_rows.append(o + oc_nat * qd_ref[...])
        outs = []
        for g in range(KV_HEADS):
            oo = jnp.dot(soft[g][0], swa_ops[g][1], preferred_element_type=F32) / soft[g][1]
            outs += [oo[pp * tl:(pp + 1) * tl, :] for pp in range(pairs_per_kv)]
        swa_rows.append(jnp.concatenate(outs, axis=1))

    gate = qkvg_ref[:, 3 * RET_W:4 * RET_W].astype(F32)
    ret = _head_norm(jnp.concatenate(o_rows, axis=0), ones_ref[...]) * gate
    mix_ref[:, 0:RET_W] = ret.astype(BF16)
    mix_ref[:, RET_W:2 * RET_W] = jnp.concatenate(swa_rows, axis=0).astype(BF16)


def _sample_tables(tl):
    decay_mat, q_dec, k_dec, chunk_dec = _decay_tables(tl)
    ds = jnp.transpose(decay_mat, (1, 0, 2)).reshape(tl, N_HEADS * tl)
    qd = jnp.repeat(q_dec, HEAD_DIM, axis=1)
    kd = jnp.repeat(k_dec, HEAD_DIM, axis=1)
    cdr = jnp.broadcast_to(jnp.repeat(chunk_dec, HEAD_DIM)[:, None], (RET_W, HEAD_DIM))
    hm = np.kron(np.eye(N_HEADS, dtype=np.float32), np.ones((tl, HEAD_DIM), np.float32))
    blk = np.kron(np.eye(4, dtype=np.float32), np.ones((HEAD_DIM, HEAD_DIM), np.float32))
    return ds, qd, kd, cdr, jnp.asarray(hm), jnp.asarray(blk / HEAD_DIM, BF16)


def _attn_sample_call(qkvg, kv, s_rows, ck2, cv2, layer, bias, sinks, tables, b, tl, tb):
    ds, qd, kd, cdr, hm, ones_bd = tables
    batch3 = lambda i: (i, 0, 0)
    layer4 = lambda i: (layer, i, 0, 0)
    mix, s_out, ck_out, cv_out = pl.pallas_call(
        _attn_sample_kernel,
        grid=(b // tb,),
        in_specs=[pl.BlockSpec(memory_space=pltpu.SMEM),
                  pl.BlockSpec((tb * tl, QKVG_W), lambda i: (i, 0)),
                  pl.BlockSpec((tb * tl, 2 * KV_W), lambda i: (i, 0)),
                  pl.BlockSpec((None, tb, N_HEADS, HEAD_DIM, HEAD_DIM), lambda i: (layer, i, 0, 0, 0)),
                  pl.BlockSpec((None, tb, WINDOW, KV_W), layer4),
                  pl.BlockSpec((None, tb, WINDOW, KV_W), layer4),
                  pl.BlockSpec((1, N_HEADS, CHUNK, 2 * CHUNK), lambda i: (0, 0, 0, 0)),
                  _const_spec(ds.shape), _const_spec(qd.shape), _const_spec(kd.shape),
                  _const_spec(cdr.shape), _const_spec(hm.shape), _const_spec(ones_bd.shape)],
        out_specs=[pl.BlockSpec((tb * tl, 2 * RET_W), lambda i: (i, 0)),
                   pl.BlockSpec((tb, N_HEADS, HEAD_DIM, HEAD_DIM), lambda i: (i, 0, 0, 0)),
                   pl.BlockSpec((tb, WINDOW, KV_W), batch3),
                   pl.BlockSpec((tb, WINDOW, KV_W), batch3)],
        out_shape=[jax.ShapeDtypeStruct((b * tl, 2 * RET_W), BF16),
                   jax.ShapeDtypeStruct((b, N_HEADS, HEAD_DIM, HEAD_DIM), F32),
                   jax.ShapeDtypeStruct((b, WINDOW, KV_W), F32),
                   jax.ShapeDtypeStruct((b, WINDOW, KV_W), F32)],
        compiler_params=_cparams(1),
        name="attn_sample",
    )(sinks, qkvg, kv, s_rows, ck2, cv2, bias, ds, qd, kd, cdr, hm, ones_bd)
    return (mix, s_out,
            ck_out.reshape(b, WINDOW, KV_HEADS, HEAD_DIM), cv_out.reshape(b, WINDOW, KV_HEADS, HEAD_DIM))


def _layer_norm(x, g, b):
    mu = jnp.mean(x, axis=-1, keepdims=True)
    xc = x - mu
    var = jnp.mean(xc * xc, axis=-1, keepdims=True)
    return xc * lax.rsqrt(var + LN_EPS) * g + b


def _ffn_kernel(alpha, carry_rows, mix_ref, x_ref, g1_ref, sh2_ref, sc2_ref, g2_ref, wout_ref, ln1g_ref, ln1b_ref,
                wup_ref, cw_ref, cb_ref, prev_ref, wdn_ref, ln2g_ref, ln2b_ref, xo_ref, tail_ref, carry_scr):
    tb, tl, _ = x_ref.shape
    tm = tb * tl
    sm = FF_ROWS
    n_sub = tm // sm
    sb, sl = (1, sm) if carry_rows else (tb // n_sub, tl)
    j = pl.program_id(1)

    def seq(ref, s):
        if carry_rows:
            return ref[:, s * sm:(s + 1) * sm, :] if ref.shape[1] == tl else ref[...]
        return ref[s * sb:(s + 1) * sb]

    if carry_rows:
        @pl.when(j == 0)
        def _():
            carry_scr[carry_rows - 2:carry_rows, :] = prev_ref[0]

    fix_rows = 8 if carry_rows else sm
    t_idx = lax.broadcasted_iota(jnp.int32, (fix_rows, 1), 0) % sl
    is_t0 = t_idx == 0
    is_t1 = t_idx == 1
    n_chunks = D_FF // FF_CHUNK

    def up_cols(c, half):
        return slice(half * D_FF + c * FF_CHUNK, half * D_FF + (c + 1) * FF_CHUNK)

    def pre(s):
        x1 = alpha * seq(x_ref, s) + seq(g1_ref, s) * y[s].reshape(sb, sl, D_MODEL)
        x1 = _layer_norm(x1, ln1g_ref[...], ln1b_ref[...])
        h2 = (x1 * (1.0 + seq(sc2_ref, s)) + seq(sh2_ref, s)).reshape(sm, D_MODEL).astype(BF16)
        return x1, h2

    def up_dots(h2, c):
        return [jnp.dot(h2, wup_ref[:, up_cols(c, half)], preferred_element_type=F32) for half in range(2)]

    def conv(s, up, cols):
        if carry_rows:
            src = carry_scr if s == 0 else tail_ref
            p0 = src[carry_rows - 2:carry_rows - 1, cols]
            p1 = src[carry_rows - 1:carry_rows, cols]
            tail_ref[:, cols] = up[sm - carry_rows:sm, :]
        else:
            p0 = jnp.broadcast_to(seq(prev_ref, s)[:, 0:1, cols], (sb, sl, FF_CHUNK)).reshape(sm, FF_CHUNK)
            p1 = jnp.broadcast_to(seq(prev_ref, s)[:, 1:2, cols], (sb, sl, FF_CHUNK)).reshape(sm, FF_CHUNK)
            tail_ref[s * sm:(s + 1) * sm, cols] = up
        r1 = pltpu.roll(up, 1, 0)
        r2 = pltpu.roll(up, 2, 0)
        s1 = jnp.where(is_t0, p1, r1[0:fix_rows])
        s2 = jnp.where(is_t0, p0, jnp.where(is_t1, p1, r2[0:fix_rows]))
        if fix_rows < sm:
            s1 = jnp.concatenate([s1, r1[fix_rows:]], axis=0)
            s2 = jnp.concatenate([s2, r2[fix_rows:]], axis=0)
        return cb_ref[:, cols] + s2 * cw_ref[0:1, cols] + s1 * cw_ref[1:2, cols] + up * cw_ref[2:3, cols]

    y = [jnp.dot(mix_ref[s * sm:(s + 1) * sm, :], wout_ref[...], preferred_element_type=F32) for s in range(n_sub)]
    items = [(s, c) for s in range(n_sub) for c in range(n_chunks)]
    staged, ups = {}, {}

    def issue_up(k):
        s, c = items[k]
        if c == 0:
            staged[s] = pre(s)
        ups[k] = up_dots(staged[s][1], c)

    for k in range(min(FF_LOOKAHEAD, len(items))):
        issue_up(k)
    acc = None
    for k, (s, c) in enumerate(items):
        if k + FF_LOOKAHEAD < len(items):
            issue_up(k + FF_LOOKAHEAD)
        up_a, up_b = ups.pop(k)
        ua = conv(s, up_a, up_cols(c, 0))
        ub = conv(s, up_b, up_cols(c, 1))
        gated = (ua * jax.nn.sigmoid(ua) * ub).astype(BF16)
        down = jnp.dot(gated, wdn_ref[c * FF_CHUNK:(c + 1) * FF_CHUNK, :], preferred_element_type=F32)
        acc = down if c == 0 else acc + down
        if c == n_chunks - 1:
            x1, _ = staged.pop(s)
            x2 = alpha * x1 + seq(g2_ref, s) * acc.reshape(sb, sl, D_MODEL)
            out = _layer_norm(x2, ln2g_ref[...], ln2b_ref[...])
            if carry_rows:
                xo_ref[:, s * sm:(s + 1) * sm, :] = out
            else:
                xo_ref[s * sb:(s + 1) * sb] = out

    if carry_rows:
        carry_scr[...] = tail_ref[...]


def _ffn_call(mix, x, mod, conv_prev, w_out, ln1_g, ln1_b, w_up, conv_w, conv_b, w_down, ln2_g, ln2_b,
              layer, alpha, tb, tl):
    prev_layer = layer if conv_prev.shape[0] > 1 else 0
    b, l, _ = x.shape
    nj = l // tl
    tm = tb * tl
    carry_rows = 8 if tb == 1 else 0
    assert carry_rows or (nj == 1 and tl == 8)
    tail_rows = 8 if carry_rows else tm
    kern = functools.partial(_ffn_kernel, alpha, carry_rows)
    return pl.pallas_call(
        kern,
        grid=(b // tb, nj),
        in_specs=[pl.BlockSpec((tm, 2 * RET_W), lambda i, j: (i * nj + j, 0)),
                  pl.BlockSpec((tb, tl, D_MODEL), lambda i, j: (i, j, 0)),
                  _mod_spec(tb, layer, 2), _mod_spec(tb, layer, 3), _mod_spec(tb, layer, 4), _mod_spec(tb, layer, 5),
                  _layer_spec((2 * RET_W, D_MODEL), layer),
                  _layer_spec((1, D_MODEL), layer), _layer_spec((1, D_MODEL), layer),
                  _layer_spec((D_MODEL, UP_W), layer),
                  _layer_spec((3, UP_W), layer), _layer_spec((1, UP_W), layer),
                  pl.BlockSpec((None, tb, 2, UP_W), lambda i, j: (prev_layer, i, 0, 0)),
                  _layer_spec((D_FF, D_MODEL), layer),
                  _layer_spec((1, D_MODEL), layer), _layer_spec((1, D_MODEL), layer)],
        out_specs=[pl.BlockSpec((tb, tl, D_MODEL), lambda i, j: (i, j, 0)),
                   pl.BlockSpec((tail_rows, UP_W), lambda i, j: (i, 0))],
        out_shape=[jax.ShapeDtypeStruct((b, l, D_MODEL), F32),
                   jax.ShapeDtypeStruct((b * 8, UP_W), F32)],
        scratch_shapes=[pltpu.VMEM((8, UP_W), F32)],
        compiler_params=_cparams(2),
        name="ffn",
    )(mix, x, mod, mod, mod, mod, w_out, ln1_g, ln1_b, w_up, conv_w, conv_b, conv_prev, w_down, ln2_g, ln2_b)


def kernel(x_prompt, x_sample, c_prompt, c_sample, state_ret, cache_swa_k, cache_swa_v, state_conv, rel_bias, w_ada, b_ada, w_in, swa_sinks, w_out, ln1_g, ln1_b, w_up, conv_w, conv_b, w_down, ln2_g, ln2_b):
    depth = w_ada.shape[0]
    bp, lp, _ = x_prompt.shape
    bs, ls, _ = x_sample.shape
    alpha = (2.0 * depth) ** 0.25
    tl_proj = 512
    tl_p = 2 * FF_ROWS
    tb_s = FF_ROWS // ls
    tb_attn = 8
    blk_attn = 2

    c_all = jnp.concatenate([c_prompt, c_sample], axis=0)
    mod_all = _ada_call(c_all, w_ada, b_ada)
    bias = _bias_call(rel_bias)

    cos_p, sin_p = _rope_tables(jnp.arange(lp, dtype=jnp.int32))
    cos_s, sin_s = _rope_tables(PAST_LEN + jnp.arange(ls, dtype=jnp.int32))
    cos_s, sin_s = jnp.tile(cos_s, (tb_s, 1)), jnp.tile(sin_s, (tb_s, 1))
    tab_p = _prompt_tables()
    tab_s = _sample_tables(ls)
    conv0 = jnp.zeros((1, bp, 2, UP_W), F32)

    w_in_b = w_in.astype(BF16)
    vec = lambda a: a.reshape(depth, 1, a.shape[-1])
    ffn_w = (w_out.astype(BF16), vec(ln1_g), vec(ln1_b), w_up.astype(BF16), conv_w, vec(conv_b),
             w_down.astype(BF16), vec(ln2_g), vec(ln2_b))
    mod_p = mod_all[:, :bp].reshape(depth, bp, 1, 6 * D_MODEL)
    mod_s = mod_all[:, bp:].reshape(depth, bs, 1, 6 * D_MODEL)
    s_rows = state_ret
    ck2 = cache_swa_k.reshape(depth, bs, WINDOW, KV_W)
    cv2 = cache_swa_v.reshape(depth, bs, WINDOW, KV_W)

    xp, xs = x_prompt, x_sample
    p_ret, p_k, p_v, p_conv = [], [], [], []
    s_ret, s_k, s_v, s_conv = [], [], [], []
    for l in range(depth):
        qkvg, kv = _proj_call(xp, mod_p, w_in_b, l, cos_p, sin_p, 1, tl_proj)
        mix, r_p = _attn_prompt_call(qkvg, kv, bias, swa_sinks[l], tab_p, bp, lp, blk_attn)
        xp, tail = _ffn_call(mix, xp, mod_p, conv0, *ffn_w, l, alpha, 1, tl_p)
        kv3 = kv.reshape(bp, lp, 2, KV_HEADS, HEAD_DIM)
        p_ret.append(r_p)
        p_k.append(kv3[:, lp - WINDOW:, 0])
        p_v.append(kv3[:, lp - WINDOW:, 1])
        p_conv.append(tail.reshape(bp, 8, UP_W)[:, 6:8])

        qkvg, kv = _proj_call(xs, mod_s, w_in_b, l, cos_s, sin_s, tb_s, ls)
        mix, r_s, k_s, v_s = _attn_sample_call(qkvg, kv, s_rows, ck2, cv2, l, bias, swa_sinks[l], tab_s,
                                               bs, ls, tb_attn)
        xs, tail = _ffn_call(mix, xs, mod_s, state_conv, *ffn_w, l, alpha, tb_s, ls)
        s_ret.append(r_s)
        s_k.append(k_s)
        s_v.append(v_s)
        s_conv.append(tail.reshape(bs, ls, UP_W)[:, ls - 2:])

    return (xp, xs,
            jnp.stack(p_ret), jnp.stack(p_k), jnp.stack(p_v), jnp.stack(p_conv),
            jnp.stack(s_ret), jnp.stack(s_k), jnp.stack(s_v), jnp.stack(s_conv))
```

```python
import functools
import math

import jax
import jax.numpy as jnp
import numpy as np
from jax import lax
from jax.experimental import pallas as pl
from jax.experimental.pallas import tpu as pltpu

F32 = jnp.float32
BF16 = jnp.bfloat16

D_MODEL = 1024
HEAD_DIM = 64
N_HEADS = 8
RET_W = N_HEADS * HEAD_DIM
KV_HEADS = 2
KV_W = KV_HEADS * HEAD_DIM
D_FF = 2816
UP_W = 2 * D_FF
IN_COLS = 4 * RET_W + RET_W + 2 * KV_W
QKVG_W = 5 * RET_W
WINDOW = 128
CHUNK = 128
NUM_BUCKETS = 32
PAST_LEN = 8192
ROPE_BASE = 10000.0
LN_EPS = 1e-5
NEG_INF = -1e30
LANES = 128
FF_CHUNK = 256
FF_LOOKAHEAD = 2
FF_ROWS = 256
VMEM_LIMIT = 56 * 1024 * 1024


def _cparams(n_axes):
    return pltpu.CompilerParams(dimension_semantics=("arbitrary",) * n_axes,
                                vmem_limit_bytes=VMEM_LIMIT)


def _const_spec(shape):
    nd = len(shape)
    return pl.BlockSpec(shape, lambda *_: (0,) * nd, pipeline_mode=pl.Buffered(1))


def _rope_tables(pos):
    half = HEAD_DIM // 2
    inv = 1.0 / (ROPE_BASE ** (jnp.arange(half, dtype=F32) / half))
    ang = pos.astype(F32)[:, None] * inv[None, :]
    cos = jnp.cos(ang)
    sin = jnp.sin(ang)
    cos_h = jnp.concatenate([cos, cos], axis=-1)
    sin_h = jnp.concatenate([-sin, sin], axis=-1)
    return jnp.tile(cos_h, (1, N_HEADS)), jnp.tile(sin_h, (1, N_HEADS))


def _decay_tables(chunk):
    log_g = jnp.log(1.0 - 2.0 ** (-5.0 - jnp.arange(N_HEADS, dtype=F32)))
    idx = jnp.arange(chunk)
    diff = idx[:, None] - idx[None, :]
    decay_mat = jnp.where(diff[None] >= 0,
                          jnp.exp(log_g[:, None, None] * jnp.maximum(diff, 0)[None].astype(F32)), 0.0)
    q_dec = jnp.exp(log_g[None, :] * (idx[:, None] + 1).astype(F32))
    k_dec = jnp.exp(log_g[None, :] * (chunk - 1 - idx[:, None]).astype(F32))
    chunk_dec = jnp.exp(log_g * chunk)
    return decay_mat, q_dec, k_dec, chunk_dec


def _bucket_table(n_q, n_k):
    i = np.arange(n_q)[:, None]
    j = np.arange(n_k)[None, :]
    dist = i - j + WINDOW
    n = np.maximum(dist, 0)
    max_exact = NUM_BUCKETS // 2
    nf = np.maximum(n, max_exact).astype(np.float64)
    large = max_exact + (np.log(nf / max_exact) / math.log(WINDOW / max_exact)
                         * (NUM_BUCKETS - max_exact)).astype(np.int32)
    large = np.minimum(large, NUM_BUCKETS - 1)
    bucket = np.where(n < max_exact, n, large).astype(np.int32)
    valid = ((dist >= 0) & (dist < WINDOW)).astype(np.int32)
    return bucket, valid


def _ada_kernel(c_ref, w_ref, b_ref, o_ref):
    c = c_ref[...]
    s = (c * jax.nn.sigmoid(c)).astype(BF16)
    o_ref[0] = jnp.dot(s, w_ref[0].astype(BF16), preferred_element_type=F32) + b_ref[0]


def _ada_call(c_all, w_ada, b_ada):
    depth = w_ada.shape[0]
    n_rows = c_all.shape[0]
    tn = 1536
    return pl.pallas_call(
        _ada_kernel,
        grid=(depth, 6 * D_MODEL // tn),
        in_specs=[pl.BlockSpec((n_rows, D_MODEL), lambda l, n: (0, 0)),
                  pl.BlockSpec((1, D_MODEL, tn), lambda l, n: (l, 0, n)),
                  pl.BlockSpec((1, 1, tn), lambda l, n: (l, 0, n))],
        out_specs=pl.BlockSpec((1, n_rows, tn), lambda l, n: (l, 0, n)),
        out_shape=jax.ShapeDtypeStruct((depth, n_rows, 6 * D_MODEL), F32),
        compiler_params=_cparams(2),
        name="ada",
    )(c_all, w_ada, b_ada.reshape(depth, 1, 6 * D_MODEL))


def _bias_kernel(rb_ref, bucket_ref, valid_ref, o_ref):
    bucket = bucket_ref[...]
    valid = valid_ref[...] > 0
    first_ok = lax.broadcasted_iota(jnp.int32, bucket.shape, 1) >= WINDOW
    for h in range(N_HEADS):
        acc = jnp.zeros(bucket.shape, F32)
        for b in range(NUM_BUCKETS):
            acc = jnp.where(bucket == b, rb_ref[b, h], acc)
        o_ref[0, h] = jnp.where(valid, acc, NEG_INF)
        o_ref[1, h] = jnp.where(valid & first_ok, acc, NEG_INF)


def _bias_call(rel_bias):
    bucket, valid = _bucket_table(CHUNK, 2 * CHUNK)
    return pl.pallas_call(
        _bias_kernel,
        in_specs=[pl.BlockSpec(memory_space=pltpu.SMEM),
                  pl.BlockSpec((CHUNK, 2 * CHUNK), lambda: (0, 0)),
                  pl.BlockSpec((CHUNK, 2 * CHUNK), lambda: (0, 0))],
        out_specs=pl.BlockSpec((2, N_HEADS, CHUNK, 2 * CHUNK), lambda: (0, 0, 0, 0)),
        out_shape=jax.ShapeDtypeStruct((2, N_HEADS, CHUNK, 2 * CHUNK), F32),
        name="swa_bias",
    )(rel_bias, jnp.asarray(bucket), jnp.asarray(valid))


def _swap_halves(x):
    lane = lax.broadcasted_iota(jnp.int32, (1, LANES), 1)
    first = (lane % HEAD_DIM) < (HEAD_DIM // 2)
    cols = []
    for c in range(x.shape[1] // LANES):
        xc = x[:, c * LANES:(c + 1) * LANES]
        cols.append(jnp.where(first, pltpu.roll(xc, LANES - HEAD_DIM // 2, 1), pltpu.roll(xc, HEAD_DIM // 2, 1)))
    return jnp.concatenate(cols, axis=1)


def _proj_kernel(x_ref, sc_ref, sh_ref, w_ref, cos_ref, sin_ref, qkvg_ref, kv_ref):
    tb, tl, _ = x_ref.shape
    tm = tb * tl
    h = x_ref[...] * (1.0 + sc_ref[...]) + sh_ref[...]
    h = h.reshape(tm, D_MODEL).astype(BF16)
    proj = jnp.dot(h, w_ref[...], preferred_element_type=F32)
    cos = cos_ref[...]
    sin = sin_ref[...]
    rq = proj[:, 0:RET_W]
    rk = proj[:, RET_W:2 * RET_W]
    rq = rq * cos + _swap_halves(rq) * sin
    rk = (rk * cos + _swap_halves(rk) * sin) * (HEAD_DIM ** -0.5)
    rg = proj[:, 3 * RET_W:4 * RET_W]
    qkvg_ref[:, 0:RET_W] = rq.astype(BF16)
    qkvg_ref[:, RET_W:2 * RET_W] = rk.astype(BF16)
    qkvg_ref[:, 2 * RET_W:3 * RET_W] = proj[:, 2 * RET_W:3 * RET_W].astype(BF16)
    qkvg_ref[:, 3 * RET_W:4 * RET_W] = (rg * jax.nn.sigmoid(rg)).astype(BF16)
    qkvg_ref[:, 4 * RET_W:5 * RET_W] = (proj[:, 4 * RET_W:5 * RET_W] * (HEAD_DIM ** -0.5)).astype(BF16)
    kv_ref[...] = proj[:, 5 * RET_W:IN_COLS]


def _layer_spec(shape, layer):
    nd = len(shape)
    return pl.BlockSpec((None,) + tuple(shape), lambda *_: (layer,) + (0,) * nd, pipeline_mode=pl.Buffered(1))


def _mod_spec(tb, layer, k):
    return pl.BlockSpec((None, tb, 1, D_MODEL), lambda i, j: (layer, i, 0, k))


def _proj_call(x, mod, w_in, layer, cos, sin, tb, tl):
    b, l, _ = x.shape
    nj = l // tl
    tm = tb * tl
    tab_idx = (lambda i, j: (j, 0)) if cos.shape[0] == l and nj > 1 else (lambda i, j: (0, 0))
    return pl.pallas_call(
        _proj_kernel,
        grid=(b // tb, nj),
        in_specs=[pl.BlockSpec((tb, tl, D_MODEL), lambda i, j: (i, j, 0)),
                  _mod_spec(tb, layer, 1),
                  _mod_spec(tb, layer, 0),
                  _layer_spec((D_MODEL, IN_COLS), layer),
                  pl.BlockSpec((tm, RET_W), tab_idx),
                  pl.BlockSpec((tm, RET_W), tab_idx)],
        out_specs=[pl.BlockSpec((tm, QKVG_W), lambda i, j: (i * nj + j, 0)),
                   pl.BlockSpec((tm, 2 * KV_W), lambda i, j: (i * nj + j, 0))],
        out_shape=[jax.ShapeDtypeStruct((b * l, QKVG_W), BF16),
                   jax.ShapeDtypeStruct((b * l, 2 * KV_W), F32)],
        compiler_params=_cparams(2),
        name="proj",
    )(x, mod, mod, w_in, cos, sin)


def _lane_mask_low():
    return lax.broadcasted_iota(jnp.int32, (1, LANES), 1) < HEAD_DIM


def _head_norm(o, ones_bd):
    outs = []
    gw = ones_bd.shape[0]
    for g in range(RET_W // gw):
        og = o[:, g * gw:(g + 1) * gw]
        mu = jnp.dot(og.astype(BF16), ones_bd, preferred_element_type=F32)
        d = og - mu
        var = jnp.dot((d * d).astype(BF16), ones_bd, preferred_element_type=F32)
        outs.append(d * lax.rsqrt(var + LN_EPS))
    return jnp.concatenate(outs, axis=1)


def _swa_operands(k_all, v_all):
    low = _lane_mask_low()
    zero = jnp.zeros_like(k_all)
    k_rot = pltpu.roll(k_all, HEAD_DIM, 1)
    v_rot = pltpu.roll(v_all, HEAD_DIM, 1)
    ops = []
    for g in range(KV_HEADS):
        k_lo, k_hi = (k_all, k_rot) if g == 0 else (k_rot, k_all)
        v_lo, v_hi = (v_all, v_rot) if g == 0 else (v_rot, v_all)
        kc = jnp.concatenate([jnp.where(low, k_lo, zero), jnp.where(low, zero, k_hi)], axis=0)
        vc = jnp.concatenate([jnp.where(low, v_lo, zero), jnp.where(low, zero, v_hi)], axis=0)
        ops.append((kc, vc))
    return ops


def _swa_softmax(lg, p, bias_ref, sink_ref):
    rows = lg.shape[0]
    low = _lane_mask_low()
    probs, dens = [], []
    for s in range(2):
        hh = 2 * p + s
        lo = lg[:, s * 2 * CHUNK:(s + 1) * 2 * CHUNK] + bias_ref[hh, 0:rows, :]
        sink = sink_ref[hh]
        m = jnp.maximum(jnp.max(lo, axis=-1, keepdims=True), sink)
        pe = jnp.exp(lo - m)
        dens.append(jnp.sum(pe, axis=-1, keepdims=True) + jnp.exp(sink - m))
        probs.append(pe)
    return jnp.concatenate(probs, axis=1), jnp.where(low, dens[0], dens[1])


def _attn_prompt_kernel(sink_ref, qkvg_ref, kv_ref, bias_ref, bias_all_ref, dm_ref, qd_ref, kd_ref, cd_ref, ones_ref,
                        mix_ref, s_out_ref, s_scr, pk_scr, pv_scr):
    i = pl.program_id(1)

    @pl.when(i == 0)
    def _():
        s_scr[...] = jnp.zeros_like(s_scr)
        pk_scr[...] = jnp.zeros_like(pk_scr)
        pv_scr[...] = jnp.zeros_like(pv_scr)

    low = _lane_mask_low()
    zero = jnp.zeros((CHUNK, LANES), BF16)
    n_pairs = N_HEADS // 2
    pairs_per_kv = n_pairs // KV_HEADS
    gw = cd_ref.shape[1]
    n_groups = RET_W // gw
    n_blk = qkvg_ref.shape[0] // CHUNK
    nt = (((1,), (1,)), ((), ()))
    tn = (((0,), (0,)), ((), ()))
    ones_bd = ones_ref[...]

    s_cur = [s_scr[g] for g in range(n_groups)]
    k_prev, v_prev = pk_scr[...], pv_scr[...]
    wave1, mid = [], []
    for c in range(n_blk):
        rows = slice(c * CHUNK, (c + 1) * CHUNK)
        q = qkvg_ref[rows, 0:RET_W]
        k = qkvg_ref[rows, RET_W:2 * RET_W]
        v = qkvg_ref[rows, 2 * RET_W:3 * RET_W]
        sq = qkvg_ref[rows, 4 * RET_W:5 * RET_W]
        kd = (k.astype(F32) * kd_ref[...]).astype(BF16)
        ret_ops = []
        for p in range(n_pairs):
            kp = k[:, p * LANES:(p + 1) * LANES]
            vp = v[:, p * LANES:(p + 1) * LANES]
            ret_ops.append((jnp.concatenate([jnp.where(low, kp, zero), jnp.where(low, zero, kp)], axis=0),
                            jnp.concatenate([jnp.where(low, vp, zero), jnp.where(low, zero, vp)], axis=0)))
        k_new = kv_ref[rows, 0:KV_W].astype(BF16)
        v_new = kv_ref[rows, KV_W:2 * KV_W].astype(BF16)
        swa_ops = _swa_operands(jnp.concatenate([k_prev, k_new], axis=0), jnp.concatenate([v_prev, v_new], axis=0))
        k_prev, v_prev = k_new, v_new

        scores = [lax.dot_general(q[:, p * LANES:(p + 1) * LANES], ret_ops[p][0], nt, preferred_element_type=F32)
                  for p in range(n_pairs)]
        cross = [jnp.dot(q[:, g * gw:(g + 1) * gw], s_cur[g].astype(BF16), preferred_element_type=F32)
                 for g in range(n_groups)]
        upd = [lax.dot_general(kd[:, g * gw:(g + 1) * gw], v[:, g * gw:(g + 1) * gw], tn,
                               preferred_element_type=F32) for g in range(n_groups)]
        logits = [lax.dot_general(sq[:, p * LANES:(p + 1) * LANES], swa_ops[p // pairs_per_kv][0], nt,
                                  preferred_element_type=F32) for p in range(n_pairs)]
        s_cur = [s_cur[g] * cd_ref[g] + jnp.where(cd_ref[g] > 0.0, upd[g], 0.0) for g in range(n_groups)]
        wave1.append((scores, cross, logits, ret_ops, swa_ops))
    for g in range(n_groups):
        s_scr[g] = s_cur[g]
    pk_scr[...] = k_prev
    pv_scr[...] = v_prev

    for c in range(n_blk):
        scores, cross, logits, ret_ops, swa_ops = wave1[c]
        bias_c = bias_ref.at[0] if c == 0 else bias_all_ref.at[0]
        scores = [(scores[p] * dm_ref[p]).astype(BF16) for p in range(n_pairs)]
        soft = [_swa_softmax(logits[p], p, bias_c, sink_ref) for p in range(n_pairs)]
        intra = [jnp.dot(scores[p], ret_ops[p][1], preferred_element_type=F32) for p in range(n_pairs)]
        swa = [jnp.dot(soft[p][0].astype(BF16), swa_ops[p // pairs_per_kv][1], preferred_element_type=F32)
               / soft[p][1] for p in range(n_pairs)]
        o = jnp.concatenate(intra, axis=1) + jnp.concatenate(cross, axis=1) * qd_ref[...]
        mid.append((o, swa))

    o = jnp.concatenate([m[0] for m in mid], axis=0)
    ret = _head_norm(o, ones_bd) * qkvg_ref[:, 3 * RET_W:4 * RET_W].astype(F32)
    mix_ref[:, 0:RET_W] = ret.astype(BF16)
    for c in range(n_blk):
        mix_ref[c * CHUNK:(c + 1) * CHUNK, RET_W:2 * RET_W] = jnp.concatenate(mid[c][1], axis=1).astype(BF16)

    @pl.when(i == pl.num_programs(1) - 1)
    def _():
        for h in range(N_HEADS):
            g, hl = divmod(h, gw // HEAD_DIM)
            s_out_ref[0, h] = s_scr[g, hl * HEAD_DIM:(hl + 1) * HEAD_DIM, hl * HEAD_DIM:(hl + 1) * HEAD_DIM]


def _prompt_tables():
    decay_mat, q_dec, k_dec, chunk_dec = _decay_tables(CHUNK)
    dm = jnp.concatenate([decay_mat[0::2], decay_mat[1::2]], axis=2)
    qd = jnp.repeat(q_dec, HEAD_DIM, axis=1)
    kd = jnp.repeat(k_dec, HEAD_DIM, axis=1)
    gw = 4 * HEAD_DIM
    blk = np.kron(np.eye(4, dtype=np.float32), np.ones((HEAD_DIM, HEAD_DIM), np.float32))
    cd = jnp.repeat(chunk_dec.reshape(2, 4), HEAD_DIM, axis=1)[:, :, None] * jnp.asarray(blk)[None]
    ones_bd = jnp.asarray(blk / HEAD_DIM, BF16)
    assert cd.shape == (2, gw, gw)
    return dm, qd, kd, cd, ones_bd


def _attn_prompt_call(qkvg, kv, bias, sinks, tables, b, l, n_blk):
    dm, qd, kd, cd, ones_bd = tables
    rows = n_blk * CHUNK
    nc = l // rows
    gw = cd.shape[1]
    bias_spec = lambda idx: pl.BlockSpec((1, N_HEADS, CHUNK, 2 * CHUNK), idx)
    return pl.pallas_call(
        _attn_prompt_kernel,
        grid=(b, nc),
        in_specs=[pl.BlockSpec(memory_space=pltpu.SMEM),
                  pl.BlockSpec((rows, QKVG_W), lambda bi, i: (bi * nc + i, 0)),
                  pl.BlockSpec((rows, 2 * KV_W), lambda bi, i: (bi * nc + i, 0)),
                  bias_spec(lambda bi, i: (jnp.where(i == 0, 1, 0), 0, 0, 0)),
                  bias_spec(lambda bi, i: (0, 0, 0, 0)),
                  _const_spec(dm.shape), _const_spec(qd.shape), _const_spec(kd.shape),
                  _const_spec(cd.shape), _const_spec(ones_bd.shape)],
        out_specs=[pl.BlockSpec((rows, 2 * RET_W), lambda bi, i: (bi * nc + i, 0)),
                   pl.BlockSpec((1, N_HEADS, HEAD_DIM, HEAD_DIM), lambda bi, i: (bi, 0, 0, 0))],
        out_shape=[jax.ShapeDtypeStruct((b * l, 2 * RET_W), BF16),
                   jax.ShapeDtypeStruct((b, N_HEADS, HEAD_DIM, HEAD_DIM), F32)],
        scratch_shapes=[pltpu.VMEM((RET_W // gw, gw, gw), F32),
                        pltpu.VMEM((CHUNK, KV_W), BF16),
                        pltpu.VMEM((CHUNK, KV_W), BF16)],
        compiler_params=_cparams(2),
        name="attn_prompt",
    )(sinks, qkvg, kv, bias, bias, dm, qd, kd, cd, ones_bd)


def _ret_sample_kernel(dec_ref, tq_ref, s_ref, buf_ref, ret_ref, s_out_ref, q_scr, k_scr, v_scr, cross_scr):
    del buf_ref
    h = pl.program_id(0)
    n_t = tq_ref.shape[1]
    cd = dec_ref[h, 2 * n_t]
    for t in range(n_t):
        q_scr[t] = tq_ref[0, t].astype(F32)
        k_scr[t] = tq_ref[1, t].astype(F32) * dec_ref[h, t]
        v_scr[t] = tq_ref[2, t].astype(F32)
    cross_scr[...] = jnp.zeros_like(cross_scr)

    def per_d(d, carry):
        s_d = s_ref[d]
        upd = s_d * cd
        for t in range(n_t):
            upd = upd + k_scr[t, pl.ds(d, 1), :] * v_scr[t]
            cross_scr[t] += q_scr[t, pl.ds(d, 1), :] * s_d
        s_out_ref[d] = upd
        return carry

    lax.fori_loop(0, HEAD_DIM, per_d, 0)

    for t in range(n_t):
        o = cross_scr[t] * dec_ref[h, n_t + t]
        for t2 in range(t + 1):
            sc = jnp.sum(q_scr[t] * (tq_ref[1, t2].astype(F32)), axis=0, keepdims=True)
            o = o + (sc * dec_ref[h, 2 * n_t + 1 + t * n_t + t2]) * v_scr[t2]
        mu = jnp.mean(o, axis=0, keepdims=True)
        d0 = o - mu
        var = jnp.mean(d0 * d0, axis=0, keepdims=True)
        ret_ref[t] = (d0 * lax.rsqrt(var + LN_EPS) * tq_ref[3, t].astype(F32)).astype(ret_ref.dtype)


def _ret_sample_call(tq, state5, s_buf, layer, dec):
    _, n_t, _, b = tq.shape
    state_blk = (None, None, HEAD_DIM, HEAD_DIM, b)
    return pl.pallas_call(
        _ret_sample_kernel,
        grid=(N_HEADS,),
        in_specs=[pl.BlockSpec(memory_space=pltpu.SMEM),
                  pl.BlockSpec((4, n_t, HEAD_DIM, b), lambda h: (0, 0, h, 0)),
                  pl.BlockSpec(state_blk, lambda h: (layer, h, 0, 0, 0)),
                  pl.BlockSpec(memory_space=pl.ANY)],
        out_specs=[pl.BlockSpec((n_t, HEAD_DIM, b), lambda h: (0, h, 0)),
                   pl.BlockSpec(state_blk, lambda h: (layer, h, 0, 0, 0))],
        out_shape=[jax.ShapeDtypeStruct((n_t, RET_W, b), BF16),
                   jax.ShapeDtypeStruct(s_buf.shape, F32)],
        scratch_shapes=[pltpu.VMEM((n_t, HEAD_DIM, b), F32)] * 4,
        input_output_aliases={3: 1},
        compiler_params=_cparams(1),
        name="ret_sample",
    )(dec, tq, state5, s_buf)


def _ret_sample_table(tl):
    decay_mat, q_dec, k_dec, chunk_dec = _decay_tables(tl)
    return jnp.concatenate([k_dec.T, q_dec.T, chunk_dec[:, None], decay_mat.reshape(N_HEADS, tl * tl)], axis=1)


def _swa_sample_kernel(sink_ref, qkvg_ref, kv_ref, ck_ref, cv_ref, bias_ref, swa_ref, ck_out_ref, cv_out_ref):
    tb = ck_ref.shape[0]
    tl = qkvg_ref.shape[0] // tb
    nt = (((1,), (1,)), ((), ()))
    pairs_per_kv = N_HEADS // KV_HEADS // 2
    pad = jnp.zeros((CHUNK - tl, KV_W), F32)

    wave1 = []
    for e in range(tb):
        rows = slice(e * tl, (e + 1) * tl)
        sq32 = qkvg_ref[rows, 4 * RET_W:5 * RET_W].astype(F32)

        k_new = kv_ref[rows, 0:KV_W]
        v_new = kv_ref[rows, KV_W:2 * KV_W]
        k_all = jnp.concatenate([ck_ref[e].astype(BF16), jnp.concatenate([k_new, pad], axis=0).astype(BF16)], axis=0)
        v_all = jnp.concatenate([cv_ref[e].astype(BF16), jnp.concatenate([v_new, pad], axis=0).astype(BF16)], axis=0)
        swa_ops = _swa_operands(k_all, v_all)
        logits = []
        for g in range(KV_HEADS):
            lhs = jnp.concatenate([sq32[:, (g * pairs_per_kv + pp) * LANES:(g * pairs_per_kv + pp + 1) * LANES]
                                   for pp in range(pairs_per_kv)], axis=0).astype(BF16)
            logits.append(lax.dot_general(lhs, swa_ops[g][0], nt, preferred_element_type=F32))
        ck_out_ref[e, 0:WINDOW - tl, :] = ck_ref[e, tl:WINDOW, :]
        cv_out_ref[e, 0:WINDOW - tl, :] = cv_ref[e, tl:WINDOW, :]
        ck_out_ref[e, WINDOW - tl:WINDOW, :] = k_new
        cv_out_ref[e, WINDOW - tl:WINDOW, :] = v_new
        wave1.append((swa_ops, logits))

    soft_all = []
    for e in range(tb):
        _, logits = wave1[e]
        soft = []
        for g in range(KV_HEADS):
            per_pair = [_swa_softmax(logits[g][pp * tl:(pp + 1) * tl, :], g * pairs_per_kv + pp, bias_ref.at[0],
                                     sink_ref) for pp in range(pairs_per_kv)]
            soft.append((jnp.concatenate([x[0] for x in per_pair], axis=0).astype(BF16),
                         jnp.concatenate([x[1] for x in per_pair], axis=0)))
        soft_all.append(soft)
    swa_rows = []
    for e in range(tb):
        swa_ops, soft = wave1[e][0], soft_all[e]
        outs = []
        for g in range(KV_HEADS):
            oo = jnp.dot(soft[g][0], swa_ops[g][1], preferred_element_type=F32) / soft[g][1]
            outs += [oo[pp * tl:(pp + 1) * tl, :] for pp in range(pairs_per_kv)]
        swa_rows.append(jnp.concatenate(outs, axis=1))
    swa_ref[...] = jnp.concatenate(swa_rows, axis=0).astype(BF16)


def _swa_sample_call(qkvg, kv, ck2, cv2, layer, bias, sinks, b, tl, tb):
    batch3 = lambda i: (i, 0, 0)
    layer4 = lambda i: (layer, i, 0, 0)
    swa, ck_out, cv_out = pl.pallas_call(
        _swa_sample_kernel,
        grid=(b // tb,),
        in_specs=[pl.BlockSpec(memory_space=pltpu.SMEM),
                  pl.BlockSpec((tb * tl, QKVG_W), lambda i: (i, 0)),
                  pl.BlockSpec((tb * tl, 2 * KV_W), lambda i: (i, 0)),
                  pl.BlockSpec((None, tb, WINDOW, KV_W), layer4),
                  pl.BlockSpec((None, tb, WINDOW, KV_W), layer4),
                  pl.BlockSpec((1, N_HEADS, CHUNK, 2 * CHUNK), lambda i: (0, 0, 0, 0))],
        out_specs=[pl.BlockSpec((tb * tl, RET_W), lambda i: (i, 0)),
                   pl.BlockSpec((tb, WINDOW, KV_W), batch3),
                   pl.BlockSpec((tb, WINDOW, KV_W), batch3)],
        out_shape=[jax.ShapeDtypeStruct((b * tl, RET_W), BF16),
                   jax.ShapeDtypeStruct((b, WINDOW, KV_W), F32),
                   jax.ShapeDtypeStruct((b, WINDOW, KV_W), F32)],
        compiler_params=_cparams(1),
        name="swa_sample",
    )(sinks, qkvg, kv, ck2, cv2, bias)
    return swa, ck_out.reshape(b, WINDOW, KV_HEADS, HEAD_DIM), cv_out.reshape(b, WINDOW, KV_HEADS, HEAD_DIM)


def _layer_norm(x, g, b):
    mu = jnp.mean(x, axis=-1, keepdims=True)
    xc = x - mu
    var = jnp.mean(xc * xc, axis=-1, keepdims=True)
    return xc * lax.rsqrt(var + LN_EPS) * g + b


def _ffn_kernel(alpha, carry_rows, mix_ref, x_ref, g1_ref, sh2_ref, sc2_ref, g2_ref, wout_ref, ln1g_ref, ln1b_ref,
                wup_ref, cw_ref, cb_ref, prev_ref, wdn_ref, ln2g_ref, ln2b_ref, xo_ref, tail_ref, carry_scr):
    tb, tl, _ = x_ref.shape
    tm = tb * tl
    sm = FF_ROWS
    n_sub = tm // sm
    sb, sl = (1, sm) if carry_rows else (tb // n_sub, tl)
    j = pl.program_id(1)

    def seq(ref, s):
        if carry_rows:
            return ref[:, s * sm:(s + 1) * sm, :] if ref.shape[1] == tl else ref[...]
        return ref[s * sb:(s + 1) * sb]

    if carry_rows:
        @pl.when(j == 0)
        def _():
            carry_scr[carry_rows - 2:carry_rows, :] = prev_ref[0]

    fix_rows = 8 if carry_rows else sm
    t_idx = lax.broadcasted_iota(jnp.int32, (fix_rows, 1), 0) % sl
    is_t0 = t_idx == 0
    is_t1 = t_idx == 1
    n_chunks = D_FF // FF_CHUNK

    def up_cols(c, half):
        return slice(half * D_FF + c * FF_CHUNK, half * D_FF + (c + 1) * FF_CHUNK)

    def pre(s):
        x1 = alpha * seq(x_ref, s) + seq(g1_ref, s) * y[s].reshape(sb, sl, D_MODEL)
        x1 = _layer_norm(x1, ln1g_ref[...], ln1b_ref[...])
        h2 = (x1 * (1.0 + seq(sc2_ref, s)) + seq(sh2_ref, s)).reshape(sm, D_MODEL).astype(BF16)
        return x1, h2

    def up_dots(h2, c):
        return [jnp.dot(h2, wup_ref[:, up_cols(c, half)], preferred_element_type=F32) for half in range(2)]

    def conv(s, up, cols):
        if carry_rows:
            src = carry_scr if s == 0 else tail_ref
            p0 = src[carry_rows - 2:carry_rows - 1, cols]
            p1 = src[carry_rows - 1:carry_rows, cols]
            tail_ref[:, cols] = up[sm - carry_rows:sm, :]
        else:
            p0 = jnp.broadcast_to(seq(prev_ref, s)[:, 0:1, cols], (sb, sl, FF_CHUNK)).reshape(sm, FF_CHUNK)
            p1 = jnp.broadcast_to(seq(prev_ref, s)[:, 1:2, cols], (sb, sl, FF_CHUNK)).reshape(sm, FF_CHUNK)
            tail_ref[s * sm:(s + 1) * sm, cols] = up
        r1 = pltpu.roll(up, 1, 0)
        r2 = pltpu.roll(up, 2, 0)
        s1 = jnp.where(is_t0, p1, r1[0:fix_rows])
        s2 = jnp.where(is_t0, p0, jnp.where(is_t1, p1, r2[0:fix_rows]))
        if fix_rows < sm:
            s1 = jnp.concatenate([s1, r1[fix_rows:]], axis=0)
            s2 = jnp.concatenate([s2, r2[fix_rows:]], axis=0)
        return cb_ref[:, cols] + s2 * cw_ref[0:1, cols] + s1 * cw_ref[1:2, cols] + up * cw_ref[2:3, cols]

    y = [jnp.dot(mix_ref[s * sm:(s + 1) * sm, :], wout_ref[...], preferred_element_type=F32) for s in range(n_sub)]
    items = [(s, c) for s in range(n_sub) for c in range(n_chunks)]
    staged, ups = {}, {}

    def issue_up(k):
        s, c = items[k]
        if c == 0:
            staged[s] = pre(s)
        ups[k] = up_dots(staged[s][1], c)

    for k in range(min(FF_LOOKAHEAD, len(items))):
        issue_up(k)
    acc = None
    for k, (s, c) in enumerate(items):
        if k + FF_LOOKAHEAD < len(items):
            issue_up(k + FF_LOOKAHEAD)
        up_a, up_b = ups.pop(k)
        ua = conv(s, up_a, up_cols(c, 0))
        ub = conv(s, up_b, up_cols(c, 1))
        gated = (ua * jax.nn.sigmoid(ua) * ub).astype(BF16)
        down = jnp.dot(gated, wdn_ref[c * FF_CHUNK:(c + 1) * FF_CHUNK, :], preferred_element_type=F32)
        acc = down if c == 0 else acc + down
        if c == n_chunks - 1:
            x1, _ = staged.pop(s)
            x2 = alpha * x1 + seq(g2_ref, s) * acc.reshape(sb, sl, D_MODEL)
            out = _layer_norm(x2, ln2g_ref[...], ln2b_ref[...])
            if carry_rows:
                xo_ref[:, s * sm:(s + 1) * sm, :] = out
            else:
                xo_ref[s * sb:(s + 1) * sb] = out

    if carry_rows:
        carry_scr[...] = tail_ref[...]


def _ffn_call(mix, x, mod, conv_prev, w_out, ln1_g, ln1_b, w_up, conv_w, conv_b, w_down, ln2_g, ln2_b,
              layer, alpha, tb, tl):
    prev_layer = layer if conv_prev.shape[0] > 1 else 0
    b, l, _ = x.shape
    nj = l // tl
    tm = tb * tl
    carry_rows = 8 if tb == 1 else 0
    assert carry_rows or (nj == 1 and tl == 8)
    tail_rows = 8 if carry_rows else tm
    kern = functools.partial(_ffn_kernel, alpha, carry_rows)
    return pl.pallas_call(
        kern,
        grid=(b // tb, nj),
        in_specs=[pl.BlockSpec((tm, 2 * RET_W), lambda i, j: (i * nj + j, 0)),
                  pl.BlockSpec((tb, tl, D_MODEL), lambda i, j: (i, j, 0)),
                  _mod_spec(tb, layer, 2), _mod_spec(tb, layer, 3), _mod_spec(tb, layer, 4), _mod_spec(tb, layer, 5),
                  _layer_spec((2 * RET_W, D_MODEL), layer),
                  _layer_spec((1, D_MODEL), layer), _layer_spec((1, D_MODEL), layer),
                  _layer_spec((D_MODEL, UP_W), layer),
                  _layer_spec((3, UP_W), layer), _layer_spec((1, UP_W), layer),
                  pl.BlockSpec((None, tb, 2, UP_W), lambda i, j: (prev_layer, i, 0, 0)),
                  _layer_spec((D_FF, D_MODEL), layer),
                  _layer_spec((1, D_MODEL), layer), _layer_spec((1, D_MODEL), layer)],
        out_specs=[pl.BlockSpec((tb, tl, D_MODEL), lambda i, j: (i, j, 0)),
                   pl.BlockSpec((tail_rows, UP_W), lambda i, j: (i, 0))],
        out_shape=[jax.ShapeDtypeStruct((b, l, D_MODEL), F32),
                   jax.ShapeDtypeStruct((b * 8, UP_W), F32)],
        scratch_shapes=[pltpu.VMEM((8, UP_W), F32)],
        compiler_params=_cparams(2),
        name="ffn",
    )(mix, x, mod, mod, mod, mod, w_out, ln1_g, ln1_b, w_up, conv_w, conv_b, conv_prev, w_down, ln2_g, ln2_b)


def kernel(x_prompt, x_sample, c_prompt, c_sample, state_ret, cache_swa_k, cache_swa_v, state_conv, rel_bias, w_ada, b_ada, w_in, swa_sinks, w_out, ln1_g, ln1_b, w_up, conv_w, conv_b, w_down, ln2_g, ln2_b):
    depth = w_ada.shape[0]
    bp, lp, _ = x_prompt.shape
    bs, ls, _ = x_sample.shape
    alpha = (2.0 * depth) ** 0.25
    tl_proj = 512
    tl_p = 2 * FF_ROWS
    tb_s = FF_ROWS // ls
    tb_attn = 8
    blk_attn = 2

    c_all = jnp.concatenate([c_prompt, c_sample], axis=0)
    mod_all = _ada_call(c_all, w_ada, b_ada)
    bias = _bias_call(rel_bias)

    cos_p, sin_p = _rope_tables(jnp.arange(lp, dtype=jnp.int32))
    cos_s, sin_s = _rope_tables(PAST_LEN + jnp.arange(ls, dtype=jnp.int32))
    cos_s, sin_s = jnp.tile(cos_s, (tb_s, 1)), jnp.tile(sin_s, (tb_s, 1))
    tab_p = _prompt_tables()
    dec_s = _ret_sample_table(ls)
    conv0 = jnp.zeros((1, bp, 2, UP_W), F32)

    w_in_b = w_in.astype(BF16)
    vec = lambda a: a.reshape(depth, 1, a.shape[-1])
    ffn_w = (w_out.astype(BF16), vec(ln1_g), vec(ln1_b), w_up.astype(BF16), conv_w, vec(conv_b),
             w_down.astype(BF16), vec(ln2_g), vec(ln2_b))
    mod_p = mod_all[:, :bp].reshape(depth, bp, 1, 6 * D_MODEL)
    mod_s = mod_all[:, bp:].reshape(depth, bs, 1, 6 * D_MODEL)
    state5 = jnp.transpose(state_ret, (0, 2, 3, 4, 1))
    s_buf = jnp.zeros(state5.shape, F32)
    ck2 = cache_swa_k.reshape(depth, bs, WINDOW, KV_W)
    cv2 = cache_swa_v.reshape(depth, bs, WINDOW, KV_W)

    xp, xs = x_prompt, x_sample
    p_ret, p_k, p_v, p_conv = [], [], [], []
    s_ret, s_k, s_v, s_conv = [], [], [], []
    for l in range(depth):
        qkvg, kv = _proj_call(xp, mod_p, w_in_b, l, cos_p, sin_p, 1, tl_proj)
        mix, r_p = _attn_prompt_call(qkvg, kv, bias, swa_sinks[l], tab_p, bp, lp, blk_attn)
        xp, tail = _ffn_call(mix, xp, mod_p, conv0, *ffn_w, l, alpha, 1, tl_p)
        kv3 = kv.reshape(bp, lp, 2, KV_HEADS, HEAD_DIM)
        p_ret.append(r_p)
        p_k.append(kv3[:, lp - WINDOW:, 0])
        p_v.append(kv3[:, lp - WINDOW:, 1])
        p_conv.append(tail.reshape(bp, 8, UP_W)[:, 6:8])

        qkvg, kv = _proj_call(xs, mod_s, w_in_b, l, cos_s, sin_s, tb_s, ls)
        tq = jnp.transpose(qkvg.reshape(bs, ls, 5, RET_W)[:, :, 0:4], (2, 1, 3, 0))
        ret_t, s_buf = _ret_sample_call(tq, state5, s_buf, l, dec_s)
        ret = jnp.transpose(ret_t, (2, 0, 1)).reshape(bs * ls, RET_W)
        swa, k_s, v_s = _swa_sample_call(qkvg, kv, ck2, cv2, l, bias, swa_sinks[l], bs, ls, tb_attn)
        mix = jnp.concatenate([ret, swa], axis=1)
        xs, tail = _ffn_call(mix, xs, mod_s, state_conv, *ffn_w, l, alpha, tb_s, ls)
        s_k.append(k_s)
        s_v.append(v_s)
        s_conv.append(tail.reshape(bs, ls, UP_W)[:, ls - 2:])

    return (xp, xs,
            jnp.stack(p_ret), jnp.stack(p_k), jnp.stack(p_v), jnp.stack(p_conv),
            jnp.transpose(s_buf, (0, 4, 1, 2, 3)), jnp.stack(s_k), jnp.stack(s_v), jnp.stack(s_conv))
```

```python
import functools
import math

import jax
import jax.numpy as jnp
import numpy as np
from jax import lax
from jax.experimental import pallas as pl
from jax.experimental.pallas import tpu as pltpu

F32 = jnp.float32
BF16 = jnp.bfloat16

D_MODEL = 1024
HEAD_DIM = 64
N_HEADS = 8
RET_W = N_HEADS * HEAD_DIM
KV_HEADS = 2
KV_W = KV_HEADS * HEAD_DIM
D_FF = 2816
UP_W = 2 * D_FF
IN_COLS = 4 * RET_W + RET_W + 2 * KV_W
QKVG_W = 5 * RET_W
WINDOW = 128
CHUNK = 128
NUM_BUCKETS = 32
PAST_LEN = 8192
ROPE_BASE = 10000.0
LN_EPS = 1e-5
NEG_INF = -1e30
LANES = 128
FF_CHUNK = 256
FF_LOOKAHEAD = 2
FF_ROWS = 256
VMEM_LIMIT = 56 * 1024 * 1024


def _cparams(n_axes):
    return pltpu.CompilerParams(dimension_semantics=("arbitrary",) * n_axes,
                                vmem_limit_bytes=VMEM_LIMIT)


def _const_spec(shape):
    nd = len(shape)
    return pl.BlockSpec(shape, lambda *_: (0,) * nd, pipeline_mode=pl.Buffered(1))


def _rope_tables(pos):
    half = HEAD_DIM // 2
    inv = 1.0 / (ROPE_BASE ** (jnp.arange(half, dtype=F32) / half))
    ang = pos.astype(F32)[:, None] * inv[None, :]
    cos = jnp.cos(ang)
    sin = jnp.sin(ang)
    cos_h = jnp.concatenate([cos, cos], axis=-1)
    sin_h = jnp.concatenate([-sin, sin], axis=-1)
    return jnp.tile(cos_h, (1, N_HEADS)), jnp.tile(sin_h, (1, N_HEADS))


def _decay_tables(chunk):
    log_g = jnp.log(1.0 - 2.0 ** (-5.0 - jnp.arange(N_HEADS, dtype=F32)))
    idx = jnp.arange(chunk)
    diff = idx[:, None] - idx[None, :]
    decay_mat = jnp.where(diff[None] >= 0,
                          jnp.exp(log_g[:, None, None] * jnp.maximum(diff, 0)[None].astype(F32)), 0.0)
    q_dec = jnp.exp(log_g[None, :] * (idx[:, None] + 1).astype(F32))
    k_dec = jnp.exp(log_g[None, :] * (chunk - 1 - idx[:, None]).astype(F32))
    chunk_dec = jnp.exp(log_g * chunk)
    return decay_mat, q_dec, k_dec, chunk_dec


def _bucket_table(n_q, n_k):
    i = np.arange(n_q)[:, None]
    j = np.arange(n_k)[None, :]
    dist = i - j + WINDOW
    n = np.maximum(dist, 0)
    max_exact = NUM_BUCKETS // 2
    nf = np.maximum(n, max_exact).astype(np.float64)
    large = max_exact + (np.log(nf / max_exact) / math.log(WINDOW / max_exact)
                         * (NUM_BUCKETS - max_exact)).astype(np.int32)
    large = np.minimum(large, NUM_BUCKETS - 1)
    bucket = np.where(n < max_exact, n, large).astype(np.int32)
    valid = ((dist >= 0) & (dist < WINDOW)).astype(np.int32)
    return bucket, valid


def _ada_kernel(c_ref, w_ref, b_ref, o_ref):
    c = c_ref[...]
    s = (c * jax.nn.sigmoid(c)).astype(BF16)
    o_ref[0] = jnp.dot(s, w_ref[0].astype(BF16), preferred_element_type=F32) + b_ref[0]


def _ada_call(c_all, w_ada, b_ada):
    depth = w_ada.shape[0]
    n_rows = c_all.shape[0]
    tn = 1536
    return pl.pallas_call(
        _ada_kernel,
        grid=(depth, 6 * D_MODEL // tn),
        in_specs=[pl.BlockSpec((n_rows, D_MODEL), lambda l, n: (0, 0)),
                  pl.BlockSpec((1, D_MODEL, tn), lambda l, n: (l, 0, n)),
                  pl.BlockSpec((1, 1, tn), lambda l, n: (l, 0, n))],
        out_specs=pl.BlockSpec((1, n_rows, tn), lambda l, n: (l, 0, n)),
        out_shape=jax.ShapeDtypeStruct((depth, n_rows, 6 * D_MODEL), F32),
        compiler_params=_cparams(2),
        name="ada",
    )(c_all, w_ada, b_ada.reshape(depth, 1, 6 * D_MODEL))


def _bias_kernel(rb_ref, bucket_ref, valid_ref, o_ref):
    bucket = bucket_ref[...]
    valid = valid_ref[...] > 0
    first_ok = lax.broadcasted_iota(jnp.int32, bucket.shape, 1) >= WINDOW
    for h in range(N_HEADS):
        acc = jnp.zeros(bucket.shape, F32)
        for b in range(NUM_BUCKETS):
            acc = jnp.where(bucket == b, rb_ref[b, h], acc)
        o_ref[0, h] = jnp.where(valid, acc, NEG_INF)
        o_ref[1, h] = jnp.where(valid & first_ok, acc, NEG_INF)


def _bias_call(rel_bias):
    bucket, valid = _bucket_table(CHUNK, 2 * CHUNK)
    return pl.pallas_call(
        _bias_kernel,
        in_specs=[pl.BlockSpec(memory_space=pltpu.SMEM),
                  pl.BlockSpec((CHUNK, 2 * CHUNK), lambda: (0, 0)),
                  pl.BlockSpec((CHUNK, 2 * CHUNK), lambda: (0, 0))],
        out_specs=pl.BlockSpec((2, N_HEADS, CHUNK, 2 * CHUNK), lambda: (0, 0, 0, 0)),
        out_shape=jax.ShapeDtypeStruct((2, N_HEADS, CHUNK, 2 * CHUNK), F32),
        name="swa_bias",
    )(rel_bias, jnp.asarray(bucket), jnp.asarray(valid))


def _swap_halves(x):
    lane = lax.broadcasted_iota(jnp.int32, (1, LANES), 1)
    first = (lane % HEAD_DIM) < (HEAD_DIM // 2)
    cols = []
    for c in range(x.shape[1] // LANES):
        xc = x[:, c * LANES:(c + 1) * LANES]
        cols.append(jnp.where(first, pltpu.roll(xc, LANES - HEAD_DIM // 2, 1), pltpu.roll(xc, HEAD_DIM // 2, 1)))
    return jnp.concatenate(cols, axis=1)


def _proj_kernel(x_ref, sc_ref, sh_ref, w_ref, cos_ref, sin_ref, qkvg_ref, kv_ref):
    tb, tl, _ = x_ref.shape
    tm = tb * tl
    h = x_ref[...] * (1.0 + sc_ref[...]) + sh_ref[...]
    h = h.reshape(tm, D_MODEL).astype(BF16)
    proj = jnp.dot(h, w_ref[...], preferred_element_type=F32)
    cos = cos_ref[...]
    sin = sin_ref[...]
    rq = proj[:, 0:RET_W]
    rk = proj[:, RET_W:2 * RET_W]
    rq = rq * cos + _swap_halves(rq) * sin
    rk = (rk * cos + _swap_halves(rk) * sin) * (HEAD_DIM ** -0.5)
    rg = proj[:, 3 * RET_W:4 * RET_W]
    qkvg_ref[:, 0:RET_W] = rq.astype(BF16)
    qkvg_ref[:, RET_W:2 * RET_W] = rk.astype(BF16)
    qkvg_ref[:, 2 * RET_W:3 * RET_W] = proj[:, 2 * RET_W:3 * RET_W].astype(BF16)
    qkvg_ref[:, 3 * RET_W:4 * RET_W] = (rg * jax.nn.sigmoid(rg)).astype(BF16)
    qkvg_ref[:, 4 * RET_W:5 * RET_W] = (proj[:, 4 * RET_W:5 * RET_W] * (HEAD_DIM ** -0.5)).astype(BF16)
    kv_ref[...] = proj[:, 5 * RET_W:IN_COLS]


def _layer_spec(shape, layer):
    nd = len(shape)
    return pl.BlockSpec((None,) + tuple(shape), lambda *_: (layer,) + (0,) * nd, pipeline_mode=pl.Buffered(1))


def _mod_spec(tb, layer, k):
    return pl.BlockSpec((None, tb, 1, D_MODEL), lambda i, j: (layer, i, 0, k))


def _proj_call(x, mod, w_in, layer, cos, sin, tb, tl):
    b, l, _ = x.shape
    nj = l // tl
    tm = tb * tl
    tab_idx = (lambda i, j: (j, 0)) if cos.shape[0] == l and nj > 1 else (lambda i, j: (0, 0))
    return pl.pallas_call(
        _proj_kernel,
        grid=(b // tb, nj),
        in_specs=[pl.BlockSpec((tb, tl, D_MODEL), lambda i, j: (i, j, 0)),
                  _mod_spec(tb, layer, 1),
                  _mod_spec(tb, layer, 0),
                  _layer_spec((D_MODEL, IN_COLS), layer),
                  pl.BlockSpec((tm, RET_W), tab_idx),
                  pl.BlockSpec((tm, RET_W), tab_idx)],
        out_specs=[pl.BlockSpec((tm, QKVG_W), lambda i, j: (i * nj + j, 0)),
                   pl.BlockSpec((tm, 2 * KV_W), lambda i, j: (i * nj + j, 0))],
        out_shape=[jax.ShapeDtypeStruct((b * l, QKVG_W), BF16),
                   jax.ShapeDtypeStruct((b * l, 2 * KV_W), F32)],
        compiler_params=_cparams(2),
        name="proj",
    )(x, mod, mod, w_in, cos, sin)


def _lane_mask_low():
    return lax.broadcasted_iota(jnp.int32, (1, LANES), 1) < HEAD_DIM


def _head_norm(o, ones_bd):
    outs = []
    gw = ones_bd.shape[0]
    for g in range(RET_W // gw):
        og = o[:, g * gw:(g + 1) * gw]
        mu = jnp.dot(og.astype(BF16), ones_bd, preferred_element_type=F32)
        d = og - mu
        var = jnp.dot((d * d).astype(BF16), ones_bd, preferred_element_type=F32)
        outs.append(d * lax.rsqrt(var + LN_EPS))
    return jnp.concatenate(outs, axis=1)


def _swa_operands(k_all, v_all, sum_columns):
    low = _lane_mask_low()
    zero = jnp.zeros_like(k_all)
    ones_low = jnp.broadcast_to(jnp.where(low, 1.0, 0.0), k_all.shape)
    ones_cols = jnp.concatenate([ones_low, 1.0 - ones_low], axis=0).astype(BF16)
    k_rot = pltpu.roll(k_all, HEAD_DIM, 1)
    v_rot = pltpu.roll(v_all, HEAD_DIM, 1)
    ops = []
    for g in range(KV_HEADS):
        k_lo, k_hi = (k_all, k_rot) if g == 0 else (k_rot, k_all)
        v_lo, v_hi = (v_all, v_rot) if g == 0 else (v_rot, v_all)
        kc = jnp.concatenate([jnp.where(low, k_lo, zero), jnp.where(low, zero, k_hi)], axis=0)
        vc = jnp.concatenate([jnp.where(low, v_lo, zero), jnp.where(low, zero, v_hi)], axis=0)
        ops.append((kc, jnp.concatenate([vc, ones_cols], axis=1) if sum_columns else vc))
    return ops


def _swa_softmax(lg, p, bias_ref, sink_ref, with_sum):
    rows = lg.shape[0]
    low = _lane_mask_low()
    probs, dens = [], []
    for s in range(2):
        hh = 2 * p + s
        lo = lg[:, s * 2 * CHUNK:(s + 1) * 2 * CHUNK] + bias_ref[hh, 0:rows, :]
        sink = sink_ref[hh]
        m = jnp.maximum(jnp.max(lo, axis=-1, keepdims=True), sink)
        pe = jnp.exp(lo - m)
        den = jnp.exp(sink - m)
        if with_sum:
            den = den + jnp.sum(pe, axis=-1, keepdims=True)
        probs.append(pe)
        dens.append(den)
    return jnp.concatenate(probs, axis=1), jnp.where(low, dens[0], dens[1])


def _swa_values(probs, den_term, vc):
    oo = jnp.dot(probs, vc, preferred_element_type=F32)
    if vc.shape[1] == 2 * LANES:
        return oo[:, 0:LANES] / (oo[:, LANES:2 * LANES] + den_term)
    return oo / den_term


def _attn_prompt_kernel(sink_ref, qkvg_ref, kv_ref, bias_ref, bias_all_ref, dm_ref, qd_ref, kd_ref, cd_ref, ones_ref,
                        mix_ref, s_out_ref, s_scr, pk_scr, pv_scr):
    i = pl.program_id(1)

    @pl.when(i == 0)
    def _():
        s_scr[...] = jnp.zeros_like(s_scr)
        pk_scr[...] = jnp.zeros_like(pk_scr)
        pv_scr[...] = jnp.zeros_like(pv_scr)

    low = _lane_mask_low()
    zero = jnp.zeros((CHUNK, LANES), BF16)
    n_pairs = N_HEADS // 2
    pairs_per_kv = n_pairs // KV_HEADS
    gw = cd_ref.shape[1]
    n_groups = RET_W // gw
    n_blk = qkvg_ref.shape[0] // CHUNK
    nt = (((1,), (1,)), ((), ()))
    tn = (((0,), (0,)), ((), ()))
    ones_bd = ones_ref[...]

    s_cur = [s_scr[g] for g in range(n_groups)]
    k_prev, v_prev = pk_scr[...], pv_scr[...]
    wave1, mid = [], []
    for c in range(n_blk):
        rows = slice(c * CHUNK, (c + 1) * CHUNK)
        q = qkvg_ref[rows, 0:RET_W]
        k = qkvg_ref[rows, RET_W:2 * RET_W]
        v = qkvg_ref[rows, 2 * RET_W:3 * RET_W]
        sq = qkvg_ref[rows, 4 * RET_W:5 * RET_W]
        kd = (k.astype(F32) * kd_ref[...]).astype(BF16)
        ret_ops = []
        for p in range(n_pairs):
            kp = k[:, p * LANES:(p + 1) * LANES]
            vp = v[:, p * LANES:(p + 1) * LANES]
            ret_ops.append((jnp.concatenate([jnp.where(low, kp, zero), jnp.where(low, zero, kp)], axis=0),
                            jnp.concatenate([jnp.where(low, vp, zero), jnp.where(low, zero, vp)], axis=0)))
        k_new = kv_ref[rows, 0:KV_W].astype(BF16)
        v_new = kv_ref[rows, KV_W:2 * KV_W].astype(BF16)
        swa_ops = _swa_operands(jnp.concatenate([k_prev, k_new], axis=0), jnp.concatenate([v_prev, v_new], axis=0),
                                True)
        k_prev, v_prev = k_new, v_new

        scores = [lax.dot_general(q[:, p * LANES:(p + 1) * LANES], ret_ops[p][0], nt, preferred_element_type=F32)
                  for p in range(n_pairs)]
        cross = [jnp.dot(q[:, g * gw:(g + 1) * gw], s_cur[g].astype(BF16), preferred_element_type=F32)
                 for g in range(n_groups)]
        upd = [lax.dot_general(kd[:, g * gw:(g + 1) * gw], v[:, g * gw:(g + 1) * gw], tn,
                               preferred_element_type=F32) for g in range(n_groups)]
        logits = [lax.dot_general(
            jnp.concatenate([sq[:, (g * pairs_per_kv + pp) * LANES:(g * pairs_per_kv + pp + 1) * LANES]
                             for pp in range(pairs_per_kv)], axis=0),
            swa_ops[g][0], nt, preferred_element_type=F32) for g in range(KV_HEADS)]
        s_cur = [s_cur[g] * cd_ref[g] + jnp.where(cd_ref[g] > 0.0, upd[g], 0.0) for g in range(n_groups)]
        wave1.append((scores, cross, logits, ret_ops, swa_ops))
    for g in range(n_groups):
        s_scr[g] = s_cur[g]
    pk_scr[...] = k_prev
    pv_scr[...] = v_prev

    for c in range(n_blk):
        scores, cross, logits, ret_ops, swa_ops = wave1[c]
        bias_c = bias_ref.at[0] if c == 0 else bias_all_ref.at[0]
        scores = [(scores[p] * dm_ref[p]).astype(BF16) for p in range(n_pairs)]
        soft = [[_swa_softmax(logits[g][pp * CHUNK:(pp + 1) * CHUNK, :], g * pairs_per_kv + pp, bias_c, sink_ref,
                              False) for pp in range(pairs_per_kv)] for g in range(KV_HEADS)]
        intra = [jnp.dot(scores[p], ret_ops[p][1], preferred_element_type=F32) for p in range(n_pairs)]
        swa = []
        for g in range(KV_HEADS):
            oo = _swa_values(jnp.concatenate([x[0].astype(BF16) for x in soft[g]], axis=0),
                             jnp.concatenate([x[1] for x in soft[g]], axis=0), swa_ops[g][1])
            swa += [oo[pp * CHUNK:(pp + 1) * CHUNK, :] for pp in range(pairs_per_kv)]
        o = jnp.concatenate(intra, axis=1) + jnp.concatenate(cross, axis=1) * qd_ref[...]
        mid.append((o, swa))

    o = jnp.concatenate([m[0] for m in mid], axis=0)
    ret = _head_norm(o, ones_bd) * qkvg_ref[:, 3 * RET_W:4 * RET_W].astype(F32)
    mix_ref[:, 0:RET_W] = ret.astype(BF16)
    for c in range(n_blk):
        mix_ref[c * CHUNK:(c + 1) * CHUNK, RET_W:2 * RET_W] = jnp.concatenate(mid[c][1], axis=1).astype(BF16)

    @pl.when(i == pl.num_programs(1) - 1)
    def _():
        for h in range(N_HEADS):
            g, hl = divmod(h, gw // HEAD_DIM)
            s_out_ref[0, h] = s_scr[g, hl * HEAD_DIM:(hl + 1) * HEAD_DIM, hl * HEAD_DIM:(hl + 1) * HEAD_DIM]


def _prompt_tables():
    decay_mat, q_dec, k_dec, chunk_dec = _decay_tables(CHUNK)
    dm = jnp.concatenate([decay_mat[0::2], decay_mat[1::2]], axis=2)
    qd = jnp.repeat(q_dec, HEAD_DIM, axis=1)
    kd = jnp.repeat(k_dec, HEAD_DIM, axis=1)
    gw = 4 * HEAD_DIM
    blk = np.kron(np.eye(4, dtype=np.float32), np.ones((HEAD_DIM, HEAD_DIM), np.float32))
    cd = jnp.repeat(chunk_dec.reshape(2, 4), HEAD_DIM, axis=1)[:, :, None] * jnp.asarray(blk)[None]
    ones_bd = jnp.asarray(blk / HEAD_DIM, BF16)
    assert cd.shape == (2, gw, gw)
    return dm, qd, kd, cd, ones_bd


def _attn_prompt_call(qkvg, kv, bias, sinks, tables, b, l, n_blk):
    dm, qd, kd, cd, ones_bd = tables
    rows = n_blk * CHUNK
    nc = l // rows
    gw = cd.shape[1]
    bias_spec = lambda idx: pl.BlockSpec((1, N_HEADS, CHUNK, 2 * CHUNK), idx)
    return pl.pallas_call(
        _attn_prompt_kernel,
        grid=(b, nc),
        in_specs=[pl.BlockSpec(memory_space=pltpu.SMEM),
                  pl.BlockSpec((rows, QKVG_W), lambda bi, i: (bi * nc + i, 0)),
                  pl.BlockSpec((rows, 2 * KV_W), lambda bi, i: (bi * nc + i, 0)),
                  bias_spec(lambda bi, i: (jnp.where(i == 0, 1, 0), 0, 0, 0)),
                  bias_spec(lambda bi, i: (0, 0, 0, 0)),
                  _const_spec(dm.shape), _const_spec(qd.shape), _const_spec(kd.shape),
                  _const_spec(cd.shape), _const_spec(ones_bd.shape)],
        out_specs=[pl.BlockSpec((rows, 2 * RET_W), lambda bi, i: (bi * nc + i, 0)),
                   pl.BlockSpec((1, N_HEADS, HEAD_DIM, HEAD_DIM), lambda bi, i: (bi, 0, 0, 0))],
        out_shape=[jax.ShapeDtypeStruct((b * l, 2 * RET_W), BF16),
                   jax.ShapeDtypeStruct((b, N_HEADS, HEAD_DIM, HEAD_DIM), F32)],
        scratch_shapes=[pltpu.VMEM((RET_W // gw, gw, gw), F32),
                        pltpu.VMEM((CHUNK, KV_W), BF16),
                        pltpu.VMEM((CHUNK, KV_W), BF16)],
        compiler_params=_cparams(2),
        name="attn_prompt",
    )(sinks, qkvg, kv, bias, bias, dm, qd, kd, cd, ones_bd)


def _ret_sample_kernel(dec_ref, tq_ref, s_ref, buf_ref, ret_ref, s_out_ref, q_scr, k_scr, v_scr, cross_scr):
    del buf_ref
    h = pl.program_id(0)
    n_t = tq_ref.shape[1]
    cd = dec_ref[h, 2 * n_t]
    for t in range(n_t):
        q_scr[t] = tq_ref[0, t].astype(F32)
        k_scr[t] = tq_ref[1, t].astype(F32) * dec_ref[h, t]
        v_scr[t] = tq_ref[2, t].astype(F32)
    cross_scr[...] = jnp.zeros_like(cross_scr)

    def per_d(d, carry):
        s_d = s_ref[d]
        upd = s_d * cd
        for t in range(n_t):
            upd = upd + k_scr[t, pl.ds(d, 1), :] * v_scr[t]
            cross_scr[t] += q_scr[t, pl.ds(d, 1), :] * s_d
        s_out_ref[d] = upd
        return carry

    lax.fori_loop(0, HEAD_DIM, per_d, 0)

    for t in range(n_t):
        o = cross_scr[t] * dec_ref[h, n_t + t]
        for t2 in range(t + 1):
            sc = jnp.sum(q_scr[t] * (tq_ref[1, t2].astype(F32)), axis=0, keepdims=True)
            o = o + (sc * dec_ref[h, 2 * n_t + 1 + t * n_t + t2]) * v_scr[t2]
        mu = jnp.mean(o, axis=0, keepdims=True)
        d0 = o - mu
        var = jnp.mean(d0 * d0, axis=0, keepdims=True)
        ret_ref[t] = (d0 * lax.rsqrt(var + LN_EPS) * tq_ref[3, t].astype(F32)).astype(ret_ref.dtype)


def _ret_sample_call(tq, state5, s_buf, layer, dec):
    _, n_t, _, b = tq.shape
    state_blk = (None, None, HEAD_DIM, HEAD_DIM, b)
    return pl.pallas_call(
        _ret_sample_kernel,
        grid=(N_HEADS,),
        in_specs=[pl.BlockSpec(memory_space=pltpu.SMEM),
                  pl.BlockSpec((4, n_t, HEAD_DIM, b), lambda h: (0, 0, h, 0)),
                  pl.BlockSpec(state_blk, lambda h: (layer, h, 0, 0, 0)),
                  pl.BlockSpec(memory_space=pl.ANY)],
        out_specs=[pl.BlockSpec((n_t, HEAD_DIM, b), lambda h: (0, h, 0)),
                   pl.BlockSpec(state_blk, lambda h: (layer, h, 0, 0, 0))],
        out_shape=[jax.ShapeDtypeStruct((n_t, RET_W, b), BF16),
                   jax.ShapeDtypeStruct(s_buf.shape, F32)],
        scratch_shapes=[pltpu.VMEM((n_t, HEAD_DIM, b), F32)] * 4,
        input_output_aliases={3: 1},
        compiler_params=_cparams(1),
        name="ret_sample",
    )(dec, tq, state5, s_buf)


def _ret_sample_table(tl):
    decay_mat, q_dec, k_dec, chunk_dec = _decay_tables(tl)
    return jnp.concatenate([k_dec.T, q_dec.T, chunk_dec[:, None], decay_mat.reshape(N_HEADS, tl * tl)], axis=1)


def _swa_sample_kernel(sink_ref, qkvg_ref, kv_ref, ck_ref, cv_ref, bias_ref, swa_ref, ck_out_ref, cv_out_ref):
    tb = ck_ref.shape[0]
    tl = qkvg_ref.shape[0] // tb
    nt = (((1,), (1,)), ((), ()))
    pairs_per_kv = N_HEADS // KV_HEADS // 2
    pad = jnp.zeros((CHUNK - tl, KV_W), F32)

    wave1 = []
    for e in range(tb):
        rows = slice(e * tl, (e + 1) * tl)
        sq32 = qkvg_ref[rows, 4 * RET_W:5 * RET_W].astype(F32)

        k_new = kv_ref[rows, 0:KV_W]
        v_new = kv_ref[rows, KV_W:2 * KV_W]
        k_all = jnp.concatenate([ck_ref[e].astype(BF16), jnp.concatenate([k_new, pad], axis=0).astype(BF16)], axis=0)
        v_all = jnp.concatenate([cv_ref[e].astype(BF16), jnp.concatenate([v_new, pad], axis=0).astype(BF16)], axis=0)
        swa_ops = _swa_operands(k_all, v_all, False)
        logits = []
        for g in range(KV_HEADS):
            lhs = jnp.concatenate([sq32[:, (g * pairs_per_kv + pp) * LANES:(g * pairs_per_kv + pp + 1) * LANES]
                                   for pp in range(pairs_per_kv)], axis=0).astype(BF16)
            logits.append(lax.dot_general(lhs, swa_ops[g][0], nt, preferred_element_type=F32))
        ck_out_ref[e, 0:WINDOW - tl, :] = ck_ref[e, tl:WINDOW, :]
        cv_out_ref[e, 0:WINDOW - tl, :] = cv_ref[e, tl:WINDOW, :]
        ck_out_ref[e, WINDOW - tl:WINDOW, :] = k_new
        cv_out_ref[e, WINDOW - tl:WINDOW, :] = v_new
        wave1.append((swa_ops, logits))

    soft_all = []
    for e in range(tb):
        _, logits = wave1[e]
        soft = []
        for g in range(KV_HEADS):
            per_pair = [_swa_softmax(logits[g][pp * tl:(pp + 1) * tl, :], g * pairs_per_kv + pp, bias_ref.at[0],
                                     sink_ref, True) for pp in range(pairs_per_kv)]
            soft.append((jnp.concatenate([x[0] for x in per_pair], axis=0).astype(BF16),
                         jnp.concatenate([x[1] for x in per_pair], axis=0)))
        soft_all.append(soft)
    swa_rows = []
    for e in range(tb):
        swa_ops, soft = wave1[e][0], soft_all[e]
        outs = []
        for g in range(KV_HEADS):
            oo = _swa_values(soft[g][0], soft[g][1], swa_ops[g][1])
            outs += [oo[pp * tl:(pp + 1) * tl, :] for pp in range(pairs_per_kv)]
        swa_rows.append(jnp.concatenate(outs, axis=1))
    swa_ref[...] = jnp.concatenate(swa_rows, axis=0).astype(BF16)


def _swa_sample_call(qkvg, kv, ck2, cv2, layer, bias, sinks, b, tl, tb):
    batch3 = lambda i: (i, 0, 0)
    layer4 = lambda i: (layer, i, 0, 0)
    swa, ck_out, cv_out = pl.pallas_call(
        _swa_sample_kernel,
        grid=(b // tb,),
        in_specs=[pl.BlockSpec(memory_space=pltpu.SMEM),
                  pl.BlockSpec((tb * tl, QKVG_W), lambda i: (i, 0)),
                  pl.BlockSpec((tb * tl, 2 * KV_W), lambda i: (i, 0)),
                  pl.BlockSpec((None, tb, WINDOW, KV_W), layer4),
                  pl.BlockSpec((None, tb, WINDOW, KV_W), layer4),
                  pl.BlockSpec((1, N_HEADS, CHUNK, 2 * CHUNK), lambda i: (0, 0, 0, 0))],
        out_specs=[pl.BlockSpec((tb * tl, RET_W), lambda i: (i, 0)),
                   pl.BlockSpec((tb, WINDOW, KV_W), batch3),
                   pl.BlockSpec((tb, WINDOW, KV_W), batch3)],
        out_shape=[jax.ShapeDtypeStruct((b * tl, RET_W), BF16),
                   jax.ShapeDtypeStruct((b, WINDOW, KV_W), F32),
                   jax.ShapeDtypeStruct((b, WINDOW, KV_W), F32)],
        compiler_params=_cparams(1),
        name="swa_sample",
    )(sinks, qkvg, kv, ck2, cv2, bias)
    return swa, ck_out.reshape(b, WINDOW, KV_HEADS, HEAD_DIM), cv_out.reshape(b, WINDOW, KV_HEADS, HEAD_DIM)


def _layer_norm(x, g, b):
    mu = jnp.mean(x, axis=-1, keepdims=True)
    xc = x - mu
    var = jnp.mean(xc * xc, axis=-1, keepdims=True)
    return xc * lax.rsqrt(var + LN_EPS) * g + b


def _ffn_kernel(alpha, carry_rows, mix_ref, x_ref, g1_ref, sh2_ref, sc2_ref, g2_ref, wout_ref, ln1g_ref, ln1b_ref,
                wup_ref, cw_ref, cb_ref, prev_ref, wdn_ref, ln2g_ref, ln2b_ref, xo_ref, tail_ref, carry_scr):
    tb, tl, _ = x_ref.shape
    tm = tb * tl
    sm = FF_ROWS
    n_sub = tm // sm
    sb, sl = (1, sm) if carry_rows else (tb // n_sub, tl)
    j = pl.program_id(1)

    def seq(ref, s):
        if carry_rows:
            return ref[:, s * sm:(s + 1) * sm, :] if ref.shape[1] == tl else ref[...]
        return ref[s * sb:(s + 1) * sb]

    if carry_rows:
        @pl.when(j == 0)
        def _():
            carry_scr[carry_rows - 2:carry_rows, :] = prev_ref[0]

    fix_rows = 8 if carry_rows else sm
    t_idx = lax.broadcasted_iota(jnp.int32, (fix_rows, 1), 0) % sl
    is_t0 = t_idx == 0
    is_t1 = t_idx == 1
    n_chunks = D_FF // FF_CHUNK

    def up_cols(c, half):
        return slice(half * D_FF + c * FF_CHUNK, half * D_FF + (c + 1) * FF_CHUNK)

    def pre(s):
        x1 = alpha * seq(x_ref, s) + seq(g1_ref, s) * y[s].reshape(sb, sl, D_MODEL)
        x1 = _layer_norm(x1, ln1g_ref[...], ln1b_ref[...])
        h2 = (x1 * (1.0 + seq(sc2_ref, s)) + seq(sh2_ref, s)).reshape(sm, D_MODEL).astype(BF16)
        return x1, h2

    def up_dots(h2, c):
        return [jnp.dot(h2, wup_ref[:, up_cols(c, half)], preferred_element_type=F32) for half in range(2)]

    def conv(s, up, cols):
        if carry_rows:
            src = carry_scr if s == 0 else tail_ref
            p0 = src[carry_rows - 2:carry_rows - 1, cols]
            p1 = src[carry_rows - 1:carry_rows, cols]
            tail_ref[:, cols] = up[sm - carry_rows:sm, :]
        else:
            p0 = jnp.broadcast_to(seq(prev_ref, s)[:, 0:1, cols], (sb, sl, FF_CHUNK)).reshape(sm, FF_CHUNK)
            p1 = jnp.broadcast_to(seq(prev_ref, s)[:, 1:2, cols], (sb, sl, FF_CHUNK)).reshape(sm, FF_CHUNK)
            tail_ref[s * sb:(s + 1) * sb, :, cols] = up.reshape(sb, sl, FF_CHUNK)[:, sl - 2:sl, :]
        r1 = pltpu.roll(up, 1, 0)
        r2 = pltpu.roll(up, 2, 0)
        s1 = jnp.where(is_t0, p1, r1[0:fix_rows])
        s2 = jnp.where(is_t0, p0, jnp.where(is_t1, p1, r2[0:fix_rows]))
        if fix_rows < sm:
            s1 = jnp.concatenate([s1, r1[fix_rows:]], axis=0)
            s2 = jnp.concatenate([s2, r2[fix_rows:]], axis=0)
        return cb_ref[:, cols] + s2 * cw_ref[0:1, cols] + s1 * cw_ref[1:2, cols] + up * cw_ref[2:3, cols]

    y = [jnp.dot(mix_ref[s * sm:(s + 1) * sm, :], wout_ref[...], preferred_element_type=F32) for s in range(n_sub)]
    items = [(s, c) for s in range(n_sub) for c in range(n_chunks)]
    staged, ups = {}, {}

    def issue_up(k):
        s, c = items[k]
        if c == 0:
            staged[s] = pre(s)
        ups[k] = up_dots(staged[s][1], c)

    for k in range(min(FF_LOOKAHEAD, len(items))):
        issue_up(k)
    acc = None
    for k, (s, c) in enumerate(items):
        if k + FF_LOOKAHEAD < len(items):
            issue_up(k + FF_LOOKAHEAD)
        up_a, up_b = ups.pop(k)
        ua = conv(s, up_a, up_cols(c, 0))
        ub = conv(s, up_b, up_cols(c, 1))
        gated = (ua * jax.nn.sigmoid(ua) * ub).astype(BF16)
        down = jnp.dot(gated, wdn_ref[c * FF_CHUNK:(c + 1) * FF_CHUNK, :], preferred_element_type=F32)
        acc = down if c == 0 else acc + down
        if c == n_chunks - 1:
            x1, _ = staged.pop(s)
            x2 = alpha * x1 + seq(g2_ref, s) * acc.reshape(sb, sl, D_MODEL)
            out = _layer_norm(x2, ln2g_ref[...], ln2b_ref[...])
            if carry_rows:
                xo_ref[:, s * sm:(s + 1) * sm, :] = out
            else:
                xo_ref[s * sb:(s + 1) * sb] = out

    if carry_rows:
        carry_scr[...] = tail_ref[...]


def _ffn_call(mix, x, mod, conv_prev, w_out, ln1_g, ln1_b, w_up, conv_w, conv_b, w_down, ln2_g, ln2_b,
              layer, alpha, tb, tl):
    prev_layer = layer if conv_prev.shape[0] > 1 else 0
    b, l, _ = x.shape
    nj = l // tl
    tm = tb * tl
    carry_rows = 8 if tb == 1 else 0
    assert carry_rows or (nj == 1 and tl == 8)
    if carry_rows:
        tail_spec = pl.BlockSpec((8, UP_W), lambda i, j: (i, 0))
        tail_shape = jax.ShapeDtypeStruct((b * 8, UP_W), F32)
    else:
        tail_spec = pl.BlockSpec((tb, 2, UP_W), lambda i, j: (i, 0, 0))
        tail_shape = jax.ShapeDtypeStruct((b, 2, UP_W), F32)
    kern = functools.partial(_ffn_kernel, alpha, carry_rows)
    return pl.pallas_call(
        kern,
        grid=(b // tb, nj),
        in_specs=[pl.BlockSpec((tm, 2 * RET_W), lambda i, j: (i * nj + j, 0)),
                  pl.BlockSpec((tb, tl, D_MODEL), lambda i, j: (i, j, 0)),
                  _mod_spec(tb, layer, 2), _mod_spec(tb, layer, 3), _mod_spec(tb, layer, 4), _mod_spec(tb, layer, 5),
                  _layer_spec((2 * RET_W, D_MODEL), layer),
                  _layer_spec((1, D_MODEL), layer), _layer_spec((1, D_MODEL), layer),
                  _layer_spec((D_MODEL, UP_W), layer),
                  _layer_spec((3, UP_W), layer), _layer_spec((1, UP_W), layer),
                  pl.BlockSpec((None, tb, 2, UP_W), lambda i, j: (prev_layer, i, 0, 0)),
                  _layer_spec((D_FF, D_MODEL), layer),
                  _layer_spec((1, D_MODEL), layer), _layer_spec((1, D_MODEL), layer)],
        out_specs=[pl.BlockSpec((tb, tl, D_MODEL), lambda i, j: (i, j, 0)),
                   tail_spec],
        out_shape=[jax.ShapeDtypeStruct((b, l, D_MODEL), F32), tail_shape],
        scratch_shapes=[pltpu.VMEM((8, UP_W), F32)],
        compiler_params=_cparams(2),
        name="ffn",
    )(mix, x, mod, mod, mod, mod, w_out, ln1_g, ln1_b, w_up, conv_w, conv_b, conv_prev, w_down, ln2_g, ln2_b)


def kernel(x_prompt, x_sample, c_prompt, c_sample, state_ret, cache_swa_k, cache_swa_v, state_conv, rel_bias, w_ada, b_ada, w_in, swa_sinks, w_out, ln1_g, ln1_b, w_up, conv_w, conv_b, w_down, ln2_g, ln2_b):
    depth = w_ada.shape[0]
    bp, lp, _ = x_prompt.shape
    bs, ls, _ = x_sample.shape
    alpha = (2.0 * depth) ** 0.25
    tl_proj = 512
    tl_p = 2 * FF_ROWS
    tb_s = FF_ROWS // ls
    tb_attn = 8
    blk_attn = 4

    c_all = jnp.concatenate([c_prompt, c_sample], axis=0)
    mod_all = _ada_call(c_all, w_ada, b_ada)
    bias = _bias_call(rel_bias)

    cos_p, sin_p = _rope_tables(jnp.arange(lp, dtype=jnp.int32))
    cos_s, sin_s = _rope_tables(PAST_LEN + jnp.arange(ls, dtype=jnp.int32))
    cos_s, sin_s = jnp.tile(cos_s, (tb_s, 1)), jnp.tile(sin_s, (tb_s, 1))
    tab_p = _prompt_tables()
    dec_s = _ret_sample_table(ls)
    conv0 = jnp.zeros((1, bp, 2, UP_W), F32)

    w_in_b = w_in.astype(BF16)
    vec = lambda a: a.reshape(depth, 1, a.shape[-1])
    ffn_w = (w_out.astype(BF16), vec(ln1_g), vec(ln1_b), w_up.astype(BF16), conv_w, vec(conv_b),
             w_down.astype(BF16), vec(ln2_g), vec(ln2_b))
    mod_p = mod_all[:, :bp].reshape(depth, bp, 1, 6 * D_MODEL)
    mod_s = mod_all[:, bp:].reshape(depth, bs, 1, 6 * D_MODEL)
    state5 = jnp.transpose(state_ret, (0, 2, 3, 4, 1))
    s_buf = jnp.zeros(state5.shape, F32)
    ck2 = cache_swa_k.reshape(depth, bs, WINDOW, KV_W)
    cv2 = cache_swa_v.reshape(depth, bs, WINDOW, KV_W)

    xp, xs = x_prompt, x_sample
    p_ret, p_k, p_v, p_conv = [], [], [], []
    s_ret, s_k, s_v, s_conv = [], [], [], []
    for l in range(depth):
        qkvg, kv = _proj_call(xp, mod_p, w_in_b, l, cos_p, sin_p, 1, tl_proj)
        mix, r_p = _attn_prompt_call(qkvg, kv, bias, swa_sinks[l], tab_p, bp, lp, blk_attn)
        xp, tail = _ffn_call(mix, xp, mod_p, conv0, *ffn_w, l, alpha, 1, tl_p)
        kv3 = kv.reshape(bp, lp, 2, KV_HEADS, HEAD_DIM)
        p_ret.append(r_p)
        p_k.append(kv3[:, lp - WINDOW:, 0])
        p_v.append(kv3[:, lp - WINDOW:, 1])
        p_conv.append(tail.reshape(bp, 8, UP_W)[:, 6:8])

        qkvg, kv = _proj_call(xs, mod_s, w_in_b, l, cos_s, sin_s, tb_s, ls)
        tq = jnp.transpose(qkvg.reshape(bs, ls, 5, RET_W)[:, :, 0:4], (2, 1, 3, 0))
        ret_t, s_buf = _ret_sample_call(tq, state5, s_buf, l, dec_s)
        ret = jnp.transpose(ret_t, (2, 0, 1)).reshape(bs * ls, RET_W)
        swa, k_s, v_s = _swa_sample_call(qkvg, kv, ck2, cv2, l, bias, swa_sinks[l], bs, ls, tb_attn)
        mix = jnp.concatenate([ret, swa], axis=1)
        xs, tail = _ffn_call(mix, xs, mod_s, state_conv, *ffn_w, l, alpha, tb_s, ls)
        s_k.append(k_s)
        s_v.append(v_s)
        s_conv.append(tail)

    return (xp, xs,
            jnp.stack(p_ret), jnp.stack(p_k), jnp.stack(p_v), jnp.stack(p_conv),
            jnp.transpose(s_buf, (0, 4, 1, 2, 3)), jnp.stack(s_k), jnp.stack(s_v), jnp.stack(s_conv))
```

```python
import functools
import math

import jax
import jax.numpy as jnp
import numpy as np
from jax import lax
from jax.experimental import pallas as pl
from jax.experimental.pallas import tpu as pltpu

F32 = jnp.float32
BF16 = jnp.bfloat16

D_MODEL = 1024
HEAD_DIM = 64
N_HEADS = 8
RET_W = N_HEADS * HEAD_DIM
KV_HEADS = 2
KV_W = KV_HEADS * HEAD_DIM
D_FF = 2816
UP_W = 2 * D_FF
IN_COLS = 4 * RET_W + RET_W + 2 * KV_W
QKVG_W = 5 * RET_W
WINDOW = 128
CHUNK = 128
NUM_BUCKETS = 32
PAST_LEN = 8192
ROPE_BASE = 10000.0
LN_EPS = 1e-5
NEG_INF = -1e30
LANES = 128
FF_CHUNK = 256
FF_LOOKAHEAD = 2
FF_ROWS = 256
VMEM_LIMIT = 56 * 1024 * 1024


def _cparams(n_axes):
    return pltpu.CompilerParams(dimension_semantics=("arbitrary",) * n_axes,
                                vmem_limit_bytes=VMEM_LIMIT)


def _const_spec(shape):
    nd = len(shape)
    return pl.BlockSpec(shape, lambda *_: (0,) * nd, pipeline_mode=pl.Buffered(1))


def _rope_tables(pos):
    half = HEAD_DIM // 2
    inv = 1.0 / (ROPE_BASE ** (jnp.arange(half, dtype=F32) / half))
    ang = pos.astype(F32)[:, None] * inv[None, :]
    cos = jnp.cos(ang)
    sin = jnp.sin(ang)
    cos_h = jnp.concatenate([cos, cos], axis=-1)
    sin_h = jnp.concatenate([-sin, sin], axis=-1)
    return jnp.tile(cos_h, (1, N_HEADS)), jnp.tile(sin_h, (1, N_HEADS))


def _decay_tables(chunk):
    log_g = jnp.log(1.0 - 2.0 ** (-5.0 - jnp.arange(N_HEADS, dtype=F32)))
    idx = jnp.arange(chunk)
    diff = idx[:, None] - idx[None, :]
    decay_mat = jnp.where(diff[None] >= 0,
                          jnp.exp(log_g[:, None, None] * jnp.maximum(diff, 0)[None].astype(F32)), 0.0)
    q_dec = jnp.exp(log_g[None, :] * (idx[:, None] + 1).astype(F32))
    k_dec = jnp.exp(log_g[None, :] * (chunk - 1 - idx[:, None]).astype(F32))
    chunk_dec = jnp.exp(log_g * chunk)
    return decay_mat, q_dec, k_dec, chunk_dec


def _bucket_table(n_q, n_k):
    i = np.arange(n_q)[:, None]
    j = np.arange(n_k)[None, :]
    dist = i - j + WINDOW
    n = np.maximum(dist, 0)
    max_exact = NUM_BUCKETS // 2
    nf = np.maximum(n, max_exact).astype(np.float64)
    large = max_exact + (np.log(nf / max_exact) / math.log(WINDOW / max_exact)
                         * (NUM_BUCKETS - max_exact)).astype(np.int32)
    large = np.minimum(large, NUM_BUCKETS - 1)
    bucket = np.where(n < max_exact, n, large).astype(np.int32)
    valid = ((dist >= 0) & (dist < WINDOW)).astype(np.int32)
    return bucket, valid


def _ada_kernel(c_ref, w_ref, b_ref, o_ref):
    c = c_ref[...]
    s = (c * jax.nn.sigmoid(c)).astype(BF16)
    o_ref[0] = jnp.dot(s, w_ref[0].astype(BF16), preferred_element_type=F32) + b_ref[0]


def _ada_call(c_all, w_ada, b_ada):
    depth = w_ada.shape[0]
    n_rows = c_all.shape[0]
    tn = 1536
    return pl.pallas_call(
        _ada_kernel,
        grid=(depth, 6 * D_MODEL // tn),
        in_specs=[pl.BlockSpec((n_rows, D_MODEL), lambda l, n: (0, 0)),
                  pl.BlockSpec((1, D_MODEL, tn), lambda l, n: (l, 0, n)),
                  pl.BlockSpec((1, 1, tn), lambda l, n: (l, 0, n))],
        out_specs=pl.BlockSpec((1, n_rows, tn), lambda l, n: (l, 0, n)),
        out_shape=jax.ShapeDtypeStruct((depth, n_rows, 6 * D_MODEL), F32),
        compiler_params=_cparams(2),
        name="ada",
    )(c_all, w_ada, b_ada.reshape(depth, 1, 6 * D_MODEL))


def _bias_kernel(rb_ref, bucket_ref, valid_ref, o_ref):
    bucket = bucket_ref[...]
    valid = valid_ref[...] > 0
    first_ok = lax.broadcasted_iota(jnp.int32, bucket.shape, 1) >= WINDOW
    for h in range(N_HEADS):
        acc = jnp.zeros(bucket.shape, F32)
        for b in range(NUM_BUCKETS):
            acc = jnp.where(bucket == b, rb_ref[b, h], acc)
        o_ref[0, h] = jnp.where(valid, acc, NEG_INF)
        o_ref[1, h] = jnp.where(valid & first_ok, acc, NEG_INF)


def _bias_call(rel_bias):
    bucket, valid = _bucket_table(CHUNK, 2 * CHUNK)
    return pl.pallas_call(
        _bias_kernel,
        in_specs=[pl.BlockSpec(memory_space=pltpu.SMEM),
                  pl.BlockSpec((CHUNK, 2 * CHUNK), lambda: (0, 0)),
                  pl.BlockSpec((CHUNK, 2 * CHUNK), lambda: (0, 0))],
        out_specs=pl.BlockSpec((2, N_HEADS, CHUNK, 2 * CHUNK), lambda: (0, 0, 0, 0)),
        out_shape=jax.ShapeDtypeStruct((2, N_HEADS, CHUNK, 2 * CHUNK), F32),
        name="swa_bias",
    )(rel_bias, jnp.asarray(bucket), jnp.asarray(valid))


def _swap_halves(x):
    lane = lax.broadcasted_iota(jnp.int32, (1, LANES), 1)
    first = (lane % HEAD_DIM) < (HEAD_DIM // 2)
    cols = []
    for c in range(x.shape[1] // LANES):
        xc = x[:, c * LANES:(c + 1) * LANES]
        cols.append(jnp.where(first, pltpu.roll(xc, LANES - HEAD_DIM // 2, 1), pltpu.roll(xc, HEAD_DIM // 2, 1)))
    return jnp.concatenate(cols, axis=1)


def _proj_kernel(x_ref, sc_ref, sh_ref, w_ref, cos_ref, sin_ref, qkvg_ref, kv_ref):
    tb, tl, _ = x_ref.shape
    tm = tb * tl
    h = x_ref[...] * (1.0 + sc_ref[...]) + sh_ref[...]
    h = h.reshape(tm, D_MODEL).astype(BF16)
    proj = jnp.dot(h, w_ref[...], preferred_element_type=F32)
    cos = cos_ref[...]
    sin = sin_ref[...]
    rq = proj[:, 0:RET_W]
    rk = proj[:, RET_W:2 * RET_W]
    rq = rq * cos + _swap_halves(rq) * sin
    rk = (rk * cos + _swap_halves(rk) * sin) * (HEAD_DIM ** -0.5)
    rg = proj[:, 3 * RET_W:4 * RET_W]
    qkvg_ref[:, 0:RET_W] = rq.astype(BF16)
    qkvg_ref[:, RET_W:2 * RET_W] = rk.astype(BF16)
    qkvg_ref[:, 2 * RET_W:3 * RET_W] = proj[:, 2 * RET_W:3 * RET_W].astype(BF16)
    qkvg_ref[:, 3 * RET_W:4 * RET_W] = (rg * jax.nn.sigmoid(rg)).astype(BF16)
    qkvg_ref[:, 4 * RET_W:5 * RET_W] = (proj[:, 4 * RET_W:5 * RET_W] * (HEAD_DIM ** -0.5)).astype(BF16)
    kv_ref[...] = proj[:, 5 * RET_W:IN_COLS]


def _layer_spec(shape, layer):
    nd = len(shape)
    return pl.BlockSpec((None,) + tuple(shape), lambda *_: (layer,) + (0,) * nd, pipeline_mode=pl.Buffered(1))


def _mod_spec(tb, layer, k):
    return pl.BlockSpec((None, tb, 1, D_MODEL), lambda i, j: (layer, i, 0, k))


def _proj_call(x, mod, w_in, layer, cos, sin, tb, tl):
    b, l, _ = x.shape
    nj = l // tl
    tm = tb * tl
    tab_idx = (lambda i, j: (j, 0)) if cos.shape[0] == l and nj > 1 else (lambda i, j: (0, 0))
    return pl.pallas_call(
        _proj_kernel,
        grid=(b // tb, nj),
        in_specs=[pl.BlockSpec((tb, tl, D_MODEL), lambda i, j: (i, j, 0)),
                  _mod_spec(tb, layer, 1),
                  _mod_spec(tb, layer, 0),
                  _layer_spec((D_MODEL, IN_COLS), layer),
                  pl.BlockSpec((tm, RET_W), tab_idx),
                  pl.BlockSpec((tm, RET_W), tab_idx)],
        out_specs=[pl.BlockSpec((tm, QKVG_W), lambda i, j: (i * nj + j, 0)),
                   pl.BlockSpec((tm, 2 * KV_W), lambda i, j: (i * nj + j, 0))],
        out_shape=[jax.ShapeDtypeStruct((b * l, QKVG_W), BF16),
                   jax.ShapeDtypeStruct((b * l, 2 * KV_W), F32)],
        compiler_params=_cparams(2),
        name="proj",
    )(x, mod, mod, w_in, cos, sin)


def _lane_mask_low():
    return lax.broadcasted_iota(jnp.int32, (1, LANES), 1) < HEAD_DIM


def _head_norm(o, ones_bd):
    outs = []
    gw = ones_bd.shape[0]
    for g in range(RET_W // gw):
        og = o[:, g * gw:(g + 1) * gw]
        mu = jnp.dot(og.astype(BF16), ones_bd, preferred_element_type=F32)
        d = og - mu
        var = jnp.dot((d * d).astype(BF16), ones_bd, preferred_element_type=F32)
        outs.append(d * lax.rsqrt(var + LN_EPS))
    return jnp.concatenate(outs, axis=1)


def _swa_operands(k_all, v_all, sum_columns):
    low = _lane_mask_low()
    zero = jnp.zeros_like(k_all)
    ones_low = jnp.broadcast_to(jnp.where(low, 1.0, 0.0), k_all.shape)
    ones_cols = jnp.concatenate([ones_low, 1.0 - ones_low], axis=0).astype(BF16)
    k_rot = pltpu.roll(k_all, HEAD_DIM, 1)
    v_rot = pltpu.roll(v_all, HEAD_DIM, 1)
    ops = []
    for g in range(KV_HEADS):
        k_lo, k_hi = (k_all, k_rot) if g == 0 else (k_rot, k_all)
        v_lo, v_hi = (v_all, v_rot) if g == 0 else (v_rot, v_all)
        kc = jnp.concatenate([jnp.where(low, k_lo, zero), jnp.where(low, zero, k_hi)], axis=0)
        vc = jnp.concatenate([jnp.where(low, v_lo, zero), jnp.where(low, zero, v_hi)], axis=0)
        ops.append((kc, jnp.concatenate([vc, ones_cols], axis=1) if sum_columns else vc))
    return ops


def _swa_softmax(lg, p, bias_ref, sink_ref, with_sum):
    rows = lg.shape[0]
    low = _lane_mask_low()
    probs, dens = [], []
    for s in range(2):
        hh = 2 * p + s
        lo = lg[:, s * 2 * CHUNK:(s + 1) * 2 * CHUNK] + bias_ref[hh, 0:rows, :]
        sink = sink_ref[hh]
        m = jnp.maximum(jnp.max(lo, axis=-1, keepdims=True), sink)
        pe = jnp.exp(lo - m)
        den = jnp.exp(sink - m)
        if with_sum:
            den = den + jnp.sum(pe, axis=-1, keepdims=True)
        probs.append(pe)
        dens.append(den)
    return jnp.concatenate(probs, axis=1), jnp.where(low, dens[0], dens[1])


def _swa_values(probs, den_term, vc):
    oo = jnp.dot(probs, vc, preferred_element_type=F32)
    if vc.shape[1] == 2 * LANES:
        return oo[:, 0:LANES] / (oo[:, LANES:2 * LANES] + den_term)
    return oo / den_term


def _attn_prompt_kernel(sink_ref, qkvg_ref, kv_ref, bias_ref, bias_all_ref, dm_ref, qd_ref, kd_ref, cd_ref, ones_ref,
                        mix_ref, s_out_ref, s_scr, pk_scr, pv_scr):
    i = pl.program_id(1)

    @pl.when(i == 0)
    def _():
        s_scr[...] = jnp.zeros_like(s_scr)
        pk_scr[...] = jnp.zeros_like(pk_scr)
        pv_scr[...] = jnp.zeros_like(pv_scr)

    low = _lane_mask_low()
    zero = jnp.zeros((CHUNK, LANES), BF16)
    n_pairs = N_HEADS // 2
    pairs_per_kv = n_pairs // KV_HEADS
    gw = cd_ref.shape[1]
    n_groups = RET_W // gw
    n_blk = qkvg_ref.shape[0] // CHUNK
    nt = (((1,), (1,)), ((), ()))
    tn = (((0,), (0,)), ((), ()))
    ones_bd = ones_ref[...]

    s_cur = [s_scr[g] for g in range(n_groups)]
    k_prev, v_prev = pk_scr[...], pv_scr[...]
    wave1, mid = [], []
    for c in range(n_blk):
        rows = slice(c * CHUNK, (c + 1) * CHUNK)
        q = qkvg_ref[rows, 0:RET_W]
        k = qkvg_ref[rows, RET_W:2 * RET_W]
        v = qkvg_ref[rows, 2 * RET_W:3 * RET_W]
        sq = qkvg_ref[rows, 4 * RET_W:5 * RET_W]
        kd = (k.astype(F32) * kd_ref[...]).astype(BF16)
        ret_ops = []
        for p in range(n_pairs):
            kp = k[:, p * LANES:(p + 1) * LANES]
            vp = v[:, p * LANES:(p + 1) * LANES]
            ret_ops.append((jnp.concatenate([jnp.where(low, kp, zero), jnp.where(low, zero, kp)], axis=0),
                            jnp.concatenate([jnp.where(low, vp, zero), jnp.where(low, zero, vp)], axis=0)))
        k_new = kv_ref[rows, 0:KV_W].astype(BF16)
        v_new = kv_ref[rows, KV_W:2 * KV_W].astype(BF16)
        swa_ops = _swa_operands(jnp.concatenate([k_prev, k_new], axis=0), jnp.concatenate([v_prev, v_new], axis=0),
                                True)
        k_prev, v_prev = k_new, v_new

        scores = [lax.dot_general(q[:, p * LANES:(p + 1) * LANES], ret_ops[p][0], nt, preferred_element_type=F32)
                  for p in range(n_pairs)]
        cross = [jnp.dot(q[:, g * gw:(g + 1) * gw], s_cur[g].astype(BF16), preferred_element_type=F32)
                 for g in range(n_groups)]
        upd = [lax.dot_general(kd[:, g * gw:(g + 1) * gw], v[:, g * gw:(g + 1) * gw], tn,
                               preferred_element_type=F32) for g in range(n_groups)]
        logits = [lax.dot_general(
            jnp.concatenate([sq[:, (g * pairs_per_kv + pp) * LANES:(g * pairs_per_kv + pp + 1) * LANES]
                             for pp in range(pairs_per_kv)], axis=0),
            swa_ops[g][0], nt, preferred_element_type=F32) for g in range(KV_HEADS)]
        s_cur = [s_cur[g] * cd_ref[g] + jnp.where(cd_ref[g] > 0.0, upd[g], 0.0) for g in range(n_groups)]
        wave1.append((scores, cross, logits, ret_ops, swa_ops))
    for g in range(n_groups):
        s_scr[g] = s_cur[g]
    pk_scr[...] = k_prev
    pv_scr[...] = v_prev

    for c in range(n_blk):
        scores, cross, logits, ret_ops, swa_ops = wave1[c]
        bias_c = bias_ref.at[0] if c == 0 else bias_all_ref.at[0]
        scores = [(scores[p] * dm_ref[p]).astype(BF16) for p in range(n_pairs)]
        soft = [[_swa_softmax(logits[g][pp * CHUNK:(pp + 1) * CHUNK, :], g * pairs_per_kv + pp, bias_c, sink_ref,
                              False) for pp in range(pairs_per_kv)] for g in range(KV_HEADS)]
        intra = [jnp.dot(scores[p], ret_ops[p][1], preferred_element_type=F32) for p in range(n_pairs)]
        swa = []
        for g in range(KV_HEADS):
            oo = _swa_values(jnp.concatenate([x[0].astype(BF16) for x in soft[g]], axis=0),
                             jnp.concatenate([x[1] for x in soft[g]], axis=0), swa_ops[g][1])
            swa += [oo[pp * CHUNK:(pp + 1) * CHUNK, :] for pp in range(pairs_per_kv)]
        o = jnp.concatenate(intra, axis=1) + jnp.concatenate(cross, axis=1) * qd_ref[...]
        mid.append((o, swa))

    o = jnp.concatenate([m[0] for m in mid], axis=0)
    ret = _head_norm(o, ones_bd) * qkvg_ref[:, 3 * RET_W:4 * RET_W].astype(F32)
    mix_ref[:, 0:RET_W] = ret.astype(BF16)
    for c in range(n_blk):
        mix_ref[c * CHUNK:(c + 1) * CHUNK, RET_W:2 * RET_W] = jnp.concatenate(mid[c][1], axis=1).astype(BF16)

    @pl.when(i == pl.num_programs(1) - 1)
    def _():
        for h in range(N_HEADS):
            g, hl = divmod(h, gw // HEAD_DIM)
            s_out_ref[0, h] = s_scr[g, hl * HEAD_DIM:(hl + 1) * HEAD_DIM, hl * HEAD_DIM:(hl + 1) * HEAD_DIM]


def _prompt_tables():
    decay_mat, q_dec, k_dec, chunk_dec = _decay_tables(CHUNK)
    dm = jnp.concatenate([decay_mat[0::2], decay_mat[1::2]], axis=2)
    qd = jnp.repeat(q_dec, HEAD_DIM, axis=1)
    kd = jnp.repeat(k_dec, HEAD_DIM, axis=1)
    gw = 4 * HEAD_DIM
    blk = np.kron(np.eye(4, dtype=np.float32), np.ones((HEAD_DIM, HEAD_DIM), np.float32))
    cd = jnp.repeat(chunk_dec.reshape(2, 4), HEAD_DIM, axis=1)[:, :, None] * jnp.asarray(blk)[None]
    ones_bd = jnp.asarray(blk / HEAD_DIM, BF16)
    assert cd.shape == (2, gw, gw)
    return dm, qd, kd, cd, ones_bd


def _attn_prompt_call(qkvg, kv, bias, sinks, tables, b, l, n_blk):
    dm, qd, kd, cd, ones_bd = tables
    rows = n_blk * CHUNK
    nc = l // rows
    gw = cd.shape[1]
    bias_spec = lambda idx: pl.BlockSpec((1, N_HEADS, CHUNK, 2 * CHUNK), idx)
    return pl.pallas_call(
        _attn_prompt_kernel,
        grid=(b, nc),
        in_specs=[pl.BlockSpec(memory_space=pltpu.SMEM),
                  pl.BlockSpec((rows, QKVG_W), lambda bi, i: (bi * nc + i, 0)),
                  pl.BlockSpec((rows, 2 * KV_W), lambda bi, i: (bi * nc + i, 0)),
                  bias_spec(lambda bi, i: (jnp.where(i == 0, 1, 0), 0, 0, 0)),
                  bias_spec(lambda bi, i: (0, 0, 0, 0)),
                  _const_spec(dm.shape), _const_spec(qd.shape), _const_spec(kd.shape),
                  _const_spec(cd.shape), _const_spec(ones_bd.shape)],
        out_specs=[pl.BlockSpec((rows, 2 * RET_W), lambda bi, i: (bi * nc + i, 0)),
                   pl.BlockSpec((1, N_HEADS, HEAD_DIM, HEAD_DIM), lambda bi, i: (bi, 0, 0, 0))],
        out_shape=[jax.ShapeDtypeStruct((b * l, 2 * RET_W), BF16),
                   jax.ShapeDtypeStruct((b, N_HEADS, HEAD_DIM, HEAD_DIM), F32)],
        scratch_shapes=[pltpu.VMEM((RET_W // gw, gw, gw), F32),
                        pltpu.VMEM((CHUNK, KV_W), BF16),
                        pltpu.VMEM((CHUNK, KV_W), BF16)],
        compiler_params=_cparams(2),
        name="attn_prompt",
    )(sinks, qkvg, kv, bias, bias, dm, qd, kd, cd, ones_bd)


def _ret_sample_kernel(dec_ref, tq_ref, s_ref, buf_ref, ret_ref, s_out_ref, q_scr, k_scr, v_scr, cross_scr):
    del buf_ref
    h = pl.program_id(0)
    n_t = tq_ref.shape[1]
    cd = dec_ref[h, 2 * n_t]
    for t in range(n_t):
        q_scr[t] = tq_ref[0, t].astype(F32)
        k_scr[t] = tq_ref[1, t].astype(F32) * dec_ref[h, t]
        v_scr[t] = tq_ref[2, t].astype(F32)
    cross_scr[...] = jnp.zeros_like(cross_scr)

    def per_d(d, carry):
        s_d = s_ref[d]
        upd = s_d * cd
        for t in range(n_t):
            upd = upd + k_scr[t, pl.ds(d, 1), :] * v_scr[t]
            cross_scr[t] += q_scr[t, pl.ds(d, 1), :] * s_d
        s_out_ref[d] = upd
        return carry

    lax.fori_loop(0, HEAD_DIM, per_d, 0)

    for t in range(n_t):
        o = cross_scr[t] * dec_ref[h, n_t + t]
        for t2 in range(t + 1):
            sc = jnp.sum(q_scr[t] * (tq_ref[1, t2].astype(F32)), axis=0, keepdims=True)
            o = o + (sc * dec_ref[h, 2 * n_t + 1 + t * n_t + t2]) * v_scr[t2]
        mu = jnp.mean(o, axis=0, keepdims=True)
        d0 = o - mu
        var = jnp.mean(d0 * d0, axis=0, keepdims=True)
        ret_ref[t] = (d0 * lax.rsqrt(var + LN_EPS) * tq_ref[3, t].astype(F32)).astype(ret_ref.dtype)


def _ret_sample_call(tq, state5, s_buf, layer, dec):
    _, n_t, _, b = tq.shape
    state_blk = (None, None, HEAD_DIM, HEAD_DIM, b)
    return pl.pallas_call(
        _ret_sample_kernel,
        grid=(N_HEADS,),
        in_specs=[pl.BlockSpec(memory_space=pltpu.SMEM),
                  pl.BlockSpec((4, n_t, HEAD_DIM, b), lambda h: (0, 0, h, 0)),
                  pl.BlockSpec(state_blk, lambda h: (layer, h, 0, 0, 0)),
                  pl.BlockSpec(memory_space=pl.ANY)],
        out_specs=[pl.BlockSpec((n_t, HEAD_DIM, b), lambda h: (0, h, 0)),
                   pl.BlockSpec(state_blk, lambda h: (layer, h, 0, 0, 0))],
        out_shape=[jax.ShapeDtypeStruct((n_t, RET_W, b), BF16),
                   jax.ShapeDtypeStruct(s_buf.shape, F32)],
        scratch_shapes=[pltpu.VMEM((n_t, HEAD_DIM, b), F32)] * 4,
        input_output_aliases={3: 1},
        compiler_params=_cparams(1),
        name="ret_sample",
    )(dec, tq, state5, s_buf)


def _ret_sample_table(tl):
    decay_mat, q_dec, k_dec, chunk_dec = _decay_tables(tl)
    return jnp.concatenate([k_dec.T, q_dec.T, chunk_dec[:, None], decay_mat.reshape(N_HEADS, tl * tl)], axis=1)


def _swa_sample_kernel(sink_ref, qkvg_ref, kv_ref, ckt_ref, cvt_ref, bias_ref, kbuf_ref, vbuf_ref,
                       swa_ref, ck_out_ref, cv_out_ref):
    del kbuf_ref, vbuf_ref
    tb = ckt_ref.shape[0]
    tl = qkvg_ref.shape[0] // tb
    nt = (((1,), (1,)), ((), ()))
    pairs_per_kv = N_HEADS // KV_HEADS // 2
    low = _lane_mask_low()
    zero_t =jnp.zeros((HEAD_DIM, WINDOW), BF16)
    front = jnp.zeros((WINDOW - tl, KV_W), F32)
    is_new =lax.broadcasted_iota(jnp.int32, (1, WINDOW), 1) >= WINDOW - tl
    bias = bias_ref.at[0]

    def block_diag(t):
        return jnp.concatenate([jnp.concatenate([t, zero_t], axis=1), jnp.concatenate([zero_t, t], axis=1)], axis=0)

    wave1 = []
    for e in range(tb):
        rows = slice(e * tl, (e + 1) * tl)
        sq32 = qkvg_ref[rows, 4 * RET_W:5 * RET_W].astype(F32)
        k_new = kv_ref[rows, 0:KV_W]
        v_new = kv_ref[rows, KV_W:2 * KV_W]
        k_rot = pltpu.roll(k_new, HEAD_DIM, 1)
        v_rot = pltpu.roll(v_new, HEAD_DIM, 1)
        k_placed = jnp.concatenate([front, k_new], axis=0).T
        v_placed = jnp.concatenate([front, v_new], axis=0).T
        per_g = []
        for g in range(KV_HEADS):
            kt = ckt_ref[e, g]
            vt = cvt_ref[e, g]
            hd = slice(g * HEAD_DIM, (g + 1) * HEAD_DIM)
            ck_out_ref[e, g] = jnp.where(is_new, k_placed[hd, :], pltpu.roll(kt, WINDOW - tl, 1))
            cv_out_ref[e, g] = jnp.where(is_new, v_placed[hd, :], pltpu.roll(vt, WINDOW - tl, 1))
            lhs = jnp.concatenate([sq32[:, (g * pairs_per_kv + pp) * LANES:(g * pairs_per_kv + pp + 1) * LANES]
                                   for pp in range(pairs_per_kv)], axis=0).astype(BF16)
            kn_lo, kn_hi = (k_new, k_rot) if g == 0 else (k_rot, k_new)
            vn_lo, vn_hi = (v_new, v_rot) if g == 0 else (v_rot, v_new)
            kn = [jnp.where(low, kn_lo, 0.0).astype(BF16), jnp.where(low, 0.0, kn_hi).astype(BF16)]
            vn = [jnp.where(low, vn_lo, 0.0).astype(BF16), jnp.where(low, 0.0, vn_hi).astype(BF16)]
            lc = jnp.dot(lhs, block_diag(kt.astype(BF16)), preferred_element_type=F32)
            ln = [lax.dot_general(lhs, kn[s], nt, preferred_element_type=F32) for s in range(2)]
            per_g.append((lc, ln, vn, block_diag(vt.astype(BF16))))
        wave1.append(per_g)

    swa_rows = []
    for e in range(tb):
        outs = []
        for g in range(KV_HEADS):
            lc, ln, vn, vbd = wave1[e][g]
            pcs, pns, dens = [], [[], []], []
            for pp in range(pairs_per_kv):
                r = slice(pp * tl, (pp + 1) * tl)
                den_pair = []
                for s in range(2):
                    hh = 2 * (g * pairs_per_kv + pp) + s
                    lo_c = lc[r, s * WINDOW:(s + 1) * WINDOW] + bias[hh, 0:tl, 0:WINDOW]
                    lo_n = ln[s][r, :] + bias[hh, 0:tl, WINDOW:WINDOW + tl]
                    sink = sink_ref[hh]
                    m = jnp.maximum(jnp.maximum(jnp.max(lo_c, axis=-1, keepdims=True),
                                                jnp.max(lo_n, axis=-1, keepdims=True)), sink)
                    pc = jnp.exp(lo_c - m)
                    pn = jnp.exp(lo_n - m)
                    den_pair.append(jnp.sum(pc, axis=-1, keepdims=True) + jnp.sum(pn, axis=-1, keepdims=True)
                                    + jnp.exp(sink - m))
                    pcs.append((pp, pc))
                    pns[s].append(pn)
                dens.append(jnp.where(low, den_pair[0], den_pair[1]))
            pc_rows = [jnp.concatenate([x[1] for x in pcs if x[0] == pp], axis=1) for pp in range(pairs_per_kv)]
            pc_all = jnp.concatenate(pc_rows, axis=0).astype(BF16)
            oo = lax.dot_general(pc_all, vbd, nt, preferred_element_type=F32)
            for s in range(2):
                oo = oo + jnp.dot(jnp.concatenate(pns[s], axis=0).astype(BF16), vn[s], preferred_element_type=F32)
            oo = oo / jnp.concatenate(dens, axis=0)
            outs += [oo[pp * tl:(pp + 1) * tl, :] for pp in range(pairs_per_kv)]
        swa_rows.append(jnp.concatenate(outs, axis=1))
    swa_ref[...] = jnp.concatenate(swa_rows, axis=0).astype(BF16)


def _swa_sample_call(qkvg, kv, ckt, cvt, kbuf, vbuf, layer, bias, sinks, b, tl, tb):
    cache_blk = (None, tb, KV_HEADS, HEAD_DIM, WINDOW)
    layer5 = lambda i: (layer, i, 0, 0, 0)
    return pl.pallas_call(
        _swa_sample_kernel,
        grid=(b // tb,),
        in_specs=[pl.BlockSpec(memory_space=pltpu.SMEM),
                  pl.BlockSpec((tb * tl, QKVG_W), lambda i: (i, 0)),
                  pl.BlockSpec((tb * tl, 2 * KV_W), lambda i: (i, 0)),
                  pl.BlockSpec(cache_blk, layer5),
                  pl.BlockSpec(cache_blk, layer5),
                  pl.BlockSpec((1, N_HEADS, CHUNK, 2 * CHUNK), lambda i: (0, 0, 0, 0)),
                  pl.BlockSpec(memory_space=pl.ANY),
                  pl.BlockSpec(memory_space=pl.ANY)],
        out_specs=[pl.BlockSpec((tb * tl, RET_W), lambda i: (i, 0)),
                   pl.BlockSpec(cache_blk, layer5),
                   pl.BlockSpec(cache_blk, layer5)],
        out_shape=[jax.ShapeDtypeStruct((b * tl, RET_W), BF16),
                   jax.ShapeDtypeStruct(kbuf.shape, F32),
                   jax.ShapeDtypeStruct(vbuf.shape, F32)],
        input_output_aliases={6: 1, 7: 2},
        compiler_params=_cparams(1),
        name="swa_sample",
    )(sinks, qkvg, kv, ckt, cvt, bias, kbuf, vbuf)


def _layer_norm(x, g, b):
    mu = jnp.mean(x, axis=-1, keepdims=True)
    xc = x - mu
    var = jnp.mean(xc * xc, axis=-1, keepdims=True)
    return xc * lax.rsqrt(var + LN_EPS) * g + b


def _ffn_kernel(alpha, carry_rows, mix_ref, x_ref, g1_ref, sh2_ref, sc2_ref, g2_ref, wout_ref, ln1g_ref, ln1b_ref,
                wup_ref, cw_ref, cb_ref, prev_ref, wdn_ref, ln2g_ref, ln2b_ref, xo_ref, tail_ref, carry_scr):
    tb, tl, _ = x_ref.shape
    tm = tb * tl
    sm = FF_ROWS
    n_sub = tm // sm
    sb, sl = (1, sm) if carry_rows else (tb // n_sub, tl)
    j = pl.program_id(1)

    def seq(ref, s):
        if carry_rows:
            return ref[:, s * sm:(s + 1) * sm, :] if ref.shape[1] == tl else ref[...]
        return ref[s * sb:(s + 1) * sb]

    if carry_rows:
        @pl.when(j == 0)
        def _():
            carry_scr[carry_rows - 2:carry_rows, :] = prev_ref[0]

    fix_rows = 8 if carry_rows else sm
    t_idx = lax.broadcasted_iota(jnp.int32, (fix_rows, 1), 0) % sl
    is_t0 = t_idx == 0
    is_t1 = t_idx == 1
    n_chunks = D_FF // FF_CHUNK

    def up_cols(c, half):
        return slice(half * D_FF + c * FF_CHUNK, half * D_FF + (c + 1) * FF_CHUNK)

    def pre(s):
        x1 = alpha * seq(x_ref, s) + seq(g1_ref, s) * y[s].reshape(sb, sl, D_MODEL)
        x1 = _layer_norm(x1, ln1g_ref[...], ln1b_ref[...])
        h2 = (x1 * (1.0 + seq(sc2_ref, s)) + seq(sh2_ref, s)).reshape(sm, D_MODEL).astype(BF16)
        return x1, h2

    def up_dots(h2, c):
        return [jnp.dot(h2, wup_ref[:, up_cols(c, half)], preferred_element_type=F32) for half in range(2)]

    def conv(s, up, cols):
        if carry_rows:
            src = carry_scr if s == 0 else tail_ref
            p0 = src[carry_rows - 2:carry_rows - 1, cols]
            p1 = src[carry_rows - 1:carry_rows, cols]
            tail_ref[:, cols] = up[sm - carry_rows:sm, :]
        else:
            p0 = jnp.broadcast_to(seq(prev_ref, s)[:, 0:1, cols], (sb, sl, FF_CHUNK)).reshape(sm, FF_CHUNK)
            p1 = jnp.broadcast_to(seq(prev_ref, s)[:, 1:2, cols], (sb, sl, FF_CHUNK)).reshape(sm, FF_CHUNK)
            tail_ref[s * sb:(s + 1) * sb, :, cols] = up.reshape(sb, sl, FF_CHUNK)[:, sl - 2:sl, :]
        r1 = pltpu.roll(up, 1, 0)
        r2 = pltpu.roll(up, 2, 0)
        s1 = jnp.where(is_t0, p1, r1[0:fix_rows])
        s2 = jnp.where(is_t0, p0, jnp.where(is_t1, p1, r2[0:fix_rows]))
        if fix_rows < sm:
            s1 = jnp.concatenate([s1, r1[fix_rows:]], axis=0)
            s2 = jnp.concatenate([s2, r2[fix_rows:]], axis=0)
        return cb_ref[:, cols] + s2 * cw_ref[0:1, cols] + s1 * cw_ref[1:2, cols] + up * cw_ref[2:3, cols]

    y = [jnp.dot(mix_ref[s * sm:(s + 1) * sm, :], wout_ref[...], preferred_element_type=F32) for s in range(n_sub)]
    items = [(s, c) for s in range(n_sub) for c in range(n_chunks)]
    staged, ups = {}, {}

    def issue_up(k):
        s, c = items[k]
        if c == 0:
            staged[s] = pre(s)
        ups[k] = up_dots(staged[s][1], c)

    for k in range(min(FF_LOOKAHEAD, len(items))):
        issue_up(k)
    acc = None
    for k, (s, c) in enumerate(items):
        if k + FF_LOOKAHEAD < len(items):
            issue_up(k + FF_LOOKAHEAD)
        up_a, up_b = ups.pop(k)
        ua = conv(s, up_a, up_cols(c, 0))
        ub = conv(s, up_b, up_cols(c, 1))
        gated = (ua * jax.nn.sigmoid(ua) * ub).astype(BF16)
        down = jnp.dot(gated, wdn_ref[c * FF_CHUNK:(c + 1) * FF_CHUNK, :], preferred_element_type=F32)
        acc = down if c == 0 else acc + down
        if c == n_chunks - 1:
            x1, _ = staged.pop(s)
            x2 = alpha * x1 + seq(g2_ref, s) * acc.reshape(sb, sl, D_MODEL)
            out = _layer_norm(x2, ln2g_ref[...], ln2b_ref[...])
            if carry_rows:
                xo_ref[:, s * sm:(s + 1) * sm, :] = out
            else:
                xo_ref[s * sb:(s + 1) * sb] = out

    if carry_rows:
        carry_scr[...] = tail_ref[...]


def _ffn_call(mix, x, mod, conv_prev, w_out, ln1_g, ln1_b, w_up, conv_w, conv_b, w_down, ln2_g, ln2_b,
              layer, alpha, tb, tl):
    prev_layer = layer if conv_prev.shape[0] > 1 else 0
    b, l, _ = x.shape
    nj = l // tl
    tm = tb * tl
    carry_rows = 8 if tb == 1 else 0
    assert carry_rows or (nj == 1 and tl == 8)
    if carry_rows:
        tail_spec = pl.BlockSpec((8, UP_W), lambda i, j: (i, 0))
        tail_shape = jax.ShapeDtypeStruct((b * 8, UP_W), F32)
    else:
        tail_spec = pl.BlockSpec((tb, 2, UP_W), lambda i, j: (i, 0, 0))
        tail_shape = jax.ShapeDtypeStruct((b, 2, UP_W), F32)
    kern = functools.partial(_ffn_kernel, alpha, carry_rows)
    return pl.pallas_call(
        kern,
        grid=(b // tb, nj),
        in_specs=[pl.BlockSpec((tm, 2 * RET_W), lambda i, j: (i * nj + j, 0)),
                  pl.BlockSpec((tb, tl, D_MODEL), lambda i, j: (i, j, 0)),
                  _mod_spec(tb, layer, 2), _mod_spec(tb, layer, 3), _mod_spec(tb, layer, 4), _mod_spec(tb, layer, 5),
                  _layer_spec((2 * RET_W, D_MODEL), layer),
                  _layer_spec((1, D_MODEL), layer), _layer_spec((1, D_MODEL), layer),
                  _layer_spec((D_MODEL, UP_W), layer),
                  _layer_spec((3, UP_W), layer), _layer_spec((1, UP_W), layer),
                  pl.BlockSpec((None, tb, 2, UP_W), lambda i, j: (prev_layer, i, 0, 0)),
                  _layer_spec((D_FF, D_MODEL), layer),
                  _layer_spec((1, D_MODEL), layer), _layer_spec((1, D_MODEL), layer)],
        out_specs=[pl.BlockSpec((tb, tl, D_MODEL), lambda i, j: (i, j, 0)),
                   tail_spec],
        out_shape=[jax.ShapeDtypeStruct((b, l, D_MODEL), F32), tail_shape],
        scratch_shapes=[pltpu.VMEM((8, UP_W), F32)],
        compiler_params=_cparams(2),
        name="ffn",
    )(mix, x, mod, mod, mod, mod, w_out, ln1_g, ln1_b, w_up, conv_w, conv_b, conv_prev, w_down, ln2_g, ln2_b)


def kernel(x_prompt, x_sample, c_prompt, c_sample, state_ret, cache_swa_k, cache_swa_v, state_conv, rel_bias, w_ada, b_ada, w_in, swa_sinks, w_out, ln1_g, ln1_b, w_up, conv_w, conv_b, w_down, ln2_g, ln2_b):
    depth = w_ada.shape[0]
    bp, lp, _ = x_prompt.shape
    bs, ls, _ = x_sample.shape
    alpha = (2.0 * depth) ** 0.25
    tl_proj = 512
    tl_p = 2 * FF_ROWS
    tb_s = FF_ROWS // ls
    tb_attn = 8
    blk_attn = 4

    c_all = jnp.concatenate([c_prompt, c_sample], axis=0)
    mod_all = _ada_call(c_all, w_ada, b_ada)
    bias = _bias_call(rel_bias)

    cos_p, sin_p = _rope_tables(jnp.arange(lp, dtype=jnp.int32))
    cos_s, sin_s = _rope_tables(PAST_LEN + jnp.arange(ls, dtype=jnp.int32))
    cos_s, sin_s = jnp.tile(cos_s, (tb_s, 1)), jnp.tile(sin_s, (tb_s, 1))
    tab_p = _prompt_tables()
    dec_s = _ret_sample_table(ls)
    conv0 = jnp.zeros((1, bp, 2, UP_W), F32)

    w_in_b = w_in.astype(BF16)
    vec = lambda a: a.reshape(depth, 1, a.shape[-1])
    ffn_w = (w_out.astype(BF16), vec(ln1_g), vec(ln1_b), w_up.astype(BF16), conv_w, vec(conv_b),
             w_down.astype(BF16), vec(ln2_g), vec(ln2_b))
    mod_p = mod_all[:, :bp].reshape(depth, bp, 1, 6 * D_MODEL)
    mod_s = mod_all[:, bp:].reshape(depth, bs, 1, 6 * D_MODEL)
    state5 = jnp.transpose(state_ret, (0, 2, 3, 4, 1))
    s_buf = jnp.zeros(state5.shape, F32)
    ckt = jnp.transpose(cache_swa_k, (0, 1, 3, 4, 2))
    cvt = jnp.transpose(cache_swa_v, (0, 1, 3, 4, 2))
    k_buf = jnp.zeros(ckt.shape, F32)
    v_buf = jnp.zeros(cvt.shape, F32)

    xp, xs = x_prompt, x_sample
    p_ret, p_k, p_v, p_conv = [], [], [], []
    s_conv = []
    for l in range(depth):
        qkvg, kv = _proj_call(xp, mod_p, w_in_b, l, cos_p, sin_p, 1, tl_proj)
        mix, r_p = _attn_prompt_call(qkvg, kv, bias, swa_sinks[l], tab_p, bp, lp, blk_attn)
        xp, tail = _ffn_call(mix, xp, mod_p, conv0, *ffn_w, l, alpha, 1, tl_p)
        kv3 = kv.reshape(bp, lp, 2, KV_HEADS, HEAD_DIM)
        p_ret.append(r_p)
        p_k.append(kv3[:, lp - WINDOW:, 0])
        p_v.append(kv3[:, lp - WINDOW:, 1])
        p_conv.append(tail.reshape(bp, 8, UP_W)[:, 6:8])

        qkvg, kv = _proj_call(xs, mod_s, w_in_b, l, cos_s, sin_s, tb_s, ls)
        tq = jnp.transpose(qkvg.reshape(bs, ls, 5, RET_W)[:, :, 0:4], (2, 1, 3, 0))
        ret_t, s_buf = _ret_sample_call(tq, state5, s_buf, l, dec_s)
        ret = jnp.transpose(ret_t, (2, 0, 1)).reshape(bs * ls, RET_W)
        swa, k_buf, v_buf = _swa_sample_call(qkvg, kv, ckt, cvt, k_buf, v_buf, l, bias, swa_sinks[l], bs, ls, tb_attn)
        mix = jnp.concatenate([ret, swa], axis=1)
        xs, tail = _ffn_call(mix, xs, mod_s, state_conv, *ffn_w, l, alpha, tb_s, ls)
        s_conv.append(tail)

    return (xp, xs,
            jnp.stack(p_ret), jnp.stack(p_k), jnp.stack(p_v), jnp.stack(p_conv),
            jnp.transpose(s_buf, (0, 4, 1, 2, 3)), jnp.transpose(k_buf, (0, 1, 4, 2, 3)),
            jnp.transpose(v_buf, (0, 1, 4, 2, 3)), jnp.stack(s_conv))
```

```python
import functools
import math

import jax
import jax.numpy as jnp
import numpy as np
from jax import lax
from jax.experimental import pallas as pl
from jax.experimental.pallas import tpu as pltpu

F32 = jnp.float32
BF16 = jnp.bfloat16

D_MODEL = 1024
HEAD_DIM = 64
N_HEADS = 8
RET_W = N_HEADS * HEAD_DIM
KV_HEADS = 2
KV_W = KV_HEADS * HEAD_DIM
D_FF = 2816
UP_W = 2 * D_FF
IN_COLS = 4 * RET_W + RET_W + 2 * KV_W
QKVG_W = 5 * RET_W
WINDOW = 128
CHUNK = 128
NUM_BUCKETS = 32
PAST_LEN = 8192
ROPE_BASE = 10000.0
LN_EPS = 1e-5
NEG_INF = -1e30
LANES = 128
FF_CHUNK = 256
FF_LOOKAHEAD = 2
FF_ROWS = 256
VMEM_LIMIT = 56 * 1024 * 1024


def _cparams(n_axes):
    return pltpu.CompilerParams(dimension_semantics=("arbitrary",) * n_axes,
                                vmem_limit_bytes=VMEM_LIMIT)


def _const_spec(shape):
    nd = len(shape)
    return pl.BlockSpec(shape, lambda *_: (0,) * nd, pipeline_mode=pl.Buffered(1))


def _rope_tables(pos):
    half = HEAD_DIM // 2
    inv = 1.0 / (ROPE_BASE ** (jnp.arange(half, dtype=F32) / half))
    ang = pos.astype(F32)[:, None] * inv[None, :]
    cos = jnp.cos(ang)
    sin = jnp.sin(ang)
    cos_h = jnp.concatenate([cos, cos], axis=-1)
    sin_h = jnp.concatenate([-sin, sin], axis=-1)
    return jnp.tile(cos_h, (1, N_HEADS)), jnp.tile(sin_h, (1, N_HEADS))


def _decay_tables(chunk):
    log_g = jnp.log(1.0 - 2.0 ** (-5.0 - jnp.arange(N_HEADS, dtype=F32)))
    idx = jnp.arange(chunk)
    diff = idx[:, None] - idx[None, :]
    decay_mat = jnp.where(diff[None] >= 0,
                          jnp.exp(log_g[:, None, None] * jnp.maximum(diff, 0)[None].astype(F32)), 0.0)
    q_dec = jnp.exp(log_g[None, :] * (idx[:, None] + 1).astype(F32))
    k_dec = jnp.exp(log_g[None, :] * (chunk - 1 - idx[:, None]).astype(F32))
    chunk_dec = jnp.exp(log_g * chunk)
    return decay_mat, q_dec, k_dec, chunk_dec


def _bucket_table(n_q, n_k):
    i = np.arange(n_q)[:, None]
    j = np.arange(n_k)[None, :]
    dist = i - j + WINDOW
    n = np.maximum(dist, 0)
    max_exact = NUM_BUCKETS // 2
    nf = np.maximum(n, max_exact).astype(np.float64)
    large = max_exact + (np.log(nf / max_exact) / math.log(WINDOW / max_exact)
                         * (NUM_BUCKETS - max_exact)).astype(np.int32)
    large = np.minimum(large, NUM_BUCKETS - 1)
    bucket = np.where(n < max_exact, n, large).astype(np.int32)
    valid = ((dist >= 0) & (dist < WINDOW)).astype(np.int32)
    return bucket, valid


def _ada_kernel(c_ref, w_ref, b_ref, op_ref, os_ref):
    c = c_ref[...]
    s = (c * jax.nn.sigmoid(c)).astype(BF16)
    mod = jnp.dot(s, w_ref[0].astype(BF16), preferred_element_type=F32) + b_ref[0]
    bp = op_ref.shape[0]
    op_ref[:, 0, :] = mod[0:bp]
    os_ref[:, 0, :] = mod[bp:]


def _ada_call(c_all, bp, w_ada, b_ada):
    depth = w_ada.shape[0]
    n_rows = c_all.shape[0]
    bs = n_rows - bp
    tn = 1536
    out_blk = lambda b: pl.BlockSpec((None, b, 1, tn), lambda l, n: (l, 0, 0, n))
    return pl.pallas_call(
        _ada_kernel,
        grid=(depth, 6 * D_MODEL // tn),
        in_specs=[pl.BlockSpec((n_rows, D_MODEL), lambda l, n: (0, 0)),
                  pl.BlockSpec((1, D_MODEL, tn), lambda l, n: (l, 0, n)),
                  pl.BlockSpec((1, 1, tn), lambda l, n: (l, 0, n))],
        out_specs=[out_blk(bp), out_blk(bs)],
        out_shape=[jax.ShapeDtypeStruct((depth, bp, 1, 6 * D_MODEL), F32),
                   jax.ShapeDtypeStruct((depth, bs, 1, 6 * D_MODEL), F32)],
        compiler_params=_cparams(2),
        name="ada",
    )(c_all, w_ada, b_ada.reshape(depth, 1, 6 * D_MODEL))


def _bias_kernel(rb_ref, bucket_ref, valid_ref, o_ref):
    bucket = bucket_ref[...]
    valid = valid_ref[...] > 0
    first_ok = lax.broadcasted_iota(jnp.int32, bucket.shape, 1) >= WINDOW
    for h in range(N_HEADS):
        acc = jnp.zeros(bucket.shape, F32)
        for b in range(NUM_BUCKETS):
            acc = jnp.where(bucket == b, rb_ref[b, h], acc)
        o_ref[0, h] = jnp.where(valid, acc, NEG_INF)
        o_ref[1, h] = jnp.where(valid & first_ok, acc, NEG_INF)


def _bias_call(rel_bias):
    bucket, valid = _bucket_table(CHUNK, 2 * CHUNK)
    return pl.pallas_call(
        _bias_kernel,
        in_specs=[pl.BlockSpec(memory_space=pltpu.SMEM),
                  pl.BlockSpec((CHUNK, 2 * CHUNK), lambda: (0, 0)),
                  pl.BlockSpec((CHUNK, 2 * CHUNK), lambda: (0, 0))],
        out_specs=pl.BlockSpec((2, N_HEADS, CHUNK, 2 * CHUNK), lambda: (0, 0, 0, 0)),
        out_shape=jax.ShapeDtypeStruct((2, N_HEADS, CHUNK, 2 * CHUNK), F32),
        name="swa_bias",
    )(rel_bias, jnp.asarray(bucket), jnp.asarray(valid))


def _swap_halves(x):
    lane = lax.broadcasted_iota(jnp.int32, (1, LANES), 1)
    first = (lane % HEAD_DIM) < (HEAD_DIM // 2)
    cols = []
    for c in range(x.shape[1] // LANES):
        xc = x[:, c * LANES:(c + 1) * LANES]
        cols.append(jnp.where(first, pltpu.roll(xc, LANES - HEAD_DIM // 2, 1), pltpu.roll(xc, HEAD_DIM // 2, 1)))
    return jnp.concatenate(cols, axis=1)


def _proj_kernel(x_ref, sc_ref, sh_ref, w_ref, cos_ref, sin_ref, qkvg_ref, kv_ref):
    tb, tl, _ = x_ref.shape
    tm = tb * tl
    h = x_ref[...] * (1.0 + sc_ref[...]) + sh_ref[...]
    h = h.reshape(tm, D_MODEL).astype(BF16)
    proj = jnp.dot(h, w_ref[...], preferred_element_type=F32)
    cos = cos_ref[...]
    sin = sin_ref[...]
    rq = proj[:, 0:RET_W]
    rk = proj[:, RET_W:2 * RET_W]
    rq = rq * cos + _swap_halves(rq) * sin
    rk = (rk * cos + _swap_halves(rk) * sin) * (HEAD_DIM ** -0.5)
    rg = proj[:, 3 * RET_W:4 * RET_W]
    qkvg_ref[:, 0:RET_W] = rq.astype(BF16)
    qkvg_ref[:, RET_W:2 * RET_W] = rk.astype(BF16)
    qkvg_ref[:, 2 * RET_W:3 * RET_W] = proj[:, 2 * RET_W:3 * RET_W].astype(BF16)
    qkvg_ref[:, 3 * RET_W:4 * RET_W] = (rg * jax.nn.sigmoid(rg)).astype(BF16)
    qkvg_ref[:, 4 * RET_W:5 * RET_W] = (proj[:, 4 * RET_W:5 * RET_W] * (HEAD_DIM ** -0.5)).astype(BF16)
    kv_ref[...] = proj[:, 5 * RET_W:IN_COLS]


def _layer_spec(shape, layer):
    nd = len(shape)
    return pl.BlockSpec((None,) + tuple(shape), lambda *_: (layer,) + (0,) * nd, pipeline_mode=pl.Buffered(1))


def _mod_spec(tb, layer, k):
    return pl.BlockSpec((None, tb, 1, D_MODEL), lambda i, j: (layer, i, 0, k))


def _proj_call(x, mod, w_in, layer, cos, sin, tb, tl):
    b, l, _ = x.shape
    nj = l // tl
    tm = tb * tl
    tab_idx = (lambda i, j: (j, 0)) if cos.shape[0] == l and nj > 1 else (lambda i, j: (0, 0))
    return pl.pallas_call(
        _proj_kernel,
        grid=(b // tb, nj),
        in_specs=[pl.BlockSpec((tb, tl, D_MODEL), lambda i, j: (i, j, 0)),
                  _mod_spec(tb, layer, 1),
                  _mod_spec(tb, layer, 0),
                  _layer_spec((D_MODEL, IN_COLS), layer),
                  pl.BlockSpec((tm, RET_W), tab_idx),
                  pl.BlockSpec((tm, RET_W), tab_idx)],
        out_specs=[pl.BlockSpec((tm, QKVG_W), lambda i, j: (i * nj + j, 0)),
                   pl.BlockSpec((tm, 2 * KV_W), lambda i, j: (i * nj + j, 0))],
        out_shape=[jax.ShapeDtypeStruct((b * l, QKVG_W), BF16),
                   jax.ShapeDtypeStruct((b * l, 2 * KV_W), F32)],
        compiler_params=_cparams(2),
        name="proj",
    )(x, mod, mod, w_in, cos, sin)


def _lane_mask_low():
    return lax.broadcasted_iota(jnp.int32, (1, LANES), 1) < HEAD_DIM


def _head_norm(o, ones_bd):
    outs = []
    gw = ones_bd.shape[0]
    for g in range(RET_W // gw):
        og = o[:, g * gw:(g + 1) * gw]
        mu = jnp.dot(og.astype(BF16), ones_bd, preferred_element_type=F32)
        d = og - mu
        var = jnp.dot((d * d).astype(BF16), ones_bd, preferred_element_type=F32)
        outs.append(d * lax.rsqrt(var + LN_EPS))
    return jnp.concatenate(outs, axis=1)


def _swa_operands(k_all, v_all, sum_columns):
    low = _lane_mask_low()
    zero = jnp.zeros_like(k_all)
    ones_low = jnp.broadcast_to(jnp.where(low, 1.0, 0.0), k_all.shape)
    ones_cols = jnp.concatenate([ones_low, 1.0 - ones_low], axis=0).astype(BF16)
    k_rot = pltpu.roll(k_all, HEAD_DIM, 1)
    v_rot = pltpu.roll(v_all, HEAD_DIM, 1)
    ops = []
    for g in range(KV_HEADS):
        k_lo, k_hi = (k_all, k_rot) if g == 0 else (k_rot, k_all)
        v_lo, v_hi = (v_all, v_rot) if g == 0 else (v_rot, v_all)
        kc = jnp.concatenate([jnp.where(low, k_lo, zero), jnp.where(low, zero, k_hi)], axis=0)
        vc = jnp.concatenate([jnp.where(low, v_lo, zero), jnp.where(low, zero, v_hi)], axis=0)
        ops.append((kc, jnp.concatenate([vc, ones_cols], axis=1) if sum_columns else vc))
    return ops


def _swa_softmax(lg, p, bias_ref, sink_ref, with_sum):
    rows = lg.shape[0]
    low = _lane_mask_low()
    probs, dens = [], []
    for s in range(2):
        hh = 2 * p + s
        lo = lg[:, s * 2 * CHUNK:(s + 1) * 2 * CHUNK] + bias_ref[hh, 0:rows, :]
        sink = sink_ref[hh]
        m = jnp.maximum(jnp.max(lo, axis=-1, keepdims=True), sink)
        pe = jnp.exp(lo - m)
        den = jnp.exp(sink - m)
        if with_sum:
            den = den + jnp.sum(pe, axis=-1, keepdims=True)
        probs.append(pe)
        dens.append(den)
    return jnp.concatenate(probs, axis=1), jnp.where(low, dens[0], dens[1])


def _swa_values(probs, den_term, vc):
    oo = jnp.dot(probs, vc, preferred_element_type=F32)
    if vc.shape[1] == 2 * LANES:
        return oo[:, 0:LANES] / (oo[:, LANES:2 * LANES] + den_term)
    return oo / den_term


def _attn_prompt_kernel(sink_ref, qkvg_ref, kv_ref, bias_ref, bias_all_ref, dm_ref, qd_ref, kd_ref, cd_ref, ones_ref,
                        mix_ref, s_out_ref, s_scr, pk_scr, pv_scr):
    i = pl.program_id(1)

    @pl.when(i == 0)
    def _():
        s_scr[...] = jnp.zeros_like(s_scr)
        pk_scr[...] = jnp.zeros_like(pk_scr)
        pv_scr[...] = jnp.zeros_like(pv_scr)

    low = _lane_mask_low()
    zero = jnp.zeros((CHUNK, LANES), BF16)
    n_pairs = N_HEADS // 2
    pairs_per_kv = n_pairs // KV_HEADS
    gw = cd_ref.shape[1]
    n_groups = RET_W // gw
    n_blk = qkvg_ref.shape[0] // CHUNK
    nt = (((1,), (1,)), ((), ()))
    tn = (((0,), (0,)), ((), ()))
    ones_bd = ones_ref[...]

    s_cur = [s_scr[g] for g in range(n_groups)]
    k_prev, v_prev = pk_scr[...], pv_scr[...]
    wave1, mid = [], []
    for c in range(n_blk):
        rows = slice(c * CHUNK, (c + 1) * CHUNK)
        q = qkvg_ref[rows, 0:RET_W]
        k = qkvg_ref[rows, RET_W:2 * RET_W]
        v = qkvg_ref[rows, 2 * RET_W:3 * RET_W]
        sq = qkvg_ref[rows, 4 * RET_W:5 * RET_W]
        kd = (k.astype(F32) * kd_ref[...]).astype(BF16)
        ret_ops = []
        for p in range(n_pairs):
            kp = k[:, p * LANES:(p + 1) * LANES]
            vp = v[:, p * LANES:(p + 1) * LANES]
            ret_ops.append((jnp.concatenate([jnp.where(low, kp, zero), jnp.where(low, zero, kp)], axis=0),
                            jnp.concatenate([jnp.where(low, vp, zero), jnp.where(low, zero, vp)], axis=0)))
        k_new = kv_ref[rows, 0:KV_W].astype(BF16)
        v_new = kv_ref[rows, KV_W:2 * KV_W].astype(BF16)
        swa_ops = _swa_operands(jnp.concatenate([k_prev, k_new], axis=0), jnp.concatenate([v_prev, v_new], axis=0),
                                True)
        k_prev, v_prev = k_new, v_new

        scores = [lax.dot_general(q[:, p * LANES:(p + 1) * LANES], ret_ops[p][0], nt, preferred_element_type=F32)
                  for p in range(n_pairs)]
        cross = [jnp.dot(q[:, g * gw:(g + 1) * gw], s_cur[g].astype(BF16), preferred_element_type=F32)
                 for g in range(n_groups)]
        upd = [lax.dot_general(kd[:, g * gw:(g + 1) * gw], v[:, g * gw:(g + 1) * gw], tn,
                               preferred_element_type=F32) for g in range(n_groups)]
        logits = [lax.dot_general(
            jnp.concatenate([sq[:, (g * pairs_per_kv + pp) * LANES:(g * pairs_per_kv + pp + 1) * LANES]
                             for pp in range(pairs_per_kv)], axis=0),
            swa_ops[g][0], nt, preferred_element_type=F32) for g in range(KV_HEADS)]
        s_cur = [s_cur[g] * cd_ref[g] + jnp.where(cd_ref[g] > 0.0, upd[g], 0.0) for g in range(n_groups)]
        wave1.append((scores, cross, logits, ret_ops, swa_ops))
    for g in range(n_groups):
        s_scr[g] = s_cur[g]
    pk_scr[...] = k_prev
    pv_scr[...] = v_prev

    for c in range(n_blk):
        scores, cross, logits, ret_ops, swa_ops = wave1[c]
        bias_c = bias_ref.at[0] if c == 0 else bias_all_ref.at[0]
        scores = [(scores[p] * dm_ref[p]).astype(BF16) for p in range(n_pairs)]
        soft = [[_swa_softmax(logits[g][pp * CHUNK:(pp + 1) * CHUNK, :], g * pairs_per_kv + pp, bias_c, sink_ref,
                              False) for pp in range(pairs_per_kv)] for g in range(KV_HEADS)]
        intra = [jnp.dot(scores[p], ret_ops[p][1], preferred_element_type=F32) for p in range(n_pairs)]
        swa = []
        for g in range(KV_HEADS):
            oo = _swa_values(jnp.concatenate([x[0].astype(BF16) for x in soft[g]], axis=0),
                             jnp.concatenate([x[1] for x in soft[g]], axis=0), swa_ops[g][1])
            swa += [oo[pp * CHUNK:(pp + 1) * CHUNK, :] for pp in range(pairs_per_kv)]
        o = jnp.concatenate(intra, axis=1) + jnp.concatenate(cross, axis=1) * qd_ref[...]
        mid.append((o, swa))

    o = jnp.concatenate([m[0] for m in mid], axis=0)
    ret = _head_norm(o, ones_bd) * qkvg_ref[:, 3 * RET_W:4 * RET_W].astype(F32)
    mix_ref[:, 0:RET_W] = ret.astype(BF16)
    for c in range(n_blk):
        mix_ref[c * CHUNK:(c + 1) * CHUNK, RET_W:2 * RET_W] = jnp.concatenate(mid[c][1], axis=1).astype(BF16)

    @pl.when(i == pl.num_programs(1) - 1)
    def _():
        for h in range(N_HEADS):
            g, hl = divmod(h, gw // HEAD_DIM)
            s_out_ref[0, h] = s_scr[g, hl * HEAD_DIM:(hl + 1) * HEAD_DIM, hl * HEAD_DIM:(hl + 1) * HEAD_DIM]


def _prompt_tables():
    decay_mat, q_dec, k_dec, chunk_dec = _decay_tables(CHUNK)
    dm = jnp.concatenate([decay_mat[0::2], decay_mat[1::2]], axis=2)
    qd = jnp.repeat(q_dec, HEAD_DIM, axis=1)
    kd = jnp.repeat(k_dec, HEAD_DIM, axis=1)
    gw = 4 * HEAD_DIM
    blk = np.kron(np.eye(4, dtype=np.float32), np.ones((HEAD_DIM, HEAD_DIM), np.float32))
    cd = jnp.repeat(chunk_dec.reshape(2, 4), HEAD_DIM, axis=1)[:, :, None] * jnp.asarray(blk)[None]
    ones_bd = jnp.asarray(blk / HEAD_DIM, BF16)
    assert cd.shape == (2, gw, gw)
    return dm, qd, kd, cd, ones_bd


def _attn_prompt_call(qkvg, kv, bias, sinks, tables, b, l, n_blk):
    dm, qd, kd, cd, ones_bd = tables
    rows = n_blk * CHUNK
    nc = l // rows
    gw = cd.shape[1]
    bias_spec = lambda idx: pl.BlockSpec((1, N_HEADS, CHUNK, 2 * CHUNK), idx)
    return pl.pallas_call(
        _attn_prompt_kernel,
        grid=(b, nc),
        in_specs=[pl.BlockSpec(memory_space=pltpu.SMEM),
                  pl.BlockSpec((rows, QKVG_W), lambda bi, i: (bi * nc + i, 0)),
                  pl.BlockSpec((rows, 2 * KV_W), lambda bi, i: (bi * nc + i, 0)),
                  bias_spec(lambda bi, i: (jnp.where(i == 0, 1, 0), 0, 0, 0)),
                  bias_spec(lambda bi, i: (0, 0, 0, 0)),
                  _const_spec(dm.shape), _const_spec(qd.shape), _const_spec(kd.shape),
                  _const_spec(cd.shape), _const_spec(ones_bd.shape)],
        out_specs=[pl.BlockSpec((rows, 2 * RET_W), lambda bi, i: (bi * nc + i, 0)),
                   pl.BlockSpec((1, N_HEADS, HEAD_DIM, HEAD_DIM), lambda bi, i: (bi, 0, 0, 0))],
        out_shape=[jax.ShapeDtypeStruct((b * l, 2 * RET_W), BF16),
                   jax.ShapeDtypeStruct((b, N_HEADS, HEAD_DIM, HEAD_DIM), F32)],
        scratch_shapes=[pltpu.VMEM((RET_W // gw, gw, gw), F32),
                        pltpu.VMEM((CHUNK, KV_W), BF16),
                        pltpu.VMEM((CHUNK, KV_W), BF16)],
        compiler_params=_cparams(2),
        name="attn_prompt",
    )(sinks, qkvg, kv, bias, bias, dm, qd, kd, cd, ones_bd)


def _ret_sample_kernel(first, dec_ref, tq_ref, s_ref, *refs):
    if first:
        ret_ref, s_all_ref, q_scr, k_scr, v_scr, cross_scr = refs
        s_out_ref = s_all_ref.at[0]
        s_all_ref[1:] = jnp.zeros((s_all_ref.shape[0] - 1,) + s_all_ref.shape[1:], F32)
    else:
        _, ret_ref, s_out_ref, q_scr, k_scr, v_scr, cross_scr = refs
    h = pl.program_id(0)
    n_t = tq_ref.shape[1]
    cd = dec_ref[h, 2 * n_t]
    for t in range(n_t):
        q_scr[t] = tq_ref[0, t].astype(F32)
        k_scr[t] = tq_ref[1, t].astype(F32) * dec_ref[h, t]
        v_scr[t] = tq_ref[2, t].astype(F32)
    cross_scr[...] = jnp.zeros_like(cross_scr)

    def per_d(d, carry):
        s_d = s_ref[d]
        upd = s_d * cd
        for t in range(n_t):
            upd = upd + k_scr[t, pl.ds(d, 1), :] * v_scr[t]
            cross_scr[t] += q_scr[t, pl.ds(d, 1), :] * s_d
        s_out_ref[d] = upd
        return carry

    lax.fori_loop(0, HEAD_DIM, per_d, 0)

    for t in range(n_t):
        o = cross_scr[t] * dec_ref[h, n_t + t]
        for t2 in range(t + 1):
            sc = jnp.sum(q_scr[t] * (tq_ref[1, t2].astype(F32)), axis=0, keepdims=True)
            o = o + (sc * dec_ref[h, 2 * n_t + 1 + t * n_t + t2]) * v_scr[t2]
        mu = jnp.mean(o, axis=0, keepdims=True)
        d0 = o - mu
        var = jnp.mean(d0 * d0, axis=0, keepdims=True)
        ret_ref[t] = (d0 * lax.rsqrt(var + LN_EPS) * tq_ref[3, t].astype(F32)).astype(ret_ref.dtype)


def _ret_sample_call(tq, state5, s_buf, layer, dec):
    _, n_t, _, b = tq.shape
    first = s_buf is None
    assert first == (layer == 0)
    state_blk = (None, None, HEAD_DIM, HEAD_DIM, b)
    in_specs = [pl.BlockSpec(memory_space=pltpu.SMEM),
                pl.BlockSpec((4, n_t, HEAD_DIM, b), lambda h: (0, 0, h, 0)),
                pl.BlockSpec(state_blk, lambda h: (layer, h, 0, 0, 0))]
    if first:
        out_state = pl.BlockSpec((state5.shape[0], None, HEAD_DIM, HEAD_DIM, b), lambda h: (0, h, 0, 0, 0))
        operands, aliases = (dec, tq, state5), {}
    else:
        in_specs.append(pl.BlockSpec(memory_space=pl.ANY))
        out_state = pl.BlockSpec(state_blk, lambda h: (layer, h, 0, 0, 0))
        operands, aliases = (dec, tq, state5, s_buf), {3: 1}
    return pl.pallas_call(
        functools.partial(_ret_sample_kernel, first),
        grid=(N_HEADS,),
        in_specs=in_specs,
        out_specs=[pl.BlockSpec((n_t, HEAD_DIM, b), lambda h: (0, h, 0)), out_state],
        out_shape=[jax.ShapeDtypeStruct((n_t, RET_W, b), BF16),
                   jax.ShapeDtypeStruct(state5.shape, F32)],
        scratch_shapes=[pltpu.VMEM((n_t, HEAD_DIM, b), F32)] * 4,
        input_output_aliases=aliases,
        compiler_params=_cparams(1),
        name="ret_sample",
    )(*operands)


def _ret_sample_table(tl):
    decay_mat, q_dec, k_dec, chunk_dec = _decay_tables(tl)
    return jnp.concatenate([k_dec.T, q_dec.T, chunk_dec[:, None], decay_mat.reshape(N_HEADS, tl * tl)], axis=1)


def _swa_sample_kernel(first, sink_ref, qkvg_ref, kv_ref, ckt_ref, cvt_ref, bias_ref, *refs):
    if first:
        swa_ref, ck_all_ref, cv_all_ref = refs
        ck_out_ref, cv_out_ref = ck_all_ref.at[0], cv_all_ref.at[0]
        rest = (ck_all_ref.shape[0] - 1,) + ck_all_ref.shape[1:]
        ck_all_ref[1:] = jnp.zeros(rest, F32)
        cv_all_ref[1:] = jnp.zeros(rest, F32)
    else:
        _, _, swa_ref, ck_out_ref, cv_out_ref = refs
    tb = ckt_ref.shape[0]
    tl = qkvg_ref.shape[0] // tb
    nt = (((1,), (1,)), ((), ()))
    pairs_per_kv = N_HEADS // KV_HEADS // 2
    low = _lane_mask_low()
    zero_t =jnp.zeros((HEAD_DIM, WINDOW), BF16)
    front = jnp.zeros((WINDOW - tl, KV_W), F32)
    is_new =lax.broadcasted_iota(jnp.int32, (1, WINDOW), 1) >= WINDOW - tl
    bias = bias_ref.at[0]

    def block_diag(t):
        return jnp.concatenate([jnp.concatenate([t, zero_t], axis=1), jnp.concatenate([zero_t, t], axis=1)], axis=0)

    wave1 = []
    for e in range(tb):
        rows = slice(e * tl, (e + 1) * tl)
        sq32 = qkvg_ref[rows, 4 * RET_W:5 * RET_W].astype(F32)
        k_new = kv_ref[rows, 0:KV_W]
        v_new = kv_ref[rows, KV_W:2 * KV_W]
        k_rot = pltpu.roll(k_new, HEAD_DIM, 1)
        v_rot = pltpu.roll(v_new, HEAD_DIM, 1)
        k_placed = jnp.concatenate([front, k_new], axis=0).T
        v_placed = jnp.concatenate([front, v_new], axis=0).T
        per_g = []
        for g in range(KV_HEADS):
            kt = ckt_ref[e, g]
            vt = cvt_ref[e, g]
            hd = slice(g * HEAD_DIM, (g + 1) * HEAD_DIM)
            ck_out_ref[e, g] = jnp.where(is_new, k_placed[hd, :], pltpu.roll(kt, WINDOW - tl, 1))
            cv_out_ref[e, g] = jnp.where(is_new, v_placed[hd, :], pltpu.roll(vt, WINDOW - tl, 1))
            lhs = jnp.concatenate([sq32[:, (g * pairs_per_kv + pp) * LANES:(g * pairs_per_kv + pp + 1) * LANES]
                                   for pp in range(pairs_per_kv)], axis=0).astype(BF16)
            kn_lo, kn_hi = (k_new, k_rot) if g == 0 else (k_rot, k_new)
            vn_lo, vn_hi = (v_new, v_rot) if g == 0 else (v_rot, v_new)
            kn = [jnp.where(low, kn_lo, 0.0).astype(BF16), jnp.where(low, 0.0, kn_hi).astype(BF16)]
            vn = [jnp.where(low, vn_lo, 0.0).astype(BF16), jnp.where(low, 0.0, vn_hi).astype(BF16)]
            lc = jnp.dot(lhs, block_diag(kt.astype(BF16)), preferred_element_type=F32)
            ln = [lax.dot_general(lhs, kn[s], nt, preferred_element_type=F32) for s in range(2)]
            per_g.append((lc, ln, vn, block_diag(vt.astype(BF16))))
        wave1.append(per_g)

    swa_rows = []
    for e in range(tb):
        outs = []
        for g in range(KV_HEADS):
            lc, ln, vn, vbd = wave1[e][g]
            pcs, pns, dens = [], [[], []], []
            for pp in range(pairs_per_kv):
                r = slice(pp * tl, (pp + 1) * tl)
                den_pair = []
                for s in range(2):
                    hh = 2 * (g * pairs_per_kv + pp) + s
                    lo_c = lc[r, s * WINDOW:(s + 1) * WINDOW] + bias[hh, 0:tl, 0:WINDOW]
                    lo_n = ln[s][r, :] + bias[hh, 0:tl, WINDOW:WINDOW + tl]
                    sink = sink_ref[hh]
                    m = jnp.maximum(jnp.maximum(jnp.max(lo_c, axis=-1, keepdims=True),
                                                jnp.max(lo_n, axis=-1, keepdims=True)), sink)
                    pc = jnp.exp(lo_c - m)
                    pn = jnp.exp(lo_n - m)
                    den_pair.append(jnp.sum(pc, axis=-1, keepdims=True) + jnp.sum(pn, axis=-1, keepdims=True)
                                    + jnp.exp(sink - m))
                    pcs.append((pp, pc))
                    pns[s].append(pn)
                dens.append(jnp.where(low, den_pair[0], den_pair[1]))
            pc_rows = [jnp.concatenate([x[1] for x in pcs if x[0] == pp], axis=1) for pp in range(pairs_per_kv)]
            pc_all = jnp.concatenate(pc_rows, axis=0).astype(BF16)
            oo = lax.dot_general(pc_all, vbd, nt, preferred_element_type=F32)
            for s in range(2):
                oo = oo + jnp.dot(jnp.concatenate(pns[s], axis=0).astype(BF16), vn[s], preferred_element_type=F32)
            oo = oo / jnp.concatenate(dens, axis=0)
            outs += [oo[pp * tl:(pp + 1) * tl, :] for pp in range(pairs_per_kv)]
        swa_rows.append(jnp.concatenate(outs, axis=1))
    swa_ref[...] = jnp.concatenate(swa_rows, axis=0).astype(BF16)


def _swa_sample_call(qkvg, kv, ckt, cvt, kbuf, vbuf, layer, bias, sinks, b, tl, tb):
    first = kbuf is None
    assert first == (layer == 0)
    cache_blk = (None, tb, KV_HEADS, HEAD_DIM, WINDOW)
    layer5 = lambda i: (layer, i, 0, 0, 0)
    in_specs = [pl.BlockSpec(memory_space=pltpu.SMEM),
                pl.BlockSpec((tb * tl, QKVG_W), lambda i: (i, 0)),
                pl.BlockSpec((tb * tl, 2 * KV_W), lambda i: (i, 0)),
                pl.BlockSpec(cache_blk, layer5),
                pl.BlockSpec(cache_blk, layer5),
                pl.BlockSpec((1, N_HEADS, CHUNK, 2 * CHUNK), lambda i: (0, 0, 0, 0))]
    if first:
        out_cache = pl.BlockSpec((ckt.shape[0],) + cache_blk[1:], lambda i: (0, i, 0, 0, 0))
        operands, aliases = (sinks, qkvg, kv, ckt, cvt, bias), {}
    else:
        in_specs += [pl.BlockSpec(memory_space=pl.ANY)] * 2
        out_cache = pl.BlockSpec(cache_blk, layer5)
        operands, aliases = (sinks, qkvg, kv, ckt, cvt, bias, kbuf, vbuf), {6: 1, 7: 2}
    return pl.pallas_call(
        functools.partial(_swa_sample_kernel, first),
        grid=(b // tb,),
        in_specs=in_specs,
        out_specs=[pl.BlockSpec((tb * tl, RET_W), lambda i: (i, 0)), out_cache, out_cache],
        out_shape=[jax.ShapeDtypeStruct((b * tl, RET_W), BF16),
                   jax.ShapeDtypeStruct(ckt.shape, F32),
                   jax.ShapeDtypeStruct(cvt.shape, F32)],
        input_output_aliases=aliases,
        compiler_params=_cparams(1),
        name="swa_sample",
    )(*operands)


def _layer_norm(x, g, b):
    mu = jnp.mean(x, axis=-1, keepdims=True)
    xc = x - mu
    var = jnp.mean(xc * xc, axis=-1, keepdims=True)
    return xc * lax.rsqrt(var + LN_EPS) * g + b


def _ffn_kernel(alpha, carry_rows, mix_ref, x_ref, g1_ref, sh2_ref, sc2_ref, g2_ref, wout_ref, ln1g_ref, ln1b_ref,
                wup_ref, cw_ref, cb_ref, prev_ref, wdn_ref, ln2g_ref, ln2b_ref, xo_ref, tail_ref, carry_scr):
    tb, tl, _ = x_ref.shape
    tm = tb * tl
    sm = FF_ROWS
    n_sub = tm // sm
    sb, sl = (1, sm) if carry_rows else (tb // n_sub, tl)
    j = pl.program_id(1)

    def seq(ref, s):
        if carry_rows:
            return ref[:, s * sm:(s + 1) * sm, :] if ref.shape[1] == tl else ref[...]
        return ref[s * sb:(s + 1) * sb]

    if carry_rows:
        @pl.when(j == 0)
        def _():
            carry_scr[carry_rows - 2:carry_rows, :] = prev_ref[0]

    fix_rows = 8 if carry_rows else sm
    t_idx = lax.broadcasted_iota(jnp.int32, (fix_rows, 1), 0) % sl
    is_t0 = t_idx == 0
    is_t1 = t_idx == 1
    n_chunks = D_FF // FF_CHUNK

    def up_cols(c, half):
        return slice(half * D_FF + c * FF_CHUNK, half * D_FF + (c + 1) * FF_CHUNK)

    def pre(s):
        x1 = alpha * seq(x_ref, s) + seq(g1_ref, s) * y[s].reshape(sb, sl, D_MODEL)
        x1 = _layer_norm(x1, ln1g_ref[...], ln1b_ref[...])
        h2 = (x1 * (1.0 + seq(sc2_ref, s)) + seq(sh2_ref, s)).reshape(sm, D_MODEL).astype(BF16)
        return x1, h2

    def up_dots(h2, c):
        return [jnp.dot(h2, wup_ref[:, up_cols(c, half)], preferred_element_type=F32) for half in range(2)]

    def conv(s, up, cols):
        if carry_rows:
            src = carry_scr if s == 0 else tail_ref
            p0 = src[carry_rows - 2:carry_rows - 1, cols]
            p1 = src[carry_rows - 1:carry_rows, cols]
            tail_ref[:, cols] = up[sm - carry_rows:sm, :]
        else:
            p0 = jnp.broadcast_to(seq(prev_ref, s)[:, 0:1, cols], (sb, sl, FF_CHUNK)).reshape(sm, FF_CHUNK)
            p1 = jnp.broadcast_to(seq(prev_ref, s)[:, 1:2, cols], (sb, sl, FF_CHUNK)).reshape(sm, FF_CHUNK)
            tail_ref[s * sb:(s + 1) * sb, :, cols] = up.reshape(sb, sl, FF_CHUNK)[:, sl - 2:sl, :]
        r1 = pltpu.roll(up, 1, 0)
        r2 = pltpu.roll(up, 2, 0)
        s1 = jnp.where(is_t0, p1, r1[0:fix_rows])
        s2 = jnp.where(is_t0, p0, jnp.where(is_t1, p1, r2[0:fix_rows]))
        if fix_rows < sm:
            s1 = jnp.concatenate([s1, r1[fix_rows:]], axis=0)
            s2 = jnp.concatenate([s2, r2[fix_rows:]], axis=0)
        return cb_ref[:, cols] + s2 * cw_ref[0:1, cols] + s1 * cw_ref[1:2, cols] + up * cw_ref[2:3, cols]

    y = [jnp.dot(mix_ref[s * sm:(s + 1) * sm, :], wout_ref[...], preferred_element_type=F32) for s in range(n_sub)]
    items = [(s, c) for s in range(n_sub) for c in range(n_chunks)]
    staged, ups = {}, {}

    def issue_up(k):
        s, c = items[k]
        if c == 0:
            staged[s] = pre(s)
        ups[k] = up_dots(staged[s][1], c)

    for k in range(min(FF_LOOKAHEAD, len(items))):
        issue_up(k)
    acc = None
    for k, (s, c) in enumerate(items):
        if k + FF_LOOKAHEAD < len(items):
            issue_up(k + FF_LOOKAHEAD)
        up_a, up_b = ups.pop(k)
        ua = conv(s, up_a, up_cols(c, 0))
        ub = conv(s, up_b, up_cols(c, 1))
        gated = (ua * jax.nn.sigmoid(ua) * ub).astype(BF16)
        down = jnp.dot(gated, wdn_ref[c * FF_CHUNK:(c + 1) * FF_CHUNK, :], preferred_element_type=F32)
        acc = down if c == 0 else acc + down
        if c == n_chunks - 1:
            x1, _ = staged.pop(s)
            x2 = alpha * x1 + seq(g2_ref, s) * acc.reshape(sb, sl, D_MODEL)
            out = _layer_norm(x2, ln2g_ref[...], ln2b_ref[...])
            if carry_rows:
                xo_ref[:, s * sm:(s + 1) * sm, :] = out
            else:
                xo_ref[s * sb:(s + 1) * sb] = out

    if carry_rows:
        carry_scr[...] = tail_ref[...]


def _ffn_call(mix, x, mod, conv_prev, w_out, ln1_g, ln1_b, w_up, conv_w, conv_b, w_down, ln2_g, ln2_b,
              layer, alpha, tb, tl):
    prev_layer = layer if conv_prev.shape[0] > 1 else 0
    b, l, _ = x.shape
    nj = l // tl
    tm = tb * tl
    carry_rows = 8 if tb == 1 else 0
    assert carry_rows or (nj == 1 and tl == 8)
    if carry_rows:
        tail_spec = pl.BlockSpec((8, UP_W), lambda i, j: (i, 0))
        tail_shape = jax.ShapeDtypeStruct((b * 8, UP_W), F32)
    else:
        tail_spec = pl.BlockSpec((tb, 2, UP_W), lambda i, j: (i, 0, 0))
        tail_shape = jax.ShapeDtypeStruct((b, 2, UP_W), F32)
    kern = functools.partial(_ffn_kernel, alpha, carry_rows)
    return pl.pallas_call(
        kern,
        grid=(b // tb, nj),
        in_specs=[pl.BlockSpec((tm, 2 * RET_W), lambda i, j: (i * nj + j, 0)),
                  pl.BlockSpec((tb, tl, D_MODEL), lambda i, j: (i, j, 0)),
                  _mod_spec(tb, layer, 2), _mod_spec(tb, layer, 3), _mod_spec(tb, layer, 4), _mod_spec(tb, layer, 5),
                  _layer_spec((2 * RET_W, D_MODEL), layer),
                  _layer_spec((1, D_MODEL), layer), _layer_spec((1, D_MODEL), layer),
                  _layer_spec((D_MODEL, UP_W), layer),
                  _layer_spec((3, UP_W), layer), _layer_spec((1, UP_W), layer),
                  pl.BlockSpec((None, tb, 2, UP_W), lambda i, j: (prev_layer, i, 0, 0)),
                  _layer_spec((D_FF, D_MODEL), layer),
                  _layer_spec((1, D_MODEL), layer), _layer_spec((1, D_MODEL), layer)],
        out_specs=[pl.BlockSpec((tb, tl, D_MODEL), lambda i, j: (i, j, 0)),
                   tail_spec],
        out_shape=[jax.ShapeDtypeStruct((b, l, D_MODEL), F32), tail_shape],
        scratch_shapes=[pltpu.VMEM((8, UP_W), F32)],
        compiler_params=_cparams(2),
        name="ffn",
    )(mix, x, mod, mod, mod, mod, w_out, ln1_g, ln1_b, w_up, conv_w, conv_b, conv_prev, w_down, ln2_g, ln2_b)


def kernel(x_prompt, x_sample, c_prompt, c_sample, state_ret, cache_swa_k, cache_swa_v, state_conv, rel_bias, w_ada, b_ada, w_in, swa_sinks, w_out, ln1_g, ln1_b, w_up, conv_w, conv_b, w_down, ln2_g, ln2_b):
    depth = w_ada.shape[0]
    bp, lp, _ = x_prompt.shape
    bs, ls, _ = x_sample.shape
    alpha = (2.0 * depth) ** 0.25
    tl_proj = 512
    tl_p = 2 * FF_ROWS
    tb_s = FF_ROWS // ls
    tb_attn = 8
    blk_attn = 4

    c_all = jnp.concatenate([c_prompt, c_sample], axis=0)
    mod_p, mod_s = _ada_call(c_all, bp, w_ada, b_ada)
    bias = _bias_call(rel_bias)

    cos_p, sin_p = _rope_tables(jnp.arange(lp, dtype=jnp.int32))
    cos_s, sin_s = _rope_tables(PAST_LEN + jnp.arange(ls, dtype=jnp.int32))
    cos_s, sin_s = jnp.tile(cos_s, (tb_s, 1)), jnp.tile(sin_s, (tb_s, 1))
    tab_p = _prompt_tables()
    dec_s = _ret_sample_table(ls)
    conv0 = jnp.zeros((1, bp, 2, UP_W), F32)

    w_in_b = w_in.astype(BF16)
    vec = lambda a: a.reshape(depth, 1, a.shape[-1])
    ffn_w = (w_out.astype(BF16), vec(ln1_g), vec(ln1_b), w_up.astype(BF16), conv_w, vec(conv_b),
             w_down.astype(BF16), vec(ln2_g), vec(ln2_b))
    state5 = jnp.transpose(state_ret, (0, 2, 3, 4, 1))
    s_buf = None
    ckt = jnp.transpose(cache_swa_k, (0, 1, 3, 4, 2))
    cvt = jnp.transpose(cache_swa_v, (0, 1, 3, 4, 2))
    k_buf = v_buf = None

    xp, xs = x_prompt, x_sample
    p_ret, p_k, p_v, p_conv = [], [], [], []
    s_conv = []
    for l in range(depth):
        qkvg, kv = _proj_call(xp, mod_p, w_in_b, l, cos_p, sin_p, 1, tl_proj)
        mix, r_p = _attn_prompt_call(qkvg, kv, bias, swa_sinks[l], tab_p, bp, lp, blk_attn)
        xp, tail = _ffn_call(mix, xp, mod_p, conv0, *ffn_w, l, alpha, 1, tl_p)
        kv3 = kv.reshape(bp, lp, 2, KV_HEADS, HEAD_DIM)
        p_ret.append(r_p)
        p_k.append(kv3[:, lp - WINDOW:, 0])
        p_v.append(kv3[:, lp - WINDOW:, 1])
        p_conv.append(tail.reshape(bp, 8, UP_W)[:, 6:8])

        qkvg, kv = _proj_call(xs, mod_s, w_in_b, l, cos_s, sin_s, tb_s, ls)
        tq = jnp.transpose(qkvg.reshape(bs, ls, 5, RET_W)[:, :, 0:4], (2, 1, 3, 0))
        ret_t, s_buf = _ret_sample_call(tq, state5, s_buf, l, dec_s)
        ret = jnp.transpose(ret_t, (2, 0, 1)).reshape(bs * ls, RET_W)
        swa, k_buf, v_buf = _swa_sample_call(qkvg, kv, ckt, cvt, k_buf, v_buf, l, bias, swa_sinks[l], bs, ls, tb_attn)
        mix = jnp.concatenate([ret, swa], axis=1)
        xs, tail = _ffn_call(mix, xs, mod_s, state_conv, *ffn_w, l, alpha, tb_s, ls)
        s_conv.append(tail)

    return (xp, xs,
            jnp.stack(p_ret), jnp.stack(p_k), jnp.stack(p_v), jnp.stack(p_conv),
            jnp.transpose(s_buf, (0, 4, 1, 2, 3)), jnp.transpose(k_buf, (0, 1, 4, 2, 3)),
            jnp.transpose(v_buf, (0, 1, 4, 2, 3)), jnp.stack(s_conv))
```

```python
import functools
import math

import jax
import jax.numpy as jnp
import numpy as np
from jax import lax
from jax.experimental import pallas as pl
from jax.experimental.pallas import tpu as pltpu

F32 = jnp.float32
BF16 = jnp.bfloat16

D_MODEL = 1024
HEAD_DIM = 64
N_HEADS = 8
RET_W = N_HEADS * HEAD_DIM
KV_HEADS = 2
KV_W = KV_HEADS * HEAD_DIM
D_FF = 2816
UP_W = 2 * D_FF
IN_COLS = 4 * RET_W + RET_W + 2 * KV_W
QKVG_W = 5 * RET_W
WINDOW = 128
CHUNK = 128
NUM_BUCKETS = 32
PAST_LEN = 8192
ROPE_BASE = 10000.0
LN_EPS = 1e-5
NEG_INF = -1e30
LANES = 128
FF_CHUNK = 256
FF_LOOKAHEAD = 2
FF_ROWS = 256
VMEM_LIMIT = 56 * 1024 * 1024


def _cparams(n_axes):
    return pltpu.CompilerParams(dimension_semantics=("arbitrary",) * n_axes,
                                vmem_limit_bytes=VMEM_LIMIT)


def _const_spec(shape):
    nd = len(shape)
    return pl.BlockSpec(shape, lambda *_: (0,) * nd, pipeline_mode=pl.Buffered(1))


def _rope_tables(pos):
    half = HEAD_DIM // 2
    inv = 1.0 / (ROPE_BASE ** (jnp.arange(half, dtype=F32) / half))
    ang = pos.astype(F32)[:, None] * inv[None, :]
    cos = jnp.cos(ang)
    sin = jnp.sin(ang)
    cos_h = jnp.concatenate([cos, cos], axis=-1)
    sin_h = jnp.concatenate([-sin, sin], axis=-1)
    return jnp.tile(cos_h, (1, N_HEADS)), jnp.tile(sin_h, (1, N_HEADS))


def _decay_tables(chunk):
    log_g = jnp.log(1.0 - 2.0 ** (-5.0 - jnp.arange(N_HEADS, dtype=F32)))
    idx = jnp.arange(chunk)
    diff = idx[:, None] - idx[None, :]
    decay_mat = jnp.where(diff[None] >= 0,
                          jnp.exp(log_g[:, None, None] * jnp.maximum(diff, 0)[None].astype(F32)), 0.0)
    q_dec = jnp.exp(log_g[None, :] * (idx[:, None] + 1).astype(F32))
    k_dec = jnp.exp(log_g[None, :] * (chunk - 1 - idx[:, None]).astype(F32))
    chunk_dec = jnp.exp(log_g * chunk)
    return decay_mat, q_dec, k_dec, chunk_dec


def _bucket_table(n_q, n_k):
    i = np.arange(n_q)[:, None]
    j = np.arange(n_k)[None, :]
    dist = i - j + WINDOW
    n = np.maximum(dist, 0)
    max_exact = NUM_BUCKETS // 2
    nf = np.maximum(n, max_exact).astype(np.float64)
    large = max_exact + (np.log(nf / max_exact) / math.log(WINDOW / max_exact)
                         * (NUM_BUCKETS - max_exact)).astype(np.int32)
    large = np.minimum(large, NUM_BUCKETS - 1)
    bucket = np.where(n < max_exact, n, large).astype(np.int32)
    valid = ((dist >= 0) & (dist < WINDOW)).astype(np.int32)
    return bucket, valid


def _ada_kernel(c_ref, w_ref, b_ref, op_ref, os_ref):
    c = c_ref[...]
    s = (c * jax.nn.sigmoid(c)).astype(BF16)
    mod = jnp.dot(s, w_ref[0].astype(BF16), preferred_element_type=F32) + b_ref[0]
    bp = op_ref.shape[0]
    op_ref[:, 0, :] = mod[0:bp]
    os_ref[:, 0, :] = mod[bp:]


def _ada_call(c_all, bp, w_ada, b_ada):
    depth = w_ada.shape[0]
    n_rows = c_all.shape[0]
    bs = n_rows - bp
    tn = 1536
    out_blk = lambda b: pl.BlockSpec((None, b, 1, tn), lambda l, n: (l, 0, 0, n))
    return pl.pallas_call(
        _ada_kernel,
        grid=(depth, 6 * D_MODEL // tn),
        in_specs=[pl.BlockSpec((n_rows, D_MODEL), lambda l, n: (0, 0)),
                  pl.BlockSpec((1, D_MODEL, tn), lambda l, n: (l, 0, n)),
                  pl.BlockSpec((1, 1, tn), lambda l, n: (l, 0, n))],
        out_specs=[out_blk(bp), out_blk(bs)],
        out_shape=[jax.ShapeDtypeStruct((depth, bp, 1, 6 * D_MODEL), F32),
                   jax.ShapeDtypeStruct((depth, bs, 1, 6 * D_MODEL), F32)],
        compiler_params=_cparams(2),
        name="ada",
    )(c_all, w_ada, b_ada.reshape(depth, 1, 6 * D_MODEL))


def _bias_kernel(rb_ref, bucket_ref, valid_ref, o_ref):
    bucket = bucket_ref[...]
    valid = valid_ref[...] > 0
    first_ok = lax.broadcasted_iota(jnp.int32, bucket.shape, 1) >= WINDOW
    for h in range(N_HEADS):
        acc = jnp.zeros(bucket.shape, F32)
        for b in range(NUM_BUCKETS):
            acc = jnp.where(bucket == b, rb_ref[b, h], acc)
        o_ref[0, h] = jnp.where(valid, acc, NEG_INF)
        o_ref[1, h] = jnp.where(valid & first_ok, acc, NEG_INF)


def _bias_call(rel_bias):
    bucket, valid = _bucket_table(CHUNK, 2 * CHUNK)
    return pl.pallas_call(
        _bias_kernel,
        in_specs=[pl.BlockSpec(memory_space=pltpu.SMEM),
                  pl.BlockSpec((CHUNK, 2 * CHUNK), lambda: (0, 0)),
                  pl.BlockSpec((CHUNK, 2 * CHUNK), lambda: (0, 0))],
        out_specs=pl.BlockSpec((2, N_HEADS, CHUNK, 2 * CHUNK), lambda: (0, 0, 0, 0)),
        out_shape=jax.ShapeDtypeStruct((2, N_HEADS, CHUNK, 2 * CHUNK), F32),
        name="swa_bias",
    )(rel_bias, jnp.asarray(bucket), jnp.asarray(valid))


def _swap_halves(x):
    lane = lax.broadcasted_iota(jnp.int32, (1, LANES), 1)
    first = (lane % HEAD_DIM) < (HEAD_DIM // 2)
    cols = []
    for c in range(x.shape[1] // LANES):
        xc = x[:, c * LANES:(c + 1) * LANES]
        cols.append(jnp.where(first, pltpu.roll(xc, LANES - HEAD_DIM // 2, 1), pltpu.roll(xc, HEAD_DIM // 2, 1)))
    return jnp.concatenate(cols, axis=1)


def _proj_kernel(x_ref, sc_ref, sh_ref, w_ref, cos_ref, sin_ref, qkvg_ref, kv_ref):
    tb, tl, _ = x_ref.shape
    tm = tb * tl
    h = x_ref[...] * (1.0 + sc_ref[...]) + sh_ref[...]
    h = h.reshape(tm, D_MODEL).astype(BF16)
    proj = jnp.dot(h, w_ref[...], preferred_element_type=F32)
    cos = cos_ref[...]
    sin = sin_ref[...]
    rq = proj[:, 0:RET_W]
    rk = proj[:, RET_W:2 * RET_W]
    rq = rq * cos + _swap_halves(rq) * sin
    rk = (rk * cos + _swap_halves(rk) * sin) * (HEAD_DIM ** -0.5)
    rg = proj[:, 3 * RET_W:4 * RET_W]
    qkvg_ref[:, 0:RET_W] = rq.astype(BF16)
    qkvg_ref[:, RET_W:2 * RET_W] = rk.astype(BF16)
    qkvg_ref[:, 2 * RET_W:3 * RET_W] = proj[:, 2 * RET_W:3 * RET_W].astype(BF16)
    qkvg_ref[:, 3 * RET_W:4 * RET_W] = (rg * jax.nn.sigmoid(rg)).astype(BF16)
    qkvg_ref[:, 4 * RET_W:5 * RET_W] = (proj[:, 4 * RET_W:5 * RET_W] * (HEAD_DIM ** -0.5)).astype(BF16)
    kv_ref[...] = proj[:, 5 * RET_W:IN_COLS]


def _layer_spec(shape, layer):
    nd = len(shape)
    return pl.BlockSpec((None,) + tuple(shape), lambda *_: (layer,) + (0,) * nd, pipeline_mode=pl.Buffered(1))


def _mod_spec(tb, layer, k):
    return pl.BlockSpec((None, tb, 1, D_MODEL), lambda i, j: (layer, i, 0, k))


def _proj_call(x, mod, w_in, layer, cos, sin, tb, tl):
    b, l, _ = x.shape
    nj = l // tl
    tm = tb * tl
    tab_idx = (lambda i, j: (j, 0)) if cos.shape[0] == l and nj > 1 else (lambda i, j: (0, 0))
    return pl.pallas_call(
        _proj_kernel,
        grid=(b // tb, nj),
        in_specs=[pl.BlockSpec((tb, tl, D_MODEL), lambda i, j: (i, j, 0)),
                  _mod_spec(tb, layer, 1),
                  _mod_spec(tb, layer, 0),
                  _layer_spec((D_MODEL, IN_COLS), layer),
                  pl.BlockSpec((tm, RET_W), tab_idx),
                  pl.BlockSpec((tm, RET_W), tab_idx)],
        out_specs=[pl.BlockSpec((tm, QKVG_W), lambda i, j: (i * nj + j, 0)),
                   pl.BlockSpec((tm, 2 * KV_W), lambda i, j: (i * nj + j, 0))],
        out_shape=[jax.ShapeDtypeStruct((b * l, QKVG_W), BF16),
                   jax.ShapeDtypeStruct((b * l, 2 * KV_W), F32)],
        compiler_params=_cparams(2),
        name="proj",
    )(x, mod, mod, w_in, cos, sin)


def _lane_mask_low():
    return lax.broadcasted_iota(jnp.int32, (1, LANES), 1) < HEAD_DIM


def _head_norm(o, ones_bd):
    outs = []
    gw = ones_bd.shape[0]
    for g in range(RET_W // gw):
        og = o[:, g * gw:(g + 1) * gw]
        mu = jnp.dot(og.astype(BF16), ones_bd, preferred_element_type=F32)
        d = og - mu
        var = jnp.dot((d * d).astype(BF16), ones_bd, preferred_element_type=F32)
        outs.append(d * lax.rsqrt(var + LN_EPS))
    return jnp.concatenate(outs, axis=1)


def _swa_operands(k_all, v_all, sum_columns):
    low = _lane_mask_low()
    zero = jnp.zeros_like(k_all)
    ones_low = jnp.broadcast_to(jnp.where(low, 1.0, 0.0), k_all.shape)
    ones_cols = jnp.concatenate([ones_low, 1.0 - ones_low], axis=0).astype(BF16)
    k_rot = pltpu.roll(k_all, HEAD_DIM, 1)
    v_rot = pltpu.roll(v_all, HEAD_DIM, 1)
    ops = []
    for g in range(KV_HEADS):
        k_lo, k_hi = (k_all, k_rot) if g == 0 else (k_rot, k_all)
        v_lo, v_hi = (v_all, v_rot) if g == 0 else (v_rot, v_all)
        kc = jnp.concatenate([jnp.where(low, k_lo, zero), jnp.where(low, zero, k_hi)], axis=0)
        vc = jnp.concatenate([jnp.where(low, v_lo, zero), jnp.where(low, zero, v_hi)], axis=0)
        ops.append((kc, jnp.concatenate([vc, ones_cols], axis=1) if sum_columns else vc))
    return ops


def _swa_softmax(lg, p, bias_ref, sink_ref, with_sum):
    rows = lg.shape[0]
    low = _lane_mask_low()
    probs, dens = [], []
    for s in range(2):
        hh = 2 * p + s
        lo = lg[:, s * 2 * CHUNK:(s + 1) * 2 * CHUNK] + bias_ref[hh, 0:rows, :]
        sink = sink_ref[hh]
        m = jnp.maximum(jnp.max(lo, axis=-1, keepdims=True), sink)
        pe = jnp.exp(lo - m)
        den = jnp.exp(sink - m)
        if with_sum:
            den = den + jnp.sum(pe, axis=-1, keepdims=True)
        probs.append(pe)
        dens.append(den)
    return jnp.concatenate(probs, axis=1), jnp.where(low, dens[0], dens[1])


def _swa_values(probs, den_term, vc):
    oo = jnp.dot(probs, vc, preferred_element_type=F32)
    if vc.shape[1] == 2 * LANES:
        return oo[:, 0:LANES] / (oo[:, LANES:2 * LANES] + den_term)
    return oo / den_term


def _attn_prompt_kernel(sink_ref, qkvg_ref, kv_ref, bias_ref, bias_all_ref, dm_ref, qd_ref, kd_ref, cd_ref, ones_ref,
                        mix_ref, s_out_ref, s_scr, pk_scr, pv_scr):
    i = pl.program_id(1)

    @pl.when(i == 0)
    def _():
        s_scr[...] = jnp.zeros_like(s_scr)
        pk_scr[...] = jnp.zeros_like(pk_scr)
        pv_scr[...] = jnp.zeros_like(pv_scr)

    low = _lane_mask_low()
    zero = jnp.zeros((CHUNK, LANES), BF16)
    n_pairs = N_HEADS // 2
    pairs_per_kv = n_pairs // KV_HEADS
    gw = cd_ref.shape[1]
    n_groups = RET_W // gw
    n_blk = qkvg_ref.shape[0] // CHUNK
    nt = (((1,), (1,)), ((), ()))
    tn = (((0,), (0,)), ((), ()))
    ones_bd = ones_ref[...]

    s_cur = [s_scr[g] for g in range(n_groups)]
    k_prev, v_prev = pk_scr[...], pv_scr[...]
    wave1, mid = [], []
    for c in range(n_blk):
        rows = slice(c * CHUNK, (c + 1) * CHUNK)
        q = qkvg_ref[rows, 0:RET_W]
        k = qkvg_ref[rows, RET_W:2 * RET_W]
        v = qkvg_ref[rows, 2 * RET_W:3 * RET_W]
        sq = qkvg_ref[rows, 4 * RET_W:5 * RET_W]
        kd = (k.astype(F32) * kd_ref[...]).astype(BF16)
        ret_ops = []
        for p in range(n_pairs):
            kp = k[:, p * LANES:(p + 1) * LANES]
            vp = v[:, p * LANES:(p + 1) * LANES]
            ret_ops.append((jnp.concatenate([jnp.where(low, kp, zero), jnp.where(low, zero, kp)], axis=0),
                            jnp.concatenate([jnp.where(low, vp, zero), jnp.where(low, zero, vp)], axis=0)))
        k_new = kv_ref[rows, 0:KV_W].astype(BF16)
        v_new = kv_ref[rows, KV_W:2 * KV_W].astype(BF16)
        swa_ops = _swa_operands(jnp.concatenate([k_prev, k_new], axis=0), jnp.concatenate([v_prev, v_new], axis=0),
                                True)
        k_prev, v_prev = k_new, v_new

        scores = [lax.dot_general(q[:, p * LANES:(p + 1) * LANES], ret_ops[p][0], nt, preferred_element_type=F32)
                  for p in range(n_pairs)]
        cross = [jnp.dot(q[:, g * gw:(g + 1) * gw], s_cur[g].astype(BF16), preferred_element_type=F32)
                 for g in range(n_groups)]
        upd = [lax.dot_general(kd[:, g * gw:(g + 1) * gw], v[:, g * gw:(g + 1) * gw], tn,
                               preferred_element_type=F32) for g in range(n_groups)]
        logits = [lax.dot_general(
            jnp.concatenate([sq[:, (g * pairs_per_kv + pp) * LANES:(g * pairs_per_kv + pp + 1) * LANES]
                             for pp in range(pairs_per_kv)], axis=0),
            swa_ops[g][0], nt, preferred_element_type=F32) for g in range(KV_HEADS)]
        s_cur = [s_cur[g] * cd_ref[g] + jnp.where(cd_ref[g] > 0.0, upd[g], 0.0) for g in range(n_groups)]
        wave1.append((scores, cross, logits, ret_ops, swa_ops))
    for g in range(n_groups):
        s_scr[g] = s_cur[g]
    pk_scr[...] = k_prev
    pv_scr[...] = v_prev

    for c in range(n_blk):
        scores, cross, logits, ret_ops, swa_ops = wave1[c]
        bias_c = bias_ref.at[0] if c == 0 else bias_all_ref.at[0]
        scores = [(scores[p] * dm_ref[p]).astype(BF16) for p in range(n_pairs)]
        soft = [[_swa_softmax(logits[g][pp * CHUNK:(pp + 1) * CHUNK, :], g * pairs_per_kv + pp, bias_c, sink_ref,
                              False) for pp in range(pairs_per_kv)] for g in range(KV_HEADS)]
        intra = [jnp.dot(scores[p], ret_ops[p][1], preferred_element_type=F32) for p in range(n_pairs)]
        swa = []
        for g in range(KV_HEADS):
            oo = _swa_values(jnp.concatenate([x[0].astype(BF16) for x in soft[g]], axis=0),
                             jnp.concatenate([x[1] for x in soft[g]], axis=0), swa_ops[g][1])
            swa += [oo[pp * CHUNK:(pp + 1) * CHUNK, :] for pp in range(pairs_per_kv)]
        o = jnp.concatenate(intra, axis=1) + jnp.concatenate(cross, axis=1) * qd_ref[...]
        mid.append((o, swa))

    o = jnp.concatenate([m[0] for m in mid], axis=0)
    ret = _head_norm(o, ones_bd) * qkvg_ref[:, 3 * RET_W:4 * RET_W].astype(F32)
    mix_ref[:, 0:RET_W] = ret.astype(BF16)
    for c in range(n_blk):
        mix_ref[c * CHUNK:(c + 1) * CHUNK, RET_W:2 * RET_W] = jnp.concatenate(mid[c][1], axis=1).astype(BF16)

    @pl.when(i == pl.num_programs(1) - 1)
    def _():
        for h in range(N_HEADS):
            g, hl = divmod(h, gw // HEAD_DIM)
            s_out_ref[0, h] = s_scr[g, hl * HEAD_DIM:(hl + 1) * HEAD_DIM, hl * HEAD_DIM:(hl + 1) * HEAD_DIM]


def _prompt_tables():
    decay_mat, q_dec, k_dec, chunk_dec = _decay_tables(CHUNK)
    dm = jnp.concatenate([decay_mat[0::2], decay_mat[1::2]], axis=2)
    qd = jnp.repeat(q_dec, HEAD_DIM, axis=1)
    kd = jnp.repeat(k_dec, HEAD_DIM, axis=1)
    gw = 4 * HEAD_DIM
    blk = np.kron(np.eye(4, dtype=np.float32), np.ones((HEAD_DIM, HEAD_DIM), np.float32))
    cd = jnp.repeat(chunk_dec.reshape(2, 4), HEAD_DIM, axis=1)[:, :, None] * jnp.asarray(blk)[None]
    ones_bd = jnp.asarray(blk / HEAD_DIM, BF16)
    assert cd.shape == (2, gw, gw)
    return dm, qd, kd, cd, ones_bd


def _attn_prompt_call(qkvg, kv, bias, sinks, tables, b, l, n_blk):
    dm, qd, kd, cd, ones_bd = tables
    rows = n_blk * CHUNK
    nc = l // rows
    gw = cd.shape[1]
    bias_spec = lambda idx: pl.BlockSpec((1, N_HEADS, CHUNK, 2 * CHUNK), idx)
    return pl.pallas_call(
        _attn_prompt_kernel,
        grid=(b, nc),
        in_specs=[pl.BlockSpec(memory_space=pltpu.SMEM),
                  pl.BlockSpec((rows, QKVG_W), lambda bi, i: (bi * nc + i, 0)),
                  pl.BlockSpec((rows, 2 * KV_W), lambda bi, i: (bi * nc + i, 0)),
                  bias_spec(lambda bi, i: (jnp.where(i == 0, 1, 0), 0, 0, 0)),
                  bias_spec(lambda bi, i: (0, 0, 0, 0)),
                  _const_spec(dm.shape), _const_spec(qd.shape), _const_spec(kd.shape),
                  _const_spec(cd.shape), _const_spec(ones_bd.shape)],
        out_specs=[pl.BlockSpec((rows, 2 * RET_W), lambda bi, i: (bi * nc + i, 0)),
                   pl.BlockSpec((1, N_HEADS, HEAD_DIM, HEAD_DIM), lambda bi, i: (bi, 0, 0, 0))],
        out_shape=[jax.ShapeDtypeStruct((b * l, 2 * RET_W), BF16),
                   jax.ShapeDtypeStruct((b, N_HEADS, HEAD_DIM, HEAD_DIM), F32)],
        scratch_shapes=[pltpu.VMEM((RET_W // gw, gw, gw), F32),
                        pltpu.VMEM((CHUNK, KV_W), BF16),
                        pltpu.VMEM((CHUNK, KV_W), BF16)],
        compiler_params=_cparams(2),
        name="attn_prompt",
    )(sinks, qkvg, kv, bias, bias, dm, qd, kd, cd, ones_bd)


def _ret_sample_kernel(dec_ref, tq_ref, s_ref, buf_ref, ret_ref, s_out_ref, q_scr, k_scr, v_scr, cross_scr):
    del buf_ref
    h = pl.program_id(0)
    n_t = tq_ref.shape[1]
    cd = dec_ref[h, 2 * n_t]
    for t in range(n_t):
        q_scr[t] = tq_ref[0, t].astype(F32)
        k_scr[t] = tq_ref[1, t].astype(F32) * dec_ref[h, t]
        v_scr[t] = tq_ref[2, t].astype(F32)
    cross_scr[...] = jnp.zeros_like(cross_scr)

    def per_d(d, carry):
        s_d = s_ref[d]
        upd = s_d * cd
        for t in range(n_t):
            upd = upd + k_scr[t, pl.ds(d, 1), :] * v_scr[t]
            cross_scr[t] += q_scr[t, pl.ds(d, 1), :] * s_d
        s_out_ref[d] = upd
        return carry

    lax.fori_loop(0, HEAD_DIM, per_d, 0)

    for t in range(n_t):
        o = cross_scr[t] * dec_ref[h, n_t + t]
        for t2 in range(t + 1):
            sc = jnp.sum(q_scr[t] * (tq_ref[1, t2].astype(F32)), axis=0, keepdims=True)
            o = o + (sc * dec_ref[h, 2 * n_t + 1 + t * n_t + t2]) * v_scr[t2]
        mu = jnp.mean(o, axis=0, keepdims=True)
        d0 = o - mu
        var = jnp.mean(d0 * d0, axis=0, keepdims=True)
        ret_ref[t] = (d0 * lax.rsqrt(var + LN_EPS) * tq_ref[3, t].astype(F32)).astype(ret_ref.dtype)


def _ret_sample_call(tq, state5, s_buf, layer, dec):
    _, n_t, _, b = tq.shape
    state_blk = (None, None, HEAD_DIM, HEAD_DIM, b)
    return pl.pallas_call(
        _ret_sample_kernel,
        grid=(N_HEADS,),
        in_specs=[pl.BlockSpec(memory_space=pltpu.SMEM),
                  pl.BlockSpec((4, n_t, HEAD_DIM, b), lambda h: (0, 0, h, 0)),
                  pl.BlockSpec(state_blk, lambda h: (layer, h, 0, 0, 0)),
                  pl.BlockSpec(memory_space=pl.ANY)],
        out_specs=[pl.BlockSpec((n_t, HEAD_DIM, b), lambda h: (0, h, 0)),
                   pl.BlockSpec(state_blk, lambda h: (layer, h, 0, 0, 0))],
        out_shape=[jax.ShapeDtypeStruct((n_t, RET_W, b), BF16),
                   jax.ShapeDtypeStruct(s_buf.shape, F32)],
        scratch_shapes=[pltpu.VMEM((n_t, HEAD_DIM, b), F32)] * 4,
        input_output_aliases={3: 1},
        compiler_params=_cparams(1),
        name="ret_sample",
    )(dec, tq, state5, s_buf)


def _ret_sample_table(tl):
    decay_mat, q_dec, k_dec, chunk_dec = _decay_tables(tl)
    return jnp.concatenate([k_dec.T, q_dec.T, chunk_dec[:, None], decay_mat.reshape(N_HEADS, tl * tl)], axis=1)


def _swa_sample_kernel(sink_ref, qkvg_ref, kv_ref, ckt_ref, cvt_ref, bias_ref, kbuf_ref, vbuf_ref,
                       swa_ref, ck_out_ref, cv_out_ref):
    del kbuf_ref, vbuf_ref
    tb = ckt_ref.shape[0]
    tl = qkvg_ref.shape[0] // tb
    nt = (((1,), (1,)), ((), ()))
    pairs_per_kv = N_HEADS // KV_HEADS // 2
    low = _lane_mask_low()
    zero_t =jnp.zeros((HEAD_DIM, WINDOW), BF16)
    front = jnp.zeros((WINDOW - tl, KV_W), F32)
    is_new =lax.broadcasted_iota(jnp.int32, (1, WINDOW), 1) >= WINDOW - tl
    bias = bias_ref.at[0]

    def block_diag(t):
        return jnp.concatenate([jnp.concatenate([t, zero_t], axis=1), jnp.concatenate([zero_t, t], axis=1)], axis=0)

    wave1 = []
    for e in range(tb):
        rows = slice(e * tl, (e + 1) * tl)
        sq32 = qkvg_ref[rows, 4 * RET_W:5 * RET_W].astype(F32)
        k_new = kv_ref[rows, 0:KV_W]
        v_new = kv_ref[rows, KV_W:2 * KV_W]
        k_rot = pltpu.roll(k_new, HEAD_DIM, 1)
        v_rot = pltpu.roll(v_new, HEAD_DIM, 1)
        k_placed = jnp.concatenate([front, k_new], axis=0).T
        v_placed = jnp.concatenate([front, v_new], axis=0).T
        per_g = []
        for g in range(KV_HEADS):
            kt = ckt_ref[e, g]
            vt = cvt_ref[e, g]
            hd = slice(g * HEAD_DIM, (g + 1) * HEAD_DIM)
            ck_out_ref[e, g] = jnp.where(is_new, k_placed[hd, :], pltpu.roll(kt, WINDOW - tl, 1))
            cv_out_ref[e, g] = jnp.where(is_new, v_placed[hd, :], pltpu.roll(vt, WINDOW - tl, 1))
            lhs = jnp.concatenate([sq32[:, (g * pairs_per_kv + pp) * LANES:(g * pairs_per_kv + pp + 1) * LANES]
                                   for pp in range(pairs_per_kv)], axis=0).astype(BF16)
            kn_lo, kn_hi = (k_new, k_rot) if g == 0 else (k_rot, k_new)
            vn_lo, vn_hi = (v_new, v_rot) if g == 0 else (v_rot, v_new)
            kn = [jnp.where(low, kn_lo, 0.0).astype(BF16), jnp.where(low, 0.0, kn_hi).astype(BF16)]
            vn = [jnp.where(low, vn_lo, 0.0).astype(BF16), jnp.where(low, 0.0, vn_hi).astype(BF16)]
            lc = jnp.dot(lhs, block_diag(kt.astype(BF16)), preferred_element_type=F32)
            ln = [lax.dot_general(lhs, kn[s], nt, preferred_element_type=F32) for s in range(2)]
            per_g.append((lc, ln, vn, block_diag(vt.astype(BF16))))
        wave1.append(per_g)

    swa_rows = []
    for e in range(tb):
        outs = []
        for g in range(KV_HEADS):
            lc, ln, vn, vbd = wave1[e][g]
            pcs, pns, dens = [], [[], []], []
            for pp in range(pairs_per_kv):
                r = slice(pp * tl, (pp + 1) * tl)
                den_pair = []
                for s in range(2):
                    hh = 2 * (g * pairs_per_kv + pp) + s
                    lo_c = lc[r, s * WINDOW:(s + 1) * WINDOW] + bias[hh, 0:tl, 0:WINDOW]
                    lo_n = ln[s][r, :] + bias[hh, 0:tl, WINDOW:WINDOW + tl]
                    sink = sink_ref[hh]
                    m = jnp.maximum(jnp.maximum(jnp.max(lo_c, axis=-1, keepdims=True),
                                                jnp.max(lo_n, axis=-1, keepdims=True)), sink)
                    pc = jnp.exp(lo_c - m)
                    pn = jnp.exp(lo_n - m)
                    den_pair.append(jnp.sum(pc, axis=-1, keepdims=True) + jnp.sum(pn, axis=-1, keepdims=True)
                                    + jnp.exp(sink - m))
                    pcs.append((pp, pc))
                    pns[s].append(pn)
                dens.append(jnp.where(low, den_pair[0], den_pair[1]))
            pc_rows = [jnp.concatenate([x[1] for x in pcs if x[0] == pp], axis=1) for pp in range(pairs_per_kv)]
            pc_all = jnp.concatenate(pc_rows, axis=0).astype(BF16)
            oo = lax.dot_general(pc_all, vbd, nt, preferred_element_type=F32)
            for s in range(2):
                oo = oo + jnp.dot(jnp.concatenate(pns[s], axis=0).astype(BF16), vn[s], preferred_element_type=F32)
            oo = oo / jnp.concatenate(dens, axis=0)
            outs += [oo[pp * tl:(pp + 1) * tl, :] for pp in range(pairs_per_kv)]
        swa_rows.append(jnp.concatenate(outs, axis=1))
    swa_ref[...] = jnp.concatenate(swa_rows, axis=0).astype(BF16)


def _swa_sample_call(qkvg, kv, ckt, cvt, kbuf, vbuf, layer, bias, sinks, b, tl, tb):
    cache_blk = (None, tb, KV_HEADS, HEAD_DIM, WINDOW)
    layer5 = lambda i: (layer, i, 0, 0, 0)
    return pl.pallas_call(
        _swa_sample_kernel,
        grid=(b // tb,),
        in_specs=[pl.BlockSpec(memory_space=pltpu.SMEM),
                  pl.BlockSpec((tb * tl, QKVG_W), lambda i: (i, 0)),
                  pl.BlockSpec((tb * tl, 2 * KV_W), lambda i: (i, 0)),
                  pl.BlockSpec(cache_blk, layer5),
                  pl.BlockSpec(cache_blk, layer5),
                  pl.BlockSpec((1, N_HEADS, CHUNK, 2 * CHUNK), lambda i: (0, 0, 0, 0)),
                  pl.BlockSpec(memory_space=pl.ANY),
                  pl.BlockSpec(memory_space=pl.ANY)],
        out_specs=[pl.BlockSpec((tb * tl, RET_W), lambda i: (i, 0)),
                   pl.BlockSpec(cache_blk, layer5),
                   pl.BlockSpec(cache_blk, layer5)],
        out_shape=[jax.ShapeDtypeStruct((b * tl, RET_W), BF16),
                   jax.ShapeDtypeStruct(kbuf.shape, F32),
                   jax.ShapeDtypeStruct(vbuf.shape, F32)],
        input_output_aliases={6: 1, 7: 2},
        compiler_params=_cparams(1),
        name="swa_sample",
    )(sinks, qkvg, kv, ckt, cvt, bias, kbuf, vbuf)


def _layer_norm(x, g, b):
    mu = jnp.mean(x, axis=-1, keepdims=True)
    xc = x - mu
    var = jnp.mean(xc * xc, axis=-1, keepdims=True)
    return xc * lax.rsqrt(var + LN_EPS) * g + b


def _ffn_kernel(alpha, carry_rows, mix_ref, x_ref, g1_ref, sh2_ref, sc2_ref, g2_ref, wout_ref, ln1g_ref, ln1b_ref,
                wup_ref, cw_ref, cb_ref, prev_ref, wdn_ref, ln2g_ref, ln2b_ref, xo_ref, tail_ref, *rest):
    *zero_refs, carry_scr = rest
    for z_ref in zero_refs:
        z_ref[...] = jnp.zeros(z_ref.shape, z_ref.dtype)
    tb, tl, _ = x_ref.shape
    tm = tb * tl
    sm = FF_ROWS
    n_sub = tm // sm
    sb, sl = (1, sm) if carry_rows else (tb // n_sub, tl)
    j = pl.program_id(1)

    def seq(ref, s):
        if carry_rows:
            return ref[:, s * sm:(s + 1) * sm, :] if ref.shape[1] == tl else ref[...]
        return ref[s * sb:(s + 1) * sb]

    if carry_rows:
        @pl.when(j == 0)
        def _():
            carry_scr[carry_rows - 2:carry_rows, :] = prev_ref[0]

    fix_rows = 8 if carry_rows else sm
    t_idx = lax.broadcasted_iota(jnp.int32, (fix_rows, 1), 0) % sl
    is_t0 = t_idx == 0
    is_t1 = t_idx == 1
    n_chunks = D_FF // FF_CHUNK

    def up_cols(c, half):
        return slice(half * D_FF + c * FF_CHUNK, half * D_FF + (c + 1) * FF_CHUNK)

    def pre(s):
        x1 = alpha * seq(x_ref, s) + seq(g1_ref, s) * y[s].reshape(sb, sl, D_MODEL)
        x1 = _layer_norm(x1, ln1g_ref[...], ln1b_ref[...])
        h2 = (x1 * (1.0 + seq(sc2_ref, s)) + seq(sh2_ref, s)).reshape(sm, D_MODEL).astype(BF16)
        return x1, h2

    def up_dots(h2, c):
        return [jnp.dot(h2, wup_ref[:, up_cols(c, half)], preferred_element_type=F32) for half in range(2)]

    def conv(s, up, cols):
        if carry_rows:
            src = carry_scr if s == 0 else tail_ref
            p0 = src[carry_rows - 2:carry_rows - 1, cols]
            p1 = src[carry_rows - 1:carry_rows, cols]
            tail_ref[:, cols] = up[sm - carry_rows:sm, :]
        else:
            p0 = jnp.broadcast_to(seq(prev_ref, s)[:, 0:1, cols], (sb, sl, FF_CHUNK)).reshape(sm, FF_CHUNK)
            p1 = jnp.broadcast_to(seq(prev_ref, s)[:, 1:2, cols], (sb, sl, FF_CHUNK)).reshape(sm, FF_CHUNK)
            tail_ref[s * sb:(s + 1) * sb, :, cols] = up.reshape(sb, sl, FF_CHUNK)[:, sl - 2:sl, :]
        r1 = pltpu.roll(up, 1, 0)
        r2 = pltpu.roll(up, 2, 0)
        s1 = jnp.where(is_t0, p1, r1[0:fix_rows])
        s2 = jnp.where(is_t0, p0, jnp.where(is_t1, p1, r2[0:fix_rows]))
        if fix_rows < sm:
            s1 = jnp.concatenate([s1, r1[fix_rows:]], axis=0)
            s2 = jnp.concatenate([s2, r2[fix_rows:]], axis=0)
        return cb_ref[:, cols] + s2 * cw_ref[0:1, cols] + s1 * cw_ref[1:2, cols] + up * cw_ref[2:3, cols]

    y = [jnp.dot(mix_ref[s * sm:(s + 1) * sm, :], wout_ref[...], preferred_element_type=F32) for s in range(n_sub)]
    items = [(s, c) for s in range(n_sub) for c in range(n_chunks)]
    staged, ups = {}, {}

    def issue_up(k):
        s, c = items[k]
        if c == 0:
            staged[s] = pre(s)
        ups[k] = up_dots(staged[s][1], c)

    for k in range(min(FF_LOOKAHEAD, len(items))):
        issue_up(k)
    acc = None
    for k, (s, c) in enumerate(items):
        if k + FF_LOOKAHEAD < len(items):
            issue_up(k + FF_LOOKAHEAD)
        up_a, up_b = ups.pop(k)
        ua = conv(s, up_a, up_cols(c, 0))
        ub = conv(s, up_b, up_cols(c, 1))
        gated = (ua * jax.nn.sigmoid(ua) * ub).astype(BF16)
        down = jnp.dot(gated, wdn_ref[c * FF_CHUNK:(c + 1) * FF_CHUNK, :], preferred_element_type=F32)
        acc = down if c == 0 else acc + down
        if c == n_chunks - 1:
            x1, _ = staged.pop(s)
            x2 = alpha * x1 + seq(g2_ref, s) * acc.reshape(sb, sl, D_MODEL)
            out = _layer_norm(x2, ln2g_ref[...], ln2b_ref[...])
            if carry_rows:
                xo_ref[:, s * sm:(s + 1) * sm, :] = out
            else:
                xo_ref[s * sb:(s + 1) * sb] = out

    if carry_rows:
        carry_scr[...] = tail_ref[...]


def _ffn_call(mix, x, mod, conv_prev, w_out, ln1_g, ln1_b, w_up, conv_w, conv_b, w_down, ln2_g, ln2_b,
              layer, alpha, tb, tl, zero_shapes=()):
    prev_layer = layer if conv_prev.shape[0] > 1 else 0
    b, l, _ = x.shape
    nj = l // tl
    tm = tb * tl
    n_steps = (b // tb) * nj
    zero_specs = []
    for shape in zero_shapes:
        per0 = n_steps // shape[0]
        blk = (1, shape[1] // per0) + tuple(shape[2:])
        assert per0 * shape[0] == n_steps and blk[1] * per0 == shape[1]
        zero_specs.append(pl.BlockSpec(blk, lambda i, j, per0=per0, nd=len(shape):
                                       ((i * nj + j) // per0, (i * nj + j) % per0) + (0,) * (nd - 2)))
    carry_rows = 8 if tb == 1 else 0
    assert carry_rows or (nj == 1 and tl == 8)
    if carry_rows:
        tail_spec = pl.BlockSpec((8, UP_W), lambda i, j: (i, 0))
        tail_shape = jax.ShapeDtypeStruct((b * 8, UP_W), F32)
    else:
        tail_spec = pl.BlockSpec((tb, 2, UP_W), lambda i, j: (i, 0, 0))
        tail_shape = jax.ShapeDtypeStruct((b, 2, UP_W), F32)
    kern = functools.partial(_ffn_kernel, alpha, carry_rows)
    return pl.pallas_call(
        kern,
        grid=(b // tb, nj),
        in_specs=[pl.BlockSpec((tm, 2 * RET_W), lambda i, j: (i * nj + j, 0)),
                  pl.BlockSpec((tb, tl, D_MODEL), lambda i, j: (i, j, 0)),
                  _mod_spec(tb, layer, 2), _mod_spec(tb, layer, 3), _mod_spec(tb, layer, 4), _mod_spec(tb, layer, 5),
                  _layer_spec((2 * RET_W, D_MODEL), layer),
                  _layer_spec((1, D_MODEL), layer), _layer_spec((1, D_MODEL), layer),
                  _layer_spec((D_MODEL, UP_W), layer),
                  _layer_spec((3, UP_W), layer), _layer_spec((1, UP_W), layer),
                  pl.BlockSpec((None, tb, 2, UP_W), lambda i, j: (prev_layer, i, 0, 0)),
                  _layer_spec((D_FF, D_MODEL), layer),
                  _layer_spec((1, D_MODEL), layer), _layer_spec((1, D_MODEL), layer)],
        out_specs=[pl.BlockSpec((tb, tl, D_MODEL), lambda i, j: (i, j, 0)),
                   tail_spec] + zero_specs,
        out_shape=[jax.ShapeDtypeStruct((b, l, D_MODEL), F32), tail_shape]
        + [jax.ShapeDtypeStruct(shape, F32) for shape in zero_shapes],
        scratch_shapes=[pltpu.VMEM((8, UP_W), F32)],
        compiler_params=_cparams(2),
        name="ffn",
    )(mix, x, mod, mod, mod, mod, w_out, ln1_g, ln1_b, w_up, conv_w, conv_b, conv_prev, w_down, ln2_g, ln2_b)


def kernel(x_prompt, x_sample, c_prompt, c_sample, state_ret, cache_swa_k, cache_swa_v, state_conv, rel_bias, w_ada, b_ada, w_in, swa_sinks, w_out, ln1_g, ln1_b, w_up, conv_w, conv_b, w_down, ln2_g, ln2_b):
    depth = w_ada.shape[0]
    bp, lp, _ = x_prompt.shape
    bs, ls, _ = x_sample.shape
    alpha = (2.0 * depth) ** 0.25
    tl_proj = 1024
    tl_p = 2 * FF_ROWS
    tb_s = FF_ROWS // ls
    tb_attn = 8
    blk_attn = 8

    c_all = jnp.concatenate([c_prompt, c_sample], axis=0)
    mod_p, mod_s = _ada_call(c_all, bp, w_ada, b_ada)
    bias = _bias_call(rel_bias)

    cos_p, sin_p = _rope_tables(jnp.arange(lp, dtype=jnp.int32))
    cos_s, sin_s = _rope_tables(PAST_LEN + jnp.arange(ls, dtype=jnp.int32))
    cos_s, sin_s = jnp.tile(cos_s, (tb_s, 1)), jnp.tile(sin_s, (tb_s, 1))
    tab_p = _prompt_tables()
    dec_s = _ret_sample_table(ls)
    conv0 = jnp.zeros((1, bp, 2, UP_W), F32)

    w_in_b = w_in.astype(BF16)
    vec = lambda a: a.reshape(depth, 1, a.shape[-1])
    ffn_w = (w_out.astype(BF16), vec(ln1_g), vec(ln1_b), w_up.astype(BF16), conv_w, vec(conv_b),
             w_down.astype(BF16), vec(ln2_g), vec(ln2_b))
    state5 = jnp.transpose(state_ret, (0, 2, 3, 4, 1))
    ckt = jnp.transpose(cache_swa_k, (0, 1, 3, 4, 2))
    cvt = jnp.transpose(cache_swa_v, (0, 1, 3, 4, 2))

    xp, xs = x_prompt, x_sample
    p_ret, p_k, p_v, p_conv = [], [], [], []
    s_conv = []
    for l in range(depth):
        qkvg, kv = _proj_call(xp, mod_p, w_in_b, l, cos_p, sin_p, 1, tl_proj)
        mix, r_p = _attn_prompt_call(qkvg, kv, bias, swa_sinks[l], tab_p, bp, lp, blk_attn)
        if l == 0:
            xp, tail, s_buf, k_buf, v_buf = _ffn_call(mix, xp, mod_p, conv0, *ffn_w, l, alpha, 1, tl_p,
                                                      (state5.shape, ckt.shape, cvt.shape))
        else:
            xp, tail = _ffn_call(mix, xp, mod_p, conv0, *ffn_w, l, alpha, 1, tl_p)
        kv3 =kv.reshape(bp, lp, 2, KV_HEADS, HEAD_DIM)
        p_ret.append(r_p)
        p_k.append(kv3[:, lp - WINDOW:, 0])
        p_v.append(kv3[:, lp - WINDOW:, 1])
        p_conv.append(tail.reshape(bp, 8, UP_W)[:, 6:8])

        qkvg, kv = _proj_call(xs, mod_s, w_in_b, l, cos_s, sin_s, tb_s, ls)
        tq = jnp.transpose(qkvg.reshape(bs, ls, 5, RET_W)[:, :, 0:4], (2, 1, 3, 0))
        ret_t, s_buf = _ret_sample_call(tq, state5, s_buf, l, dec_s)
        ret = jnp.transpose(ret_t, (2, 0, 1)).reshape(bs * ls, RET_W)
        swa, k_buf, v_buf = _swa_sample_call(qkvg, kv, ckt, cvt, k_buf, v_buf, l, bias, swa_sinks[l], bs, ls, tb_attn)
        mix = jnp.concatenate([ret, swa], axis=1)
        xs, tail = _ffn_call(mix, xs, mod_s, state_conv, *ffn_w, l, alpha, tb_s, ls)
        s_conv.append(tail)

    return (xp, xs,
            jnp.stack(p_ret), jnp.stack(p_k), jnp.stack(p_v), jnp.stack(p_conv),
            jnp.transpose(s_buf, (0, 4, 1, 2, 3)), jnp.transpose(k_buf, (0, 1, 4, 2, 3)),
            jnp.transpose(v_buf, (0, 1, 4, 2, 3)), jnp.stack(s_conv))
```

```python
import functools
import math

import jax
import jax.numpy as jnp
import numpy as np
from jax import lax
from jax.experimental import pallas as pl
from jax.experimental.pallas import tpu as pltpu

F32 = jnp.float32
BF16 = jnp.bfloat16

D_MODEL = 1024
HEAD_DIM = 64
N_HEADS = 8
RET_W = N_HEADS * HEAD_DIM
KV_HEADS = 2
KV_W = KV_HEADS * HEAD_DIM
D_FF = 2816
UP_W = 2 * D_FF
IN_COLS = 4 * RET_W + RET_W + 2 * KV_W
QKVG_W = 5 * RET_W
WINDOW = 128
CHUNK = 128
NUM_BUCKETS = 32
PAST_LEN = 8192
ROPE_BASE = 10000.0
LN_EPS = 1e-5
NEG_INF = -1e30
LANES = 128
FF_CHUNK = 256
FF_LOOKAHEAD = 2
FF_ROWS = 256
VMEM_LIMIT = 56 * 1024 * 1024


def _cparams(n_axes):
    return pltpu.CompilerParams(dimension_semantics=("arbitrary",) * n_axes,
                                vmem_limit_bytes=VMEM_LIMIT)


def _const_spec(shape):
    nd = len(shape)
    return pl.BlockSpec(shape, lambda *_: (0,) * nd, pipeline_mode=pl.Buffered(1))


def _rope_tables(pos):
    half = HEAD_DIM // 2
    inv = 1.0 / (ROPE_BASE ** (jnp.arange(half, dtype=F32) / half))
    ang = pos.astype(F32)[:, None] * inv[None, :]
    cos = jnp.cos(ang)
    sin = jnp.sin(ang)
    cos_h = jnp.concatenate([cos, cos], axis=-1)
    sin_h = jnp.concatenate([-sin, sin], axis=-1)
    return jnp.tile(cos_h, (1, N_HEADS)), jnp.tile(sin_h, (1, N_HEADS))


def _decay_tables(chunk):
    log_g = jnp.log(1.0 - 2.0 ** (-5.0 - jnp.arange(N_HEADS, dtype=F32)))
    idx = jnp.arange(chunk)
    diff = idx[:, None] - idx[None, :]
    decay_mat = jnp.where(diff[None] >= 0,
                          jnp.exp(log_g[:, None, None] * jnp.maximum(diff, 0)[None].astype(F32)), 0.0)
    q_dec = jnp.exp(log_g[None, :] * (idx[:, None] + 1).astype(F32))
    k_dec = jnp.exp(log_g[None, :] * (chunk - 1 - idx[:, None]).astype(F32))
    chunk_dec = jnp.exp(log_g * chunk)
    return decay_mat, q_dec, k_dec, chunk_dec


def _bucket_table(n_q, n_k):
    i = np.arange(n_q)[:, None]
    j = np.arange(n_k)[None, :]
    dist = i - j + WINDOW
    n = np.maximum(dist, 0)
    max_exact = NUM_BUCKETS // 2
    nf = np.maximum(n, max_exact).astype(np.float64)
    large = max_exact + (np.log(nf / max_exact) / math.log(WINDOW / max_exact)
                         * (NUM_BUCKETS - max_exact)).astype(np.int32)
    large = np.minimum(large, NUM_BUCKETS - 1)
    bucket = np.where(n < max_exact, n, large).astype(np.int32)
    valid = ((dist >= 0) & (dist < WINDOW)).astype(np.int32)
    return bucket, valid


def _ada_kernel(c_ref, w_ref, b_ref, op_ref, os_ref):
    c = c_ref[...]
    s = (c * jax.nn.sigmoid(c)).astype(BF16)
    mod = jnp.dot(s, w_ref[0].astype(BF16), preferred_element_type=F32) + b_ref[0]
    bp = op_ref.shape[0]
    op_ref[:, 0, :] = mod[0:bp]
    os_ref[:, 0, :] = mod[bp:]


def _ada_call(c_all, bp, w_ada, b_ada):
    depth = w_ada.shape[0]
    n_rows = c_all.shape[0]
    bs = n_rows - bp
    tn = 1536
    out_blk = lambda b: pl.BlockSpec((None, b, 1, tn), lambda l, n: (l, 0, 0, n))
    return pl.pallas_call(
        _ada_kernel,
        grid=(depth, 6 * D_MODEL // tn),
        in_specs=[pl.BlockSpec((n_rows, D_MODEL), lambda l, n: (0, 0)),
                  pl.BlockSpec((1, D_MODEL, tn), lambda l, n: (l, 0, n)),
                  pl.BlockSpec((1, 1, tn), lambda l, n: (l, 0, n))],
        out_specs=[out_blk(bp), out_blk(bs)],
        out_shape=[jax.ShapeDtypeStruct((depth, bp, 1, 6 * D_MODEL), F32),
                   jax.ShapeDtypeStruct((depth, bs, 1, 6 * D_MODEL), F32)],
        compiler_params=_cparams(2),
        name="ada",
    )(c_all, w_ada, b_ada.reshape(depth, 1, 6 * D_MODEL))


def _bias_kernel(rb_ref, bucket_ref, valid_ref, o_ref):
    bucket = bucket_ref[...]
    valid = valid_ref[...] > 0
    first_ok = lax.broadcasted_iota(jnp.int32, bucket.shape, 1) >= WINDOW
    for h in range(N_HEADS):
        acc = jnp.zeros(bucket.shape, F32)
        for b in range(NUM_BUCKETS):
            acc = jnp.where(bucket == b, rb_ref[b, h], acc)
        o_ref[0, h] = jnp.where(valid, acc, NEG_INF)
        o_ref[1, h] = jnp.where(valid & first_ok, acc, NEG_INF)


def _bias_call(rel_bias):
    bucket, valid = _bucket_table(CHUNK, 2 * CHUNK)
    return pl.pallas_call(
        _bias_kernel,
        in_specs=[pl.BlockSpec(memory_space=pltpu.SMEM),
                  pl.BlockSpec((CHUNK, 2 * CHUNK), lambda: (0, 0)),
                  pl.BlockSpec((CHUNK, 2 * CHUNK), lambda: (0, 0))],
        out_specs=pl.BlockSpec((2, N_HEADS, CHUNK, 2 * CHUNK), lambda: (0, 0, 0, 0)),
        out_shape=jax.ShapeDtypeStruct((2, N_HEADS, CHUNK, 2 * CHUNK), F32),
        name="swa_bias",
    )(rel_bias, jnp.asarray(bucket), jnp.asarray(valid))


def _swap_halves(x):
    lane = lax.broadcasted_iota(jnp.int32, (1, LANES), 1)
    first = (lane % HEAD_DIM) < (HEAD_DIM // 2)
    cols = []
    for c in range(x.shape[1] // LANES):
        xc = x[:, c * LANES:(c + 1) * LANES]
        cols.append(jnp.where(first, pltpu.roll(xc, LANES - HEAD_DIM // 2, 1), pltpu.roll(xc, HEAD_DIM // 2, 1)))
    return jnp.concatenate(cols, axis=1)


def _proj_kernel(x_ref, sc_ref, sh_ref, w_ref, cos_ref, sin_ref, *refs):
    n_cast = (len(refs) - 2) // 2
    qkvg_ref, kv_ref = refs[n_cast], refs[n_cast + 1]
    for src_ref, dst_ref in zip(refs[:n_cast], refs[n_cast + 2:]):
        dst_ref[...] = src_ref[...].astype(dst_ref.dtype)
    tb, tl, _ = x_ref.shape
    tm = tb * tl
    h = x_ref[...] * (1.0 + sc_ref[...]) + sh_ref[...]
    h = h.reshape(tm, D_MODEL).astype(BF16)
    proj = jnp.dot(h, w_ref[...], preferred_element_type=F32)
    cos = cos_ref[...]
    sin = sin_ref[...]
    rq = proj[:, 0:RET_W]
    rk = proj[:, RET_W:2 * RET_W]
    rq = rq * cos + _swap_halves(rq) * sin
    rk = (rk * cos + _swap_halves(rk) * sin) * (HEAD_DIM ** -0.5)
    rg = proj[:, 3 * RET_W:4 * RET_W]
    qkvg_ref[:, 0:RET_W] = rq.astype(BF16)
    qkvg_ref[:, RET_W:2 * RET_W] = rk.astype(BF16)
    qkvg_ref[:, 2 * RET_W:3 * RET_W] = proj[:, 2 * RET_W:3 * RET_W].astype(BF16)
    qkvg_ref[:, 3 * RET_W:4 * RET_W] = (rg * jax.nn.sigmoid(rg)).astype(BF16)
    qkvg_ref[:, 4 * RET_W:5 * RET_W] = (proj[:, 4 * RET_W:5 * RET_W] * (HEAD_DIM ** -0.5)).astype(BF16)
    kv_ref[...] = proj[:, 5 * RET_W:IN_COLS]


def _layer_spec(shape, layer):
    nd = len(shape)
    return pl.BlockSpec((None,) + tuple(shape), lambda *_: (layer,) + (0,) * nd, pipeline_mode=pl.Buffered(1))


def _mod_spec(tb, layer, k):
    return pl.BlockSpec((None, tb, 1, D_MODEL), lambda i, j: (layer, i, 0, k))


def _proj_call(x, mod, w_in, layer, cos, sin, tb, tl, cast=()):
    b, l, _ = x.shape
    nj = l // tl
    tm = tb * tl
    n_steps = (b // tb) * nj
    tab_idx = (lambda i, j: (j, 0)) if cos.shape[0] == l and nj > 1 else (lambda i, j: (0, 0))
    cast_in, cast_out, cast_shape = [], [], []
    for w in cast:
        _, rows, cols = w.shape
        blk = (None, rows // n_steps, cols)
        assert blk[1] * n_steps == rows and blk[1] % 16 == 0
        cast_in.append(pl.BlockSpec(blk, lambda i, j: (layer, i * nj + j, 0)))
        cast_out.append(pl.BlockSpec(blk, lambda i, j: (0, i * nj + j, 0)))
        cast_shape.append(jax.ShapeDtypeStruct((1, rows, cols), BF16))
    return pl.pallas_call(
        _proj_kernel,
        grid=(b // tb, nj),
        in_specs=[pl.BlockSpec((tb, tl, D_MODEL), lambda i, j: (i, j, 0)),
                  _mod_spec(tb, layer, 1),
                  _mod_spec(tb, layer, 0),
                  _layer_spec((D_MODEL, IN_COLS), layer),
                  pl.BlockSpec((tm, RET_W), tab_idx),
                  pl.BlockSpec((tm, RET_W), tab_idx)] + cast_in,
        out_specs=[pl.BlockSpec((tm, QKVG_W), lambda i, j: (i * nj + j, 0)),
                   pl.BlockSpec((tm, 2 * KV_W), lambda i, j: (i * nj + j, 0))] + cast_out,
        out_shape=[jax.ShapeDtypeStruct((b * l, QKVG_W), BF16),
                   jax.ShapeDtypeStruct((b * l, 2 * KV_W), F32)] + cast_shape,
        compiler_params=_cparams(2),
        name="proj",
    )(x, mod, mod, w_in, cos, sin, *cast)


def _lane_mask_low():
    return lax.broadcasted_iota(jnp.int32, (1, LANES), 1) < HEAD_DIM


def _head_norm(o, ones_bd):
    outs = []
    gw = ones_bd.shape[0]
    for g in range(RET_W // gw):
        og = o[:, g * gw:(g + 1) * gw]
        mu = jnp.dot(og.astype(BF16), ones_bd, preferred_element_type=F32)
        d = og - mu
        var = jnp.dot((d * d).astype(BF16), ones_bd, preferred_element_type=F32)
        outs.append(d * lax.rsqrt(var + LN_EPS))
    return jnp.concatenate(outs, axis=1)


def _swa_operands(k_all, v_all, sum_columns):
    low = _lane_mask_low()
    zero = jnp.zeros_like(k_all)
    ones_low = jnp.broadcast_to(jnp.where(low, 1.0, 0.0), k_all.shape)
    ones_cols = jnp.concatenate([ones_low, 1.0 - ones_low], axis=0).astype(BF16)
    k_rot = pltpu.roll(k_all, HEAD_DIM, 1)
    v_rot = pltpu.roll(v_all, HEAD_DIM, 1)
    ops = []
    for g in range(KV_HEADS):
        k_lo, k_hi = (k_all, k_rot) if g == 0 else (k_rot, k_all)
        v_lo, v_hi = (v_all, v_rot) if g == 0 else (v_rot, v_all)
        kc = jnp.concatenate([jnp.where(low, k_lo, zero), jnp.where(low, zero, k_hi)], axis=0)
        vc = jnp.concatenate([jnp.where(low, v_lo, zero), jnp.where(low, zero, v_hi)], axis=0)
        ops.append((kc, jnp.concatenate([vc, ones_cols], axis=1) if sum_columns else vc))
    return ops


def _swa_softmax(lg, p, bias_ref, sink_ref, with_sum):
    rows = lg.shape[0]
    low = _lane_mask_low()
    probs, dens = [], []
    for s in range(2):
        hh = 2 * p + s
        lo = lg[:, s * 2 * CHUNK:(s + 1) * 2 * CHUNK] + bias_ref[hh, 0:rows, :]
        sink = sink_ref[hh]
        m = jnp.maximum(jnp.max(lo, axis=-1, keepdims=True), sink)
        pe = jnp.exp(lo - m)
        den = jnp.exp(sink - m)
        if with_sum:
            den = den + jnp.sum(pe, axis=-1, keepdims=True)
        probs.append(pe)
        dens.append(den)
    return jnp.concatenate(probs, axis=1), jnp.where(low, dens[0], dens[1])


def _swa_values(probs, den_term, vc):
    oo = jnp.dot(probs, vc, preferred_element_type=F32)
    if vc.shape[1] == 2 * LANES:
        return oo[:, 0:LANES] / (oo[:, LANES:2 * LANES] + den_term)
    return oo / den_term


def _attn_prompt_kernel(sink_ref, qkvg_ref, kv_ref, bias_ref, bias_all_ref, dm_ref, qd_ref, kd_ref, cd_ref, ones_ref,
                        mix_ref, s_out_ref, s_scr, pk_scr, pv_scr):
    i = pl.program_id(1)

    @pl.when(i == 0)
    def _():
        s_scr[...] = jnp.zeros_like(s_scr)
        pk_scr[...] = jnp.zeros_like(pk_scr)
        pv_scr[...] = jnp.zeros_like(pv_scr)

    low = _lane_mask_low()
    zero = jnp.zeros((CHUNK, LANES), BF16)
    n_pairs = N_HEADS // 2
    pairs_per_kv = n_pairs // KV_HEADS
    gw = cd_ref.shape[1]
    n_groups = RET_W // gw
    n_blk = qkvg_ref.shape[0] // CHUNK
    nt = (((1,), (1,)), ((), ()))
    tn = (((0,), (0,)), ((), ()))
    ones_bd = ones_ref[...]

    s_cur = [s_scr[g] for g in range(n_groups)]
    k_prev, v_prev = pk_scr[...], pv_scr[...]
    wave1, mid = [], []
    for c in range(n_blk):
        rows = slice(c * CHUNK, (c + 1) * CHUNK)
        q = qkvg_ref[rows, 0:RET_W]
        k = qkvg_ref[rows, RET_W:2 * RET_W]
        v = qkvg_ref[rows, 2 * RET_W:3 * RET_W]
        sq = qkvg_ref[rows, 4 * RET_W:5 * RET_W]
        kd = (k.astype(F32) * kd_ref[...]).astype(BF16)
        ret_ops = []
        for p in range(n_pairs):
            kp = k[:, p * LANES:(p + 1) * LANES]
            vp = v[:, p * LANES:(p + 1) * LANES]
            ret_ops.append((jnp.concatenate([jnp.where(low, kp, zero), jnp.where(low, zero, kp)], axis=0),
                            jnp.concatenate([jnp.where(low, vp, zero), jnp.where(low, zero, vp)], axis=0)))
        k_new = kv_ref[rows, 0:KV_W].astype(BF16)
        v_new = kv_ref[rows, KV_W:2 * KV_W].astype(BF16)
        swa_ops = _swa_operands(jnp.concatenate([k_prev, k_new], axis=0), jnp.concatenate([v_prev, v_new], axis=0),
                                True)
        k_prev, v_prev = k_new, v_new

        scores = [lax.dot_general(q[:, p * LANES:(p + 1) * LANES], ret_ops[p][0], nt, preferred_element_type=F32)
                  for p in range(n_pairs)]
        cross = [jnp.dot(q[:, g * gw:(g + 1) * gw], s_cur[g].astype(BF16), preferred_element_type=F32)
                 for g in range(n_groups)]
        upd = [lax.dot_general(kd[:, g * gw:(g + 1) * gw], v[:, g * gw:(g + 1) * gw], tn,
                               preferred_element_type=F32) for g in range(n_groups)]
        logits = [lax.dot_general(
            jnp.concatenate([sq[:, (g * pairs_per_kv + pp) * LANES:(g * pairs_per_kv + pp + 1) * LANES]
                             for pp in range(pairs_per_kv)], axis=0),
            swa_ops[g][0], nt, preferred_element_type=F32) for g in range(KV_HEADS)]
        s_cur = [s_cur[g] * cd_ref[g] + jnp.where(cd_ref[g] > 0.0, upd[g], 0.0) for g in range(n_groups)]
        wave1.append((scores, cross, logits, ret_ops, swa_ops))
    for g in range(n_groups):
        s_scr[g] = s_cur[g]
    pk_scr[...] = k_prev
    pv_scr[...] = v_prev

    for c in range(n_blk):
        scores, cross, logits, ret_ops, swa_ops = wave1[c]
        bias_c = bias_ref.at[0] if c == 0 else bias_all_ref.at[0]
        scores = [(scores[p] * dm_ref[p]).astype(BF16) for p in range(n_pairs)]
        soft = [[_swa_softmax(logits[g][pp * CHUNK:(pp + 1) * CHUNK, :], g * pairs_per_kv + pp, bias_c, sink_ref,
                              False) for pp in range(pairs_per_kv)] for g in range(KV_HEADS)]
        intra = [jnp.dot(scores[p], ret_ops[p][1], preferred_element_type=F32) for p in range(n_pairs)]
        swa = []
        for g in range(KV_HEADS):
            oo = _swa_values(jnp.concatenate([x[0].astype(BF16) for x in soft[g]], axis=0),
                             jnp.concatenate([x[1] for x in soft[g]], axis=0), swa_ops[g][1])
            swa += [oo[pp * CHUNK:(pp + 1) * CHUNK, :] for pp in range(pairs_per_kv)]
        o = jnp.concatenate(intra, axis=1) + jnp.concatenate(cross, axis=1) * qd_ref[...]
        mid.append((o, swa))

    o = jnp.concatenate([m[0] for m in mid], axis=0)
    ret = _head_norm(o, ones_bd) * qkvg_ref[:, 3 * RET_W:4 * RET_W].astype(F32)
    mix_ref[:, 0:RET_W] = ret.astype(BF16)
    for c in range(n_blk):
        mix_ref[c * CHUNK:(c + 1) * CHUNK, RET_W:2 * RET_W] = jnp.concatenate(mid[c][1], axis=1).astype(BF16)

    @pl.when(i == pl.num_programs(1) - 1)
    def _():
        for h in range(N_HEADS):
            g, hl = divmod(h, gw // HEAD_DIM)
            s_out_ref[0, h] = s_scr[g, hl * HEAD_DIM:(hl + 1) * HEAD_DIM, hl * HEAD_DIM:(hl + 1) * HEAD_DIM]


def _prompt_tables():
    decay_mat, q_dec, k_dec, chunk_dec = _decay_tables(CHUNK)
    dm = jnp.concatenate([decay_mat[0::2], decay_mat[1::2]], axis=2)
    qd = jnp.repeat(q_dec, HEAD_DIM, axis=1)
    kd = jnp.repeat(k_dec, HEAD_DIM, axis=1)
    gw = 4 * HEAD_DIM
    blk = np.kron(np.eye(4, dtype=np.float32), np.ones((HEAD_DIM, HEAD_DIM), np.float32))
    cd = jnp.repeat(chunk_dec.reshape(2, 4), HEAD_DIM, axis=1)[:, :, None] * jnp.asarray(blk)[None]
    ones_bd = jnp.asarray(blk / HEAD_DIM, BF16)
    assert cd.shape == (2, gw, gw)
    return dm, qd, kd, cd, ones_bd


def _attn_prompt_call(qkvg, kv, bias, sinks, tables, b, l, n_blk):
    dm, qd, kd, cd, ones_bd = tables
    rows = n_blk * CHUNK
    nc = l // rows
    gw = cd.shape[1]
    bias_spec = lambda idx: pl.BlockSpec((1, N_HEADS, CHUNK, 2 * CHUNK), idx)
    return pl.pallas_call(
        _attn_prompt_kernel,
        grid=(b, nc),
        in_specs=[pl.BlockSpec(memory_space=pltpu.SMEM),
                  pl.BlockSpec((rows, QKVG_W), lambda bi, i: (bi * nc + i, 0)),
                  pl.BlockSpec((rows, 2 * KV_W), lambda bi, i: (bi * nc + i, 0)),
                  bias_spec(lambda bi, i: (jnp.where(i == 0, 1, 0), 0, 0, 0)),
                  bias_spec(lambda bi, i: (0, 0, 0, 0)),
                  _const_spec(dm.shape), _const_spec(qd.shape), _const_spec(kd.shape),
                  _const_spec(cd.shape), _const_spec(ones_bd.shape)],
        out_specs=[pl.BlockSpec((rows, 2 * RET_W), lambda bi, i: (bi * nc + i, 0)),
                   pl.BlockSpec((1, N_HEADS, HEAD_DIM, HEAD_DIM), lambda bi, i: (bi, 0, 0, 0))],
        out_shape=[jax.ShapeDtypeStruct((b * l, 2 * RET_W), BF16),
                   jax.ShapeDtypeStruct((b, N_HEADS, HEAD_DIM, HEAD_DIM), F32)],
        scratch_shapes=[pltpu.VMEM((RET_W // gw, gw, gw), F32),
                        pltpu.VMEM((CHUNK, KV_W), BF16),
                        pltpu.VMEM((CHUNK, KV_W), BF16)],
        compiler_params=_cparams(2),
        name="attn_prompt",
    )(sinks, qkvg, kv, bias, bias, dm, qd, kd, cd, ones_bd)


def _ret_sample_kernel(dec_ref, tq_ref, s_ref, buf_ref, ret_ref, s_out_ref, q_scr, k_scr, v_scr, cross_scr):
    del buf_ref
    h = pl.program_id(0)
    n_t = tq_ref.shape[1]
    cd = dec_ref[h, 2 * n_t]
    for t in range(n_t):
        q_scr[t] = tq_ref[0, t].astype(F32)
        k_scr[t] = tq_ref[1, t].astype(F32) * dec_ref[h, t]
        v_scr[t] = tq_ref[2, t].astype(F32)
    cross_scr[...] = jnp.zeros_like(cross_scr)

    def per_d(d, carry):
        s_d = s_ref[d]
        upd = s_d * cd
        for t in range(n_t):
            upd = upd + k_scr[t, pl.ds(d, 1), :] * v_scr[t]
            cross_scr[t] += q_scr[t, pl.ds(d, 1), :] * s_d
        s_out_ref[d] = upd
        return carry

    lax.fori_loop(0, HEAD_DIM, per_d, 0)

    for t in range(n_t):
        o = cross_scr[t] * dec_ref[h, n_t + t]
        for t2 in range(t + 1):
            sc = jnp.sum(q_scr[t] * (tq_ref[1, t2].astype(F32)), axis=0, keepdims=True)
            o = o + (sc * dec_ref[h, 2 * n_t + 1 + t * n_t + t2]) * v_scr[t2]
        mu = jnp.mean(o, axis=0, keepdims=True)
        d0 = o - mu
        var = jnp.mean(d0 * d0, axis=0, keepdims=True)
        ret_ref[t] = (d0 * lax.rsqrt(var + LN_EPS) * tq_ref[3, t].astype(F32)).astype(ret_ref.dtype)


def _ret_sample_call(tq, state5, s_buf, layer, dec):
    _, n_t, _, b = tq.shape
    state_blk = (None, None, HEAD_DIM, HEAD_DIM, b)
    return pl.pallas_call(
        _ret_sample_kernel,
        grid=(N_HEADS,),
        in_specs=[pl.BlockSpec(memory_space=pltpu.SMEM),
                  pl.BlockSpec((4, n_t, HEAD_DIM, b), lambda h: (0, 0, h, 0)),
                  pl.BlockSpec(state_blk, lambda h: (layer, h, 0, 0, 0)),
                  pl.BlockSpec(memory_space=pl.ANY)],
        out_specs=[pl.BlockSpec((n_t, HEAD_DIM, b), lambda h: (0, h, 0)),
                   pl.BlockSpec(state_blk, lambda h: (layer, h, 0, 0, 0))],
        out_shape=[jax.ShapeDtypeStruct((n_t, RET_W, b), BF16),
                   jax.ShapeDtypeStruct(s_buf.shape, F32)],
        scratch_shapes=[pltpu.VMEM((n_t, HEAD_DIM, b), F32)] * 4,
        input_output_aliases={3: 1},
        compiler_params=_cparams(1),
        name="ret_sample",
    )(dec, tq, state5, s_buf)


def _ret_sample_table(tl):
    decay_mat, q_dec, k_dec, chunk_dec = _decay_tables(tl)
    return jnp.concatenate([k_dec.T, q_dec.T, chunk_dec[:, None], decay_mat.reshape(N_HEADS, tl * tl)], axis=1)


def _swa_sample_kernel(sink_ref, qkvg_ref, kv_ref, ckt_ref, cvt_ref, bias_ref, kbuf_ref, vbuf_ref,
                       swa_ref, ck_out_ref, cv_out_ref):
    del kbuf_ref, vbuf_ref
    tb = ckt_ref.shape[0]
    tl = qkvg_ref.shape[0] // tb
    nt = (((1,), (1,)), ((), ()))
    pairs_per_kv = N_HEADS // KV_HEADS // 2
    low = _lane_mask_low()
    zero_t =jnp.zeros((HEAD_DIM, WINDOW), BF16)
    front = jnp.zeros((WINDOW - tl, KV_W), F32)
    is_new =lax.broadcasted_iota(jnp.int32, (1, WINDOW), 1) >= WINDOW - tl
    bias = bias_ref.at[0]

    def block_diag(t):
        return jnp.concatenate([jnp.concatenate([t, zero_t], axis=1), jnp.concatenate([zero_t, t], axis=1)], axis=0)

    wave1 = []
    for e in range(tb):
        rows = slice(e * tl, (e + 1) * tl)
        sq32 = qkvg_ref[rows, 4 * RET_W:5 * RET_W].astype(F32)
        k_new = kv_ref[rows, 0:KV_W]
        v_new = kv_ref[rows, KV_W:2 * KV_W]
        k_rot = pltpu.roll(k_new, HEAD_DIM, 1)
        v_rot = pltpu.roll(v_new, HEAD_DIM, 1)
        k_placed = jnp.concatenate([front, k_new], axis=0).T
        v_placed = jnp.concatenate([front, v_new], axis=0).T
        per_g = []
        for g in range(KV_HEADS):
            kt = ckt_ref[e, g]
            vt = cvt_ref[e, g]
            hd = slice(g * HEAD_DIM, (g + 1) * HEAD_DIM)
            ck_out_ref[e, g] = jnp.where(is_new, k_placed[hd, :], pltpu.roll(kt, WINDOW - tl, 1))
            cv_out_ref[e, g] = jnp.where(is_new, v_placed[hd, :], pltpu.roll(vt, WINDOW - tl, 1))
            lhs = jnp.concatenate([sq32[:, (g * pairs_per_kv + pp) * LANES:(g * pairs_per_kv + pp + 1) * LANES]
                                   for pp in range(pairs_per_kv)], axis=0).astype(BF16)
            kn_lo, kn_hi = (k_new, k_rot) if g == 0 else (k_rot, k_new)
            vn_lo, vn_hi = (v_new, v_rot) if g == 0 else (v_rot, v_new)
            kn = [jnp.where(low, kn_lo, 0.0).astype(BF16), jnp.where(low, 0.0, kn_hi).astype(BF16)]
            vn = [jnp.where(low, vn_lo, 0.0).astype(BF16), jnp.where(low, 0.0, vn_hi).astype(BF16)]
            lc = jnp.dot(lhs, block_diag(kt.astype(BF16)), preferred_element_type=F32)
            ln = [lax.dot_general(lhs, kn[s], nt, preferred_element_type=F32) for s in range(2)]
            per_g.append((lc, ln, vn, block_diag(vt.astype(BF16))))
        wave1.append(per_g)

    swa_rows = []
    for e in range(tb):
        outs = []
        for g in range(KV_HEADS):
            lc, ln, vn, vbd = wave1[e][g]
            pcs, pns, dens = [], [[], []], []
            for pp in range(pairs_per_kv):
                r = slice(pp * tl, (pp + 1) * tl)
                den_pair = []
                for s in range(2):
                    hh = 2 * (g * pairs_per_kv + pp) + s
                    lo_c = lc[r, s * WINDOW:(s + 1) * WINDOW] + bias[hh, 0:tl, 0:WINDOW]
                    lo_n = ln[s][r, :] + bias[hh, 0:tl, WINDOW:WINDOW + tl]
                    sink = sink_ref[hh]
                    m = jnp.maximum(jnp.maximum(jnp.max(lo_c, axis=-1, keepdims=True),
                                                jnp.max(lo_n, axis=-1, keepdims=True)), sink)
                    pc = jnp.exp(lo_c - m)
                    pn = jnp.exp(lo_n - m)
                    den_pair.append(jnp.sum(pc, axis=-1, keepdims=True) + jnp.sum(pn, axis=-1, keepdims=True)
                                    + jnp.exp(sink - m))
                    pcs.append((pp, pc))
                    pns[s].append(pn)
                dens.append(jnp.where(low, den_pair[0], den_pair[1]))
            pc_rows = [jnp.concatenate([x[1] for x in pcs if x[0] == pp], axis=1) for pp in range(pairs_per_kv)]
            pc_all = jnp.concatenate(pc_rows, axis=0).astype(BF16)
            oo = lax.dot_general(pc_all, vbd, nt, preferred_element_type=F32)
            for s in range(2):
                oo = oo + jnp.dot(jnp.concatenate(pns[s], axis=0).astype(BF16), vn[s], preferred_element_type=F32)
            oo = oo / jnp.concatenate(dens, axis=0)
            outs += [oo[pp * tl:(pp + 1) * tl, :] for pp in range(pairs_per_kv)]
        swa_rows.append(jnp.concatenate(outs, axis=1))
    swa_ref[...] = jnp.concatenate(swa_rows, axis=0).astype(BF16)


def _swa_sample_call(qkvg, kv, ckt, cvt, kbuf, vbuf, layer, bias, sinks, b, tl, tb):
    cache_blk = (None, tb, KV_HEADS, HEAD_DIM, WINDOW)
    layer5 = lambda i: (layer, i, 0, 0, 0)
    return pl.pallas_call(
        _swa_sample_kernel,
        grid=(b // tb,),
        in_specs=[pl.BlockSpec(memory_space=pltpu.SMEM),
                  pl.BlockSpec((tb * tl, QKVG_W), lambda i: (i, 0)),
                  pl.BlockSpec((tb * tl, 2 * KV_W), lambda i: (i, 0)),
                  pl.BlockSpec(cache_blk, layer5),
                  pl.BlockSpec(cache_blk, layer5),
                  pl.BlockSpec((1, N_HEADS, CHUNK, 2 * CHUNK), lambda i: (0, 0, 0, 0)),
                  pl.BlockSpec(memory_space=pl.ANY),
                  pl.BlockSpec(memory_space=pl.ANY)],
        out_specs=[pl.BlockSpec((tb * tl, RET_W), lambda i: (i, 0)),
                   pl.BlockSpec(cache_blk, layer5),
                   pl.BlockSpec(cache_blk, layer5)],
        out_shape=[jax.ShapeDtypeStruct((b * tl, RET_W), BF16),
                   jax.ShapeDtypeStruct(kbuf.shape, F32),
                   jax.ShapeDtypeStruct(vbuf.shape, F32)],
        input_output_aliases={6: 1, 7: 2},
        compiler_params=_cparams(1),
        name="swa_sample",
    )(sinks, qkvg, kv, ckt, cvt, bias, kbuf, vbuf)


def _layer_norm(x, g, b):
    mu = jnp.mean(x, axis=-1, keepdims=True)
    xc = x - mu
    var = jnp.mean(xc * xc, axis=-1, keepdims=True)
    return xc * lax.rsqrt(var + LN_EPS) * g + b


def _ffn_kernel(alpha, carry_rows, mix_ref, x_ref, g1_ref, sh2_ref, sc2_ref, g2_ref, wout_ref, ln1g_ref, ln1b_ref,
                wup_ref, cw_ref, cb_ref, prev_ref, wdn_ref, ln2g_ref, ln2b_ref, xo_ref, tail_ref, *rest):
    *zero_refs, carry_scr = rest
    for z_ref in zero_refs:
        z_ref[...] = jnp.zeros(z_ref.shape, z_ref.dtype)
    tb, tl, _ = x_ref.shape
    tm = tb * tl
    sm = FF_ROWS
    n_sub = tm // sm
    sb, sl = (1, sm) if carry_rows else (tb // n_sub, tl)
    j = pl.program_id(1)

    def seq(ref, s):
        if carry_rows:
            return ref[:, s * sm:(s + 1) * sm, :] if ref.shape[1] == tl else ref[...]
        return ref[s * sb:(s + 1) * sb]

    if carry_rows:
        @pl.when(j == 0)
        def _():
            carry_scr[carry_rows - 2:carry_rows, :] = prev_ref[0]

    fix_rows = 8 if carry_rows else sm
    t_idx = lax.broadcasted_iota(jnp.int32, (fix_rows, 1), 0) % sl
    is_t0 = t_idx == 0
    is_t1 = t_idx == 1
    n_chunks = D_FF // FF_CHUNK

    def up_cols(c, half):
        return slice(half * D_FF + c * FF_CHUNK, half * D_FF + (c + 1) * FF_CHUNK)

    def pre(s):
        x1 = alpha * seq(x_ref, s) + seq(g1_ref, s) * y[s].reshape(sb, sl, D_MODEL)
        x1 = _layer_norm(x1, ln1g_ref[...], ln1b_ref[...])
        h2 = (x1 * (1.0 + seq(sc2_ref, s)) + seq(sh2_ref, s)).reshape(sm, D_MODEL).astype(BF16)
        return x1, h2

    def up_dots(h2, c):
        return [jnp.dot(h2, wup_ref[:, up_cols(c, half)], preferred_element_type=F32) for half in range(2)]

    def conv(s, up, cols):
        if carry_rows:
            src = carry_scr if s == 0 else tail_ref
            p0 = src[carry_rows - 2:carry_rows - 1, cols]
            p1 = src[carry_rows - 1:carry_rows, cols]
            tail_ref[:, cols] = up[sm - carry_rows:sm, :]
        else:
            p0 = jnp.broadcast_to(seq(prev_ref, s)[:, 0:1, cols], (sb, sl, FF_CHUNK)).reshape(sm, FF_CHUNK)
            p1 = jnp.broadcast_to(seq(prev_ref, s)[:, 1:2, cols], (sb, sl, FF_CHUNK)).reshape(sm, FF_CHUNK)
            tail_ref[s * sb:(s + 1) * sb, :, cols] = up.reshape(sb, sl, FF_CHUNK)[:, sl - 2:sl, :]
        r1 = pltpu.roll(up, 1, 0)
        r2 = pltpu.roll(up, 2, 0)
        s1 = jnp.where(is_t0, p1, r1[0:fix_rows])
        s2 = jnp.where(is_t0, p0, jnp.where(is_t1, p1, r2[0:fix_rows]))
        if fix_rows < sm:
            s1 = jnp.concatenate([s1, r1[fix_rows:]], axis=0)
            s2 = jnp.concatenate([s2, r2[fix_rows:]], axis=0)
        return cb_ref[:, cols] + s2 * cw_ref[0:1, cols] + s1 * cw_ref[1:2, cols] + up * cw_ref[2:3, cols]

    y = [jnp.dot(mix_ref[s * sm:(s + 1) * sm, :], wout_ref[...], preferred_element_type=F32) for s in range(n_sub)]
    items = [(s, c) for s in range(n_sub) for c in range(n_chunks)]
    staged, ups = {}, {}

    def issue_up(k):
        s, c = items[k]
        if c == 0:
            staged[s] = pre(s)
        ups[k] = up_dots(staged[s][1], c)

    for k in range(min(FF_LOOKAHEAD, len(items))):
        issue_up(k)
    acc = None
    for k, (s, c) in enumerate(items):
        if k + FF_LOOKAHEAD < len(items):
            issue_up(k + FF_LOOKAHEAD)
        up_a, up_b = ups.pop(k)
        ua = conv(s, up_a, up_cols(c, 0))
        ub = conv(s, up_b, up_cols(c, 1))
        gated = (ua * jax.nn.sigmoid(ua) * ub).astype(BF16)
        down = jnp.dot(gated, wdn_ref[c * FF_CHUNK:(c + 1) * FF_CHUNK, :], preferred_element_type=F32)
        acc = down if c == 0 else acc + down
        if c == n_chunks - 1:
            x1, _ = staged.pop(s)
            x2 = alpha * x1 + seq(g2_ref, s) * acc.reshape(sb, sl, D_MODEL)
            out = _layer_norm(x2, ln2g_ref[...], ln2b_ref[...])
            if carry_rows:
                xo_ref[:, s * sm:(s + 1) * sm, :] = out
            else:
                xo_ref[s * sb:(s + 1) * sb] = out

    if carry_rows:
        carry_scr[...] = tail_ref[...]


def _ffn_call(mix, x, mod, conv_prev, w_out, ln1_g, ln1_b, w_up, conv_w, conv_b, w_down, ln2_g, ln2_b,
              layer, alpha, tb, tl, zero_shapes=()):
    prev_layer = layer if conv_prev.shape[0] > 1 else 0
    w_layer = layer if w_up.shape[0] > 1 else 0
    b, l, _ = x.shape
    nj = l // tl
    tm = tb * tl
    n_steps = (b // tb) * nj
    zero_specs = []
    for shape in zero_shapes:
        per0 = n_steps // shape[0]
        blk = (1, shape[1] // per0) + tuple(shape[2:])
        assert per0 * shape[0] == n_steps and blk[1] * per0 == shape[1]
        zero_specs.append(pl.BlockSpec(blk, lambda i, j, per0=per0, nd=len(shape):
                                       ((i * nj + j) // per0, (i * nj + j) % per0) + (0,) * (nd - 2)))
    carry_rows = 8 if tb == 1 else 0
    assert carry_rows or (nj == 1 and tl == 8)
    if carry_rows:
        tail_spec = pl.BlockSpec((8, UP_W), lambda i, j: (i, 0))
        tail_shape = jax.ShapeDtypeStruct((b * 8, UP_W), F32)
    else:
        tail_spec = pl.BlockSpec((tb, 2, UP_W), lambda i, j: (i, 0, 0))
        tail_shape = jax.ShapeDtypeStruct((b, 2, UP_W), F32)
    kern = functools.partial(_ffn_kernel, alpha, carry_rows)
    return pl.pallas_call(
        kern,
        grid=(b // tb, nj),
        in_specs=[pl.BlockSpec((tm, 2 * RET_W), lambda i, j: (i * nj + j, 0)),
                  pl.BlockSpec((tb, tl, D_MODEL), lambda i, j: (i, j, 0)),
                  _mod_spec(tb, layer, 2), _mod_spec(tb, layer, 3), _mod_spec(tb, layer, 4), _mod_spec(tb, layer, 5),
                  _layer_spec((2 * RET_W, D_MODEL), w_layer),
                  _layer_spec((1, D_MODEL), layer), _layer_spec((1, D_MODEL), layer),
                  _layer_spec((D_MODEL, UP_W), w_layer),
                  _layer_spec((3, UP_W), layer), _layer_spec((1, UP_W), layer),
                  pl.BlockSpec((None, tb, 2, UP_W), lambda i, j: (prev_layer, i, 0, 0)),
                  _layer_spec((D_FF, D_MODEL), w_layer),
                  _layer_spec((1, D_MODEL), layer), _layer_spec((1, D_MODEL), layer)],
        out_specs=[pl.BlockSpec((tb, tl, D_MODEL), lambda i, j: (i, j, 0)),
                   tail_spec] + zero_specs,
        out_shape=[jax.ShapeDtypeStruct((b, l, D_MODEL), F32), tail_shape]
        + [jax.ShapeDtypeStruct(shape, F32) for shape in zero_shapes],
        scratch_shapes=[pltpu.VMEM((8, UP_W), F32)],
        compiler_params=_cparams(2),
        name="ffn",
    )(mix, x, mod, mod, mod, mod, w_out, ln1_g, ln1_b, w_up, conv_w, conv_b, conv_prev, w_down, ln2_g, ln2_b)


def kernel(x_prompt, x_sample, c_prompt, c_sample, state_ret, cache_swa_k, cache_swa_v, state_conv, rel_bias, w_ada, b_ada, w_in, swa_sinks, w_out, ln1_g, ln1_b, w_up, conv_w, conv_b, w_down, ln2_g, ln2_b):
    depth = w_ada.shape[0]
    bp, lp, _ = x_prompt.shape
    bs, ls, _ = x_sample.shape
    alpha = (2.0 * depth) ** 0.25
    tl_proj = 1024
    tl_p = 2 * FF_ROWS
    tb_s = FF_ROWS // ls
    tb_attn = 8
    blk_attn = 8

    c_all = jnp.concatenate([c_prompt, c_sample], axis=0)
    mod_p, mod_s = _ada_call(c_all, bp, w_ada, b_ada)
    bias = _bias_call(rel_bias)

    cos_p, sin_p = _rope_tables(jnp.arange(lp, dtype=jnp.int32))
    cos_s, sin_s = _rope_tables(PAST_LEN + jnp.arange(ls, dtype=jnp.int32))
    cos_s, sin_s = jnp.tile(cos_s, (tb_s, 1)), jnp.tile(sin_s, (tb_s, 1))
    tab_p = _prompt_tables()
    dec_s = _ret_sample_table(ls)
    conv0 = jnp.zeros((1, bp, 2, UP_W), F32)

    w_in_b = w_in.astype(BF16)
    vec = lambda a: a.reshape(depth, 1, a.shape[-1])
    ffn_vec = (vec(ln1_g), vec(ln1_b), conv_w, vec(conv_b), vec(ln2_g), vec(ln2_b))
    state5 = jnp.transpose(state_ret, (0, 2, 3, 4, 1))
    ckt = jnp.transpose(cache_swa_k, (0, 1, 3, 4, 2))
    cvt = jnp.transpose(cache_swa_v, (0, 1, 3, 4, 2))

    xp, xs = x_prompt, x_sample
    p_ret, p_k, p_v, p_conv = [], [], [], []
    s_conv = []
    for l in range(depth):
        qkvg, kv, wo_b, wu_b, wd_b = _proj_call(xp, mod_p, w_in_b, l, cos_p, sin_p, 1, tl_proj, (w_out, w_up, w_down))
        ffn_w = (wo_b, ffn_vec[0], ffn_vec[1], wu_b, ffn_vec[2], ffn_vec[3], wd_b, ffn_vec[4], ffn_vec[5])
        mix, r_p =_attn_prompt_call(qkvg, kv, bias, swa_sinks[l], tab_p, bp, lp, blk_attn)
        if l == 0:
            xp, tail, s_buf, k_buf, v_buf = _ffn_call(mix, xp, mod_p, conv0, *ffn_w, l, alpha, 1, tl_p,
                                                      (state5.shape, ckt.shape, cvt.shape))
        else:
            xp, tail = _ffn_call(mix, xp, mod_p, conv0, *ffn_w, l, alpha, 1, tl_p)
        kv3 =kv.reshape(bp, lp, 2, KV_HEADS, HEAD_DIM)
        p_ret.append(r_p)
        p_k.append(kv3[:, lp - WINDOW:, 0])
        p_v.append(kv3[:, lp - WINDOW:, 1])
        p_conv.append(tail.reshape(bp, 8, UP_W)[:, 6:8])

        qkvg, kv = _proj_call(xs, mod_s, w_in_b, l, cos_s, sin_s, tb_s, ls)
        tq = jnp.transpose(qkvg.reshape(bs, ls, 5, RET_W)[:, :, 0:4], (2, 1, 3, 0))
        ret_t, s_buf = _ret_sample_call(tq, state5, s_buf, l, dec_s)
        ret = jnp.transpose(ret_t, (2, 0, 1)).reshape(bs * ls, RET_W)
        swa, k_buf, v_buf = _swa_sample_call(qkvg, kv, ckt, cvt, k_buf, v_buf, l, bias, swa_sinks[l], bs, ls, tb_attn)
        mix = jnp.concatenate([ret, swa], axis=1)
        xs, tail = _ffn_call(mix, xs, mod_s, state_conv, *ffn_w, l, alpha, tb_s, ls)
        s_conv.append(tail)

    return (xp, xs,
            jnp.stack(p_ret), jnp.stack(p_k), jnp.stack(p_v), jnp.stack(p_conv),
            jnp.transpose(s_buf, (0, 4, 1, 2, 3)), jnp.transpose(k_buf, (0, 1, 4, 2, 3)),
            jnp.transpose(v_buf, (0, 1, 4, 2, 3)), jnp.stack(s_conv))
```

```python
import functools
import math

import jax
import jax.numpy as jnp
import numpy as np
from jax import lax
from jax.experimental import pallas as pl
from jax.experimental.pallas import tpu as pltpu

F32 = jnp.float32
BF16 = jnp.bfloat16

D_MODEL = 1024
HEAD_DIM = 64
N_HEADS = 8
RET_W = N_HEADS * HEAD_DIM
KV_HEADS = 2
KV_W = KV_HEADS * HEAD_DIM
D_FF = 2816
UP_W = 2 * D_FF
IN_COLS = 4 * RET_W + RET_W + 2 * KV_W
QKVG_W = 5 * RET_W
WINDOW = 128
CHUNK = 128
NUM_BUCKETS = 32
PAST_LEN = 8192
ROPE_BASE = 10000.0
LN_EPS = 1e-5
NEG_INF = -1e30
LANES = 128
FF_CHUNK = 256
FF_LOOKAHEAD = 2
FF_ROWS = 256
VMEM_LIMIT = 56 * 1024 * 1024


def _cparams(n_axes):
    return pltpu.CompilerParams(dimension_semantics=("arbitrary",) * n_axes,
                                vmem_limit_bytes=VMEM_LIMIT)


def _const_spec(shape):
    nd = len(shape)
    return pl.BlockSpec(shape, lambda *_: (0,) * nd, pipeline_mode=pl.Buffered(1))


def _rope_tables(pos):
    half = HEAD_DIM // 2
    inv = 1.0 / (ROPE_BASE ** (jnp.arange(half, dtype=F32) / half))
    ang = pos.astype(F32)[:, None] * inv[None, :]
    cos = jnp.cos(ang)
    sin = jnp.sin(ang)
    cos_h = jnp.concatenate([cos, cos], axis=-1)
    sin_h = jnp.concatenate([-sin, sin], axis=-1)
    return jnp.tile(cos_h, (1, N_HEADS)), jnp.tile(sin_h, (1, N_HEADS))


def _decay_tables(chunk):
    log_g = jnp.log(1.0 - 2.0 ** (-5.0 - jnp.arange(N_HEADS, dtype=F32)))
    idx = jnp.arange(chunk)
    diff = idx[:, None] - idx[None, :]
    decay_mat = jnp.where(diff[None] >= 0,
                          jnp.exp(log_g[:, None, None] * jnp.maximum(diff, 0)[None].astype(F32)), 0.0)
    q_dec = jnp.exp(log_g[None, :] * (idx[:, None] + 1).astype(F32))
    k_dec = jnp.exp(log_g[None, :] * (chunk - 1 - idx[:, None]).astype(F32))
    chunk_dec = jnp.exp(log_g * chunk)
    return decay_mat, q_dec, k_dec, chunk_dec


def _bucket_table(n_q, n_k):
    i = np.arange(n_q)[:, None]
    j = np.arange(n_k)[None, :]
    dist = i - j + WINDOW
    n = np.maximum(dist, 0)
    max_exact = NUM_BUCKETS // 2
    nf = np.maximum(n, max_exact).astype(np.float64)
    large = max_exact + (np.log(nf / max_exact) / math.log(WINDOW / max_exact)
                         * (NUM_BUCKETS - max_exact)).astype(np.int32)
    large = np.minimum(large, NUM_BUCKETS - 1)
    bucket = np.where(n < max_exact, n, large).astype(np.int32)
    valid = ((dist >= 0) & (dist < WINDOW)).astype(np.int32)
    return bucket, valid


def _ada_kernel(c_ref, w_ref, b_ref, op_ref, os_ref):
    c = c_ref[...]
    s = (c * jax.nn.sigmoid(c)).astype(BF16)
    mod = jnp.dot(s, w_ref[0].astype(BF16), preferred_element_type=F32) + b_ref[0]
    bp = op_ref.shape[0]
    op_ref[:, 0, :] = mod[0:bp]
    os_ref[:, 0, :] = mod[bp:]


def _ada_call(c_all, bp, w_ada, b_ada):
    depth = w_ada.shape[0]
    n_rows = c_all.shape[0]
    bs = n_rows - bp
    tn = 1536
    out_blk = lambda b: pl.BlockSpec((None, b, 1, tn), lambda l, n: (l, 0, 0, n))
    return pl.pallas_call(
        _ada_kernel,
        grid=(depth, 6 * D_MODEL // tn),
        in_specs=[pl.BlockSpec((n_rows, D_MODEL), lambda l, n: (0, 0)),
                  pl.BlockSpec((1, D_MODEL, tn), lambda l, n: (l, 0, n)),
                  pl.BlockSpec((1, 1, tn), lambda l, n: (l, 0, n))],
        out_specs=[out_blk(bp), out_blk(bs)],
        out_shape=[jax.ShapeDtypeStruct((depth, bp, 1, 6 * D_MODEL), F32),
                   jax.ShapeDtypeStruct((depth, bs, 1, 6 * D_MODEL), F32)],
        compiler_params=_cparams(2),
        name="ada",
    )(c_all, w_ada, b_ada.reshape(depth, 1, 6 * D_MODEL))


def _bias_kernel(rb_ref, bucket_ref, valid_ref, o_ref):
    bucket = bucket_ref[...]
    valid = valid_ref[...] > 0
    first_ok = lax.broadcasted_iota(jnp.int32, bucket.shape, 1) >= WINDOW
    for h in range(N_HEADS):
        acc = jnp.zeros(bucket.shape, F32)
        for b in range(NUM_BUCKETS):
            acc = jnp.where(bucket == b, rb_ref[b, h], acc)
        o_ref[0, h] = jnp.where(valid, acc, NEG_INF)
        o_ref[1, h] = jnp.where(valid & first_ok, acc, NEG_INF)


def _bias_call(rel_bias):
    bucket, valid = _bucket_table(CHUNK, 2 * CHUNK)
    return pl.pallas_call(
        _bias_kernel,
        in_specs=[pl.BlockSpec(memory_space=pltpu.SMEM),
                  pl.BlockSpec((CHUNK, 2 * CHUNK), lambda: (0, 0)),
                  pl.BlockSpec((CHUNK, 2 * CHUNK), lambda: (0, 0))],
        out_specs=pl.BlockSpec((2, N_HEADS, CHUNK, 2 * CHUNK), lambda: (0, 0, 0, 0)),
        out_shape=jax.ShapeDtypeStruct((2, N_HEADS, CHUNK, 2 * CHUNK), F32),
        name="swa_bias",
    )(rel_bias, jnp.asarray(bucket), jnp.asarray(valid))


def _swap_halves(x):
    lane = lax.broadcasted_iota(jnp.int32, (1, LANES), 1)
    first = (lane % HEAD_DIM) < (HEAD_DIM // 2)
    cols = []
    for c in range(x.shape[1] // LANES):
        xc = x[:, c * LANES:(c + 1) * LANES]
        cols.append(jnp.where(first, pltpu.roll(xc, LANES - HEAD_DIM // 2, 1), pltpu.roll(xc, HEAD_DIM // 2, 1)))
    return jnp.concatenate(cols, axis=1)


def _proj_kernel(x_ref, sc_ref, sh_ref, w_ref, cos_ref, sin_ref, *refs):
    n_cast = (len(refs) - 2) // 2
    qkvg_ref, kv_ref = refs[n_cast], refs[n_cast + 1]
    for src_ref, dst_ref in zip(refs[:n_cast], refs[n_cast + 2:]):
        dst_ref[...] = src_ref[...].astype(dst_ref.dtype)
    tb, tl, _ = x_ref.shape
    tm = tb * tl
    h = x_ref[...] * (1.0 + sc_ref[...]) + sh_ref[...]
    h = h.reshape(tm, D_MODEL).astype(BF16)
    proj = jnp.dot(h, w_ref[...], preferred_element_type=F32)
    cos = cos_ref[...]
    sin = sin_ref[...]
    rq = proj[:, 0:RET_W]
    rk = proj[:, RET_W:2 * RET_W]
    rq = rq * cos + _swap_halves(rq) * sin
    rk = (rk * cos + _swap_halves(rk) * sin) * (HEAD_DIM ** -0.5)
    rg = proj[:, 3 * RET_W:4 * RET_W]
    qkvg_ref[:, 0:RET_W] = rq.astype(BF16)
    qkvg_ref[:, RET_W:2 * RET_W] = rk.astype(BF16)
    qkvg_ref[:, 2 * RET_W:3 * RET_W] = proj[:, 2 * RET_W:3 * RET_W].astype(BF16)
    qkvg_ref[:, 3 * RET_W:4 * RET_W] = (rg * jax.nn.sigmoid(rg)).astype(BF16)
    qkvg_ref[:, 4 * RET_W:5 * RET_W] = (proj[:, 4 * RET_W:5 * RET_W] * (HEAD_DIM ** -0.5)).astype(BF16)
    kv_ref[...] = proj[:, 5 * RET_W:IN_COLS]


def _layer_spec(shape, layer):
    nd = len(shape)
    return pl.BlockSpec((None,) + tuple(shape), lambda *_: (layer,) + (0,) * nd, pipeline_mode=pl.Buffered(1))


def _mod_spec(tb, layer, k):
    return pl.BlockSpec((None, tb, 1, D_MODEL), lambda i, j: (layer, i, 0, k))


def _proj_call(x, mod, w_in, layer, cos, sin, tb, tl, cast=()):
    b, l, _ = x.shape
    nj = l // tl
    tm = tb * tl
    n_steps = (b // tb) * nj
    tab_idx = (lambda i, j: (j, 0)) if cos.shape[0] == l and nj > 1 else (lambda i, j: (0, 0))
    cast_in, cast_out, cast_shape = [], [], []
    for w in cast:
        _, rows, cols = w.shape
        blk = (None, rows // n_steps, cols)
        assert blk[1] * n_steps == rows and blk[1] % 16 == 0
        cast_in.append(pl.BlockSpec(blk, lambda i, j: (layer, i * nj + j, 0)))
        cast_out.append(pl.BlockSpec(blk, lambda i, j: (0, i * nj + j, 0)))
        cast_shape.append(jax.ShapeDtypeStruct((1, rows, cols), BF16))
    return pl.pallas_call(
        _proj_kernel,
        grid=(b // tb, nj),
        in_specs=[pl.BlockSpec((tb, tl, D_MODEL), lambda i, j: (i, j, 0)),
                  _mod_spec(tb, layer, 1),
                  _mod_spec(tb, layer, 0),
                  _layer_spec((D_MODEL, IN_COLS), layer if w_in.shape[0] > 1 else 0),
                  pl.BlockSpec((tm, RET_W), tab_idx),
                  pl.BlockSpec((tm, RET_W), tab_idx)] + cast_in,
        out_specs=[pl.BlockSpec((tm, QKVG_W), lambda i, j: (i * nj + j, 0)),
                   pl.BlockSpec((tm, 2 * KV_W), lambda i, j: (i * nj + j, 0))] + cast_out,
        out_shape=[jax.ShapeDtypeStruct((b * l, QKVG_W), BF16),
                   jax.ShapeDtypeStruct((b * l, 2 * KV_W), F32)] + cast_shape,
        compiler_params=_cparams(2),
        name="proj",
    )(x, mod, mod, w_in, cos, sin, *cast)


def _lane_mask_low():
    return lax.broadcasted_iota(jnp.int32, (1, LANES), 1) < HEAD_DIM


def _head_norm(o, ones_bd):
    outs = []
    gw = ones_bd.shape[0]
    for g in range(RET_W // gw):
        og = o[:, g * gw:(g + 1) * gw]
        mu = jnp.dot(og.astype(BF16), ones_bd, preferred_element_type=F32)
        d = og - mu
        var = jnp.dot((d * d).astype(BF16), ones_bd, preferred_element_type=F32)
        outs.append(d * lax.rsqrt(var + LN_EPS))
    return jnp.concatenate(outs, axis=1)


def _swa_operands(k_all, v_all, sum_columns):
    low = _lane_mask_low()
    zero = jnp.zeros_like(k_all)
    ones_low = jnp.broadcast_to(jnp.where(low, 1.0, 0.0), k_all.shape)
    ones_cols = jnp.concatenate([ones_low, 1.0 - ones_low], axis=0).astype(BF16)
    k_rot = pltpu.roll(k_all, HEAD_DIM, 1)
    v_rot = pltpu.roll(v_all, HEAD_DIM, 1)
    ops = []
    for g in range(KV_HEADS):
        k_lo, k_hi = (k_all, k_rot) if g == 0 else (k_rot, k_all)
        v_lo, v_hi = (v_all, v_rot) if g == 0 else (v_rot, v_all)
        kc = jnp.concatenate([jnp.where(low, k_lo, zero), jnp.where(low, zero, k_hi)], axis=0)
        vc = jnp.concatenate([jnp.where(low, v_lo, zero), jnp.where(low, zero, v_hi)], axis=0)
        ops.append((kc, jnp.concatenate([vc, ones_cols], axis=1) if sum_columns else vc))
    return ops


def _swa_softmax(lg, p, bias_ref, sink_ref, with_sum):
    rows = lg.shape[0]
    low = _lane_mask_low()
    probs, dens = [], []
    for s in range(2):
        hh = 2 * p + s
        lo = lg[:, s * 2 * CHUNK:(s + 1) * 2 * CHUNK] + bias_ref[hh, 0:rows, :]
        sink = sink_ref[hh]
        m = jnp.maximum(jnp.max(lo, axis=-1, keepdims=True), sink)
        pe = jnp.exp(lo - m)
        den = jnp.exp(sink - m)
        if with_sum:
            den = den + jnp.sum(pe, axis=-1, keepdims=True)
        probs.append(pe)
        dens.append(den)
    return jnp.concatenate(probs, axis=1), jnp.where(low, dens[0], dens[1])


def _swa_values(probs, den_term, vc):
    oo = jnp.dot(probs, vc, preferred_element_type=F32)
    if vc.shape[1] == 2 * LANES:
        return oo[:, 0:LANES] / (oo[:, LANES:2 * LANES] + den_term)
    return oo / den_term


def _attn_prompt_kernel(sink_ref, qkvg_ref, kv_ref, bias_ref, bias_all_ref, dm_ref, qd_ref, kd_ref, cd_ref, ones_ref,
                        mix_ref, s_out_ref, s_scr, pk_scr, pv_scr):
    i = pl.program_id(1)

    @pl.when(i == 0)
    def _():
        s_scr[...] = jnp.zeros_like(s_scr)
        pk_scr[...] = jnp.zeros_like(pk_scr)
        pv_scr[...] = jnp.zeros_like(pv_scr)

    low = _lane_mask_low()
    zero = jnp.zeros((CHUNK, LANES), BF16)
    n_pairs = N_HEADS // 2
    pairs_per_kv = n_pairs // KV_HEADS
    gw = cd_ref.shape[1]
    n_groups = RET_W // gw
    n_blk = qkvg_ref.shape[0] // CHUNK
    nt = (((1,), (1,)), ((), ()))
    tn = (((0,), (0,)), ((), ()))
    ones_bd = ones_ref[...]

    s_cur = [s_scr[g] for g in range(n_groups)]
    k_prev, v_prev = pk_scr[...], pv_scr[...]
    wave1, mid = [], []
    for c in range(n_blk):
        rows = slice(c * CHUNK, (c + 1) * CHUNK)
        q = qkvg_ref[rows, 0:RET_W]
        k = qkvg_ref[rows, RET_W:2 * RET_W]
        v = qkvg_ref[rows, 2 * RET_W:3 * RET_W]
        sq = qkvg_ref[rows, 4 * RET_W:5 * RET_W]
        kd = (k.astype(F32) * kd_ref[...]).astype(BF16)
        ret_ops = []
        for p in range(n_pairs):
            kp = k[:, p * LANES:(p + 1) * LANES]
            vp = v[:, p * LANES:(p + 1) * LANES]
            ret_ops.append((jnp.concatenate([jnp.where(low, kp, zero), jnp.where(low, zero, kp)], axis=0),
                            jnp.concatenate([jnp.where(low, vp, zero), jnp.where(low, zero, vp)], axis=0)))
        k_new = kv_ref[rows, 0:KV_W].astype(BF16)
        v_new = kv_ref[rows, KV_W:2 * KV_W].astype(BF16)
        swa_ops = _swa_operands(jnp.concatenate([k_prev, k_new], axis=0), jnp.concatenate([v_prev, v_new], axis=0),
                                True)
        k_prev, v_prev = k_new, v_new

        scores = [lax.dot_general(q[:, p * LANES:(p + 1) * LANES], ret_ops[p][0], nt, preferred_element_type=F32)
                  for p in range(n_pairs)]
        cross = [jnp.dot(q[:, g * gw:(g + 1) * gw], s_cur[g].astype(BF16), preferred_element_type=F32)
                 for g in range(n_groups)]
        upd = [lax.dot_general(kd[:, g * gw:(g + 1) * gw], v[:, g * gw:(g + 1) * gw], tn,
                               preferred_element_type=F32) for g in range(n_groups)]
        logits = [lax.dot_general(
            jnp.concatenate([sq[:, (g * pairs_per_kv + pp) * LANES:(g * pairs_per_kv + pp + 1) * LANES]
                             for pp in range(pairs_per_kv)], axis=0),
            swa_ops[g][0], nt, preferred_element_type=F32) for g in range(KV_HEADS)]
        s_cur = [s_cur[g] * cd_ref[g] + jnp.where(cd_ref[g] > 0.0, upd[g], 0.0) for g in range(n_groups)]
        wave1.append((scores, cross, logits, ret_ops, swa_ops))
    for g in range(n_groups):
        s_scr[g] = s_cur[g]
    pk_scr[...] = k_prev
    pv_scr[...] = v_prev

    for c in range(n_blk):
        scores, cross, logits, ret_ops, swa_ops = wave1[c]
        bias_c = bias_ref.at[0] if c == 0 else bias_all_ref.at[0]
        scores = [(scores[p] * dm_ref[p]).astype(BF16) for p in range(n_pairs)]
        soft = [[_swa_softmax(logits[g][pp * CHUNK:(pp + 1) * CHUNK, :], g * pairs_per_kv + pp, bias_c, sink_ref,
                              False) for pp in range(pairs_per_kv)] for g in range(KV_HEADS)]
        intra = [jnp.dot(scores[p], ret_ops[p][1], preferred_element_type=F32) for p in range(n_pairs)]
        swa = []
        for g in range(KV_HEADS):
            oo = _swa_values(jnp.concatenate([x[0].astype(BF16) for x in soft[g]], axis=0),
                             jnp.concatenate([x[1] for x in soft[g]], axis=0), swa_ops[g][1])
            swa += [oo[pp * CHUNK:(pp + 1) * CHUNK, :] for pp in range(pairs_per_kv)]
        o = jnp.concatenate(intra, axis=1) + jnp.concatenate(cross, axis=1) * qd_ref[...]
        mid.append((o, swa))

    o = jnp.concatenate([m[0] for m in mid], axis=0)
    ret = _head_norm(o, ones_bd) * qkvg_ref[:, 3 * RET_W:4 * RET_W].astype(F32)
    mix_ref[:, 0:RET_W] = ret.astype(BF16)
    for c in range(n_blk):
        mix_ref[c * CHUNK:(c + 1) * CHUNK, RET_W:2 * RET_W] = jnp.concatenate(mid[c][1], axis=1).astype(BF16)

    @pl.when(i == pl.num_programs(1) - 1)
    def _():
        for h in range(N_HEADS):
            g, hl = divmod(h, gw // HEAD_DIM)
            s_out_ref[0, h] = s_scr[g, hl * HEAD_DIM:(hl + 1) * HEAD_DIM, hl * HEAD_DIM:(hl + 1) * HEAD_DIM]


def _prompt_tables():
    decay_mat, q_dec, k_dec, chunk_dec = _decay_tables(CHUNK)
    dm = jnp.concatenate([decay_mat[0::2], decay_mat[1::2]], axis=2)
    qd = jnp.repeat(q_dec, HEAD_DIM, axis=1)
    kd = jnp.repeat(k_dec, HEAD_DIM, axis=1)
    gw = 4 * HEAD_DIM
    blk = np.kron(np.eye(4, dtype=np.float32), np.ones((HEAD_DIM, HEAD_DIM), np.float32))
    cd = jnp.repeat(chunk_dec.reshape(2, 4), HEAD_DIM, axis=1)[:, :, None] * jnp.asarray(blk)[None]
    ones_bd = jnp.asarray(blk / HEAD_DIM, BF16)
    assert cd.shape == (2, gw, gw)
    return dm, qd, kd, cd, ones_bd


def _attn_prompt_call(qkvg, kv, bias, sinks, tables, b, l, n_blk):
    dm, qd, kd, cd, ones_bd = tables
    rows = n_blk * CHUNK
    nc = l // rows
    gw = cd.shape[1]
    bias_spec = lambda idx: pl.BlockSpec((1, N_HEADS, CHUNK, 2 * CHUNK), idx)
    return pl.pallas_call(
        _attn_prompt_kernel,
        grid=(b, nc),
        in_specs=[pl.BlockSpec(memory_space=pltpu.SMEM),
                  pl.BlockSpec((rows, QKVG_W), lambda bi, i: (bi * nc + i, 0)),
                  pl.BlockSpec((rows, 2 * KV_W), lambda bi, i: (bi * nc + i, 0)),
                  bias_spec(lambda bi, i: (jnp.where(i == 0, 1, 0), 0, 0, 0)),
                  bias_spec(lambda bi, i: (0, 0, 0, 0)),
                  _const_spec(dm.shape), _const_spec(qd.shape), _const_spec(kd.shape),
                  _const_spec(cd.shape), _const_spec(ones_bd.shape)],
        out_specs=[pl.BlockSpec((rows, 2 * RET_W), lambda bi, i: (bi * nc + i, 0)),
                   pl.BlockSpec((1, N_HEADS, HEAD_DIM, HEAD_DIM), lambda bi, i: (bi, 0, 0, 0))],
        out_shape=[jax.ShapeDtypeStruct((b * l, 2 * RET_W), BF16),
                   jax.ShapeDtypeStruct((b, N_HEADS, HEAD_DIM, HEAD_DIM), F32)],
        scratch_shapes=[pltpu.VMEM((RET_W // gw, gw, gw), F32),
                        pltpu.VMEM((CHUNK, KV_W), BF16),
                        pltpu.VMEM((CHUNK, KV_W), BF16)],
        compiler_params=_cparams(2),
        name="attn_prompt",
    )(sinks, qkvg, kv, bias, bias, dm, qd, kd, cd, ones_bd)


def _ret_sample_kernel(dec_ref, tq_ref, s_ref, buf_ref, ret_ref, s_out_ref, q_scr, k_scr, v_scr, cross_scr):
    del buf_ref
    h = pl.program_id(0)
    n_t = tq_ref.shape[1]
    cd = dec_ref[h, 2 * n_t]
    for t in range(n_t):
        q_scr[t] = tq_ref[0, t].astype(F32)
        k_scr[t] = tq_ref[1, t].astype(F32) * dec_ref[h, t]
        v_scr[t] = tq_ref[2, t].astype(F32)
    cross_scr[...] = jnp.zeros_like(cross_scr)

    def per_d(d, carry):
        s_d = s_ref[d]
        upd = s_d * cd
        for t in range(n_t):
            upd = upd + k_scr[t, pl.ds(d, 1), :] * v_scr[t]
            cross_scr[t] += q_scr[t, pl.ds(d, 1), :] * s_d
        s_out_ref[d] = upd
        return carry

    lax.fori_loop(0, HEAD_DIM, per_d, 0)

    for t in range(n_t):
        o = cross_scr[t] * dec_ref[h, n_t + t]
        for t2 in range(t + 1):
            sc = jnp.sum(q_scr[t] * (tq_ref[1, t2].astype(F32)), axis=0, keepdims=True)
            o = o + (sc * dec_ref[h, 2 * n_t + 1 + t * n_t + t2]) * v_scr[t2]
        mu = jnp.mean(o, axis=0, keepdims=True)
        d0 = o - mu
        var = jnp.mean(d0 * d0, axis=0, keepdims=True)
        ret_ref[t] = (d0 * lax.rsqrt(var + LN_EPS) * tq_ref[3, t].astype(F32)).astype(ret_ref.dtype)


def _ret_sample_call(tq, state5, s_buf, layer, dec):
    _, n_t, _, b = tq.shape
    state_blk = (None, None, HEAD_DIM, HEAD_DIM, b)
    return pl.pallas_call(
        _ret_sample_kernel,
        grid=(N_HEADS,),
        in_specs=[pl.BlockSpec(memory_space=pltpu.SMEM),
                  pl.BlockSpec((4, n_t, HEAD_DIM, b), lambda h: (0, 0, h, 0)),
                  pl.BlockSpec(state_blk, lambda h: (layer, h, 0, 0, 0)),
                  pl.BlockSpec(memory_space=pl.ANY)],
        out_specs=[pl.BlockSpec((n_t, HEAD_DIM, b), lambda h: (0, h, 0)),
                   pl.BlockSpec(state_blk, lambda h: (layer, h, 0, 0, 0))],
        out_shape=[jax.ShapeDtypeStruct((n_t, RET_W, b), BF16),
                   jax.ShapeDtypeStruct(s_buf.shape, F32)],
        scratch_shapes=[pltpu.VMEM((n_t, HEAD_DIM, b), F32)] * 4,
        input_output_aliases={3: 1},
        compiler_params=_cparams(1),
        name="ret_sample",
    )(dec, tq, state5, s_buf)


def _ret_sample_table(tl):
    decay_mat, q_dec, k_dec, chunk_dec = _decay_tables(tl)
    return jnp.concatenate([k_dec.T, q_dec.T, chunk_dec[:, None], decay_mat.reshape(N_HEADS, tl * tl)], axis=1)


def _swa_sample_kernel(sink_ref, qkvg_ref, kv_ref, ckt_ref, cvt_ref, bias_ref, kbuf_ref, vbuf_ref,
                       swa_ref, ck_out_ref, cv_out_ref):
    del kbuf_ref, vbuf_ref
    tb = ckt_ref.shape[0]
    tl = qkvg_ref.shape[0] // tb
    nt = (((1,), (1,)), ((), ()))
    pairs_per_kv = N_HEADS // KV_HEADS // 2
    low = _lane_mask_low()
    zero_t =jnp.zeros((HEAD_DIM, WINDOW), BF16)
    front = jnp.zeros((WINDOW - tl, KV_W), F32)
    is_new =lax.broadcasted_iota(jnp.int32, (1, WINDOW), 1) >= WINDOW - tl
    bias = bias_ref.at[0]

    def block_diag(t):
        return jnp.concatenate([jnp.concatenate([t, zero_t], axis=1), jnp.concatenate([zero_t, t], axis=1)], axis=0)

    wave1 = []
    for e in range(tb):
        rows = slice(e * tl, (e + 1) * tl)
        sq32 = qkvg_ref[rows, 4 * RET_W:5 * RET_W].astype(F32)
        k_new = kv_ref[rows, 0:KV_W]
        v_new = kv_ref[rows, KV_W:2 * KV_W]
        k_rot = pltpu.roll(k_new, HEAD_DIM, 1)
        v_rot = pltpu.roll(v_new, HEAD_DIM, 1)
        k_placed = jnp.concatenate([front, k_new], axis=0).T
        v_placed = jnp.concatenate([front, v_new], axis=0).T
        per_g = []
        for g in range(KV_HEADS):
            kt = ckt_ref[e, g]
            vt = cvt_ref[e, g]
            hd = slice(g * HEAD_DIM, (g + 1) * HEAD_DIM)
            ck_out_ref[e, g] = jnp.where(is_new, k_placed[hd, :], pltpu.roll(kt, WINDOW - tl, 1))
            cv_out_ref[e, g] = jnp.where(is_new, v_placed[hd, :], pltpu.roll(vt, WINDOW - tl, 1))
            lhs = jnp.concatenate([sq32[:, (g * pairs_per_kv + pp) * LANES:(g * pairs_per_kv + pp + 1) * LANES]
                                   for pp in range(pairs_per_kv)], axis=0).astype(BF16)
            kn_lo, kn_hi = (k_new, k_rot) if g == 0 else (k_rot, k_new)
            vn_lo, vn_hi = (v_new, v_rot) if g == 0 else (v_rot, v_new)
            kn = [jnp.where(low, kn_lo, 0.0).astype(BF16), jnp.where(low, 0.0, kn_hi).astype(BF16)]
            vn = [jnp.where(low, vn_lo, 0.0).astype(BF16), jnp.where(low, 0.0, vn_hi).astype(BF16)]
            lc = jnp.dot(lhs, block_diag(kt.astype(BF16)), preferred_element_type=F32)
            ln = [lax.dot_general(lhs, kn[s], nt, preferred_element_type=F32) for s in range(2)]
            per_g.append((lc, ln, vn, block_diag(vt.astype(BF16))))
        wave1.append(per_g)

    swa_rows = []
    for e in range(tb):
        outs = []
        for g in range(KV_HEADS):
            lc, ln, vn, vbd = wave1[e][g]
            pcs, pns, dens = [], [[], []], []
            for pp in range(pairs_per_kv):
                r = slice(pp * tl, (pp + 1) * tl)
                den_pair = []
                for s in range(2):
                    hh = 2 * (g * pairs_per_kv + pp) + s
                    lo_c = lc[r, s * WINDOW:(s + 1) * WINDOW] + bias[hh, 0:tl, 0:WINDOW]
                    lo_n = ln[s][r, :] + bias[hh, 0:tl, WINDOW:WINDOW + tl]
                    sink = sink_ref[hh]
                    m = jnp.maximum(jnp.maximum(jnp.max(lo_c, axis=-1, keepdims=True),
                                                jnp.max(lo_n, axis=-1, keepdims=True)), sink)
                    pc = jnp.exp(lo_c - m)
                    pn = jnp.exp(lo_n - m)
                    den_pair.append(jnp.sum(pc, axis=-1, keepdims=True) + jnp.sum(pn, axis=-1, keepdims=True)
                                    + jnp.exp(sink - m))
                    pcs.append((pp, pc))
                    pns[s].append(pn)
                dens.append(jnp.where(low, den_pair[0], den_pair[1]))
            pc_rows = [jnp.concatenate([x[1] for x in pcs if x[0] == pp], axis=1) for pp in range(pairs_per_kv)]
            pc_all = jnp.concatenate(pc_rows, axis=0).astype(BF16)
            oo = lax.dot_general(pc_all, vbd, nt, preferred_element_type=F32)
            for s in range(2):
                oo = oo + jnp.dot(jnp.concatenate(pns[s], axis=0).astype(BF16), vn[s], preferred_element_type=F32)
            oo = oo / jnp.concatenate(dens, axis=0)
            outs += [oo[pp * tl:(pp + 1) * tl, :] for pp in range(pairs_per_kv)]
        swa_rows.append(jnp.concatenate(outs, axis=1))
    swa_ref[...] = jnp.concatenate(swa_rows, axis=0).astype(BF16)


def _swa_sample_call(qkvg, kv, ckt, cvt, kbuf, vbuf, layer, bias, sinks, b, tl, tb):
    cache_blk = (None, tb, KV_HEADS, HEAD_DIM, WINDOW)
    layer5 = lambda i: (layer, i, 0, 0, 0)
    return pl.pallas_call(
        _swa_sample_kernel,
        grid=(b // tb,),
        in_specs=[pl.BlockSpec(memory_space=pltpu.SMEM),
                  pl.BlockSpec((tb * tl, QKVG_W), lambda i: (i, 0)),
                  pl.BlockSpec((tb * tl, 2 * KV_W), lambda i: (i, 0)),
                  pl.BlockSpec(cache_blk, layer5),
                  pl.BlockSpec(cache_blk, layer5),
                  pl.BlockSpec((1, N_HEADS, CHUNK, 2 * CHUNK), lambda i: (0, 0, 0, 0)),
                  pl.BlockSpec(memory_space=pl.ANY),
                  pl.BlockSpec(memory_space=pl.ANY)],
        out_specs=[pl.BlockSpec((tb * tl, RET_W), lambda i: (i, 0)),
                   pl.BlockSpec(cache_blk, layer5),
                   pl.BlockSpec(cache_blk, layer5)],
        out_shape=[jax.ShapeDtypeStruct((b * tl, RET_W), BF16),
                   jax.ShapeDtypeStruct(kbuf.shape, F32),
                   jax.ShapeDtypeStruct(vbuf.shape, F32)],
        input_output_aliases={6: 1, 7: 2},
        compiler_params=_cparams(1),
        name="swa_sample",
    )(sinks, qkvg, kv, ckt, cvt, bias, kbuf, vbuf)


def _layer_norm(x, g, b):
    mu = jnp.mean(x, axis=-1, keepdims=True)
    xc = x - mu
    var = jnp.mean(xc * xc, axis=-1, keepdims=True)
    return xc * lax.rsqrt(var + LN_EPS) * g + b


def _ffn_kernel(alpha, carry_rows, n_cast, mix_ref, x_ref, g1_ref, sh2_ref, sc2_ref, g2_ref, wout_ref, ln1g_ref,
                ln1b_ref, wup_ref, cw_ref, cb_ref, prev_ref, wdn_ref, ln2g_ref, ln2b_ref, *rest):
    cast_in, (xo_ref, tail_ref), carry_scr = rest[:n_cast], rest[n_cast:n_cast + 2], rest[-1]
    zero_refs = rest[n_cast + 2:len(rest) - 1 - n_cast]
    for src_ref, dst_ref in zip(cast_in, rest[len(rest) - 1 - n_cast:-1]):
        dst_ref[...] = src_ref[...].astype(dst_ref.dtype)
    for z_ref in zero_refs:
        z_ref[...] = jnp.zeros(z_ref.shape, z_ref.dtype)
    tb, tl, _ = x_ref.shape
    tm = tb * tl
    sm = FF_ROWS
    n_sub = tm // sm
    sb, sl = (1, sm) if carry_rows else (tb // n_sub, tl)
    j = pl.program_id(1)

    def seq(ref, s):
        if carry_rows:
            return ref[:, s * sm:(s + 1) * sm, :] if ref.shape[1] == tl else ref[...]
        return ref[s * sb:(s + 1) * sb]

    if carry_rows:
        @pl.when(j == 0)
        def _():
            carry_scr[carry_rows - 2:carry_rows, :] = prev_ref[0]

    fix_rows = 8 if carry_rows else sm
    t_idx = lax.broadcasted_iota(jnp.int32, (fix_rows, 1), 0) % sl
    is_t0 = t_idx == 0
    is_t1 = t_idx == 1
    n_chunks = D_FF // FF_CHUNK

    def up_cols(c, half):
        return slice(half * D_FF + c * FF_CHUNK, half * D_FF + (c + 1) * FF_CHUNK)

    def pre(s):
        x1 = alpha * seq(x_ref, s) + seq(g1_ref, s) * y[s].reshape(sb, sl, D_MODEL)
        x1 = _layer_norm(x1, ln1g_ref[...], ln1b_ref[...])
        h2 = (x1 * (1.0 + seq(sc2_ref, s)) + seq(sh2_ref, s)).reshape(sm, D_MODEL).astype(BF16)
        return x1, h2

    def up_dots(h2, c):
        return [jnp.dot(h2, wup_ref[:, up_cols(c, half)], preferred_element_type=F32) for half in range(2)]

    def conv(s, up, cols):
        if carry_rows:
            src = carry_scr if s == 0 else tail_ref
            p0 = src[carry_rows - 2:carry_rows - 1, cols]
            p1 = src[carry_rows - 1:carry_rows, cols]
            tail_ref[:, cols] = up[sm - carry_rows:sm, :]
        else:
            up3 = up.reshape(sb, sl, FF_CHUNK)
            tail_ref[s * sb:(s + 1) * sb, :, cols] = up3[:, sl - 2:sl, :]
            prev = seq(prev_ref, s)
            p0, p1 = prev[:, 0:1, cols], prev[:, 1:2, cols]
            t3 = lax.broadcasted_iota(jnp.int32, (1, sl, 1), 1)
            s1 = jnp.where(t3 == 0, p1, pltpu.roll(up3, 1, 1))
            s2 = jnp.where(t3 == 0, p0, jnp.where(t3 == 1, p1, pltpu.roll(up3, 2, 1)))
            out = cb_ref[:, cols] + s2 * cw_ref[0:1, cols] + s1 * cw_ref[1:2, cols] + up3 * cw_ref[2:3, cols]
            return out.reshape(sm, FF_CHUNK)
        r1 = pltpu.roll(up, 1, 0)
        r2 = pltpu.roll(up, 2, 0)
        s1 = jnp.where(is_t0, p1, r1[0:fix_rows])
        s2 = jnp.where(is_t0, p0, jnp.where(is_t1, p1, r2[0:fix_rows]))
        if fix_rows < sm:
            s1 = jnp.concatenate([s1, r1[fix_rows:]], axis=0)
            s2 = jnp.concatenate([s2, r2[fix_rows:]], axis=0)
        return cb_ref[:, cols] + s2 * cw_ref[0:1, cols] + s1 * cw_ref[1:2, cols] + up * cw_ref[2:3, cols]

    y = [jnp.dot(mix_ref[s * sm:(s + 1) * sm, :], wout_ref[...], preferred_element_type=F32) for s in range(n_sub)]
    items = [(s, c) for s in range(n_sub) for c in range(n_chunks)]
    staged, ups = {}, {}

    def issue_up(k):
        s, c = items[k]
        if c == 0:
            staged[s] = pre(s)
        ups[k] = up_dots(staged[s][1], c)

    for k in range(min(FF_LOOKAHEAD, len(items))):
        issue_up(k)
    acc = None
    for k, (s, c) in enumerate(items):
        if k + FF_LOOKAHEAD < len(items):
            issue_up(k + FF_LOOKAHEAD)
        up_a, up_b = ups.pop(k)
        ua = conv(s, up_a, up_cols(c, 0))
        ub = conv(s, up_b, up_cols(c, 1))
        gated = (ua * jax.nn.sigmoid(ua) * ub).astype(BF16)
        down = jnp.dot(gated, wdn_ref[c * FF_CHUNK:(c + 1) * FF_CHUNK, :], preferred_element_type=F32)
        acc = down if c == 0 else acc + down
        if c == n_chunks - 1:
            x1, _ = staged.pop(s)
            x2 = alpha * x1 + seq(g2_ref, s) * acc.reshape(sb, sl, D_MODEL)
            out = _layer_norm(x2, ln2g_ref[...], ln2b_ref[...])
            if carry_rows:
                xo_ref[:, s * sm:(s + 1) * sm, :] = out
            else:
                xo_ref[s * sb:(s + 1) * sb] = out

    if carry_rows:
        carry_scr[...] = tail_ref[...]


def _ffn_call(mix, x, mod, conv_prev, w_out, ln1_g, ln1_b, w_up, conv_w, conv_b, w_down, ln2_g, ln2_b,
              layer, alpha, tb, tl, zero_shapes=(), cast=()):
    prev_layer = layer if conv_prev.shape[0] > 1 else 0
    w_layer = layer if w_up.shape[0] > 1 else 0
    b, l, _ = x.shape
    nj = l // tl
    tm = tb * tl
    n_steps = (b // tb) * nj
    zero_specs = []
    for shape in zero_shapes:
        per0 = n_steps // shape[0]
        blk = (1, shape[1] // per0) + tuple(shape[2:])
        assert per0 * shape[0] == n_steps and blk[1] * per0 == shape[1]
        zero_specs.append(pl.BlockSpec(blk, lambda i, j, per0=per0, nd=len(shape):
                                       ((i * nj + j) // per0, (i * nj + j) % per0) + (0,) * (nd - 2)))
    cast_in, cast_out, cast_shape = [], [], []
    for w, w_src_layer in cast:
        _, rows, cols = w.shape
        blk = (None, rows // n_steps, cols)
        assert blk[1] * n_steps == rows and blk[1] % 16 == 0
        cast_in.append(pl.BlockSpec(blk, lambda i, j, src=w_src_layer: (src, i * nj + j, 0)))
        cast_out.append(pl.BlockSpec(blk, lambda i, j: (0, i * nj + j, 0)))
        cast_shape.append(jax.ShapeDtypeStruct((1, rows, cols), BF16))
    carry_rows = 8 if tb == 1 else 0
    assert carry_rows or (nj == 1 and tl == 8)
    if carry_rows:
        tail_spec = pl.BlockSpec((8, UP_W), lambda i, j: (i, 0))
        tail_shape = jax.ShapeDtypeStruct((b * 8, UP_W), F32)
    else:
        tail_spec = pl.BlockSpec((tb, 2, UP_W), lambda i, j: (i, 0, 0))
        tail_shape = jax.ShapeDtypeStruct((b, 2, UP_W), F32)
    kern = functools.partial(_ffn_kernel, alpha, carry_rows, len(cast))
    return pl.pallas_call(
        kern,
        grid=(b // tb, nj),
        in_specs=[pl.BlockSpec((tm, 2 * RET_W), lambda i, j: (i * nj + j, 0)),
                  pl.BlockSpec((tb, tl, D_MODEL), lambda i, j: (i, j, 0)),
                  _mod_spec(tb, layer, 2), _mod_spec(tb, layer, 3), _mod_spec(tb, layer, 4), _mod_spec(tb, layer, 5),
                  _layer_spec((2 * RET_W, D_MODEL), w_layer),
                  _layer_spec((1, D_MODEL), layer), _layer_spec((1, D_MODEL), layer),
                  _layer_spec((D_MODEL, UP_W), w_layer),
                  _layer_spec((3, UP_W), layer), _layer_spec((1, UP_W), layer),
                  pl.BlockSpec((None, tb, 2, UP_W), lambda i, j: (prev_layer, i, 0, 0)),
                  _layer_spec((D_FF, D_MODEL), w_layer),
                  _layer_spec((1, D_MODEL), layer), _layer_spec((1, D_MODEL), layer)] + cast_in,
        out_specs=[pl.BlockSpec((tb, tl, D_MODEL), lambda i, j: (i, j, 0)),
                   tail_spec] + zero_specs + cast_out,
        out_shape=[jax.ShapeDtypeStruct((b, l, D_MODEL), F32), tail_shape]
        + [jax.ShapeDtypeStruct(shape, F32) for shape in zero_shapes] + cast_shape,
        scratch_shapes=[pltpu.VMEM((8, UP_W), F32)],
        compiler_params=_cparams(2),
        name="ffn",
    )(mix, x, mod, mod, mod, mod, w_out, ln1_g, ln1_b, w_up, conv_w, conv_b, conv_prev, w_down, ln2_g, ln2_b,
      *[w for w, _ in cast])


def kernel(x_prompt, x_sample, c_prompt, c_sample, state_ret, cache_swa_k, cache_swa_v, state_conv, rel_bias, w_ada, b_ada, w_in, swa_sinks, w_out, ln1_g, ln1_b, w_up, conv_w, conv_b, w_down, ln2_g, ln2_b):
    depth = w_ada.shape[0]
    bp, lp, _ = x_prompt.shape
    bs, ls, _ = x_sample.shape
    alpha = (2.0 * depth) ** 0.25
    tl_proj = 1024
    tl_p = 2 * FF_ROWS
    tb_s = FF_ROWS // ls
    tb_attn = 8
    blk_attn = 8

    c_all = jnp.concatenate([c_prompt, c_sample], axis=0)
    mod_p, mod_s = _ada_call(c_all, bp, w_ada, b_ada)
    bias = _bias_call(rel_bias)

    cos_p, sin_p = _rope_tables(jnp.arange(lp, dtype=jnp.int32))
    cos_s, sin_s = _rope_tables(PAST_LEN + jnp.arange(ls, dtype=jnp.int32))
    cos_s, sin_s = jnp.tile(cos_s, (tb_s, 1)), jnp.tile(sin_s, (tb_s, 1))
    tab_p = _prompt_tables()
    dec_s = _ret_sample_table(ls)
    conv0 = jnp.zeros((1, bp, 2, UP_W), F32)

    w_in_l = w_in[0:1].astype(BF16)
    vec = lambda a: a.reshape(depth, 1, a.shape[-1])
    ffn_vec = (vec(ln1_g), vec(ln1_b), conv_w, vec(conv_b), vec(ln2_g), vec(ln2_b))
    state5 = jnp.transpose(state_ret, (0, 2, 3, 4, 1))
    ckt = jnp.transpose(cache_swa_k, (0, 1, 3, 4, 2))
    cvt = jnp.transpose(cache_swa_v, (0, 1, 3, 4, 2))

    xp, xs = x_prompt, x_sample
    p_ret, p_k, p_v, p_conv = [], [], [], []
    s_conv = []
    for l in range(depth):
        qkvg, kv, wo_b, wu_b, wd_b = _proj_call(xp, mod_p, w_in_l, l, cos_p, sin_p, 1, tl_proj, (w_out, w_up, w_down))
        ffn_w = (wo_b, ffn_vec[0], ffn_vec[1], wu_b, ffn_vec[2], ffn_vec[3], wd_b, ffn_vec[4], ffn_vec[5])
        mix, r_p =_attn_prompt_call(qkvg, kv, bias, swa_sinks[l], tab_p, bp, lp, blk_attn)
        zero_shapes = (state5.shape, ckt.shape, cvt.shape) if l == 0 else ()
        cast_next = ((w_in, l + 1),) if l + 1 < depth else ()
        xp, tail, *extra = _ffn_call(mix, xp, mod_p, conv0, *ffn_w, l, alpha, 1, tl_p, zero_shapes, cast_next)
        if l == 0:
            s_buf, k_buf, v_buf = extra[0:3]
        w_in_next = extra[-1] if cast_next else None
        kv3 =kv.reshape(bp, lp, 2, KV_HEADS, HEAD_DIM)
        p_ret.append(r_p)
        p_k.append(kv3[:, lp - WINDOW:, 0])
        p_v.append(kv3[:, lp - WINDOW:, 1])
        p_conv.append(tail.reshape(bp, 8, UP_W)[:, 6:8])

        qkvg, kv = _proj_call(xs, mod_s, w_in_l, l, cos_s, sin_s, tb_s, ls)
        tq = jnp.transpose(qkvg.reshape(bs, ls, 5, RET_W)[:, :, 0:4], (2, 1, 3, 0))
        ret_t, s_buf = _ret_sample_call(tq, state5, s_buf, l, dec_s)
        ret = jnp.transpose(ret_t, (2, 0, 1)).reshape(bs * ls, RET_W)
        swa, k_buf, v_buf = _swa_sample_call(qkvg, kv, ckt, cvt, k_buf, v_buf, l, bias, swa_sinks[l], bs, ls, tb_attn)
        mix = jnp.concatenate([ret, swa], axis=1)
        xs, tail = _ffn_call(mix, xs, mod_s, state_conv, *ffn_w, l, alpha, tb_s, ls)
        s_conv.append(tail)
        w_in_l = w_in_next

    return (xp, xs,
            jnp.stack(p_ret), jnp.stack(p_k), jnp.stack(p_v), jnp.stack(p_conv),
            jnp.transpose(s_buf, (0, 4, 1, 2, 3)), jnp.transpose(k_buf, (0, 1, 4, 2, 3)),
            jnp.transpose(v_buf, (0, 1, 4, 2, 3)), jnp.stack(s_conv))
```

```python
import functools
import math

import jax
import jax.numpy as jnp
import numpy as np
from jax import lax
from jax.experimental import pallas as pl
from jax.experimental.pallas import tpu as pltpu

F32 = jnp.float32
BF16 = jnp.bfloat16

D_MODEL = 1024
HEAD_DIM = 64
N_HEADS = 8
RET_W = N_HEADS * HEAD_DIM
KV_HEADS = 2
KV_W = KV_HEADS * HEAD_DIM
D_FF = 2816
UP_W = 2 * D_FF
IN_COLS = 4 * RET_W + RET_W + 2 * KV_W
QKVG_W = 5 * RET_W
WINDOW = 128
CHUNK = 128
NUM_BUCKETS = 32
PAST_LEN = 8192
ROPE_BASE = 10000.0
LN_EPS = 1e-5
NEG_INF = -1e30
LANES = 128
BF16_ROWS = 16
FF_CHUNK = 256
FF_LOOKAHEAD = 2
FF_ROWS = 256
VMEM_LIMIT = 56 * 1024 * 1024


def _cparams(n_axes):
    return pltpu.CompilerParams(dimension_semantics=("arbitrary",) * n_axes,
                                vmem_limit_bytes=VMEM_LIMIT)


def _const_spec(shape):
    nd = len(shape)
    return pl.BlockSpec(shape, lambda *_: (0,) * nd, pipeline_mode=pl.Buffered(1))


def _rope_tables(pos):
    half = HEAD_DIM // 2
    inv = 1.0 / (ROPE_BASE ** (jnp.arange(half, dtype=F32) / half))
    ang = pos.astype(F32)[:, None] * inv[None, :]
    cos = jnp.cos(ang)
    sin = jnp.sin(ang)
    cos_h = jnp.concatenate([cos, cos], axis=-1)
    sin_h = jnp.concatenate([-sin, sin], axis=-1)
    return jnp.tile(cos_h, (1, N_HEADS)), jnp.tile(sin_h, (1, N_HEADS))


def _decay_tables(chunk):
    log_g = jnp.log(1.0 - 2.0 ** (-5.0 - jnp.arange(N_HEADS, dtype=F32)))
    idx = jnp.arange(chunk)
    diff = idx[:, None] - idx[None, :]
    decay_mat = jnp.where(diff[None] >= 0,
                          jnp.exp(log_g[:, None, None] * jnp.maximum(diff, 0)[None].astype(F32)), 0.0)
    q_dec = jnp.exp(log_g[None, :] * (idx[:, None] + 1).astype(F32))
    k_dec = jnp.exp(log_g[None, :] * (chunk - 1 - idx[:, None]).astype(F32))
    chunk_dec = jnp.exp(log_g * chunk)
    return decay_mat, q_dec, k_dec, chunk_dec


def _bucket_table(n_q, n_k):
    i = np.arange(n_q)[:, None]
    j = np.arange(n_k)[None, :]
    dist = i - j + WINDOW
    n = np.maximum(dist, 0)
    max_exact = NUM_BUCKETS // 2
    nf = np.maximum(n, max_exact).astype(np.float64)
    large = max_exact + (np.log(nf / max_exact) / math.log(WINDOW / max_exact)
                         * (NUM_BUCKETS - max_exact)).astype(np.int32)
    large = np.minimum(large, NUM_BUCKETS - 1)
    bucket = np.where(n < max_exact, n, large).astype(np.int32)
    valid = ((dist >= 0) & (dist < WINDOW)).astype(np.int32)
    return bucket, valid


def _ada_kernel(c_ref, w_ref, b_ref, op_ref, os_ref):
    c = c_ref[...]
    s = (c * jax.nn.sigmoid(c)).astype(BF16)
    mod = jnp.dot(s, w_ref[0].astype(BF16), preferred_element_type=F32) + b_ref[0]
    bp = op_ref.shape[0]
    op_ref[:, 0, :] = mod[0:bp]
    os_ref[:, 0, :] = mod[bp:]


def _ada_call(c_all, bp, w_ada, b_ada):
    depth = w_ada.shape[0]
    n_rows = c_all.shape[0]
    bs = n_rows - bp
    tn = 1536
    out_blk = lambda b: pl.BlockSpec((None, b, 1, tn), lambda l, n: (l, 0, 0, n))
    return pl.pallas_call(
        _ada_kernel,
        grid=(depth, 6 * D_MODEL // tn),
        in_specs=[pl.BlockSpec((n_rows, D_MODEL), lambda l, n: (0, 0)),
                  pl.BlockSpec((1, D_MODEL, tn), lambda l, n: (l, 0, n)),
                  pl.BlockSpec((1, 1, tn), lambda l, n: (l, 0, n))],
        out_specs=[out_blk(bp), out_blk(bs)],
        out_shape=[jax.ShapeDtypeStruct((depth, bp, 1, 6 * D_MODEL), F32),
                   jax.ShapeDtypeStruct((depth, bs, 1, 6 * D_MODEL), F32)],
        compiler_params=_cparams(2),
        name="ada",
    )(c_all, w_ada, b_ada.reshape(depth, 1, 6 * D_MODEL))


def _bias_kernel(rb_ref, bucket_ref, valid_ref, o_ref):
    bucket = bucket_ref[...]
    valid = valid_ref[...] > 0
    first_ok = lax.broadcasted_iota(jnp.int32, bucket.shape, 1) >= WINDOW
    for h in range(N_HEADS):
        acc = jnp.zeros(bucket.shape, F32)
        for b in range(NUM_BUCKETS):
            acc = jnp.where(bucket == b, rb_ref[b, h], acc)
        o_ref[0, h] = jnp.where(valid, acc, NEG_INF)
        o_ref[1, h] = jnp.where(valid & first_ok, acc, NEG_INF)


def _bias_call(rel_bias):
    bucket, valid = _bucket_table(CHUNK, 2 * CHUNK)
    return pl.pallas_call(
        _bias_kernel,
        in_specs=[pl.BlockSpec(memory_space=pltpu.SMEM),
                  pl.BlockSpec((CHUNK, 2 * CHUNK), lambda: (0, 0)),
                  pl.BlockSpec((CHUNK, 2 * CHUNK), lambda: (0, 0))],
        out_specs=pl.BlockSpec((2, N_HEADS, CHUNK, 2 * CHUNK), lambda: (0, 0, 0, 0)),
        out_shape=jax.ShapeDtypeStruct((2, N_HEADS, CHUNK, 2 * CHUNK), F32),
        name="swa_bias",
    )(rel_bias, jnp.asarray(bucket), jnp.asarray(valid))


def _swap_halves(x):
    lane = lax.broadcasted_iota(jnp.int32, (1, LANES), 1)
    first = (lane % HEAD_DIM) < (HEAD_DIM // 2)
    cols = []
    for c in range(x.shape[1] // LANES):
        xc = x[:, c * LANES:(c + 1) * LANES]
        cols.append(jnp.where(first, pltpu.roll(xc, LANES - HEAD_DIM // 2, 1), pltpu.roll(xc, HEAD_DIM // 2, 1)))
    return jnp.concatenate(cols, axis=1)


def _proj_kernel(x_ref, sc_ref, sh_ref, w_ref, cos_ref, sin_ref, *refs):
    n_cast = (len(refs) - 2) // 2
    qkvg_ref, kv_ref = refs[n_cast], refs[n_cast + 1]
    for src_ref, dst_ref in zip(refs[:n_cast], refs[n_cast + 2:]):
        dst_ref[...] = src_ref[...].astype(dst_ref.dtype)
    tb, tl, _ = x_ref.shape
    tm = tb * tl
    h = x_ref[...] * (1.0 + sc_ref[...]) + sh_ref[...]
    h = h.reshape(tm, D_MODEL).astype(BF16)
    proj = jnp.dot(h, w_ref[...], preferred_element_type=F32)
    cos = cos_ref[...]
    sin = sin_ref[...]
    rq = proj[:, 0:RET_W]
    rk = proj[:, RET_W:2 * RET_W]
    rq = rq * cos + _swap_halves(rq) * sin
    rk = (rk * cos + _swap_halves(rk) * sin) * (HEAD_DIM ** -0.5)
    rg = proj[:, 3 * RET_W:4 * RET_W]
    qkvg_ref[:, 0:RET_W] = rq.astype(BF16)
    qkvg_ref[:, RET_W:2 * RET_W] = rk.astype(BF16)
    qkvg_ref[:, 2 * RET_W:3 * RET_W] = proj[:, 2 * RET_W:3 * RET_W].astype(BF16)
    qkvg_ref[:, 3 * RET_W:4 * RET_W] = (rg * jax.nn.sigmoid(rg)).astype(BF16)
    qkvg_ref[:, 4 * RET_W:5 * RET_W] = (proj[:, 4 * RET_W:5 * RET_W] * (HEAD_DIM ** -0.5)).astype(BF16)
    kv_ref[...] = proj[:, 5 * RET_W:IN_COLS]


def _layer_spec(shape, layer):
    nd = len(shape)
    return pl.BlockSpec((None,) + tuple(shape), lambda *_: (layer,) + (0,) * nd, pipeline_mode=pl.Buffered(1))


def _mod_spec(tb, layer, k):
    return pl.BlockSpec((None, tb, 1, D_MODEL), lambda i, j: (layer, i, 0, k))


def _proj_call(x, mod, w_in, layer, cos, sin, tb, tl, cast=()):
    b, l, _ = x.shape
    nj = l // tl
    tm = tb * tl
    n_steps = (b // tb) * nj
    tab_idx = (lambda i, j: (j, 0)) if cos.shape[0] == l and nj > 1 else (lambda i, j: (0, 0))
    cast_in, cast_out, cast_shape = [], [], []
    for w in cast:
        _, rows, cols = w.shape
        blk = (None, rows // n_steps, cols)
        assert blk[1] * n_steps == rows and blk[1] % BF16_ROWS == 0
        cast_in.append(pl.BlockSpec(blk, lambda i, j: (layer, i * nj + j, 0)))
        cast_out.append(pl.BlockSpec(blk, lambda i, j: (0, i * nj + j, 0)))
        cast_shape.append(jax.ShapeDtypeStruct((1, rows, cols), BF16))
    return pl.pallas_call(
        _proj_kernel,
        grid=(b // tb, nj),
        in_specs=[pl.BlockSpec((tb, tl, D_MODEL), lambda i, j: (i, j, 0)),
                  _mod_spec(tb, layer, 1),
                  _mod_spec(tb, layer, 0),
                  _layer_spec((D_MODEL, IN_COLS), layer if w_in.shape[0] > 1 else 0),
                  pl.BlockSpec((tm, RET_W), tab_idx),
                  pl.BlockSpec((tm, RET_W), tab_idx)] + cast_in,
        out_specs=[pl.BlockSpec((tm, QKVG_W), lambda i, j: (i * nj + j, 0)),
                   pl.BlockSpec((tm, 2 * KV_W), lambda i, j: (i * nj + j, 0))] + cast_out,
        out_shape=[jax.ShapeDtypeStruct((b * l, QKVG_W), BF16),
                   jax.ShapeDtypeStruct((b * l, 2 * KV_W), F32)] + cast_shape,
        compiler_params=_cparams(2),
        name="proj",
    )(x, mod, mod, w_in, cos, sin, *cast)


def _lane_mask_low():
    return lax.broadcasted_iota(jnp.int32, (1, LANES), 1) < HEAD_DIM


def _head_norm(o, ones_bd):
    outs = []
    gw = ones_bd.shape[0]
    for g in range(RET_W // gw):
        og = o[:, g * gw:(g + 1) * gw]
        mu = jnp.dot(og.astype(BF16), ones_bd, preferred_element_type=F32)
        d = og - mu
        var = jnp.dot((d * d).astype(BF16), ones_bd, preferred_element_type=F32)
        outs.append(d * lax.rsqrt(var + LN_EPS))
    return jnp.concatenate(outs, axis=1)


def _swa_operands(k_all, v_all):
    low = _lane_mask_low()
    zero = jnp.zeros_like(k_all)
    ones_low = jnp.broadcast_to(jnp.where(low, 1.0, 0.0), k_all.shape)
    ones_cols = jnp.concatenate([ones_low, 1.0 - ones_low], axis=0).astype(BF16)
    k_rot = pltpu.roll(k_all, HEAD_DIM, 1)
    v_rot = pltpu.roll(v_all, HEAD_DIM, 1)
    ops = []
    for g in range(KV_HEADS):
        k_lo, k_hi = (k_all, k_rot) if g == 0 else (k_rot, k_all)
        v_lo, v_hi = (v_all, v_rot) if g == 0 else (v_rot, v_all)
        kc = jnp.concatenate([jnp.where(low, k_lo, zero), jnp.where(low, zero, k_hi)], axis=0)
        vc = jnp.concatenate([jnp.where(low, v_lo, zero), jnp.where(low, zero, v_hi)], axis=0)
        ops.append((kc, jnp.concatenate([vc, ones_cols], axis=1)))
    return ops


def _swa_softmax(lg, p, bias_ref, sink_ref):
    rows = lg.shape[0]
    low = _lane_mask_low()
    probs, sinks = [], []
    for s in range(2):
        hh = 2 * p + s
        lo = lg[:, s * 2 * CHUNK:(s + 1) * 2 * CHUNK] + bias_ref[hh, 0:rows, :]
        sink = sink_ref[hh]
        m = jnp.maximum(jnp.max(lo, axis=-1, keepdims=True), sink)
        probs.append(jnp.exp(lo - m))
        sinks.append(jnp.exp(sink - m))
    return jnp.concatenate(probs, axis=1), jnp.where(low, sinks[0], sinks[1])


def _swa_values(probs, sink_term, vc):
    oo = jnp.dot(probs, vc, preferred_element_type=F32)
    return oo[:, 0:LANES] / (oo[:, LANES:2 * LANES] + sink_term)


def _attn_prompt_kernel(sink_ref, qkvg_ref, kv_ref, bias_ref, bias_all_ref, dm_ref, qd_ref, kd_ref, cd_ref, ones_ref,
                        mix_ref, s_out_ref, s_scr, pk_scr, pv_scr):
    i = pl.program_id(1)

    @pl.when(i == 0)
    def _():
        s_scr[...] = jnp.zeros_like(s_scr)
        pk_scr[...] = jnp.zeros_like(pk_scr)
        pv_scr[...] = jnp.zeros_like(pv_scr)

    low = _lane_mask_low()
    zero = jnp.zeros((CHUNK, LANES), BF16)
    n_pairs = N_HEADS // 2
    pairs_per_kv = n_pairs // KV_HEADS
    gw = cd_ref.shape[1]
    n_groups = RET_W // gw
    n_blk = qkvg_ref.shape[0] // CHUNK
    nt = (((1,), (1,)), ((), ()))
    tn = (((0,), (0,)), ((), ()))
    ones_bd = ones_ref[...]

    s_cur = [s_scr[g] for g in range(n_groups)]
    k_prev, v_prev = pk_scr[...], pv_scr[...]
    wave1, mid = [], []
    for c in range(n_blk):
        rows = slice(c * CHUNK, (c + 1) * CHUNK)
        q = qkvg_ref[rows, 0:RET_W]
        k = qkvg_ref[rows, RET_W:2 * RET_W]
        v = qkvg_ref[rows, 2 * RET_W:3 * RET_W]
        sq = qkvg_ref[rows, 4 * RET_W:5 * RET_W]
        kd = (k.astype(F32) * kd_ref[...]).astype(BF16)
        ret_ops = []
        for p in range(n_pairs):
            kp = k[:, p * LANES:(p + 1) * LANES]
            vp = v[:, p * LANES:(p + 1) * LANES]
            ret_ops.append((jnp.concatenate([jnp.where(low, kp, zero), jnp.where(low, zero, kp)], axis=0),
                            jnp.concatenate([jnp.where(low, vp, zero), jnp.where(low, zero, vp)], axis=0)))
        k_new = kv_ref[rows, 0:KV_W].astype(BF16)
        v_new = kv_ref[rows, KV_W:2 * KV_W].astype(BF16)
        swa_ops = _swa_operands(jnp.concatenate([k_prev, k_new], axis=0), jnp.concatenate([v_prev, v_new], axis=0))
        k_prev, v_prev = k_new, v_new

        scores = [lax.dot_general(q[:, p * LANES:(p + 1) * LANES], ret_ops[p][0], nt, preferred_element_type=F32)
                  for p in range(n_pairs)]
        cross = [jnp.dot(q[:, g * gw:(g + 1) * gw], s_cur[g].astype(BF16), preferred_element_type=F32)
                 for g in range(n_groups)]
        upd = [lax.dot_general(kd[:, g * gw:(g + 1) * gw], v[:, g * gw:(g + 1) * gw], tn,
                               preferred_element_type=F32) for g in range(n_groups)]
        logits = [lax.dot_general(
            jnp.concatenate([sq[:, (g * pairs_per_kv + pp) * LANES:(g * pairs_per_kv + pp + 1) * LANES]
                             for pp in range(pairs_per_kv)], axis=0),
            swa_ops[g][0], nt, preferred_element_type=F32) for g in range(KV_HEADS)]
        s_cur = [s_cur[g] * cd_ref[g] + jnp.where(cd_ref[g] > 0.0, upd[g], 0.0) for g in range(n_groups)]
        wave1.append((scores, cross, logits, ret_ops, swa_ops))
    for g in range(n_groups):
        s_scr[g] = s_cur[g]
    pk_scr[...] = k_prev
    pv_scr[...] = v_prev

    for c in range(n_blk):
        scores, cross, logits, ret_ops, swa_ops = wave1[c]
        bias_c = bias_ref.at[0] if c == 0 else bias_all_ref.at[0]
        scores = [(scores[p] * dm_ref[p]).astype(BF16) for p in range(n_pairs)]
        soft = [[_swa_softmax(logits[g][pp * CHUNK:(pp + 1) * CHUNK, :], g * pairs_per_kv + pp, bias_c, sink_ref)
                 for pp in range(pairs_per_kv)] for g in range(KV_HEADS)]
        intra = [jnp.dot(scores[p], ret_ops[p][1], preferred_element_type=F32) for p in range(n_pairs)]
        swa = []
        for g in range(KV_HEADS):
            oo = _swa_values(jnp.concatenate([x[0].astype(BF16) for x in soft[g]], axis=0),
                             jnp.concatenate([x[1] for x in soft[g]], axis=0), swa_ops[g][1])
            swa += [oo[pp * CHUNK:(pp + 1) * CHUNK, :] for pp in range(pairs_per_kv)]
        o = jnp.concatenate(intra, axis=1) + jnp.concatenate(cross, axis=1) * qd_ref[...]
        mid.append((o, swa))

    o = jnp.concatenate([m[0] for m in mid], axis=0)
    ret = _head_norm(o, ones_bd) * qkvg_ref[:, 3 * RET_W:4 * RET_W].astype(F32)
    mix_ref[:, 0:RET_W] = ret.astype(BF16)
    for c in range(n_blk):
        mix_ref[c * CHUNK:(c + 1) * CHUNK, RET_W:2 * RET_W] = jnp.concatenate(mid[c][1], axis=1).astype(BF16)

    @pl.when(i == pl.num_programs(1) - 1)
    def _():
        for h in range(N_HEADS):
            g, hl = divmod(h, gw // HEAD_DIM)
            s_out_ref[0, h] = s_scr[g, hl * HEAD_DIM:(hl + 1) * HEAD_DIM, hl * HEAD_DIM:(hl + 1) * HEAD_DIM]


def _prompt_tables():
    decay_mat, q_dec, k_dec, chunk_dec = _decay_tables(CHUNK)
    dm = jnp.concatenate([decay_mat[0::2], decay_mat[1::2]], axis=2)
    qd = jnp.repeat(q_dec, HEAD_DIM, axis=1)
    kd = jnp.repeat(k_dec, HEAD_DIM, axis=1)
    gw = 4 * HEAD_DIM
    blk = np.kron(np.eye(4, dtype=np.float32), np.ones((HEAD_DIM, HEAD_DIM), np.float32))
    cd = jnp.repeat(chunk_dec.reshape(2, 4), HEAD_DIM, axis=1)[:, :, None] * jnp.asarray(blk)[None]
    ones_bd = jnp.asarray(blk / HEAD_DIM, BF16)
    assert cd.shape == (2, gw, gw)
    return dm, qd, kd, cd, ones_bd


def _attn_prompt_call(qkvg, kv, bias, sinks, tables, b, l, n_blk):
    dm, qd, kd, cd, ones_bd = tables
    rows = n_blk * CHUNK
    nc = l // rows
    gw = cd.shape[1]
    bias_spec = lambda idx: pl.BlockSpec((1, N_HEADS, CHUNK, 2 * CHUNK), idx)
    return pl.pallas_call(
        _attn_prompt_kernel,
        grid=(b, nc),
        in_specs=[pl.BlockSpec(memory_space=pltpu.SMEM),
                  pl.BlockSpec((rows, QKVG_W), lambda bi, i: (bi * nc + i, 0)),
                  pl.BlockSpec((rows, 2 * KV_W), lambda bi, i: (bi * nc + i, 0)),
                  bias_spec(lambda bi, i: (jnp.where(i == 0, 1, 0), 0, 0, 0)),
                  bias_spec(lambda bi, i: (0, 0, 0, 0)),
                  _const_spec(dm.shape), _const_spec(qd.shape), _const_spec(kd.shape),
                  _const_spec(cd.shape), _const_spec(ones_bd.shape)],
        out_specs=[pl.BlockSpec((rows, 2 * RET_W), lambda bi, i: (bi * nc + i, 0)),
                   pl.BlockSpec((1, N_HEADS, HEAD_DIM, HEAD_DIM), lambda bi, i: (bi, 0, 0, 0))],
        out_shape=[jax.ShapeDtypeStruct((b * l, 2 * RET_W), BF16),
                   jax.ShapeDtypeStruct((b, N_HEADS, HEAD_DIM, HEAD_DIM), F32)],
        scratch_shapes=[pltpu.VMEM((RET_W // gw, gw, gw), F32),
                        pltpu.VMEM((CHUNK, KV_W), BF16),
                        pltpu.VMEM((CHUNK, KV_W), BF16)],
        compiler_params=_cparams(2),
        name="attn_prompt",
    )(sinks, qkvg, kv, bias, bias, dm, qd, kd, cd, ones_bd)


def _ret_sample_kernel(dec_ref, tq_ref, s_ref, buf_ref, ret_ref, s_out_ref, q_scr, k_scr, v_scr, cross_scr):
    del buf_ref
    h = pl.program_id(0)
    n_t = tq_ref.shape[1]
    cd = dec_ref[h, 2 * n_t]
    for t in range(n_t):
        q_scr[t] = tq_ref[0, t].astype(F32)
        k_scr[t] = tq_ref[1, t].astype(F32) * dec_ref[h, t]
        v_scr[t] = tq_ref[2, t].astype(F32)
    cross_scr[...] = jnp.zeros_like(cross_scr)

    def per_d(d, carry):
        s_d = s_ref[d]
        upd = s_d * cd
        for t in range(n_t):
            upd = upd + k_scr[t, pl.ds(d, 1), :] * v_scr[t]
            cross_scr[t] += q_scr[t, pl.ds(d, 1), :] * s_d
        s_out_ref[d] = upd
        return carry

    lax.fori_loop(0, HEAD_DIM, per_d, 0)

    for t in range(n_t):
        o = cross_scr[t] * dec_ref[h, n_t + t]
        for t2 in range(t + 1):
            sc = jnp.sum(q_scr[t] * (tq_ref[1, t2].astype(F32)), axis=0, keepdims=True)
            o = o + (sc * dec_ref[h, 2 * n_t + 1 + t * n_t + t2]) * v_scr[t2]
        mu = jnp.mean(o, axis=0, keepdims=True)
        d0 = o - mu
        var = jnp.mean(d0 * d0, axis=0, keepdims=True)
        ret_ref[t] = (d0 * lax.rsqrt(var + LN_EPS) * tq_ref[3, t].astype(F32)).astype(ret_ref.dtype)


def _ret_sample_call(tq, state5, s_buf, layer, dec):
    _, n_t, _, b = tq.shape
    state_blk = (None, None, HEAD_DIM, HEAD_DIM, b)
    return pl.pallas_call(
        _ret_sample_kernel,
        grid=(N_HEADS,),
        in_specs=[pl.BlockSpec(memory_space=pltpu.SMEM),
                  pl.BlockSpec((4, n_t, HEAD_DIM, b), lambda h: (0, 0, h, 0)),
                  pl.BlockSpec(state_blk, lambda h: (layer, h, 0, 0, 0)),
                  pl.BlockSpec(memory_space=pl.ANY)],
        out_specs=[pl.BlockSpec((n_t, HEAD_DIM, b), lambda h: (0, h, 0)),
                   pl.BlockSpec(state_blk, lambda h: (layer, h, 0, 0, 0))],
        out_shape=[jax.ShapeDtypeStruct((n_t, RET_W, b), BF16),
                   jax.ShapeDtypeStruct(s_buf.shape, F32)],
        scratch_shapes=[pltpu.VMEM((n_t, HEAD_DIM, b), F32)] * 4,
        input_output_aliases={3: 1},
        compiler_params=_cparams(1),
        name="ret_sample",
    )(dec, tq, state5, s_buf)


def _ret_sample_table(tl):
    decay_mat, q_dec, k_dec, chunk_dec = _decay_tables(tl)
    return jnp.concatenate([k_dec.T, q_dec.T, chunk_dec[:, None], decay_mat.reshape(N_HEADS, tl * tl)], axis=1)


def _swa_sample_kernel(sink_ref, qkvg_ref, kv_ref, ckt_ref, cvt_ref, bias_ref, kbuf_ref, vbuf_ref,
                       swa_ref, ck_out_ref, cv_out_ref):
    del kbuf_ref, vbuf_ref
    tb = ckt_ref.shape[0]
    tl = qkvg_ref.shape[0] // tb
    nt = (((1,), (1,)), ((), ()))
    pairs_per_kv = N_HEADS // KV_HEADS // 2
    low = _lane_mask_low()
    zero_t =jnp.zeros((HEAD_DIM, WINDOW), BF16)
    front = jnp.zeros((WINDOW - tl, KV_W), F32)
    is_new =lax.broadcasted_iota(jnp.int32, (1, WINDOW), 1) >= WINDOW - tl
    bias = bias_ref.at[0]

    def block_diag(t):
        return jnp.concatenate([jnp.concatenate([t, zero_t], axis=1), jnp.concatenate([zero_t, t], axis=1)], axis=0)

    wave1 = []
    for e in range(tb):
        rows = slice(e * tl, (e + 1) * tl)
        sq32 = qkvg_ref[rows, 4 * RET_W:5 * RET_W].astype(F32)
        k_new = kv_ref[rows, 0:KV_W]
        v_new = kv_ref[rows, KV_W:2 * KV_W]
        k_rot = pltpu.roll(k_new, HEAD_DIM, 1)
        v_rot = pltpu.roll(v_new, HEAD_DIM, 1)
        k_placed = jnp.concatenate([front, k_new], axis=0).T
        v_placed = jnp.concatenate([front, v_new], axis=0).T
        per_g = []
        for g in range(KV_HEADS):
            kt = ckt_ref[e, g]
            vt = cvt_ref[e, g]
            hd = slice(g * HEAD_DIM, (g + 1) * HEAD_DIM)
            ck_out_ref[e, g] = jnp.where(is_new, k_placed[hd, :], pltpu.roll(kt, WINDOW - tl, 1))
            cv_out_ref[e, g] = jnp.where(is_new, v_placed[hd, :], pltpu.roll(vt, WINDOW - tl, 1))
            lhs = jnp.concatenate([sq32[:, (g * pairs_per_kv + pp) * LANES:(g * pairs_per_kv + pp + 1) * LANES]
                                   for pp in range(pairs_per_kv)], axis=0).astype(BF16)
            kn_lo, kn_hi = (k_new, k_rot) if g == 0 else (k_rot, k_new)
            vn_lo, vn_hi = (v_new, v_rot) if g == 0 else (v_rot, v_new)
            kn = [jnp.where(low, kn_lo, 0.0).astype(BF16), jnp.where(low, 0.0, kn_hi).astype(BF16)]
            vn = [jnp.where(low, vn_lo, 0.0).astype(BF16), jnp.where(low, 0.0, vn_hi).astype(BF16)]
            lc = jnp.dot(lhs, block_diag(kt.astype(BF16)), preferred_element_type=F32)
            ln = [lax.dot_general(lhs, kn[s], nt, preferred_element_type=F32) for s in range(2)]
            per_g.append((lc, ln, vn, block_diag(vt.astype(BF16))))
        wave1.append(per_g)

    swa_rows = []
    for e in range(tb):
        outs = []
        for g in range(KV_HEADS):
            lc, ln, vn, vbd = wave1[e][g]
            pcs, pns, dens = [], [[], []], []
            for pp in range(pairs_per_kv):
                r = slice(pp * tl, (pp + 1) * tl)
                den_pair = []
                for s in range(2):
                    hh = 2 * (g * pairs_per_kv + pp) + s
                    lo_c = lc[r, s * WINDOW:(s + 1) * WINDOW] + bias[hh, 0:tl, 0:WINDOW]
                    lo_n = ln[s][r, :] + bias[hh, 0:tl, WINDOW:WINDOW + tl]
                    sink = sink_ref[hh]
                    m = jnp.maximum(jnp.maximum(jnp.max(lo_c, axis=-1, keepdims=True),
                                                jnp.max(lo_n, axis=-1, keepdims=True)), sink)
                    pc = jnp.exp(lo_c - m)
                    pn = jnp.exp(lo_n - m)
                    den_pair.append(jnp.sum(pc, axis=-1, keepdims=True) + jnp.sum(pn, axis=-1, keepdims=True)
                                    + jnp.exp(sink - m))
                    pcs.append((pp, pc))
                    pns[s].append(pn)
                dens.append(jnp.where(low, den_pair[0], den_pair[1]))
            pc_rows = [jnp.concatenate([x[1] for x in pcs if x[0] == pp], axis=1) for pp in range(pairs_per_kv)]
            pc_all = jnp.concatenate(pc_rows, axis=0).astype(BF16)
            oo = lax.dot_general(pc_all, vbd, nt, preferred_element_type=F32)
            for s in range(2):
                oo = oo + jnp.dot(jnp.concatenate(pns[s], axis=0).astype(BF16), vn[s], preferred_element_type=F32)
            oo = oo / jnp.concatenate(dens, axis=0)
            outs += [oo[pp * tl:(pp + 1) * tl, :] for pp in range(pairs_per_kv)]
        swa_rows.append(jnp.concatenate(outs, axis=1))
    swa_ref[...] = jnp.concatenate(swa_rows, axis=0).astype(BF16)


def _swa_sample_call(qkvg, kv, ckt, cvt, kbuf, vbuf, layer, bias, sinks, b, tl, tb):
    cache_blk = (None, tb, KV_HEADS, HEAD_DIM, WINDOW)
    layer5 = lambda i: (layer, i, 0, 0, 0)
    return pl.pallas_call(
        _swa_sample_kernel,
        grid=(b // tb,),
        in_specs=[pl.BlockSpec(memory_space=pltpu.SMEM),
                  pl.BlockSpec((tb * tl, QKVG_W), lambda i: (i, 0)),
                  pl.BlockSpec((tb * tl, 2 * KV_W), lambda i: (i, 0)),
                  pl.BlockSpec(cache_blk, layer5),
                  pl.BlockSpec(cache_blk, layer5),
                  pl.BlockSpec((1, N_HEADS, CHUNK, 2 * CHUNK), lambda i: (0, 0, 0, 0)),
                  pl.BlockSpec(memory_space=pl.ANY),
                  pl.BlockSpec(memory_space=pl.ANY)],
        out_specs=[pl.BlockSpec((tb * tl, RET_W), lambda i: (i, 0)),
                   pl.BlockSpec(cache_blk, layer5),
                   pl.BlockSpec(cache_blk, layer5)],
        out_shape=[jax.ShapeDtypeStruct((b * tl, RET_W), BF16),
                   jax.ShapeDtypeStruct(kbuf.shape, F32),
                   jax.ShapeDtypeStruct(vbuf.shape, F32)],
        input_output_aliases={6: 1, 7: 2},
        compiler_params=_cparams(1),
        name="swa_sample",
    )(sinks, qkvg, kv, ckt, cvt, bias, kbuf, vbuf)


def _layer_norm(x, g, b):
    mu = jnp.mean(x, axis=-1, keepdims=True)
    xc = x - mu
    var = jnp.mean(xc * xc, axis=-1, keepdims=True)
    return xc * lax.rsqrt(var + LN_EPS) * g + b


def _ffn_kernel(alpha, carry_rows, n_cast, mix_ref, x_ref, g1_ref, sh2_ref, sc2_ref, g2_ref, wout_ref, ln1g_ref,
                ln1b_ref, wup_ref, cw_ref, cb_ref, prev_ref, wdn_ref, ln2g_ref, ln2b_ref, *rest):
    cast_in, (xo_ref, tail_ref), carry_scr = rest[:n_cast], rest[n_cast:n_cast + 2], rest[-1]
    zero_refs = rest[n_cast + 2:len(rest) - 1 - n_cast]
    for src_ref, dst_ref in zip(cast_in, rest[len(rest) - 1 - n_cast:-1]):
        dst_ref[...] = src_ref[...].astype(dst_ref.dtype)
    for z_ref in zero_refs:
        z_ref[...] = jnp.zeros(z_ref.shape, z_ref.dtype)
    tb, tl, _ = x_ref.shape
    tm = tb * tl
    sm = FF_ROWS
    n_sub = tm // sm
    sb, sl = (1, sm) if carry_rows else (tb // n_sub, tl)
    j = pl.program_id(1)

    def seq(ref, s):
        if carry_rows:
            return ref[:, s * sm:(s + 1) * sm, :] if ref.shape[1] == tl else ref[...]
        return ref[s * sb:(s + 1) * sb]

    if carry_rows:
        @pl.when(j == 0)
        def _():
            carry_scr[carry_rows - 2:carry_rows, :] = prev_ref[0]

    fix_rows = 8 if carry_rows else sm
    t_idx = lax.broadcasted_iota(jnp.int32, (fix_rows, 1), 0) % sl
    is_t0 = t_idx == 0
    is_t1 = t_idx == 1
    n_chunks = D_FF // FF_CHUNK

    def up_cols(c, half):
        return slice(half * D_FF + c * FF_CHUNK, half * D_FF + (c + 1) * FF_CHUNK)

    def pre(s):
        x1 = alpha * seq(x_ref, s) + seq(g1_ref, s) * y[s].reshape(sb, sl, D_MODEL)
        x1 = _layer_norm(x1, ln1g_ref[...], ln1b_ref[...])
        h2 = (x1 * (1.0 + seq(sc2_ref, s)) + seq(sh2_ref, s)).reshape(sm, D_MODEL).astype(BF16)
        return x1, h2

    def up_dots(h2, c):
        return [jnp.dot(h2, wup_ref[:, up_cols(c, half)], preferred_element_type=F32) for half in range(2)]

    def conv(s, up, cols):
        if carry_rows:
            src = carry_scr if s == 0 else tail_ref
            p0 = src[carry_rows - 2:carry_rows - 1, cols]
            p1 = src[carry_rows - 1:carry_rows, cols]
            tail_ref[:, cols] = up[sm - carry_rows:sm, :]
        else:
            up3 = up.reshape(sb, sl, FF_CHUNK)
            tail_ref[s * sb:(s + 1) * sb, :, cols] = up3[:, sl - 2:sl, :]
            prev = seq(prev_ref, s)
            p0, p1 = prev[:, 0:1, cols], prev[:, 1:2, cols]
            t3 = lax.broadcasted_iota(jnp.int32, (1, sl, 1), 1)
            s1 = jnp.where(t3 == 0, p1, pltpu.roll(up3, 1, 1))
            s2 = jnp.where(t3 == 0, p0, jnp.where(t3 == 1, p1, pltpu.roll(up3, 2, 1)))
            out = cb_ref[:, cols] + s2 * cw_ref[0:1, cols] + s1 * cw_ref[1:2, cols] + up3 * cw_ref[2:3, cols]
            return out.reshape(sm, FF_CHUNK)
        r1 = pltpu.roll(up, 1, 0)
        r2 = pltpu.roll(up, 2, 0)
        s1 = jnp.where(is_t0, p1, r1[0:fix_rows])
        s2 = jnp.where(is_t0, p0, jnp.where(is_t1, p1, r2[0:fix_rows]))
        if fix_rows < sm:
            s1 = jnp.concatenate([s1, r1[fix_rows:]], axis=0)
            s2 = jnp.concatenate([s2, r2[fix_rows:]], axis=0)
        return cb_ref[:, cols] + s2 * cw_ref[0:1, cols] + s1 * cw_ref[1:2, cols] + up * cw_ref[2:3, cols]

    y = [jnp.dot(mix_ref[s * sm:(s + 1) * sm, :], wout_ref[...], preferred_element_type=F32) for s in range(n_sub)]
    items = [(s, c) for s in range(n_sub) for c in range(n_chunks)]
    staged, ups = {}, {}

    def issue_up(k):
        s, c = items[k]
        if c == 0:
            staged[s] = pre(s)
        ups[k] = up_dots(staged[s][1], c)

    for k in range(min(FF_LOOKAHEAD, len(items))):
        issue_up(k)
    acc = None
    for k, (s, c) in enumerate(items):
        if k + FF_LOOKAHEAD < len(items):
            issue_up(k + FF_LOOKAHEAD)
        up_a, up_b = ups.pop(k)
        ua = conv(s, up_a, up_cols(c, 0))
        ub = conv(s, up_b, up_cols(c, 1))
        gated = (ua * jax.nn.sigmoid(ua) * ub).astype(BF16)
        down = jnp.dot(gated, wdn_ref[c * FF_CHUNK:(c + 1) * FF_CHUNK, :], preferred_element_type=F32)
        acc = down if c == 0 else acc + down
        if c == n_chunks - 1:
            x1, _ = staged.pop(s)
            x2 = alpha * x1 + seq(g2_ref, s) * acc.reshape(sb, sl, D_MODEL)
            out = _layer_norm(x2, ln2g_ref[...], ln2b_ref[...])
            if carry_rows:
                xo_ref[:, s * sm:(s + 1) * sm, :] = out
            else:
                xo_ref[s * sb:(s + 1) * sb] = out

    if carry_rows:
        carry_scr[...] = tail_ref[...]


def _ffn_call(mix, x, mod, conv_prev, w_out, ln1_g, ln1_b, w_up, conv_w, conv_b, w_down, ln2_g, ln2_b,
              layer, alpha, tb, tl, zero_shapes=(), cast=()):
    prev_layer = layer if conv_prev.shape[0] > 1 else 0
    w_layer = layer if w_up.shape[0] > 1 else 0
    b, l, _ = x.shape
    nj = l // tl
    tm = tb * tl
    n_steps = (b // tb) * nj
    zero_specs = []
    for shape in zero_shapes:
        per0 = n_steps // shape[0]
        blk = (1, shape[1] // per0) + tuple(shape[2:])
        assert per0 * shape[0] == n_steps and blk[1] * per0 == shape[1]
        zero_specs.append(pl.BlockSpec(blk, lambda i, j, per0=per0, nd=len(shape):
                                       ((i * nj + j) // per0, (i * nj + j) % per0) + (0,) * (nd - 2)))
    cast_in, cast_out, cast_shape = [], [], []
    for w, w_src_layer in cast:
        _, rows, cols = w.shape
        blk = (None, rows // n_steps, cols)
        assert blk[1] * n_steps == rows and blk[1] % BF16_ROWS == 0
        cast_in.append(pl.BlockSpec(blk, lambda i, j, src=w_src_layer: (src, i * nj + j, 0)))
        cast_out.append(pl.BlockSpec(blk, lambda i, j: (0, i * nj + j, 0)))
        cast_shape.append(jax.ShapeDtypeStruct((1, rows, cols), BF16))
    carry_rows = 8 if tb == 1 else 0
    assert carry_rows or (nj == 1 and tl == 8)
    if carry_rows:
        tail_spec = pl.BlockSpec((8, UP_W), lambda i, j: (i, 0))
        tail_shape = jax.ShapeDtypeStruct((b * 8, UP_W), F32)
    else:
        tail_spec = pl.BlockSpec((tb, 2, UP_W), lambda i, j: (i, 0, 0))
        tail_shape = jax.ShapeDtypeStruct((b, 2, UP_W), F32)
    kern = functools.partial(_ffn_kernel, alpha, carry_rows, len(cast))
    return pl.pallas_call(
        kern,
        grid=(b // tb, nj),
        in_specs=[pl.BlockSpec((tm, 2 * RET_W), lambda i, j: (i * nj + j, 0)),
                  pl.BlockSpec((tb, tl, D_MODEL), lambda i, j: (i, j, 0)),
                  _mod_spec(tb, layer, 2), _mod_spec(tb, layer, 3), _mod_spec(tb, layer, 4), _mod_spec(tb, layer, 5),
                  _layer_spec((2 * RET_W, D_MODEL), w_layer),
                  _layer_spec((1, D_MODEL), layer), _layer_spec((1, D_MODEL), layer),
                  _layer_spec((D_MODEL, UP_W), w_layer),
                  _layer_spec((3, UP_W), layer), _layer_spec((1, UP_W), layer),
                  pl.BlockSpec((None, tb, 2, UP_W), lambda i, j: (prev_layer, i, 0, 0)),
                  _layer_spec((D_FF, D_MODEL), w_layer),
                  _layer_spec((1, D_MODEL), layer), _layer_spec((1, D_MODEL), layer)] + cast_in,
        out_specs=[pl.BlockSpec((tb, tl, D_MODEL), lambda i, j: (i, j, 0)),
                   tail_spec] + zero_specs + cast_out,
        out_shape=[jax.ShapeDtypeStruct((b, l, D_MODEL), F32), tail_shape]
        + [jax.ShapeDtypeStruct(shape, F32) for shape in zero_shapes] + cast_shape,
        scratch_shapes=[pltpu.VMEM((8, UP_W), F32)],
        compiler_params=_cparams(2),
        name="ffn",
    )(mix, x, mod, mod, mod, mod, w_out, ln1_g, ln1_b, w_up, conv_w, conv_b, conv_prev, w_down, ln2_g, ln2_b,
      *[w for w, _ in cast])


def kernel(x_prompt, x_sample, c_prompt, c_sample, state_ret, cache_swa_k, cache_swa_v, state_conv, rel_bias, w_ada, b_ada, w_in, swa_sinks, w_out, ln1_g, ln1_b, w_up, conv_w, conv_b, w_down, ln2_g, ln2_b):
    depth = w_ada.shape[0]
    bp, lp, _ = x_prompt.shape
    bs, ls, _ = x_sample.shape
    alpha = (2.0 * depth) ** 0.25
    tl_proj = 1024
    tl_p = 2 * FF_ROWS
    tb_s = FF_ROWS // ls
    tb_attn = 8
    blk_attn = 8

    c_all = jnp.concatenate([c_prompt, c_sample], axis=0)
    mod_p, mod_s = _ada_call(c_all, bp, w_ada, b_ada)
    bias = _bias_call(rel_bias)

    cos_p, sin_p = _rope_tables(jnp.arange(lp, dtype=jnp.int32))
    cos_s, sin_s = _rope_tables(PAST_LEN + jnp.arange(ls, dtype=jnp.int32))
    cos_s, sin_s = jnp.tile(cos_s, (tb_s, 1)), jnp.tile(sin_s, (tb_s, 1))
    tab_p = _prompt_tables()
    dec_s = _ret_sample_table(ls)
    conv0 = jnp.zeros((1, bp, 2, UP_W), F32)

    w_in_l = w_in[0:1].astype(BF16)
    vec = lambda a: a.reshape(depth, 1, a.shape[-1])
    ffn_vec = (vec(ln1_g), vec(ln1_b), conv_w, vec(conv_b), vec(ln2_g), vec(ln2_b))
    state5 = jnp.transpose(state_ret, (0, 2, 3, 4, 1))
    ckt = jnp.transpose(cache_swa_k, (0, 1, 3, 4, 2))
    cvt = jnp.transpose(cache_swa_v, (0, 1, 3, 4, 2))

    xp, xs = x_prompt, x_sample
    p_ret, p_k, p_v, p_conv = [], [], [], []
    s_conv = []
    for l in range(depth):
        qkvg, kv, wo_b, wu_b, wd_b = _proj_call(xp, mod_p, w_in_l, l, cos_p, sin_p, 1, tl_proj, (w_out, w_up, w_down))
        ffn_w = (wo_b, ffn_vec[0], ffn_vec[1], wu_b, ffn_vec[2], ffn_vec[3], wd_b, ffn_vec[4], ffn_vec[5])
        mix, r_p =_attn_prompt_call(qkvg, kv, bias, swa_sinks[l], tab_p, bp, lp, blk_attn)
        zero_shapes = (state5.shape, ckt.shape, cvt.shape) if l == 0 else ()
        cast_next = ((w_in, l + 1),) if l + 1 < depth else ()
        xp, tail, *extra = _ffn_call(mix, xp, mod_p, conv0, *ffn_w, l, alpha, 1, tl_p, zero_shapes, cast_next)
        if l == 0:
            s_buf, k_buf, v_buf = extra[0:3]
        w_in_next = extra[-1] if cast_next else None
        kv3 =kv.reshape(bp, lp, 2, KV_HEADS, HEAD_DIM)
        p_ret.append(r_p)
        p_k.append(kv3[:, lp - WINDOW:, 0])
        p_v.append(kv3[:, lp - WINDOW:, 1])
        p_conv.append(tail.reshape(bp, 8, UP_W)[:, 6:8])

        qkvg, kv = _proj_call(xs, mod_s, w_in_l, l, cos_s, sin_s, tb_s, ls)
        tq = jnp.transpose(qkvg.reshape(bs, ls, 5, RET_W)[:, :, 0:4], (2, 1, 3, 0))
        ret_t, s_buf = _ret_sample_call(tq, state5, s_buf, l, dec_s)
        ret = jnp.transpose(ret_t, (2, 0, 1)).reshape(bs * ls, RET_W)
        swa, k_buf, v_buf = _swa_sample_call(qkvg, kv, ckt, cvt, k_buf, v_buf, l, bias, swa_sinks[l], bs, ls, tb_attn)
        mix = jnp.concatenate([ret, swa], axis=1)
        xs, tail = _ffn_call(mix, xs, mod_s, state_conv, *ffn_w, l, alpha, tb_s, ls)
        s_conv.append(tail)
        w_in_l = w_in_next

    return (xp, xs,
            jnp.stack(p_ret), jnp.stack(p_k), jnp.stack(p_v), jnp.stack(p_conv),
            jnp.transpose(s_buf, (0, 4, 1, 2, 3)), jnp.transpose(k_buf, (0, 1, 4, 2, 3)),
            jnp.transpose(v_buf, (0, 1, 4, 2, 3)), jnp.stack(s_conv))
```

```python
import functools
import math

import jax
import jax.numpy as jnp
import numpy as np
from jax import lax
from jax.experimental import pallas as pl
from jax.experimental.pallas import tpu as pltpu

F32 = jnp.float32
BF16 = jnp.bfloat16

D_MODEL = 1024
HEAD_DIM = 64
N_HEADS = 8
RET_W = N_HEADS * HEAD_DIM
KV_HEADS = 2
KV_W = KV_HEADS * HEAD_DIM
D_FF = 2816
UP_W = 2 * D_FF
IN_COLS = 4 * RET_W + RET_W + 2 * KV_W
QKVG_W = 5 * RET_W
WINDOW = 128
CHUNK = 128
NUM_BUCKETS = 32
PAST_LEN = 8192
ROPE_BASE = 10000.0
LN_EPS = 1e-5
NEG_INF = -1e30
LANES = 128
BF16_ROWS = 16
FF_CHUNK = 256
FF_LOOKAHEAD = 2
FF_ROWS = 256
VMEM_LIMIT = 56 * 1024 * 1024


def _cparams(n_axes):
    return pltpu.CompilerParams(dimension_semantics=("arbitrary",) * n_axes,
                                vmem_limit_bytes=VMEM_LIMIT)


def _const_spec(shape):
    nd = len(shape)
    return pl.BlockSpec(shape, lambda *_: (0,) * nd, pipeline_mode=pl.Buffered(1))


def _rope_tables(pos):
    half = HEAD_DIM // 2
    inv = 1.0 / (ROPE_BASE ** (jnp.arange(half, dtype=F32) / half))
    ang = pos.astype(F32)[:, None] * inv[None, :]
    cos = jnp.cos(ang)
    sin = jnp.sin(ang)
    cos_h = jnp.concatenate([cos, cos], axis=-1)
    sin_h = jnp.concatenate([-sin, sin], axis=-1)
    return jnp.tile(cos_h, (1, N_HEADS)), jnp.tile(sin_h, (1, N_HEADS))


def _decay_tables(chunk):
    log_g = jnp.log(1.0 - 2.0 ** (-5.0 - jnp.arange(N_HEADS, dtype=F32)))
    idx = jnp.arange(chunk)
    diff = idx[:, None] - idx[None, :]
    decay_mat = jnp.where(diff[None] >= 0,
                          jnp.exp(log_g[:, None, None] * jnp.maximum(diff, 0)[None].astype(F32)), 0.0)
    q_dec = jnp.exp(log_g[None, :] * (idx[:, None] + 1).astype(F32))
    k_dec = jnp.exp(log_g[None, :] * (chunk - 1 - idx[:, None]).astype(F32))
    chunk_dec = jnp.exp(log_g * chunk)
    return decay_mat, q_dec, k_dec, chunk_dec


def _bucket_table(n_q, n_k):
    i = np.arange(n_q)[:, None]
    j = np.arange(n_k)[None, :]
    dist = i - j + WINDOW
    n = np.maximum(dist, 0)
    max_exact = NUM_BUCKETS // 2
    nf = np.maximum(n, max_exact).astype(np.float64)
    large = max_exact + (np.log(nf / max_exact) / math.log(WINDOW / max_exact)
                         * (NUM_BUCKETS - max_exact)).astype(np.int32)
    large = np.minimum(large, NUM_BUCKETS - 1)
    bucket = np.where(n < max_exact, n, large).astype(np.int32)
    valid = ((dist >= 0) & (dist < WINDOW)).astype(np.int32)
    return bucket, valid


def _ada_kernel(c_ref, w_ref, b_ref, op_ref, os_ref):
    c = c_ref[...]
    s = (c * jax.nn.sigmoid(c)).astype(BF16)
    mod = jnp.dot(s, w_ref[0].astype(BF16), preferred_element_type=F32) + b_ref[0]
    bp = op_ref.shape[0]
    op_ref[:, 0, :] = mod[0:bp]
    os_ref[:, 0, :] = mod[bp:]


def _ada_call(c_all, bp, w_ada, b_ada):
    depth = w_ada.shape[0]
    n_rows = c_all.shape[0]
    bs = n_rows - bp
    tn = 3072
    out_blk = lambda b: pl.BlockSpec((None, b, 1, tn), lambda l, n: (l, 0, 0, n))
    return pl.pallas_call(
        _ada_kernel,
        grid=(depth, 6 * D_MODEL // tn),
        in_specs=[pl.BlockSpec((n_rows, D_MODEL), lambda l, n: (0, 0)),
                  pl.BlockSpec((1, D_MODEL, tn), lambda l, n: (l, 0, n)),
                  pl.BlockSpec((1, 1, tn), lambda l, n: (l, 0, n))],
        out_specs=[out_blk(bp), out_blk(bs)],
        out_shape=[jax.ShapeDtypeStruct((depth, bp, 1, 6 * D_MODEL), F32),
                   jax.ShapeDtypeStruct((depth, bs, 1, 6 * D_MODEL), F32)],
        compiler_params=_cparams(2),
        name="ada",
    )(c_all, w_ada, b_ada.reshape(depth, 1, 6 * D_MODEL))


def _bias_kernel(rb_ref, bucket_ref, valid_ref, o_ref):
    bucket = bucket_ref[...]
    valid = valid_ref[...] > 0
    first_ok = lax.broadcasted_iota(jnp.int32, bucket.shape, 1) >= WINDOW
    for h in range(N_HEADS):
        acc = jnp.zeros(bucket.shape, F32)
        for b in range(NUM_BUCKETS):
            acc = jnp.where(bucket == b, rb_ref[b, h], acc)
        o_ref[0, h] = jnp.where(valid, acc, NEG_INF)
        o_ref[1, h] = jnp.where(valid & first_ok, acc, NEG_INF)


def _bias_call(rel_bias):
    bucket, valid = _bucket_table(CHUNK, 2 * CHUNK)
    return pl.pallas_call(
        _bias_kernel,
        in_specs=[pl.BlockSpec(memory_space=pltpu.SMEM),
                  pl.BlockSpec((CHUNK, 2 * CHUNK), lambda: (0, 0)),
                  pl.BlockSpec((CHUNK, 2 * CHUNK), lambda: (0, 0))],
        out_specs=pl.BlockSpec((2, N_HEADS, CHUNK, 2 * CHUNK), lambda: (0, 0, 0, 0)),
        out_shape=jax.ShapeDtypeStruct((2, N_HEADS, CHUNK, 2 * CHUNK), F32),
        name="swa_bias",
    )(rel_bias, jnp.asarray(bucket), jnp.asarray(valid))


def _swap_halves(x):
    lane = lax.broadcasted_iota(jnp.int32, (1, LANES), 1)
    first = (lane % HEAD_DIM) < (HEAD_DIM // 2)
    cols = []
    for c in range(x.shape[1] // LANES):
        xc = x[:, c * LANES:(c + 1) * LANES]
        cols.append(jnp.where(first, pltpu.roll(xc, LANES - HEAD_DIM // 2, 1), pltpu.roll(xc, HEAD_DIM // 2, 1)))
    return jnp.concatenate(cols, axis=1)


def _proj_kernel(x_ref, sc_ref, sh_ref, w_ref, cos_ref, sin_ref, *refs):
    n_cast = (len(refs) - 2) // 2
    qkvg_ref, kv_ref = refs[n_cast], refs[n_cast + 1]
    for src_ref, dst_ref in zip(refs[:n_cast], refs[n_cast + 2:]):
        dst_ref[...] = src_ref[...].astype(dst_ref.dtype)
    tb, tl, _ = x_ref.shape
    tm = tb * tl
    h = x_ref[...] * (1.0 + sc_ref[...]) + sh_ref[...]
    h = h.reshape(tm, D_MODEL).astype(BF16)
    proj = jnp.dot(h, w_ref[...], preferred_element_type=F32)
    cos = cos_ref[...]
    sin = sin_ref[...]
    rq = proj[:, 0:RET_W]
    rk = proj[:, RET_W:2 * RET_W]
    rq = rq * cos + _swap_halves(rq) * sin
    rk = (rk * cos + _swap_halves(rk) * sin) * (HEAD_DIM ** -0.5)
    rg = proj[:, 3 * RET_W:4 * RET_W]
    qkvg_ref[:, 0:RET_W] = rq.astype(BF16)
    qkvg_ref[:, RET_W:2 * RET_W] = rk.astype(BF16)
    qkvg_ref[:, 2 * RET_W:3 * RET_W] = proj[:, 2 * RET_W:3 * RET_W].astype(BF16)
    qkvg_ref[:, 3 * RET_W:4 * RET_W] = (rg * jax.nn.sigmoid(rg)).astype(BF16)
    qkvg_ref[:, 4 * RET_W:5 * RET_W] = (proj[:, 4 * RET_W:5 * RET_W] * (HEAD_DIM ** -0.5)).astype(BF16)
    kv_ref[...] = proj[:, 5 * RET_W:IN_COLS]


def _layer_spec(shape, layer):
    nd = len(shape)
    return pl.BlockSpec((None,) + tuple(shape), lambda *_: (layer,) + (0,) * nd, pipeline_mode=pl.Buffered(1))


def _mod_spec(tb, layer, k):
    return pl.BlockSpec((None, tb, 1, D_MODEL), lambda i, j: (layer, i, 0, k))


def _proj_call(x, mod, w_in, layer, cos, sin, tb, tl, cast=()):
    b, l, _ = x.shape
    nj = l // tl
    tm = tb * tl
    n_steps = (b // tb) * nj
    tab_idx = (lambda i, j: (j, 0)) if cos.shape[0] == l and nj > 1 else (lambda i, j: (0, 0))
    cast_in, cast_out, cast_shape = [], [], []
    for w in cast:
        _, rows, cols = w.shape
        blk = (None, rows // n_steps, cols)
        assert blk[1] * n_steps == rows and blk[1] % BF16_ROWS == 0
        cast_in.append(pl.BlockSpec(blk, lambda i, j: (layer, i * nj + j, 0)))
        cast_out.append(pl.BlockSpec(blk, lambda i, j: (0, i * nj + j, 0)))
        cast_shape.append(jax.ShapeDtypeStruct((1, rows, cols), BF16))
    return pl.pallas_call(
        _proj_kernel,
        grid=(b // tb, nj),
        in_specs=[pl.BlockSpec((tb, tl, D_MODEL), lambda i, j: (i, j, 0)),
                  _mod_spec(tb, layer, 1),
                  _mod_spec(tb, layer, 0),
                  _layer_spec((D_MODEL, IN_COLS), layer if w_in.shape[0] > 1 else 0),
                  pl.BlockSpec((tm, RET_W), tab_idx),
                  pl.BlockSpec((tm, RET_W), tab_idx)] + cast_in,
        out_specs=[pl.BlockSpec((tm, QKVG_W), lambda i, j: (i * nj + j, 0)),
                   pl.BlockSpec((tm, 2 * KV_W), lambda i, j: (i * nj + j, 0))] + cast_out,
        out_shape=[jax.ShapeDtypeStruct((b * l, QKVG_W), BF16),
                   jax.ShapeDtypeStruct((b * l, 2 * KV_W), F32)] + cast_shape,
        compiler_params=_cparams(2),
        name="proj",
    )(x, mod, mod, w_in, cos, sin, *cast)


def _lane_mask_low():
    return lax.broadcasted_iota(jnp.int32, (1, LANES), 1) < HEAD_DIM


def _head_norm(o, ones_bd):
    outs = []
    gw = ones_bd.shape[0]
    for g in range(RET_W // gw):
        og = o[:, g * gw:(g + 1) * gw]
        mu = jnp.dot(og.astype(BF16), ones_bd, preferred_element_type=F32)
        d = og - mu
        var = jnp.dot((d * d).astype(BF16), ones_bd, preferred_element_type=F32)
        outs.append(d * lax.rsqrt(var + LN_EPS))
    return jnp.concatenate(outs, axis=1)


def _swa_operands(k_all, v_all):
    low = _lane_mask_low()
    zero = jnp.zeros_like(k_all)
    ones_low = jnp.broadcast_to(jnp.where(low, 1.0, 0.0), k_all.shape)
    ones_cols = jnp.concatenate([ones_low, 1.0 - ones_low], axis=0).astype(BF16)
    k_rot = pltpu.roll(k_all, HEAD_DIM, 1)
    v_rot = pltpu.roll(v_all, HEAD_DIM, 1)
    ops = []
    for g in range(KV_HEADS):
        k_lo, k_hi = (k_all, k_rot) if g == 0 else (k_rot, k_all)
        v_lo, v_hi = (v_all, v_rot) if g == 0 else (v_rot, v_all)
        kc = jnp.concatenate([jnp.where(low, k_lo, zero), jnp.where(low, zero, k_hi)], axis=0)
        vc = jnp.concatenate([jnp.where(low, v_lo, zero), jnp.where(low, zero, v_hi)], axis=0)
        ops.append((kc, jnp.concatenate([vc, ones_cols], axis=1)))
    return ops


def _swa_softmax(lg, p, bias_ref, sink_ref):
    rows = lg.shape[0]
    low = _lane_mask_low()
    probs, sinks = [], []
    for s in range(2):
        hh = 2 * p + s
        lo = lg[:, s * 2 * CHUNK:(s + 1) * 2 * CHUNK] + bias_ref[hh, 0:rows, :]
        sink = sink_ref[hh]
        m = jnp.maximum(jnp.max(lo, axis=-1, keepdims=True), sink)
        probs.append(jnp.exp(lo - m))
        sinks.append(jnp.exp(sink - m))
    return jnp.concatenate(probs, axis=1), jnp.where(low, sinks[0], sinks[1])


def _swa_values(probs, sink_term, vc):
    oo = jnp.dot(probs, vc, preferred_element_type=F32)
    return oo[:, 0:LANES] / (oo[:, LANES:2 * LANES] + sink_term)


def _attn_prompt_kernel(sink_ref, qkvg_ref, kv_ref, bias_ref, bias_all_ref, dm_ref, qd_ref, kd_ref, cd_ref, ones_ref,
                        mix_ref, s_out_ref, s_scr, pk_scr, pv_scr):
    i = pl.program_id(1)

    @pl.when(i == 0)
    def _():
        s_scr[...] = jnp.zeros_like(s_scr)
        pk_scr[...] = jnp.zeros_like(pk_scr)
        pv_scr[...] = jnp.zeros_like(pv_scr)

    low = _lane_mask_low()
    zero = jnp.zeros((CHUNK, LANES), BF16)
    n_pairs = N_HEADS // 2
    pairs_per_kv = n_pairs // KV_HEADS
    gw = cd_ref.shape[1]
    n_groups = RET_W // gw
    n_blk = qkvg_ref.shape[0] // CHUNK
    nt = (((1,), (1,)), ((), ()))
    tn = (((0,), (0,)), ((), ()))
    ones_bd = ones_ref[...]

    s_cur = [s_scr[g] for g in range(n_groups)]
    k_prev, v_prev = pk_scr[...], pv_scr[...]
    wave1, mid = [], []
    for c in range(n_blk):
        rows = slice(c * CHUNK, (c + 1) * CHUNK)
        q = qkvg_ref[rows, 0:RET_W]
        k = qkvg_ref[rows, RET_W:2 * RET_W]
        v = qkvg_ref[rows, 2 * RET_W:3 * RET_W]
        sq = qkvg_ref[rows, 4 * RET_W:5 * RET_W]
        kd = (k.astype(F32) * kd_ref[...]).astype(BF16)
        ret_ops = []
        for p in range(n_pairs):
            kp = k[:, p * LANES:(p + 1) * LANES]
            vp = v[:, p * LANES:(p + 1) * LANES]
            ret_ops.append((jnp.concatenate([jnp.where(low, kp, zero), jnp.where(low, zero, kp)], axis=0),
                            jnp.concatenate([jnp.where(low, vp, zero), jnp.where(low, zero, vp)], axis=0)))
        k_new = kv_ref[rows, 0:KV_W].astype(BF16)
        v_new = kv_ref[rows, KV_W:2 * KV_W].astype(BF16)
        swa_ops = _swa_operands(jnp.concatenate([k_prev, k_new], axis=0), jnp.concatenate([v_prev, v_new], axis=0))
        k_prev, v_prev = k_new, v_new

        scores = [lax.dot_general(q[:, p * LANES:(p + 1) * LANES], ret_ops[p][0], nt, preferred_element_type=F32)
                  for p in range(n_pairs)]
        cross = [jnp.dot(q[:, g * gw:(g + 1) * gw], s_cur[g].astype(BF16), preferred_element_type=F32)
                 for g in range(n_groups)]
        upd = [lax.dot_general(kd[:, g * gw:(g + 1) * gw], v[:, g * gw:(g + 1) * gw], tn,
                               preferred_element_type=F32) for g in range(n_groups)]
        logits = [lax.dot_general(
            jnp.concatenate([sq[:, (g * pairs_per_kv + pp) * LANES:(g * pairs_per_kv + pp + 1) * LANES]
                             for pp in range(pairs_per_kv)], axis=0),
            swa_ops[g][0], nt, preferred_element_type=F32) for g in range(KV_HEADS)]
        s_cur = [s_cur[g] * cd_ref[g] + jnp.where(cd_ref[g] > 0.0, upd[g], 0.0) for g in range(n_groups)]
        wave1.append((scores, cross, logits, ret_ops, swa_ops))
    for g in range(n_groups):
        s_scr[g] = s_cur[g]
    pk_scr[...] = k_prev
    pv_scr[...] = v_prev

    for c in range(n_blk):
        scores, cross, logits, ret_ops, swa_ops = wave1[c]
        bias_c = bias_ref.at[0] if c == 0 else bias_all_ref.at[0]
        scores = [(scores[p] * dm_ref[p]).astype(BF16) for p in range(n_pairs)]
        soft = [[_swa_softmax(logits[g][pp * CHUNK:(pp + 1) * CHUNK, :], g * pairs_per_kv + pp, bias_c, sink_ref)
                 for pp in range(pairs_per_kv)] for g in range(KV_HEADS)]
        intra = [jnp.dot(scores[p], ret_ops[p][1], preferred_element_type=F32) for p in range(n_pairs)]
        swa = []
        for g in range(KV_HEADS):
            oo = _swa_values(jnp.concatenate([x[0].astype(BF16) for x in soft[g]], axis=0),
                             jnp.concatenate([x[1] for x in soft[g]], axis=0), swa_ops[g][1])
            swa += [oo[pp * CHUNK:(pp + 1) * CHUNK, :] for pp in range(pairs_per_kv)]
        o = jnp.concatenate(intra, axis=1) + jnp.concatenate(cross, axis=1) * qd_ref[...]
        mid.append((o, swa))

    o = jnp.concatenate([m[0] for m in mid], axis=0)
    ret = _head_norm(o, ones_bd) * qkvg_ref[:, 3 * RET_W:4 * RET_W].astype(F32)
    mix_ref[:, 0:RET_W] = ret.astype(BF16)
    for c in range(n_blk):
        mix_ref[c * CHUNK:(c + 1) * CHUNK, RET_W:2 * RET_W] = jnp.concatenate(mid[c][1], axis=1).astype(BF16)

    @pl.when(i == pl.num_programs(1) - 1)
    def _():
        for h in range(N_HEADS):
            g, hl = divmod(h, gw // HEAD_DIM)
            s_out_ref[0, h] = s_scr[g, hl * HEAD_DIM:(hl + 1) * HEAD_DIM, hl * HEAD_DIM:(hl + 1) * HEAD_DIM]


def _prompt_tables():
    decay_mat, q_dec, k_dec, chunk_dec = _decay_tables(CHUNK)
    dm = jnp.concatenate([decay_mat[0::2], decay_mat[1::2]], axis=2)
    qd = jnp.repeat(q_dec, HEAD_DIM, axis=1)
    kd = jnp.repeat(k_dec, HEAD_DIM, axis=1)
    gw = 4 * HEAD_DIM
    blk = np.kron(np.eye(4, dtype=np.float32), np.ones((HEAD_DIM, HEAD_DIM), np.float32))
    cd = jnp.repeat(chunk_dec.reshape(2, 4), HEAD_DIM, axis=1)[:, :, None] * jnp.asarray(blk)[None]
    ones_bd = jnp.asarray(blk / HEAD_DIM, BF16)
    assert cd.shape == (2, gw, gw)
    return dm, qd, kd, cd, ones_bd


def _attn_prompt_call(qkvg, kv, bias, sinks, tables, b, l, n_blk):
    dm, qd, kd, cd, ones_bd = tables
    rows = n_blk * CHUNK
    nc = l // rows
    gw = cd.shape[1]
    bias_spec = lambda idx: pl.BlockSpec((1, N_HEADS, CHUNK, 2 * CHUNK), idx)
    return pl.pallas_call(
        _attn_prompt_kernel,
        grid=(b, nc),
        in_specs=[pl.BlockSpec(memory_space=pltpu.SMEM),
                  pl.BlockSpec((rows, QKVG_W), lambda bi, i: (bi * nc + i, 0)),
                  pl.BlockSpec((rows, 2 * KV_W), lambda bi, i: (bi * nc + i, 0)),
                  bias_spec(lambda bi, i: (jnp.where(i == 0, 1, 0), 0, 0, 0)),
                  bias_spec(lambda bi, i: (0, 0, 0, 0)),
                  _const_spec(dm.shape), _const_spec(qd.shape), _const_spec(kd.shape),
                  _const_spec(cd.shape), _const_spec(ones_bd.shape)],
        out_specs=[pl.BlockSpec((rows, 2 * RET_W), lambda bi, i: (bi * nc + i, 0)),
                   pl.BlockSpec((1, N_HEADS, HEAD_DIM, HEAD_DIM), lambda bi, i: (bi, 0, 0, 0))],
        out_shape=[jax.ShapeDtypeStruct((b * l, 2 * RET_W), BF16),
                   jax.ShapeDtypeStruct((b, N_HEADS, HEAD_DIM, HEAD_DIM), F32)],
        scratch_shapes=[pltpu.VMEM((RET_W // gw, gw, gw), F32),
                        pltpu.VMEM((CHUNK, KV_W), BF16),
                        pltpu.VMEM((CHUNK, KV_W), BF16)],
        compiler_params=_cparams(2),
        name="attn_prompt",
    )(sinks, qkvg, kv, bias, bias, dm, qd, kd, cd, ones_bd)


def _ret_sample_kernel(dec_ref, tq_ref, s_ref, buf_ref, ret_ref, s_out_ref, q_scr, k_scr, v_scr, cross_scr):
    del buf_ref
    h = pl.program_id(0)
    n_t = tq_ref.shape[1]
    cd = dec_ref[h, 2 * n_t]
    for t in range(n_t):
        q_scr[t] = tq_ref[0, t].astype(F32)
        k_scr[t] = tq_ref[1, t].astype(F32) * dec_ref[h, t]
        v_scr[t] = tq_ref[2, t].astype(F32)
    cross_scr[...] = jnp.zeros_like(cross_scr)

    def per_d(d, carry):
        s_d = s_ref[d]
        upd = s_d * cd
        for t in range(n_t):
            upd = upd + k_scr[t, pl.ds(d, 1), :] * v_scr[t]
            cross_scr[t] += q_scr[t, pl.ds(d, 1), :] * s_d
        s_out_ref[d] = upd
        return carry

    lax.fori_loop(0, HEAD_DIM, per_d, 0)

    for t in range(n_t):
        o = cross_scr[t] * dec_ref[h, n_t + t]
        for t2 in range(t + 1):
            sc = jnp.sum(q_scr[t] * (tq_ref[1, t2].astype(F32)), axis=0, keepdims=True)
            o = o + (sc * dec_ref[h, 2 * n_t + 1 + t * n_t + t2]) * v_scr[t2]
        mu = jnp.mean(o, axis=0, keepdims=True)
        d0 = o - mu
        var = jnp.mean(d0 * d0, axis=0, keepdims=True)
        ret_ref[t] = (d0 * lax.rsqrt(var + LN_EPS) * tq_ref[3, t].astype(F32)).astype(ret_ref.dtype)


def _ret_sample_call(tq, state5, s_buf, layer, dec):
    _, n_t, _, b = tq.shape
    state_blk = (None, None, HEAD_DIM, HEAD_DIM, b)
    return pl.pallas_call(
        _ret_sample_kernel,
        grid=(N_HEADS,),
        in_specs=[pl.BlockSpec(memory_space=pltpu.SMEM),
                  pl.BlockSpec((4, n_t, HEAD_DIM, b), lambda h: (0, 0, h, 0)),
                  pl.BlockSpec(state_blk, lambda h: (layer, h, 0, 0, 0)),
                  pl.BlockSpec(memory_space=pl.ANY)],
        out_specs=[pl.BlockSpec((n_t, HEAD_DIM, b), lambda h: (0, h, 0)),
                   pl.BlockSpec(state_blk, lambda h: (layer, h, 0, 0, 0))],
        out_shape=[jax.ShapeDtypeStruct((n_t, RET_W, b), BF16),
                   jax.ShapeDtypeStruct(s_buf.shape, F32)],
        scratch_shapes=[pltpu.VMEM((n_t, HEAD_DIM, b), F32)] * 4,
        input_output_aliases={3: 1},
        compiler_params=_cparams(1),
        name="ret_sample",
    )(dec, tq, state5, s_buf)


def _ret_sample_table(tl):
    decay_mat, q_dec, k_dec, chunk_dec = _decay_tables(tl)
    return jnp.concatenate([k_dec.T, q_dec.T, chunk_dec[:, None], decay_mat.reshape(N_HEADS, tl * tl)], axis=1)


def _swa_sample_kernel(sink_ref, qkvg_ref, kv_ref, ckt_ref, cvt_ref, bias_ref, kbuf_ref, vbuf_ref,
                       swa_ref, ck_out_ref, cv_out_ref):
    del kbuf_ref, vbuf_ref
    tb = ckt_ref.shape[0]
    tl = qkvg_ref.shape[0] // tb
    nt = (((1,), (1,)), ((), ()))
    pairs_per_kv = N_HEADS // KV_HEADS // 2
    low = _lane_mask_low()
    zero_t =jnp.zeros((HEAD_DIM, WINDOW), BF16)
    front = jnp.zeros((WINDOW - tl, KV_W), F32)
    is_new =lax.broadcasted_iota(jnp.int32, (1, WINDOW), 1) >= WINDOW - tl
    bias = bias_ref.at[0]

    def block_diag(t):
        return jnp.concatenate([jnp.concatenate([t, zero_t], axis=1), jnp.concatenate([zero_t, t], axis=1)], axis=0)

    wave1 = []
    for e in range(tb):
        rows = slice(e * tl, (e + 1) * tl)
        sq32 = qkvg_ref[rows, 4 * RET_W:5 * RET_W].astype(F32)
        k_new = kv_ref[rows, 0:KV_W]
        v_new = kv_ref[rows, KV_W:2 * KV_W]
        k_rot = pltpu.roll(k_new, HEAD_DIM, 1)
        v_rot = pltpu.roll(v_new, HEAD_DIM, 1)
        k_placed = jnp.concatenate([front, k_new], axis=0).T
        v_placed = jnp.concatenate([front, v_new], axis=0).T
        per_g = []
        for g in range(KV_HEADS):
            kt = ckt_ref[e, g]
            vt = cvt_ref[e, g]
            hd = slice(g * HEAD_DIM, (g + 1) * HEAD_DIM)
            ck_out_ref[e, g] = jnp.where(is_new, k_placed[hd, :], pltpu.roll(kt, WINDOW - tl, 1))
            cv_out_ref[e, g] = jnp.where(is_new, v_placed[hd, :], pltpu.roll(vt, WINDOW - tl, 1))
            lhs = jnp.concatenate([sq32[:, (g * pairs_per_kv + pp) * LANES:(g * pairs_per_kv + pp + 1) * LANES]
                                   for pp in range(pairs_per_kv)], axis=0).astype(BF16)
            kn_lo, kn_hi = (k_new, k_rot) if g == 0 else (k_rot, k_new)
            vn_lo, vn_hi = (v_new, v_rot) if g == 0 else (v_rot, v_new)
            kn = [jnp.where(low, kn_lo, 0.0).astype(BF16), jnp.where(low, 0.0, kn_hi).astype(BF16)]
            vn = [jnp.where(low, vn_lo, 0.0).astype(BF16), jnp.where(low, 0.0, vn_hi).astype(BF16)]
            lc = jnp.dot(lhs, block_diag(kt.astype(BF16)), preferred_element_type=F32)
            ln = [lax.dot_general(lhs, kn[s], nt, preferred_element_type=F32) for s in range(2)]
            per_g.append((lc, ln, vn, block_diag(vt.astype(BF16))))
        wave1.append(per_g)

    swa_rows = []
    for e in range(tb):
        outs = []
        for g in range(KV_HEADS):
            lc, ln, vn, vbd = wave1[e][g]
            pcs, pns, dens = [], [[], []], []
            for pp in range(pairs_per_kv):
                r = slice(pp * tl, (pp + 1) * tl)
                den_pair = []
                for s in range(2):
                    hh = 2 * (g * pairs_per_kv + pp) + s
                    lo_c = lc[r, s * WINDOW:(s + 1) * WINDOW] + bias[hh, 0:tl, 0:WINDOW]
                    lo_n = ln[s][r, :] + bias[hh, 0:tl, WINDOW:WINDOW + tl]
                    sink = sink_ref[hh]
                    m = jnp.maximum(jnp.maximum(jnp.max(lo_c, axis=-1, keepdims=True),
                                                jnp.max(lo_n, axis=-1, keepdims=True)), sink)
                    pc = jnp.exp(lo_c - m)
                    pn = jnp.exp(lo_n - m)
                    den_pair.append(jnp.sum(pc, axis=-1, keepdims=True) + jnp.sum(pn, axis=-1, keepdims=True)
                                    + jnp.exp(sink - m))
                    pcs.append((pp, pc))
                    pns[s].append(pn)
                dens.append(jnp.where(low, den_pair[0], den_pair[1]))
            pc_rows = [jnp.concatenate([x[1] for x in pcs if x[0] == pp], axis=1) for pp in range(pairs_per_kv)]
            pc_all = jnp.concatenate(pc_rows, axis=0).astype(BF16)
            oo = lax.dot_general(pc_all, vbd, nt, preferred_element_type=F32)
            for s in range(2):
                oo = oo + jnp.dot(jnp.concatenate(pns[s], axis=0).astype(BF16), vn[s], preferred_element_type=F32)
            oo = oo / jnp.concatenate(dens, axis=0)
            outs += [oo[pp * tl:(pp + 1) * tl, :] for pp in range(pairs_per_kv)]
        swa_rows.append(jnp.concatenate(outs, axis=1))
    swa_ref[...] = jnp.concatenate(swa_rows, axis=0).astype(BF16)


def _swa_sample_call(qkvg, kv, ckt, cvt, kbuf, vbuf, layer, bias, sinks, b, tl, tb):
    cache_blk = (None, tb, KV_HEADS, HEAD_DIM, WINDOW)
    layer5 = lambda i: (layer, i, 0, 0, 0)
    return pl.pallas_call(
        _swa_sample_kernel,
        grid=(b // tb,),
        in_specs=[pl.BlockSpec(memory_space=pltpu.SMEM),
                  pl.BlockSpec((tb * tl, QKVG_W), lambda i: (i, 0)),
                  pl.BlockSpec((tb * tl, 2 * KV_W), lambda i: (i, 0)),
                  pl.BlockSpec(cache_blk, layer5),
                  pl.BlockSpec(cache_blk, layer5),
                  pl.BlockSpec((1, N_HEADS, CHUNK, 2 * CHUNK), lambda i: (0, 0, 0, 0)),
                  pl.BlockSpec(memory_space=pl.ANY),
                  pl.BlockSpec(memory_space=pl.ANY)],
        out_specs=[pl.BlockSpec((tb * tl, RET_W), lambda i: (i, 0)),
                   pl.BlockSpec(cache_blk, layer5),
                   pl.BlockSpec(cache_blk, layer5)],
        out_shape=[jax.ShapeDtypeStruct((b * tl, RET_W), BF16),
                   jax.ShapeDtypeStruct(kbuf.shape, F32),
                   jax.ShapeDtypeStruct(vbuf.shape, F32)],
        input_output_aliases={6: 1, 7: 2},
        compiler_params=_cparams(1),
        name="swa_sample",
    )(sinks, qkvg, kv, ckt, cvt, bias, kbuf, vbuf)


def _layer_norm(x, g, b):
    mu = jnp.mean(x, axis=-1, keepdims=True)
    xc = x - mu
    var = jnp.mean(xc * xc, axis=-1, keepdims=True)
    return xc * lax.rsqrt(var + LN_EPS) * g + b


def _ffn_kernel(alpha, carry_rows, n_cast, mix_ref, x_ref, g1_ref, sh2_ref, sc2_ref, g2_ref, wout_ref, ln1g_ref,
                ln1b_ref, wup_ref, cw_ref, cb_ref, prev_ref, wdn_ref, ln2g_ref, ln2b_ref, *rest):
    cast_in, (xo_ref, tail_ref), carry_scr = rest[:n_cast], rest[n_cast:n_cast + 2], rest[-1]
    zero_refs = rest[n_cast + 2:len(rest) - 1 - n_cast]
    for src_ref, dst_ref in zip(cast_in, rest[len(rest) - 1 - n_cast:-1]):
        dst_ref[...] = src_ref[...].astype(dst_ref.dtype)
    for z_ref in zero_refs:
        z_ref[...] = jnp.zeros(z_ref.shape, z_ref.dtype)
    tb, tl, _ = x_ref.shape
    tm = tb * tl
    sm = FF_ROWS
    n_sub = tm // sm
    sb, sl = (1, sm) if carry_rows else (tb // n_sub, tl)
    j = pl.program_id(1)

    def seq(ref, s):
        if carry_rows:
            return ref[:, s * sm:(s + 1) * sm, :] if ref.shape[1] == tl else ref[...]
        return ref[s * sb:(s + 1) * sb]

    if carry_rows:
        @pl.when(j == 0)
        def _():
            carry_scr[carry_rows - 2:carry_rows, :] = prev_ref[0]

    fix_rows = 8 if carry_rows else sm
    t_idx = lax.broadcasted_iota(jnp.int32, (fix_rows, 1), 0) % sl
    is_t0 = t_idx == 0
    is_t1 = t_idx == 1
    n_chunks = D_FF // FF_CHUNK

    def up_cols(c, half):
        return slice(half * D_FF + c * FF_CHUNK, half * D_FF + (c + 1) * FF_CHUNK)

    def pre(s):
        x1 = alpha * seq(x_ref, s) + seq(g1_ref, s) * y[s].reshape(sb, sl, D_MODEL)
        x1 = _layer_norm(x1, ln1g_ref[...], ln1b_ref[...])
        h2 = (x1 * (1.0 + seq(sc2_ref, s)) + seq(sh2_ref, s)).reshape(sm, D_MODEL).astype(BF16)
        return x1, h2

    def up_dots(h2, c):
        return [jnp.dot(h2, wup_ref[:, up_cols(c, half)], preferred_element_type=F32) for half in range(2)]

    def conv(s, up, cols):
        if carry_rows:
            src = carry_scr if s == 0 else tail_ref
            p0 = src[carry_rows - 2:carry_rows - 1, cols]
            p1 = src[carry_rows - 1:carry_rows, cols]
            tail_ref[:, cols] = up[sm - carry_rows:sm, :]
        else:
            up3 = up.reshape(sb, sl, FF_CHUNK)
            tail_ref[s * sb:(s + 1) * sb, :, cols] = up3[:, sl - 2:sl, :]
            prev = seq(prev_ref, s)
            p0, p1 = prev[:, 0:1, cols], prev[:, 1:2, cols]
            t3 = lax.broadcasted_iota(jnp.int32, (1, sl, 1), 1)
            s1 = jnp.where(t3 == 0, p1, pltpu.roll(up3, 1, 1))
            s2 = jnp.where(t3 == 0, p0, jnp.where(t3 == 1, p1, pltpu.roll(up3, 2, 1)))
            out = cb_ref[:, cols] + s2 * cw_ref[0:1, cols] + s1 * cw_ref[1:2, cols] + up3 * cw_ref[2:3, cols]
            return out.reshape(sm, FF_CHUNK)
        r1 = pltpu.roll(up, 1, 0)
        r2 = pltpu.roll(up, 2, 0)
        s1 = jnp.where(is_t0, p1, r1[0:fix_rows])
        s2 = jnp.where(is_t0, p0, jnp.where(is_t1, p1, r2[0:fix_rows]))
        if fix_rows < sm:
            s1 = jnp.concatenate([s1, r1[fix_rows:]], axis=0)
            s2 = jnp.concatenate([s2, r2[fix_rows:]], axis=0)
        return cb_ref[:, cols] + s2 * cw_ref[0:1, cols] + s1 * cw_ref[1:2, cols] + up * cw_ref[2:3, cols]

    y = [jnp.dot(mix_ref[s * sm:(s + 1) * sm, :], wout_ref[...], preferred_element_type=F32) for s in range(n_sub)]
    items = [(s, c) for s in range(n_sub) for c in range(n_chunks)]
    staged, ups = {}, {}

    def issue_up(k):
        s, c = items[k]
        if c == 0:
            staged[s] = pre(s)
        ups[k] = up_dots(staged[s][1], c)

    for k in range(min(FF_LOOKAHEAD, len(items))):
        issue_up(k)
    acc = None
    for k, (s, c) in enumerate(items):
        if k + FF_LOOKAHEAD < len(items):
            issue_up(k + FF_LOOKAHEAD)
        up_a, up_b = ups.pop(k)
        ua = conv(s, up_a, up_cols(c, 0))
        ub = conv(s, up_b, up_cols(c, 1))
        gated = (ua * jax.nn.sigmoid(ua) * ub).astype(BF16)
        down = jnp.dot(gated, wdn_ref[c * FF_CHUNK:(c + 1) * FF_CHUNK, :], preferred_element_type=F32)
        acc = down if c == 0 else acc + down
        if c == n_chunks - 1:
            x1, _ = staged.pop(s)
            x2 = alpha * x1 + seq(g2_ref, s) * acc.reshape(sb, sl, D_MODEL)
            out = _layer_norm(x2, ln2g_ref[...], ln2b_ref[...])
            if carry_rows:
                xo_ref[:, s * sm:(s + 1) * sm, :] = out
            else:
                xo_ref[s * sb:(s + 1) * sb] = out

    if carry_rows:
        carry_scr[...] = tail_ref[...]


def _ffn_call(mix, x, mod, conv_prev, w_out, ln1_g, ln1_b, w_up, conv_w, conv_b, w_down, ln2_g, ln2_b,
              layer, alpha, tb, tl, zero_shapes=(), cast=()):
    prev_layer = layer if conv_prev.shape[0] > 1 else 0
    w_layer = layer if w_up.shape[0] > 1 else 0
    b, l, _ = x.shape
    nj = l // tl
    tm = tb * tl
    n_steps = (b // tb) * nj
    zero_specs = []
    for shape in zero_shapes:
        per0 = n_steps // shape[0]
        blk = (1, shape[1] // per0) + tuple(shape[2:])
        assert per0 * shape[0] == n_steps and blk[1] * per0 == shape[1]
        zero_specs.append(pl.BlockSpec(blk, lambda i, j, per0=per0, nd=len(shape):
                                       ((i * nj + j) // per0, (i * nj + j) % per0) + (0,) * (nd - 2)))
    cast_in, cast_out, cast_shape = [], [], []
    for w, w_src_layer in cast:
        _, rows, cols = w.shape
        blk = (None, rows // n_steps, cols)
        assert blk[1] * n_steps == rows and blk[1] % BF16_ROWS == 0
        cast_in.append(pl.BlockSpec(blk, lambda i, j, src=w_src_layer: (src, i * nj + j, 0)))
        cast_out.append(pl.BlockSpec(blk, lambda i, j: (0, i * nj + j, 0)))
        cast_shape.append(jax.ShapeDtypeStruct((1, rows, cols), BF16))
    carry_rows = 8 if tb == 1 else 0
    assert carry_rows or (nj == 1 and tl == 8)
    if carry_rows:
        tail_spec = pl.BlockSpec((8, UP_W), lambda i, j: (i, 0))
        tail_shape = jax.ShapeDtypeStruct((b * 8, UP_W), F32)
    else:
        tail_spec = pl.BlockSpec((tb, 2, UP_W), lambda i, j: (i, 0, 0))
        tail_shape = jax.ShapeDtypeStruct((b, 2, UP_W), F32)
    kern = functools.partial(_ffn_kernel, alpha, carry_rows, len(cast))
    return pl.pallas_call(
        kern,
        grid=(b // tb, nj),
        in_specs=[pl.BlockSpec((tm, 2 * RET_W), lambda i, j: (i * nj + j, 0)),
                  pl.BlockSpec((tb, tl, D_MODEL), lambda i, j: (i, j, 0)),
                  _mod_spec(tb, layer, 2), _mod_spec(tb, layer, 3), _mod_spec(tb, layer, 4), _mod_spec(tb, layer, 5),
                  _layer_spec((2 * RET_W, D_MODEL), w_layer),
                  _layer_spec((1, D_MODEL), layer), _layer_spec((1, D_MODEL), layer),
                  _layer_spec((D_MODEL, UP_W), w_layer),
                  _layer_spec((3, UP_W), layer), _layer_spec((1, UP_W), layer),
                  pl.BlockSpec((None, tb, 2, UP_W), lambda i, j: (prev_layer, i, 0, 0)),
                  _layer_spec((D_FF, D_MODEL), w_layer),
                  _layer_spec((1, D_MODEL), layer), _layer_spec((1, D_MODEL), layer)] + cast_in,
        out_specs=[pl.BlockSpec((tb, tl, D_MODEL), lambda i, j: (i, j, 0)),
                   tail_spec] + zero_specs + cast_out,
        out_shape=[jax.ShapeDtypeStruct((b, l, D_MODEL), F32), tail_shape]
        + [jax.ShapeDtypeStruct(shape, F32) for shape in zero_shapes] + cast_shape,
        scratch_shapes=[pltpu.VMEM((8, UP_W), F32)],
        compiler_params=_cparams(2),
        name="ffn",
    )(mix, x, mod, mod, mod, mod, w_out, ln1_g, ln1_b, w_up, conv_w, conv_b, conv_prev, w_down, ln2_g, ln2_b,
      *[w for w, _ in cast])


def kernel(x_prompt, x_sample, c_prompt, c_sample, state_ret, cache_swa_k, cache_swa_v, state_conv, rel_bias, w_ada, b_ada, w_in, swa_sinks, w_out, ln1_g, ln1_b, w_up, conv_w, conv_b, w_down, ln2_g, ln2_b):
    depth = w_ada.shape[0]
    bp, lp, _ = x_prompt.shape
    bs, ls, _ = x_sample.shape
    alpha = (2.0 * depth) ** 0.25
    tl_proj = 1024
    tl_p = 2 * FF_ROWS
    tb_s = 2 * FF_ROWS // ls
    tb_attn = 8
    blk_attn = 8

    c_all = jnp.concatenate([c_prompt, c_sample], axis=0)
    mod_p, mod_s = _ada_call(c_all, bp, w_ada, b_ada)
    bias = _bias_call(rel_bias)

    cos_p, sin_p = _rope_tables(jnp.arange(lp, dtype=jnp.int32))
    cos_s, sin_s = _rope_tables(PAST_LEN + jnp.arange(ls, dtype=jnp.int32))
    cos_s, sin_s = jnp.tile(cos_s, (tb_s, 1)), jnp.tile(sin_s, (tb_s, 1))
    tab_p = _prompt_tables()
    dec_s = _ret_sample_table(ls)
    conv0 = jnp.zeros((1, bp, 2, UP_W), F32)

    w_in_l = w_in[0:1].astype(BF16)
    vec = lambda a: a.reshape(depth, 1, a.shape[-1])
    ffn_vec = (vec(ln1_g), vec(ln1_b), conv_w, vec(conv_b), vec(ln2_g), vec(ln2_b))
    state5 = jnp.transpose(state_ret, (0, 2, 3, 4, 1))
    ckt = jnp.transpose(cache_swa_k, (0, 1, 3, 4, 2))
    cvt = jnp.transpose(cache_swa_v, (0, 1, 3, 4, 2))

    xp, xs = x_prompt, x_sample
    p_ret, p_k, p_v, p_conv = [], [], [], []
    s_conv = []
    for l in range(depth):
        qkvg, kv, wo_b, wu_b, wd_b = _proj_call(xp, mod_p, w_in_l, l, cos_p, sin_p, 1, tl_proj, (w_out, w_up, w_down))
        ffn_w = (wo_b, ffn_vec[0], ffn_vec[1], wu_b, ffn_vec[2], ffn_vec[3], wd_b, ffn_vec[4], ffn_vec[5])
        mix, r_p =_attn_prompt_call(qkvg, kv, bias, swa_sinks[l], tab_p, bp, lp, blk_attn)
        zero_shapes = (state5.shape, ckt.shape, cvt.shape) if l == 0 else ()
        cast_next = ((w_in, l + 1),) if l + 1 < depth else ()
        xp, tail, *extra = _ffn_call(mix, xp, mod_p, conv0, *ffn_w, l, alpha, 1, tl_p, zero_shapes, cast_next)
        if l == 0:
            s_buf, k_buf, v_buf = extra[0:3]
        w_in_next = extra[-1] if cast_next else None
        kv3 =kv.reshape(bp, lp, 2, KV_HEADS, HEAD_DIM)
        p_ret.append(r_p)
        p_k.append(kv3[:, lp - WINDOW:, 0])
        p_v.append(kv3[:, lp - WINDOW:, 1])
        p_conv.append(tail.reshape(bp, 8, UP_W)[:, 6:8])

        qkvg, kv = _proj_call(xs, mod_s, w_in_l, l, cos_s, sin_s, tb_s, ls)
        tq = jnp.transpose(qkvg.reshape(bs, ls, 5, RET_W)[:, :, 0:4], (2, 1, 3, 0))
        ret_t, s_buf = _ret_sample_call(tq, state5, s_buf, l, dec_s)
        ret = jnp.transpose(ret_t, (2, 0, 1)).reshape(bs * ls, RET_W)
        swa, k_buf, v_buf = _swa_sample_call(qkvg, kv, ckt, cvt, k_buf, v_buf, l, bias, swa_sinks[l], bs, ls, tb_attn)
        mix = jnp.concatenate([ret, swa], axis=1)
        xs, tail = _ffn_call(mix, xs, mod_s, state_conv, *ffn_w, l, alpha, tb_s, ls)
        s_conv.append(tail)
        w_in_l = w_in_next

    return (xp, xs,
            jnp.stack(p_ret), jnp.stack(p_k), jnp.stack(p_v), jnp.stack(p_conv),
            jnp.transpose(s_buf, (0, 4, 1, 2, 3)), jnp.transpose(k_buf, (0, 1, 4, 2, 3)),
            jnp.transpose(v_buf, (0, 1, 4, 2, 3)), jnp.stack(s_conv))
```

```python
import functools
import math

import jax
import jax.numpy as jnp
import numpy as np
from jax import lax
from jax.experimental import pallas as pl
from jax.experimental.pallas import tpu as pltpu

F32 = jnp.float32
BF16 = jnp.bfloat16

D_MODEL = 1024
HEAD_DIM = 64
N_HEADS = 8
RET_W = N_HEADS * HEAD_DIM
KV_HEADS = 2
KV_W = KV_HEADS * HEAD_DIM
D_FF = 2816
UP_W = 2 * D_FF
IN_COLS = 4 * RET_W + RET_W + 2 * KV_W
QKVG_W = 5 * RET_W
WINDOW = 128
CHUNK = 128
NUM_BUCKETS = 32
PAST_LEN = 8192
ROPE_BASE = 10000.0
LN_EPS = 1e-5
NEG_INF = -1e30
LANES = 128
BF16_ROWS = 16
FF_CHUNK = 256
FF_LOOKAHEAD = 2
FF_ROWS = 256
VMEM_LIMIT = 56 * 1024 * 1024


def _cparams(n_axes):
    return pltpu.CompilerParams(dimension_semantics=("arbitrary",) * n_axes,
                                vmem_limit_bytes=VMEM_LIMIT)


def _const_spec(shape):
    nd = len(shape)
    return pl.BlockSpec(shape, lambda *_: (0,) * nd, pipeline_mode=pl.Buffered(1))


def _rope_tables(pos):
    half = HEAD_DIM // 2
    inv = 1.0 / (ROPE_BASE ** (jnp.arange(half, dtype=F32) / half))
    ang = pos.astype(F32)[:, None] * inv[None, :]
    cos = jnp.cos(ang)
    sin = jnp.sin(ang)
    cos_h = jnp.concatenate([cos, cos], axis=-1)
    sin_h = jnp.concatenate([-sin, sin], axis=-1)
    return jnp.tile(cos_h, (1, N_HEADS)), jnp.tile(sin_h, (1, N_HEADS))


def _decay_tables(chunk):
    log_g = jnp.log(1.0 - 2.0 ** (-5.0 - jnp.arange(N_HEADS, dtype=F32)))
    idx = jnp.arange(chunk)
    diff = idx[:, None] - idx[None, :]
    decay_mat = jnp.where(diff[None] >= 0,
                          jnp.exp(log_g[:, None, None] * jnp.maximum(diff, 0)[None].astype(F32)), 0.0)
    q_dec = jnp.exp(log_g[None, :] * (idx[:, None] + 1).astype(F32))
    k_dec = jnp.exp(log_g[None, :] * (chunk - 1 - idx[:, None]).astype(F32))
    chunk_dec = jnp.exp(log_g * chunk)
    return decay_mat, q_dec, k_dec, chunk_dec


def _bucket_table(n_q, n_k):
    i = np.arange(n_q)[:, None]
    j = np.arange(n_k)[None, :]
    dist = i - j + WINDOW
    n = np.maximum(dist, 0)
    max_exact = NUM_BUCKETS // 2
    nf = np.maximum(n, max_exact).astype(np.float64)
    large = max_exact + (np.log(nf / max_exact) / math.log(WINDOW / max_exact)
                         * (NUM_BUCKETS - max_exact)).astype(np.int32)
    large = np.minimum(large, NUM_BUCKETS - 1)
    bucket = np.where(n < max_exact, n, large).astype(np.int32)
    valid = ((dist >= 0) & (dist < WINDOW)).astype(np.int32)
    return bucket, valid


def _ada_kernel(c_ref, w_ref, b_ref, op_ref, os_ref):
    c = c_ref[...]
    s = (c * jax.nn.sigmoid(c)).astype(BF16)
    mod = jnp.dot(s, w_ref[0].astype(BF16), preferred_element_type=F32) + b_ref[0]
    bp = op_ref.shape[0]
    op_ref[:, 0, :] = mod[0:bp]
    os_ref[:, 0, :] = mod[bp:]


def _ada_call(c_all, bp, w_ada, b_ada):
    depth = w_ada.shape[0]
    n_rows = c_all.shape[0]
    bs = n_rows - bp
    tn = 1536
    out_blk = lambda b: pl.BlockSpec((None, b, 1, tn), lambda l, n: (l, 0, 0, n))
    return pl.pallas_call(
        _ada_kernel,
        grid=(depth, 6 * D_MODEL // tn),
        in_specs=[pl.BlockSpec((n_rows, D_MODEL), lambda l, n: (0, 0)),
                  pl.BlockSpec((1, D_MODEL, tn), lambda l, n: (l, 0, n)),
                  pl.BlockSpec((1, 1, tn), lambda l, n: (l, 0, n))],
        out_specs=[out_blk(bp), out_blk(bs)],
        out_shape=[jax.ShapeDtypeStruct((depth, bp, 1, 6 * D_MODEL), F32),
                   jax.ShapeDtypeStruct((depth, bs, 1, 6 * D_MODEL), F32)],
        compiler_params=_cparams(2),
        name="ada",
    )(c_all, w_ada, b_ada.reshape(depth, 1, 6 * D_MODEL))


def _bias_kernel(rb_ref, bucket_ref, valid_ref, o_ref):
    bucket = bucket_ref[...]
    valid = valid_ref[...] > 0
    first_ok = lax.broadcasted_iota(jnp.int32, bucket.shape, 1) >= WINDOW
    for h in range(N_HEADS):
        acc = jnp.zeros(bucket.shape, F32)
        for b in range(NUM_BUCKETS):
            acc = jnp.where(bucket == b, rb_ref[b, h], acc)
        o_ref[0, h] = jnp.where(valid, acc, NEG_INF)
        o_ref[1, h] = jnp.where(valid & first_ok, acc, NEG_INF)


def _bias_call(rel_bias):
    bucket, valid = _bucket_table(CHUNK, 2 * CHUNK)
    return pl.pallas_call(
        _bias_kernel,
        in_specs=[pl.BlockSpec(memory_space=pltpu.SMEM),
                  pl.BlockSpec((CHUNK, 2 * CHUNK), lambda: (0, 0)),
                  pl.BlockSpec((CHUNK, 2 * CHUNK), lambda: (0, 0))],
        out_specs=pl.BlockSpec((2, N_HEADS, CHUNK, 2 * CHUNK), lambda: (0, 0, 0, 0)),
        out_shape=jax.ShapeDtypeStruct((2, N_HEADS, CHUNK, 2 * CHUNK), F32),
        name="swa_bias",
    )(rel_bias, jnp.asarray(bucket), jnp.asarray(valid))


def _swap_halves(x):
    lane = lax.broadcasted_iota(jnp.int32, (1, LANES), 1)
    first = (lane % HEAD_DIM) < (HEAD_DIM // 2)
    cols = []
    for c in range(x.shape[1] // LANES):
        xc = x[:, c * LANES:(c + 1) * LANES]
        cols.append(jnp.where(first, pltpu.roll(xc, LANES - HEAD_DIM // 2, 1), pltpu.roll(xc, HEAD_DIM // 2, 1)))
    return jnp.concatenate(cols, axis=1)


def _proj_kernel(x_ref, sc_ref, sh_ref, w_ref, cos_ref, sin_ref, *refs):
    n_cast = (len(refs) - 2) // 2
    qkvg_ref, kv_ref = refs[n_cast], refs[n_cast + 1]
    for src_ref, dst_ref in zip(refs[:n_cast], refs[n_cast + 2:]):
        dst_ref[...] = src_ref[...].astype(dst_ref.dtype)
    tb, tl, _ = x_ref.shape
    tm = tb * tl
    h = x_ref[...] * (1.0 + sc_ref[...]) + sh_ref[...]
    h = h.reshape(tm, D_MODEL).astype(BF16)
    proj = jnp.dot(h, w_ref[...], preferred_element_type=F32)
    cos = cos_ref[...]
    sin = sin_ref[...]
    rq = proj[:, 0:RET_W]
    rk = proj[:, RET_W:2 * RET_W]
    rq = rq * cos + _swap_halves(rq) * sin
    rk = (rk * cos + _swap_halves(rk) * sin) * (HEAD_DIM ** -0.5)
    rg = proj[:, 3 * RET_W:4 * RET_W]
    qkvg_ref[:, 0:RET_W] = rq.astype(BF16)
    qkvg_ref[:, RET_W:2 * RET_W] = rk.astype(BF16)
    qkvg_ref[:, 2 * RET_W:3 * RET_W] = proj[:, 2 * RET_W:3 * RET_W].astype(BF16)
    qkvg_ref[:, 3 * RET_W:4 * RET_W] = (rg * jax.nn.sigmoid(rg)).astype(BF16)
    qkvg_ref[:, 4 * RET_W:5 * RET_W] = (proj[:, 4 * RET_W:5 * RET_W] * (HEAD_DIM ** -0.5)).astype(BF16)
    kv_ref[...] = proj[:, 5 * RET_W:IN_COLS]


def _layer_spec(shape, layer):
    nd = len(shape)
    return pl.BlockSpec((None,) + tuple(shape), lambda *_: (layer,) + (0,) * nd, pipeline_mode=pl.Buffered(1))


def _mod_spec(tb, layer, k):
    return pl.BlockSpec((None, tb, 1, D_MODEL), lambda i, j: (layer, i, 0, k))


def _proj_call(x, mod, w_in, layer, cos, sin, tb, tl, cast=()):
    b, l, _ = x.shape
    nj = l // tl
    tm = tb * tl
    n_steps = (b // tb) * nj
    tab_idx = (lambda i, j: (j, 0)) if cos.shape[0] == l and nj > 1 else (lambda i, j: (0, 0))
    cast_in, cast_out, cast_shape = [], [], []
    for w in cast:
        _, rows, cols = w.shape
        blk = (None, rows // n_steps, cols)
        assert blk[1] * n_steps == rows and blk[1] % BF16_ROWS == 0
        cast_in.append(pl.BlockSpec(blk, lambda i, j: (layer, i * nj + j, 0)))
        cast_out.append(pl.BlockSpec(blk, lambda i, j: (0, i * nj + j, 0)))
        cast_shape.append(jax.ShapeDtypeStruct((1, rows, cols), BF16))
    return pl.pallas_call(
        _proj_kernel,
        grid=(b // tb, nj),
        in_specs=[pl.BlockSpec((tb, tl, D_MODEL), lambda i, j: (i, j, 0)),
                  _mod_spec(tb, layer, 1),
                  _mod_spec(tb, layer, 0),
                  _layer_spec((D_MODEL, IN_COLS), layer if w_in.shape[0] > 1 else 0),
                  pl.BlockSpec((tm, RET_W), tab_idx),
                  pl.BlockSpec((tm, RET_W), tab_idx)] + cast_in,
        out_specs=[pl.BlockSpec((tm, QKVG_W), lambda i, j: (i * nj + j, 0)),
                   pl.BlockSpec((tm, 2 * KV_W), lambda i, j: (i * nj + j, 0))] + cast_out,
        out_shape=[jax.ShapeDtypeStruct((b * l, QKVG_W), BF16),
                   jax.ShapeDtypeStruct((b * l, 2 * KV_W), F32)] + cast_shape,
        compiler_params=_cparams(2),
        name="proj",
    )(x, mod, mod, w_in, cos, sin, *cast)


def _lane_mask_low():
    return lax.broadcasted_iota(jnp.int32, (1, LANES), 1) < HEAD_DIM


def _head_norm(o, ones_bd):
    outs = []
    gw = ones_bd.shape[0]
    for g in range(RET_W // gw):
        og = o[:, g * gw:(g + 1) * gw]
        mu = jnp.dot(og.astype(BF16), ones_bd, preferred_element_type=F32)
        d = og - mu
        var = jnp.dot((d * d).astype(BF16), ones_bd, preferred_element_type=F32)
        outs.append(d * lax.rsqrt(var + LN_EPS))
    return jnp.concatenate(outs, axis=1)


def _swa_operands(k_all, v_all):
    low = _lane_mask_low()
    zero = jnp.zeros_like(k_all)
    ones_low = jnp.broadcast_to(jnp.where(low, 1.0, 0.0), k_all.shape)
    ones_cols = jnp.concatenate([ones_low, 1.0 - ones_low], axis=0).astype(BF16)
    k_rot = pltpu.roll(k_all, HEAD_DIM, 1)
    v_rot = pltpu.roll(v_all, HEAD_DIM, 1)
    ops = []
    for g in range(KV_HEADS):
        k_lo, k_hi = (k_all, k_rot) if g == 0 else (k_rot, k_all)
        v_lo, v_hi = (v_all, v_rot) if g == 0 else (v_rot, v_all)
        kc = jnp.concatenate([jnp.where(low, k_lo, zero), jnp.where(low, zero, k_hi)], axis=0)
        vc = jnp.concatenate([jnp.where(low, v_lo, zero), jnp.where(low, zero, v_hi)], axis=0)
        ops.append((kc, jnp.concatenate([vc, ones_cols], axis=1)))
    return ops


def _swa_softmax(lg, p, bias_ref, sink_ref):
    rows = lg.shape[0]
    low = _lane_mask_low()
    probs, sinks = [], []
    for s in range(2):
        hh = 2 * p + s
        lo = lg[:, s * 2 * CHUNK:(s + 1) * 2 * CHUNK] + bias_ref[hh, 0:rows, :]
        sink = sink_ref[hh]
        m = jnp.maximum(jnp.max(lo, axis=-1, keepdims=True), sink)
        probs.append(jnp.exp(lo - m))
        sinks.append(jnp.exp(sink - m))
    return jnp.concatenate(probs, axis=1), jnp.where(low, sinks[0], sinks[1])


def _swa_values(probs, sink_term, vc):
    oo = jnp.dot(probs, vc, preferred_element_type=F32)
    return oo[:, 0:LANES] / (oo[:, LANES:2 * LANES] + sink_term)


def _attn_prompt_kernel(sink_ref, qkvg_ref, kv_ref, bias_ref, bias_all_ref, dm_ref, qd_ref, kd_ref, cd_ref, ones_ref,
                        mix_ref, s_out_ref, s_scr, pk_scr, pv_scr):
    i = pl.program_id(1)

    @pl.when(i == 0)
    def _():
        s_scr[...] = jnp.zeros_like(s_scr)
        pk_scr[...] = jnp.zeros_like(pk_scr)
        pv_scr[...] = jnp.zeros_like(pv_scr)

    low = _lane_mask_low()
    zero = jnp.zeros((CHUNK, LANES), BF16)
    n_pairs = N_HEADS // 2
    pairs_per_kv = n_pairs // KV_HEADS
    gw = cd_ref.shape[1]
    n_groups = RET_W // gw
    n_blk = qkvg_ref.shape[0] // CHUNK
    nt = (((1,), (1,)), ((), ()))
    tn = (((0,), (0,)), ((), ()))
    ones_bd = ones_ref[...]

    s_cur = [s_scr[g] for g in range(n_groups)]
    k_prev, v_prev = pk_scr[...], pv_scr[...]
    wave1, mid = [], []
    for c in range(n_blk):
        rows = slice(c * CHUNK, (c + 1) * CHUNK)
        q = qkvg_ref[rows, 0:RET_W]
        k = qkvg_ref[rows, RET_W:2 * RET_W]
        v = qkvg_ref[rows, 2 * RET_W:3 * RET_W]
        sq = qkvg_ref[rows, 4 * RET_W:5 * RET_W]
        kd = (k.astype(F32) * kd_ref[...]).astype(BF16)
        ret_ops = []
        for p in range(n_pairs):
            kp = k[:, p * LANES:(p + 1) * LANES]
            vp = v[:, p * LANES:(p + 1) * LANES]
            ret_ops.append((jnp.concatenate([jnp.where(low, kp, zero), jnp.where(low, zero, kp)], axis=0),
                            jnp.concatenate([jnp.where(low, vp, zero), jnp.where(low, zero, vp)], axis=0)))
        k_new = kv_ref[rows, 0:KV_W].astype(BF16)
        v_new = kv_ref[rows, KV_W:2 * KV_W].astype(BF16)
        swa_ops = _swa_operands(jnp.concatenate([k_prev, k_new], axis=0), jnp.concatenate([v_prev, v_new], axis=0))
        k_prev, v_prev = k_new, v_new

        scores = [lax.dot_general(q[:, p * LANES:(p + 1) * LANES], ret_ops[p][0], nt, preferred_element_type=F32)
                  for p in range(n_pairs)]
        cross = [jnp.dot(q[:, g * gw:(g + 1) * gw], s_cur[g].astype(BF16), preferred_element_type=F32)
                 for g in range(n_groups)]
        upd = [lax.dot_general(kd[:, g * gw:(g + 1) * gw], v[:, g * gw:(g + 1) * gw], tn,
                               preferred_element_type=F32) for g in range(n_groups)]
        logits = [lax.dot_general(
            jnp.concatenate([sq[:, (g * pairs_per_kv + pp) * LANES:(g * pairs_per_kv + pp + 1) * LANES]
                             for pp in range(pairs_per_kv)], axis=0),
            swa_ops[g][0], nt, preferred_element_type=F32) for g in range(KV_HEADS)]
        s_cur = [s_cur[g] * cd_ref[g] + jnp.where(cd_ref[g] > 0.0, upd[g], 0.0) for g in range(n_groups)]
        wave1.append((scores, cross, logits, ret_ops, swa_ops))
    for g in range(n_groups):
        s_scr[g] = s_cur[g]
    pk_scr[...] = k_prev
    pv_scr[...] = v_prev

    for c in range(n_blk):
        scores, cross, logits, ret_ops, swa_ops = wave1[c]
        bias_c = bias_ref.at[0] if c == 0 else bias_all_ref.at[0]
        scores = [(scores[p] * dm_ref[p]).astype(BF16) for p in range(n_pairs)]
        soft = [[_swa_softmax(logits[g][pp * CHUNK:(pp + 1) * CHUNK, :], g * pairs_per_kv + pp, bias_c, sink_ref)
                 for pp in range(pairs_per_kv)] for g in range(KV_HEADS)]
        intra = [jnp.dot(scores[p], ret_ops[p][1], preferred_element_type=F32) for p in range(n_pairs)]
        swa = []
        for g in range(KV_HEADS):
            oo = _swa_values(jnp.concatenate([x[0].astype(BF16) for x in soft[g]], axis=0),
                             jnp.concatenate([x[1] for x in soft[g]], axis=0), swa_ops[g][1])
            swa += [oo[pp * CHUNK:(pp + 1) * CHUNK, :] for pp in range(pairs_per_kv)]
        o = jnp.concatenate(intra, axis=1) + jnp.concatenate(cross, axis=1) * qd_ref[...]
        mid.append((o, swa))

    o = jnp.concatenate([m[0] for m in mid], axis=0)
    ret = _head_norm(o, ones_bd) * qkvg_ref[:, 3 * RET_W:4 * RET_W].astype(F32)
    mix_ref[:, 0:RET_W] = ret.astype(BF16)
    for c in range(n_blk):
        mix_ref[c * CHUNK:(c + 1) * CHUNK, RET_W:2 * RET_W] = jnp.concatenate(mid[c][1], axis=1).astype(BF16)

    @pl.when(i == pl.num_programs(1) - 1)
    def _():
        for h in range(N_HEADS):
            g, hl = divmod(h, gw // HEAD_DIM)
            s_out_ref[0, h] = s_scr[g, hl * HEAD_DIM:(hl + 1) * HEAD_DIM, hl * HEAD_DIM:(hl + 1) * HEAD_DIM]


def _prompt_tables():
    decay_mat, q_dec, k_dec, chunk_dec = _decay_tables(CHUNK)
    dm = jnp.concatenate([decay_mat[0::2], decay_mat[1::2]], axis=2)
    qd = jnp.repeat(q_dec, HEAD_DIM, axis=1)
    kd = jnp.repeat(k_dec, HEAD_DIM, axis=1)
    gw = 4 * HEAD_DIM
    blk = np.kron(np.eye(4, dtype=np.float32), np.ones((HEAD_DIM, HEAD_DIM), np.float32))
    cd = jnp.repeat(chunk_dec.reshape(2, 4), HEAD_DIM, axis=1)[:, :, None] * jnp.asarray(blk)[None]
    ones_bd = jnp.asarray(blk / HEAD_DIM, BF16)
    assert cd.shape == (2, gw, gw)
    return dm, qd, kd, cd, ones_bd


def _attn_prompt_call(qkvg, kv, bias, sinks, tables, b, l, n_blk):
    dm, qd, kd, cd, ones_bd = tables
    rows = n_blk * CHUNK
    nc = l // rows
    gw = cd.shape[1]
    bias_spec = lambda idx: pl.BlockSpec((1, N_HEADS, CHUNK, 2 * CHUNK), idx)
    return pl.pallas_call(
        _attn_prompt_kernel,
        grid=(b, nc),
        in_specs=[pl.BlockSpec(memory_space=pltpu.SMEM),
                  pl.BlockSpec((rows, QKVG_W), lambda bi, i: (bi * nc + i, 0)),
                  pl.BlockSpec((rows, 2 * KV_W), lambda bi, i: (bi * nc + i, 0)),
                  bias_spec(lambda bi, i: (jnp.where(i == 0, 1, 0), 0, 0, 0)),
                  bias_spec(lambda bi, i: (0, 0, 0, 0)),
                  _const_spec(dm.shape), _const_spec(qd.shape), _const_spec(kd.shape),
                  _const_spec(cd.shape), _const_spec(ones_bd.shape)],
        out_specs=[pl.BlockSpec((rows, 2 * RET_W), lambda bi, i: (bi * nc + i, 0)),
                   pl.BlockSpec((1, N_HEADS, HEAD_DIM, HEAD_DIM), lambda bi, i: (bi, 0, 0, 0))],
        out_shape=[jax.ShapeDtypeStruct((b * l, 2 * RET_W), BF16),
                   jax.ShapeDtypeStruct((b, N_HEADS, HEAD_DIM, HEAD_DIM), F32)],
        scratch_shapes=[pltpu.VMEM((RET_W // gw, gw, gw), F32),
                        pltpu.VMEM((CHUNK, KV_W), BF16),
                        pltpu.VMEM((CHUNK, KV_W), BF16)],
        compiler_params=_cparams(2),
        name="attn_prompt",
    )(sinks, qkvg, kv, bias, bias, dm, qd, kd, cd, ones_bd)


def _ret_sample_kernel(dec_ref, tq_ref, s_ref, buf_ref, ret_ref, s_out_ref, q_scr, k_scr, v_scr, cross_scr):
    del buf_ref
    h = pl.program_id(0)
    n_t = tq_ref.shape[1]
    cd = dec_ref[h, 2 * n_t]
    for t in range(n_t):
        q_scr[t] = tq_ref[0, t].astype(F32)
        k_scr[t] = tq_ref[1, t].astype(F32) * dec_ref[h, t]
        v_scr[t] = tq_ref[2, t].astype(F32)
    cross_scr[...] = jnp.zeros_like(cross_scr)

    def per_d(d, carry):
        s_d = s_ref[d]
        upd = s_d * cd
        for t in range(n_t):
            upd = upd + k_scr[t, pl.ds(d, 1), :] * v_scr[t]
            cross_scr[t] += q_scr[t, pl.ds(d, 1), :] * s_d
        s_out_ref[d] = upd
        return carry

    lax.fori_loop(0, HEAD_DIM, per_d, 0)

    for t in range(n_t):
        o = cross_scr[t] * dec_ref[h, n_t + t]
        for t2 in range(t + 1):
            sc = jnp.sum(q_scr[t] * (tq_ref[1, t2].astype(F32)), axis=0, keepdims=True)
            o = o + (sc * dec_ref[h, 2 * n_t + 1 + t * n_t + t2]) * v_scr[t2]
        mu = jnp.mean(o, axis=0, keepdims=True)
        d0 = o - mu
        var = jnp.mean(d0 * d0, axis=0, keepdims=True)
        ret_ref[t] = (d0 * lax.rsqrt(var + LN_EPS) * tq_ref[3, t].astype(F32)).astype(ret_ref.dtype)


def _ret_sample_call(tq, state5, s_buf, layer, dec):
    _, n_t, _, b = tq.shape
    state_blk = (None, None, HEAD_DIM, HEAD_DIM, b)
    return pl.pallas_call(
        _ret_sample_kernel,
        grid=(N_HEADS,),
        in_specs=[pl.BlockSpec(memory_space=pltpu.SMEM),
                  pl.BlockSpec((4, n_t, HEAD_DIM, b), lambda h: (0, 0, h, 0)),
                  pl.BlockSpec(state_blk, lambda h: (layer, h, 0, 0, 0)),
                  pl.BlockSpec(memory_space=pl.ANY)],
        out_specs=[pl.BlockSpec((n_t, HEAD_DIM, b), lambda h: (0, h, 0)),
                   pl.BlockSpec(state_blk, lambda h: (layer, h, 0, 0, 0))],
        out_shape=[jax.ShapeDtypeStruct((n_t, RET_W, b), BF16),
                   jax.ShapeDtypeStruct(s_buf.shape, F32)],
        scratch_shapes=[pltpu.VMEM((n_t, HEAD_DIM, b), F32)] * 4,
        input_output_aliases={3: 1},
        compiler_params=_cparams(1),
        name="ret_sample",
    )(dec, tq, state5, s_buf)


def _ret_sample_table(tl):
    decay_mat, q_dec, k_dec, chunk_dec = _decay_tables(tl)
    return jnp.concatenate([k_dec.T, q_dec.T, chunk_dec[:, None], decay_mat.reshape(N_HEADS, tl * tl)], axis=1)


def _swa_sample_kernel(sink_ref, qkvg_ref, kv_ref, ckt_ref, cvt_ref, bias_ref, kbuf_ref, vbuf_ref,
                       swa_ref, ck_out_ref, cv_out_ref):
    del kbuf_ref, vbuf_ref
    tb = ckt_ref.shape[0]
    tl = qkvg_ref.shape[0] // tb
    nt = (((1,), (1,)), ((), ()))
    pairs_per_kv = N_HEADS // KV_HEADS // 2
    low = _lane_mask_low()
    zero_t =jnp.zeros((HEAD_DIM, WINDOW), BF16)
    front = jnp.zeros((WINDOW - tl, KV_W), F32)
    is_new =lax.broadcasted_iota(jnp.int32, (1, WINDOW), 1) >= WINDOW - tl
    bias = bias_ref.at[0]

    def block_diag(t):
        return jnp.concatenate([jnp.concatenate([t, zero_t], axis=1), jnp.concatenate([zero_t, t], axis=1)], axis=0)

    wave1 = []
    for e in range(tb):
        rows = slice(e * tl, (e + 1) * tl)
        sq32 = qkvg_ref[rows, 4 * RET_W:5 * RET_W].astype(F32)
        k_new = kv_ref[rows, 0:KV_W]
        v_new = kv_ref[rows, KV_W:2 * KV_W]
        k_rot = pltpu.roll(k_new, HEAD_DIM, 1)
        v_rot = pltpu.roll(v_new, HEAD_DIM, 1)
        k_placed = jnp.concatenate([front, k_new], axis=0).T
        v_placed = jnp.concatenate([front, v_new], axis=0).T
        per_g = []
        for g in range(KV_HEADS):
            kt = ckt_ref[e, g]
            vt = cvt_ref[e, g]
            hd = slice(g * HEAD_DIM, (g + 1) * HEAD_DIM)
            ck_out_ref[e, g] = jnp.where(is_new, k_placed[hd, :], pltpu.roll(kt, WINDOW - tl, 1))
            cv_out_ref[e, g] = jnp.where(is_new, v_placed[hd, :], pltpu.roll(vt, WINDOW - tl, 1))
            lhs = jnp.concatenate([sq32[:, (g * pairs_per_kv + pp) * LANES:(g * pairs_per_kv + pp + 1) * LANES]
                                   for pp in range(pairs_per_kv)], axis=0).astype(BF16)
            kn_lo, kn_hi = (k_new, k_rot) if g == 0 else (k_rot, k_new)
            vn_lo, vn_hi = (v_new, v_rot) if g == 0 else (v_rot, v_new)
            kn = [jnp.where(low, kn_lo, 0.0).astype(BF16), jnp.where(low, 0.0, kn_hi).astype(BF16)]
            vn = [jnp.where(low, vn_lo, 0.0).astype(BF16), jnp.where(low, 0.0, vn_hi).astype(BF16)]
            lc = jnp.dot(lhs, block_diag(kt.astype(BF16)), preferred_element_type=F32)
            ln = [lax.dot_general(lhs, kn[s], nt, preferred_element_type=F32) for s in range(2)]
            per_g.append((lc, ln, vn, block_diag(vt.astype(BF16))))
        wave1.append(per_g)

    swa_rows = []
    for e in range(tb):
        outs = []
        for g in range(KV_HEADS):
            lc, ln, vn, vbd = wave1[e][g]
            pcs, pns, dens = [], [[], []], []
            for pp in range(pairs_per_kv):
                r = slice(pp * tl, (pp + 1) * tl)
                den_pair = []
                for s in range(2):
                    hh = 2 * (g * pairs_per_kv + pp) + s
                    lo_c = lc[r, s * WINDOW:(s + 1) * WINDOW] + bias[hh, 0:tl, 0:WINDOW]
                    lo_n = ln[s][r, :] + bias[hh, 0:tl, WINDOW:WINDOW + tl]
                    sink = sink_ref[hh]
                    m = jnp.maximum(jnp.maximum(jnp.max(lo_c, axis=-1, keepdims=True),
                                                jnp.max(lo_n, axis=-1, keepdims=True)), sink)
                    pc = jnp.exp(lo_c - m)
                    pn = jnp.exp(lo_n - m)
                    den_pair.append(jnp.sum(pc, axis=-1, keepdims=True) + jnp.sum(pn, axis=-1, keepdims=True)
                                    + jnp.exp(sink - m))
                    pcs.append((pp, pc))
                    pns[s].append(pn)
                dens.append(jnp.where(low, den_pair[0], den_pair[1]))
            pc_rows = [jnp.concatenate([x[1] for x in pcs if x[0] == pp], axis=1) for pp in range(pairs_per_kv)]
            pc_all = jnp.concatenate(pc_rows, axis=0).astype(BF16)
            oo = lax.dot_general(pc_all, vbd, nt, preferred_element_type=F32)
            for s in range(2):
                oo = oo + jnp.dot(jnp.concatenate(pns[s], axis=0).astype(BF16), vn[s], preferred_element_type=F32)
            oo = oo / jnp.concatenate(dens, axis=0)
            outs += [oo[pp * tl:(pp + 1) * tl, :] for pp in range(pairs_per_kv)]
        swa_rows.append(jnp.concatenate(outs, axis=1))
    swa_ref[...] = jnp.concatenate(swa_rows, axis=0).astype(BF16)


def _swa_sample_call(qkvg, kv, ckt, cvt, kbuf, vbuf, layer, bias, sinks, b, tl, tb):
    cache_blk = (None, tb, KV_HEADS, HEAD_DIM, WINDOW)
    layer5 = lambda i: (layer, i, 0, 0, 0)
    return pl.pallas_call(
        _swa_sample_kernel,
        grid=(b // tb,),
        in_specs=[pl.BlockSpec(memory_space=pltpu.SMEM),
                  pl.BlockSpec((tb * tl, QKVG_W), lambda i: (i, 0)),
                  pl.BlockSpec((tb * tl, 2 * KV_W), lambda i: (i, 0)),
                  pl.BlockSpec(cache_blk, layer5),
                  pl.BlockSpec(cache_blk, layer5),
                  pl.BlockSpec((1, N_HEADS, CHUNK, 2 * CHUNK), lambda i: (0, 0, 0, 0)),
                  pl.BlockSpec(memory_space=pl.ANY),
                  pl.BlockSpec(memory_space=pl.ANY)],
        out_specs=[pl.BlockSpec((tb * tl, RET_W), lambda i: (i, 0)),
                   pl.BlockSpec(cache_blk, layer5),
                   pl.BlockSpec(cache_blk, layer5)],
        out_shape=[jax.ShapeDtypeStruct((b * tl, RET_W), BF16),
                   jax.ShapeDtypeStruct(kbuf.shape, F32),
                   jax.ShapeDtypeStruct(vbuf.shape, F32)],
        input_output_aliases={6: 1, 7: 2},
        compiler_params=_cparams(1),
        name="swa_sample",
    )(sinks, qkvg, kv, ckt, cvt, bias, kbuf, vbuf)


def _layer_norm(x, g, b):
    mu = jnp.mean(x, axis=-1, keepdims=True)
    xc = x - mu
    var = jnp.mean(xc * xc, axis=-1, keepdims=True)
    return xc * lax.rsqrt(var + LN_EPS) * g + b


def _ffn_kernel(alpha, carry_rows, n_cast, n_alias, mix_ref, x_ref, g1_ref, sh2_ref, sc2_ref, g2_ref, wout_ref,
                ln1g_ref, ln1b_ref, wup_ref, cw_ref, cb_ref, prev_ref, wdn_ref, ln2g_ref, ln2b_ref, *rest):
    cast_in, rest = rest[:n_cast], rest[n_cast + n_alias:]
    (xo_ref, tail_ref), carry_scr = rest[0:2], rest[-1]
    zero_refs = rest[2:len(rest) - 1 - n_cast]
    for src_ref, dst_ref in zip(cast_in, rest[len(rest) - 1 - n_cast:-1]):
        dst_ref[...] = src_ref[...].astype(dst_ref.dtype)
    for z_ref in zero_refs:
        z_ref[...] = jnp.zeros(z_ref.shape, z_ref.dtype)
    tb, tl, _ = x_ref.shape
    tm = tb * tl
    sm = FF_ROWS
    n_sub = tm // sm
    sb, sl = (1, sm) if carry_rows else (tb // n_sub, tl)
    j = pl.program_id(1)

    def seq(ref, s):
        if carry_rows:
            return ref[:, s * sm:(s + 1) * sm, :] if ref.shape[1] == tl else ref[...]
        return ref[s * sb:(s + 1) * sb]

    if carry_rows:
        @pl.when(j == 0)
        def _():
            carry_scr[carry_rows - 2:carry_rows, :] = prev_ref[0]

    fix_rows = 8 if carry_rows else sm
    t_idx = lax.broadcasted_iota(jnp.int32, (fix_rows, 1), 0) % sl
    is_t0 = t_idx == 0
    is_t1 = t_idx == 1
    n_chunks = D_FF // FF_CHUNK

    def up_cols(c, half):
        return slice(half * D_FF + c * FF_CHUNK, half * D_FF + (c + 1) * FF_CHUNK)

    def pre(s):
        x1 = alpha * seq(x_ref, s) + seq(g1_ref, s) * y[s].reshape(sb, sl, D_MODEL)
        x1 = _layer_norm(x1, ln1g_ref[...], ln1b_ref[...])
        h2 = (x1 * (1.0 + seq(sc2_ref, s)) + seq(sh2_ref, s)).reshape(sm, D_MODEL).astype(BF16)
        return x1, h2

    def up_dots(h2, c):
        return [jnp.dot(h2, wup_ref[:, up_cols(c, half)], preferred_element_type=F32) for half in range(2)]

    def conv(s, up, cols):
        if carry_rows:
            src = carry_scr if s == 0 else tail_ref
            p0 = src[carry_rows - 2:carry_rows - 1, cols]
            p1 = src[carry_rows - 1:carry_rows, cols]
            tail_ref[:, cols] = up[sm - carry_rows:sm, :]
        else:
            up3 = up.reshape(sb, sl, FF_CHUNK)
            tail_ref[s * sb:(s + 1) * sb, :, cols] = up3[:, sl - 2:sl, :]
            prev = seq(prev_ref, s)
            p0, p1 = prev[:, 0:1, cols], prev[:, 1:2, cols]
            t3 = lax.broadcasted_iota(jnp.int32, (1, sl, 1), 1)
            s1 = jnp.where(t3 == 0, p1, pltpu.roll(up3, 1, 1))
            s2 = jnp.where(t3 == 0, p0, jnp.where(t3 == 1, p1, pltpu.roll(up3, 2, 1)))
            out = cb_ref[:, cols] + s2 * cw_ref[0:1, cols] + s1 * cw_ref[1:2, cols] + up3 * cw_ref[2:3, cols]
            return out.reshape(sm, FF_CHUNK)
        r1 = pltpu.roll(up, 1, 0)
        r2 = pltpu.roll(up, 2, 0)
        s1 = jnp.where(is_t0, p1, r1[0:fix_rows])
        s2 = jnp.where(is_t0, p0, jnp.where(is_t1, p1, r2[0:fix_rows]))
        if fix_rows < sm:
            s1 = jnp.concatenate([s1, r1[fix_rows:]], axis=0)
            s2 = jnp.concatenate([s2, r2[fix_rows:]], axis=0)
        return cb_ref[:, cols] + s2 * cw_ref[0:1, cols] + s1 * cw_ref[1:2, cols] + up * cw_ref[2:3, cols]

    y = [jnp.dot(mix_ref[s * sm:(s + 1) * sm, :], wout_ref[...], preferred_element_type=F32) for s in range(n_sub)]
    items = [(s, c) for s in range(n_sub) for c in range(n_chunks)]
    staged, ups = {}, {}

    def issue_up(k):
        s, c = items[k]
        if c == 0:
            staged[s] = pre(s)
        ups[k] = up_dots(staged[s][1], c)

    for k in range(min(FF_LOOKAHEAD, len(items))):
        issue_up(k)
    acc = None
    for k, (s, c) in enumerate(items):
        if k + FF_LOOKAHEAD < len(items):
            issue_up(k + FF_LOOKAHEAD)
        up_a, up_b = ups.pop(k)
        ua = conv(s, up_a, up_cols(c, 0))
        ub = conv(s, up_b, up_cols(c, 1))
        gated = (ua * jax.nn.sigmoid(ua) * ub).astype(BF16)
        down = jnp.dot(gated, wdn_ref[c * FF_CHUNK:(c + 1) * FF_CHUNK, :], preferred_element_type=F32)
        acc = down if c == 0 else acc + down
        if c == n_chunks - 1:
            x1, _ = staged.pop(s)
            x2 = alpha * x1 + seq(g2_ref, s) * acc.reshape(sb, sl, D_MODEL)
            out = _layer_norm(x2, ln2g_ref[...], ln2b_ref[...])
            if carry_rows:
                xo_ref[:, s * sm:(s + 1) * sm, :] = out
            else:
                xo_ref[s * sb:(s + 1) * sb] = out

    if carry_rows:
        carry_scr[...] = tail_ref[...]


def _ffn_call(mix, x, mod, conv_prev, w_out, ln1_g, ln1_b, w_up, conv_w, conv_b, w_down, ln2_g, ln2_b,
              layer, alpha, tb, tl, zero_shapes=(), cast=(), tail_buf=None):
    prev_layer = layer if conv_prev.shape[0] > 1 else 0
    w_layer = layer if w_up.shape[0] > 1 else 0
    b, l, _ = x.shape
    nj = l // tl
    tm = tb * tl
    n_steps = (b // tb) * nj
    zero_specs = []
    for shape in zero_shapes:
        per0 = n_steps // shape[0]
        blk = (1, shape[1] // per0) + tuple(shape[2:])
        assert per0 * shape[0] == n_steps and blk[1] * per0 == shape[1]
        zero_specs.append(pl.BlockSpec(blk, lambda i, j, per0=per0, nd=len(shape):
                                       ((i * nj + j) // per0, (i * nj + j) % per0) + (0,) * (nd - 2)))
    cast_in, cast_out, cast_shape = [], [], []
    for w, w_src_layer in cast:
        _, rows, cols = w.shape
        blk = (None, rows // n_steps, cols)
        assert blk[1] * n_steps == rows and blk[1] % BF16_ROWS == 0
        cast_in.append(pl.BlockSpec(blk, lambda i, j, src=w_src_layer: (src, i * nj + j, 0)))
        cast_out.append(pl.BlockSpec(blk, lambda i, j: (0, i * nj + j, 0)))
        cast_shape.append(jax.ShapeDtypeStruct((1, rows, cols), BF16))
    carry_rows = 8 if tb == 1 else 0
    assert carry_rows or (nj == 1 and tl == 8)
    if carry_rows:
        tail_spec = pl.BlockSpec((8, UP_W), lambda i, j: (i, 0))
        tail_shape = jax.ShapeDtypeStruct((b * 8, UP_W), F32)
    else:
        tail_spec = pl.BlockSpec((None, tb, 2, UP_W), lambda i, j: (layer, i, 0, 0))
        tail_shape = jax.ShapeDtypeStruct(tail_buf.shape, F32)
    alias_in = [] if tail_buf is None else [tail_buf]
    operands = (mix, x, mod, mod, mod, mod, w_out, ln1_g, ln1_b, w_up, conv_w, conv_b, conv_prev, w_down, ln2_g, ln2_b,
                *[w for w, _ in cast])
    kern = functools.partial(_ffn_kernel, alpha, carry_rows, len(cast), len(alias_in))
    return pl.pallas_call(
        kern,
        grid=(b // tb, nj),
        in_specs=[pl.BlockSpec((tm, 2 * RET_W), lambda i, j: (i * nj + j, 0)),
                  pl.BlockSpec((tb, tl, D_MODEL), lambda i, j: (i, j, 0)),
                  _mod_spec(tb, layer, 2), _mod_spec(tb, layer, 3), _mod_spec(tb, layer, 4), _mod_spec(tb, layer, 5),
                  _layer_spec((2 * RET_W, D_MODEL), w_layer),
                  _layer_spec((1, D_MODEL), layer), _layer_spec((1, D_MODEL), layer),
                  _layer_spec((D_MODEL, UP_W), w_layer),
                  _layer_spec((3, UP_W), layer), _layer_spec((1, UP_W), layer),
                  pl.BlockSpec((None, tb, 2, UP_W), lambda i, j: (prev_layer, i, 0, 0)),
                  _layer_spec((D_FF, D_MODEL), w_layer),
                  _layer_spec((1, D_MODEL), layer), _layer_spec((1, D_MODEL), layer)] + cast_in
        + [pl.BlockSpec(memory_space=pl.ANY)] * len(alias_in),
        out_specs=[pl.BlockSpec((tb, tl, D_MODEL), lambda i, j: (i, j, 0)),
                   tail_spec] + zero_specs + cast_out,
        out_shape=[jax.ShapeDtypeStruct((b, l, D_MODEL), F32), tail_shape]
        + [jax.ShapeDtypeStruct(shape, F32) for shape in zero_shapes] + cast_shape,
        scratch_shapes=[pltpu.VMEM((8, UP_W), F32)],
        input_output_aliases={len(operands): 1} if alias_in else {},
        compiler_params=_cparams(2),
        name="ffn",
    )(*operands, *alias_in)


def kernel(x_prompt, x_sample, c_prompt, c_sample, state_ret, cache_swa_k, cache_swa_v, state_conv, rel_bias, w_ada, b_ada, w_in, swa_sinks, w_out, ln1_g, ln1_b, w_up, conv_w, conv_b, w_down, ln2_g, ln2_b):
    depth = w_ada.shape[0]
    bp, lp, _ = x_prompt.shape
    bs, ls, _ = x_sample.shape
    alpha = (2.0 * depth) ** 0.25
    tl_proj = 1024
    tl_p = 2 * FF_ROWS
    tb_s = FF_ROWS // ls
    tb_attn = 8
    blk_attn = 8

    c_all = jnp.concatenate([c_prompt, c_sample], axis=0)
    mod_p, mod_s = _ada_call(c_all, bp, w_ada, b_ada)
    bias = _bias_call(rel_bias)

    cos_p, sin_p = _rope_tables(jnp.arange(lp, dtype=jnp.int32))
    cos_s, sin_s = _rope_tables(PAST_LEN + jnp.arange(ls, dtype=jnp.int32))
    cos_s, sin_s = jnp.tile(cos_s, (tb_s, 1)), jnp.tile(sin_s, (tb_s, 1))
    tab_p = _prompt_tables()
    dec_s = _ret_sample_table(ls)
    conv0 = jnp.zeros((1, bp, 2, UP_W), F32)

    w_in_l = w_in[0:1].astype(BF16)
    vec = lambda a: a.reshape(depth, 1, a.shape[-1])
    ffn_vec = (vec(ln1_g), vec(ln1_b), conv_w, vec(conv_b), vec(ln2_g), vec(ln2_b))
    state5 = jnp.transpose(state_ret, (0, 2, 3, 4, 1))
    ckt = jnp.transpose(cache_swa_k, (0, 1, 3, 4, 2))
    cvt = jnp.transpose(cache_swa_v, (0, 1, 3, 4, 2))

    xp, xs = x_prompt, x_sample
    p_ret, p_k, p_v, p_conv = [], [], [], []
    for l in range(depth):
        qkvg, kv, wo_b, wu_b, wd_b = _proj_call(xp, mod_p, w_in_l, l, cos_p, sin_p, 1, tl_proj, (w_out, w_up, w_down))
        ffn_w = (wo_b, ffn_vec[0], ffn_vec[1], wu_b, ffn_vec[2], ffn_vec[3], wd_b, ffn_vec[4], ffn_vec[5])
        mix, r_p =_attn_prompt_call(qkvg, kv, bias, swa_sinks[l], tab_p, bp, lp, blk_attn)
        zero_shapes = (state5.shape, ckt.shape, cvt.shape, state_conv.shape) if l == 0 else ()
        cast_next = ((w_in, l + 1),) if l + 1 < depth else ()
        xp, tail, *extra = _ffn_call(mix, xp, mod_p, conv0, *ffn_w, l, alpha, 1, tl_p, zero_shapes, cast_next)
        if l == 0:
            s_buf, k_buf, v_buf, conv_buf = extra[0:4]
        w_in_next = extra[-1] if cast_next else None
        kv3 =kv.reshape(bp, lp, 2, KV_HEADS, HEAD_DIM)
        p_ret.append(r_p)
        p_k.append(kv3[:, lp - WINDOW:, 0])
        p_v.append(kv3[:, lp - WINDOW:, 1])
        p_conv.append(tail.reshape(bp, 8, UP_W)[:, 6:8])

        qkvg, kv = _proj_call(xs, mod_s, w_in_l, l, cos_s, sin_s, tb_s, ls)
        tq = jnp.transpose(qkvg.reshape(bs, ls, 5, RET_W)[:, :, 0:4], (2, 1, 3, 0))
        ret_t, s_buf = _ret_sample_call(tq, state5, s_buf, l, dec_s)
        ret = jnp.transpose(ret_t, (2, 0, 1)).reshape(bs * ls, RET_W)
        swa, k_buf, v_buf = _swa_sample_call(qkvg, kv, ckt, cvt, k_buf, v_buf, l, bias, swa_sinks[l], bs, ls, tb_attn)
        mix = jnp.concatenate([ret, swa], axis=1)
        xs, conv_buf = _ffn_call(mix, xs, mod_s, state_conv, *ffn_w, l, alpha, tb_s, ls, tail_buf=conv_buf)
        w_in_l = w_in_next

    return (xp, xs,
            jnp.stack(p_ret), jnp.stack(p_k), jnp.stack(p_v), jnp.stack(p_conv),
            jnp.transpose(s_buf, (0, 4, 1, 2, 3)), jnp.transpose(k_buf, (0, 1, 4, 2, 3)),
            jnp.transpose(v_buf, (0, 1, 4, 2, 3)), conv_buf)
```

```python
import functools
import math

import jax
import jax.numpy as jnp
import numpy as np
from jax import lax
from jax.experimental import pallas as pl
from jax.experimental.pallas import tpu as pltpu

F32 = jnp.float32
BF16 = jnp.bfloat16

D_MODEL = 1024
HEAD_DIM = 64
N_HEADS = 8
RET_W = N_HEADS * HEAD_DIM
KV_HEADS = 2
KV_W = KV_HEADS * HEAD_DIM
D_FF = 2816
UP_W = 2 * D_FF
IN_COLS = 4 * RET_W + RET_W + 2 * KV_W
QKVG_W = 5 * RET_W
WINDOW = 128
CHUNK = 128
NUM_BUCKETS = 32
PAST_LEN = 8192
ROPE_BASE = 10000.0
LN_EPS = 1e-5
NEG_INF = -1e30
LANES = 128
BF16_ROWS = 16
FF_CHUNK = 256
FF_LOOKAHEAD = 2
FF_ROWS = 256
VMEM_LIMIT = 56 * 1024 * 1024


def _cparams(n_axes):
    return pltpu.CompilerParams(dimension_semantics=("arbitrary",) * n_axes,
                                vmem_limit_bytes=VMEM_LIMIT)


def _const_spec(shape):
    nd = len(shape)
    return pl.BlockSpec(shape, lambda *_: (0,) * nd, pipeline_mode=pl.Buffered(1))


def _rope_tables(pos):
    half = HEAD_DIM // 2
    inv = 1.0 / (ROPE_BASE ** (jnp.arange(half, dtype=F32) / half))
    ang = pos.astype(F32)[:, None] * inv[None, :]
    cos = jnp.cos(ang)
    sin = jnp.sin(ang)
    cos_h = jnp.concatenate([cos, cos], axis=-1)
    sin_h = jnp.concatenate([-sin, sin], axis=-1)
    return jnp.tile(cos_h, (1, N_HEADS)), jnp.tile(sin_h, (1, N_HEADS))


def _decay_tables(chunk):
    log_g = jnp.log(1.0 - 2.0 ** (-5.0 - jnp.arange(N_HEADS, dtype=F32)))
    idx = jnp.arange(chunk)
    diff = idx[:, None] - idx[None, :]
    decay_mat = jnp.where(diff[None] >= 0,
                          jnp.exp(log_g[:, None, None] * jnp.maximum(diff, 0)[None].astype(F32)), 0.0)
    q_dec = jnp.exp(log_g[None, :] * (idx[:, None] + 1).astype(F32))
    k_dec = jnp.exp(log_g[None, :] * (chunk - 1 - idx[:, None]).astype(F32))
    chunk_dec = jnp.exp(log_g * chunk)
    return decay_mat, q_dec, k_dec, chunk_dec


def _bucket_table(n_q, n_k):
    i = np.arange(n_q)[:, None]
    j = np.arange(n_k)[None, :]
    dist = i - j + WINDOW
    n = np.maximum(dist, 0)
    max_exact = NUM_BUCKETS // 2
    nf = np.maximum(n, max_exact).astype(np.float64)
    large = max_exact + (np.log(nf / max_exact) / math.log(WINDOW / max_exact)
                         * (NUM_BUCKETS - max_exact)).astype(np.int32)
    large = np.minimum(large, NUM_BUCKETS - 1)
    bucket = np.where(n < max_exact, n, large).astype(np.int32)
    valid = ((dist >= 0) & (dist < WINDOW)).astype(np.int32)
    return bucket, valid


def _ada_kernel(c_ref, w_ref, b_ref, op_ref, os_ref):
    c = c_ref[...]
    s = (c * jax.nn.sigmoid(c)).astype(BF16)
    mod = jnp.dot(s, w_ref[0].astype(BF16), preferred_element_type=F32) + b_ref[0]
    bp = op_ref.shape[0]
    op_ref[:, 0, :] = mod[0:bp]
    os_ref[:, 0, :] = mod[bp:]


def _ada_call(c_all, bp, w_ada, b_ada):
    depth = w_ada.shape[0]
    n_rows = c_all.shape[0]
    bs = n_rows - bp
    tn = 1536
    out_blk = lambda b: pl.BlockSpec((None, b, 1, tn), lambda l, n: (l, 0, 0, n))
    return pl.pallas_call(
        _ada_kernel,
        grid=(depth, 6 * D_MODEL // tn),
        in_specs=[pl.BlockSpec((n_rows, D_MODEL), lambda l, n: (0, 0)),
                  pl.BlockSpec((1, D_MODEL, tn), lambda l, n: (l, 0, n)),
                  pl.BlockSpec((1, 1, tn), lambda l, n: (l, 0, n))],
        out_specs=[out_blk(bp), out_blk(bs)],
        out_shape=[jax.ShapeDtypeStruct((depth, bp, 1, 6 * D_MODEL), F32),
                   jax.ShapeDtypeStruct((depth, bs, 1, 6 * D_MODEL), F32)],
        compiler_params=_cparams(2),
        name="ada",
    )(c_all, w_ada, b_ada.reshape(depth, 1, 6 * D_MODEL))


def _bias_kernel(rb_ref, bucket_ref, valid_ref, o_ref):
    bucket = bucket_ref[...]
    valid = valid_ref[...] > 0
    first_ok = lax.broadcasted_iota(jnp.int32, bucket.shape, 1) >= WINDOW
    for h in range(N_HEADS):
        acc = jnp.zeros(bucket.shape, F32)
        for b in range(NUM_BUCKETS):
            acc = jnp.where(bucket == b, rb_ref[b, h], acc)
        o_ref[0, h] = jnp.where(valid, acc, NEG_INF)
        o_ref[1, h] = jnp.where(valid & first_ok, acc, NEG_INF)


def _bias_call(rel_bias):
    bucket, valid = _bucket_table(CHUNK, 2 * CHUNK)
    return pl.pallas_call(
        _bias_kernel,
        in_specs=[pl.BlockSpec(memory_space=pltpu.SMEM),
                  pl.BlockSpec((CHUNK, 2 * CHUNK), lambda: (0, 0)),
                  pl.BlockSpec((CHUNK, 2 * CHUNK), lambda: (0, 0))],
        out_specs=pl.BlockSpec((2, N_HEADS, CHUNK, 2 * CHUNK), lambda: (0, 0, 0, 0)),
        out_shape=jax.ShapeDtypeStruct((2, N_HEADS, CHUNK, 2 * CHUNK), F32),
        name="swa_bias",
    )(rel_bias, jnp.asarray(bucket), jnp.asarray(valid))


def _swap_halves(x):
    lane = lax.broadcasted_iota(jnp.int32, (1, LANES), 1)
    first = (lane % HEAD_DIM) < (HEAD_DIM // 2)
    cols = []
    for c in range(x.shape[1] // LANES):
        xc = x[:, c * LANES:(c + 1) * LANES]
        cols.append(jnp.where(first, pltpu.roll(xc, LANES - HEAD_DIM // 2, 1), pltpu.roll(xc, HEAD_DIM // 2, 1)))
    return jnp.concatenate(cols, axis=1)


def _proj_kernel(x_ref, sc_ref, sh_ref, w_ref, cos_ref, sin_ref, *refs):
    n_cast = (len(refs) - 2) // 2
    qkvg_ref, kv_ref = refs[n_cast], refs[n_cast + 1]
    for src_ref, dst_ref in zip(refs[:n_cast], refs[n_cast + 2:]):
        dst_ref[...] = src_ref[...].astype(dst_ref.dtype)
    tb, tl, _ = x_ref.shape
    tm = tb * tl
    h = x_ref[...] * (1.0 + sc_ref[...]) + sh_ref[...]
    h = h.reshape(tm, D_MODEL).astype(BF16)
    proj = jnp.dot(h, w_ref[...], preferred_element_type=F32)
    cos = cos_ref[...]
    sin = sin_ref[...]
    rq = proj[:, 0:RET_W]
    rk = proj[:, RET_W:2 * RET_W]
    rq = rq * cos + _swap_halves(rq) * sin
    rk = (rk * cos + _swap_halves(rk) * sin) * (HEAD_DIM ** -0.5)
    rg = proj[:, 3 * RET_W:4 * RET_W]
    qkvg_ref[:, 0:RET_W] = rq.astype(BF16)
    qkvg_ref[:, RET_W:2 * RET_W] = rk.astype(BF16)
    qkvg_ref[:, 2 * RET_W:3 * RET_W] = proj[:, 2 * RET_W:3 * RET_W].astype(BF16)
    qkvg_ref[:, 3 * RET_W:4 * RET_W] = (rg * jax.nn.sigmoid(rg)).astype(BF16)
    qkvg_ref[:, 4 * RET_W:5 * RET_W] = (proj[:, 4 * RET_W:5 * RET_W] * (HEAD_DIM ** -0.5)).astype(BF16)
    kv_ref[...] = proj[:, 5 * RET_W:IN_COLS]


def _layer_spec(shape, layer):
    nd = len(shape)
    return pl.BlockSpec((None,) + tuple(shape), lambda *_: (layer,) + (0,) * nd, pipeline_mode=pl.Buffered(1))


def _mod_spec(tb, layer, k):
    return pl.BlockSpec((None, tb, 1, D_MODEL), lambda i, j: (layer, i, 0, k))


def _proj_call(x, mod, w_in, layer, cos, sin, tb, tl, cast=()):
    b, l, _ = x.shape
    nj = l // tl
    tm = tb * tl
    n_steps = (b // tb) * nj
    tab_idx = (lambda i, j: (j, 0)) if cos.shape[0] == l and nj > 1 else (lambda i, j: (0, 0))
    cast_in, cast_out, cast_shape = [], [], []
    for w in cast:
        _, rows, cols = w.shape
        blk = (None, rows // n_steps, cols)
        assert blk[1] * n_steps == rows and blk[1] % BF16_ROWS == 0
        cast_in.append(pl.BlockSpec(blk, lambda i, j: (layer, i * nj + j, 0)))
        cast_out.append(pl.BlockSpec(blk, lambda i, j: (0, i * nj + j, 0)))
        cast_shape.append(jax.ShapeDtypeStruct((1, rows, cols), BF16))
    return pl.pallas_call(
        _proj_kernel,
        grid=(b // tb, nj),
        in_specs=[pl.BlockSpec((tb, tl, D_MODEL), lambda i, j: (i, j, 0)),
                  _mod_spec(tb, layer, 1),
                  _mod_spec(tb, layer, 0),
                  _layer_spec((D_MODEL, IN_COLS), layer if w_in.shape[0] > 1 else 0),
                  pl.BlockSpec((tm, RET_W), tab_idx),
                  pl.BlockSpec((tm, RET_W), tab_idx)] + cast_in,
        out_specs=[pl.BlockSpec((tm, QKVG_W), lambda i, j: (i * nj + j, 0)),
                   pl.BlockSpec((tm, 2 * KV_W), lambda i, j: (i * nj + j, 0))] + cast_out,
        out_shape=[jax.ShapeDtypeStruct((b * l, QKVG_W), BF16),
                   jax.ShapeDtypeStruct((b * l, 2 * KV_W), F32)] + cast_shape,
        compiler_params=_cparams(2),
        name="proj",
    )(x, mod, mod, w_in, cos, sin, *cast)


def _lane_mask_low():
    return lax.broadcasted_iota(jnp.int32, (1, LANES), 1) < HEAD_DIM


def _head_norm(o, ones_bd):
    outs = []
    gw = ones_bd.shape[0]
    for g in range(RET_W // gw):
        og = o[:, g * gw:(g + 1) * gw]
        mu = jnp.dot(og.astype(BF16), ones_bd, preferred_element_type=F32)
        d = og - mu
        var = jnp.dot((d * d).astype(BF16), ones_bd, preferred_element_type=F32)
        outs.append(d * lax.rsqrt(var + LN_EPS))
    return jnp.concatenate(outs, axis=1)


def _swa_operands(k_all, v_all):
    low = _lane_mask_low()
    zero = jnp.zeros_like(k_all)
    ones_low = jnp.broadcast_to(jnp.where(low, 1.0, 0.0), k_all.shape)
    ones_cols = jnp.concatenate([ones_low, 1.0 - ones_low], axis=0).astype(BF16)
    k_rot = pltpu.roll(k_all, HEAD_DIM, 1)
    v_rot = pltpu.roll(v_all, HEAD_DIM, 1)
    ops = []
    for g in range(KV_HEADS):
        k_lo, k_hi = (k_all, k_rot) if g == 0 else (k_rot, k_all)
        v_lo, v_hi = (v_all, v_rot) if g == 0 else (v_rot, v_all)
        kc = jnp.concatenate([jnp.where(low, k_lo, zero), jnp.where(low, zero, k_hi)], axis=0)
        vc = jnp.concatenate([jnp.where(low, v_lo, zero), jnp.where(low, zero, v_hi)], axis=0)
        ops.append((kc, jnp.concatenate([vc, ones_cols], axis=1)))
    return ops


def _swa_softmax(lg, p, bias_ref, sink_ref):
    rows = lg.shape[0]
    low = _lane_mask_low()
    probs, sinks = [], []
    for s in range(2):
        hh = 2 * p + s
        lo = lg[:, s * 2 * CHUNK:(s + 1) * 2 * CHUNK] + bias_ref[hh, 0:rows, :]
        sink = sink_ref[hh]
        m = jnp.maximum(jnp.max(lo, axis=-1, keepdims=True), sink)
        probs.append(jnp.exp(lo - m))
        sinks.append(jnp.exp(sink - m))
    return jnp.concatenate(probs, axis=1), jnp.where(low, sinks[0], sinks[1])


def _swa_values(probs, sink_term, vc):
    oo = jnp.dot(probs, vc, preferred_element_type=F32)
    return oo[:, 0:LANES] / (oo[:, LANES:2 * LANES] + sink_term)


def _attn_prompt_kernel(sink_ref, qkvg_ref, kv_ref, bias_ref, bias_all_ref, dm_ref, qd_ref, kd_ref, cd_ref, ones_ref,
                        mix_ref, s_out_ref, s_scr, pk_scr, pv_scr):
    i = pl.program_id(1)

    @pl.when(i == 0)
    def _():
        s_scr[...] = jnp.zeros_like(s_scr)
        pk_scr[...] = jnp.zeros_like(pk_scr)
        pv_scr[...] = jnp.zeros_like(pv_scr)

    low = _lane_mask_low()
    zero = jnp.zeros((CHUNK, LANES), BF16)
    n_pairs = N_HEADS // 2
    pairs_per_kv = n_pairs // KV_HEADS
    gw = cd_ref.shape[1]
    n_groups = RET_W // gw
    n_blk = qkvg_ref.shape[0] // CHUNK
    nt = (((1,), (1,)), ((), ()))
    tn = (((0,), (0,)), ((), ()))
    ones_bd = ones_ref[...]

    s_cur = [s_scr[g] for g in range(n_groups)]
    k_prev, v_prev = pk_scr[...], pv_scr[...]
    wave1, mid = [], []
    for c in range(n_blk):
        rows = slice(c * CHUNK, (c + 1) * CHUNK)
        q = qkvg_ref[rows, 0:RET_W]
        k = qkvg_ref[rows, RET_W:2 * RET_W]
        v = qkvg_ref[rows, 2 * RET_W:3 * RET_W]
        sq = qkvg_ref[rows, 4 * RET_W:5 * RET_W]
        kd = (k.astype(F32) * kd_ref[...]).astype(BF16)
        ret_ops = []
        for p in range(n_pairs):
            kp = k[:, p * LANES:(p + 1) * LANES]
            vp = v[:, p * LANES:(p + 1) * LANES]
            ret_ops.append((jnp.concatenate([jnp.where(low, kp, zero), jnp.where(low, zero, kp)], axis=0),
                            jnp.concatenate([jnp.where(low, vp, zero), jnp.where(low, zero, vp)], axis=0)))
        k_new = kv_ref[rows, 0:KV_W].astype(BF16)
        v_new = kv_ref[rows, KV_W:2 * KV_W].astype(BF16)
        swa_ops = _swa_operands(jnp.concatenate([k_prev, k_new], axis=0), jnp.concatenate([v_prev, v_new], axis=0))
        k_prev, v_prev = k_new, v_new

        scores = [lax.dot_general(q[:, p * LANES:(p + 1) * LANES], ret_ops[p][0], nt, preferred_element_type=F32)
                  for p in range(n_pairs)]
        cross = [jnp.dot(q[:, g * gw:(g + 1) * gw], s_cur[g].astype(BF16), preferred_element_type=F32)
                 for g in range(n_groups)]
        upd = [lax.dot_general(kd[:, g * gw:(g + 1) * gw], v[:, g * gw:(g + 1) * gw], tn,
                               preferred_element_type=F32) for g in range(n_groups)]
        logits = [lax.dot_general(
            jnp.concatenate([sq[:, (g * pairs_per_kv + pp) * LANES:(g * pairs_per_kv + pp + 1) * LANES]
                             for pp in range(pairs_per_kv)], axis=0),
            swa_ops[g][0], nt, preferred_element_type=F32) for g in range(KV_HEADS)]
        s_cur = [s_cur[g] * cd_ref[g] + jnp.where(cd_ref[g] > 0.0, upd[g], 0.0) for g in range(n_groups)]
        wave1.append((scores, cross, logits, ret_ops, swa_ops))
    for g in range(n_groups):
        s_scr[g] = s_cur[g]
    pk_scr[...] = k_prev
    pv_scr[...] = v_prev

    for c in range(n_blk):
        scores, cross, logits, ret_ops, swa_ops = wave1[c]
        bias_c = bias_ref.at[0] if c == 0 else bias_all_ref.at[0]
        scores = [(scores[p] * dm_ref[p]).astype(BF16) for p in range(n_pairs)]
        soft = [[_swa_softmax(logits[g][pp * CHUNK:(pp + 1) * CHUNK, :], g * pairs_per_kv + pp, bias_c, sink_ref)
                 for pp in range(pairs_per_kv)] for g in range(KV_HEADS)]
        intra = [jnp.dot(scores[p], ret_ops[p][1], preferred_element_type=F32) for p in range(n_pairs)]
        swa = []
        for g in range(KV_HEADS):
            oo = _swa_values(jnp.concatenate([x[0].astype(BF16) for x in soft[g]], axis=0),
                             jnp.concatenate([x[1] for x in soft[g]], axis=0), swa_ops[g][1])
            swa += [oo[pp * CHUNK:(pp + 1) * CHUNK, :] for pp in range(pairs_per_kv)]
        o = jnp.concatenate(intra, axis=1) + jnp.concatenate(cross, axis=1) * qd_ref[...]
        mid.append((o, swa))

    o = jnp.concatenate([m[0] for m in mid], axis=0)
    ret = _head_norm(o, ones_bd) * qkvg_ref[:, 3 * RET_W:4 * RET_W].astype(F32)
    mix_ref[:, 0:RET_W] = ret.astype(BF16)
    for c in range(n_blk):
        mix_ref[c * CHUNK:(c + 1) * CHUNK, RET_W:2 * RET_W] = jnp.concatenate(mid[c][1], axis=1).astype(BF16)

    @pl.when(i == pl.num_programs(1) - 1)
    def _():
        for h in range(N_HEADS):
            g, hl = divmod(h, gw // HEAD_DIM)
            s_out_ref[0, h] = s_scr[g, hl * HEAD_DIM:(hl + 1) * HEAD_DIM, hl * HEAD_DIM:(hl + 1) * HEAD_DIM]


def _prompt_tables():
    decay_mat, q_dec, k_dec, chunk_dec = _decay_tables(CHUNK)
    dm = jnp.concatenate([decay_mat[0::2], decay_mat[1::2]], axis=2)
    qd = jnp.repeat(q_dec, HEAD_DIM, axis=1)
    kd = jnp.repeat(k_dec, HEAD_DIM, axis=1)
    gw = 4 * HEAD_DIM
    blk = np.kron(np.eye(4, dtype=np.float32), np.ones((HEAD_DIM, HEAD_DIM), np.float32))
    cd = jnp.repeat(chunk_dec.reshape(2, 4), HEAD_DIM, axis=1)[:, :, None] * jnp.asarray(blk)[None]
    ones_bd = jnp.asarray(blk / HEAD_DIM, BF16)
    assert cd.shape == (2, gw, gw)
    return dm, qd, kd, cd, ones_bd


def _attn_prompt_call(qkvg, kv, bias, sinks, tables, b, l, n_blk):
    dm, qd, kd, cd, ones_bd = tables
    rows = n_blk * CHUNK
    nc = l // rows
    gw = cd.shape[1]
    bias_spec = lambda idx: pl.BlockSpec((1, N_HEADS, CHUNK, 2 * CHUNK), idx)
    return pl.pallas_call(
        _attn_prompt_kernel,
        grid=(b, nc),
        in_specs=[pl.BlockSpec(memory_space=pltpu.SMEM),
                  pl.BlockSpec((rows, QKVG_W), lambda bi, i: (bi * nc + i, 0)),
                  pl.BlockSpec((rows, 2 * KV_W), lambda bi, i: (bi * nc + i, 0)),
                  bias_spec(lambda bi, i: (jnp.where(i == 0, 1, 0), 0, 0, 0)),
                  bias_spec(lambda bi, i: (0, 0, 0, 0)),
                  _const_spec(dm.shape), _const_spec(qd.shape), _const_spec(kd.shape),
                  _const_spec(cd.shape), _const_spec(ones_bd.shape)],
        out_specs=[pl.BlockSpec((rows, 2 * RET_W), lambda bi, i: (bi * nc + i, 0)),
                   pl.BlockSpec((1, N_HEADS, HEAD_DIM, HEAD_DIM), lambda bi, i: (bi, 0, 0, 0))],
        out_shape=[jax.ShapeDtypeStruct((b * l, 2 * RET_W), BF16),
                   jax.ShapeDtypeStruct((b, N_HEADS, HEAD_DIM, HEAD_DIM), F32)],
        scratch_shapes=[pltpu.VMEM((RET_W // gw, gw, gw), F32),
                        pltpu.VMEM((CHUNK, KV_W), BF16),
                        pltpu.VMEM((CHUNK, KV_W), BF16)],
        compiler_params=_cparams(2),
        name="attn_prompt",
    )(sinks, qkvg, kv, bias, bias, dm, qd, kd, cd, ones_bd)


def _ret_sample_kernel(dec_ref, tq_ref, s_ref, buf_ref, ret_ref, s_out_ref, q_scr, k_scr, v_scr, cross_scr):
    del buf_ref
    h = pl.program_id(0)
    n_t = tq_ref.shape[1]
    cd = dec_ref[h, 2 * n_t]
    for t in range(n_t):
        q_scr[t] = tq_ref[0, t].astype(F32)
        k_scr[t] = tq_ref[1, t].astype(F32) * dec_ref[h, t]
        v_scr[t] = tq_ref[2, t].astype(F32)
    cross_scr[...] = jnp.zeros_like(cross_scr)

    def per_d(d, carry):
        s_d = s_ref[d]
        upd = s_d * cd
        for t in range(n_t):
            upd = upd + k_scr[t, pl.ds(d, 1), :] * v_scr[t]
            cross_scr[t] += q_scr[t, pl.ds(d, 1), :] * s_d
        s_out_ref[d] = upd
        return carry

    lax.fori_loop(0, HEAD_DIM, per_d, 0)

    for t in range(n_t):
        o = cross_scr[t] * dec_ref[h, n_t + t]
        for t2 in range(t + 1):
            sc = jnp.sum(q_scr[t] * (tq_ref[1, t2].astype(F32)), axis=0, keepdims=True)
            o = o + (sc * dec_ref[h, 2 * n_t + 1 + t * n_t + t2]) * v_scr[t2]
        mu = jnp.mean(o, axis=0, keepdims=True)
        d0 = o - mu
        var = jnp.mean(d0 * d0, axis=0, keepdims=True)
        ret_ref[t] = (d0 * lax.rsqrt(var + LN_EPS) * tq_ref[3, t].astype(F32)).astype(ret_ref.dtype)


def _ret_sample_call(tq, state5, s_buf, layer, dec):
    _, n_t, _, b = tq.shape
    state_blk = (None, None, HEAD_DIM, HEAD_DIM, b)
    return pl.pallas_call(
        _ret_sample_kernel,
        grid=(N_HEADS,),
        in_specs=[pl.BlockSpec(memory_space=pltpu.SMEM),
                  pl.BlockSpec((4, n_t, HEAD_DIM, b), lambda h: (0, 0, h, 0)),
                  pl.BlockSpec(state_blk, lambda h: (layer, h, 0, 0, 0)),
                  pl.BlockSpec(memory_space=pl.ANY)],
        out_specs=[pl.BlockSpec((n_t, HEAD_DIM, b), lambda h: (0, h, 0)),
                   pl.BlockSpec(state_blk, lambda h: (layer, h, 0, 0, 0))],
        out_shape=[jax.ShapeDtypeStruct((n_t, RET_W, b), BF16),
                   jax.ShapeDtypeStruct(s_buf.shape, F32)],
        scratch_shapes=[pltpu.VMEM((n_t, HEAD_DIM, b), F32)] * 4,
        input_output_aliases={3: 1},
        compiler_params=_cparams(1),
        name="ret_sample",
    )(dec, tq, state5, s_buf)


def _ret_sample_table(tl):
    decay_mat, q_dec, k_dec, chunk_dec = _decay_tables(tl)
    return jnp.concatenate([k_dec.T, q_dec.T, chunk_dec[:, None], decay_mat.reshape(N_HEADS, tl * tl)], axis=1)


def _swa_sample_kernel(sink_ref, qkvg_ref, kv_ref, ckt_ref, cvt_ref, bias_ref, kbuf_ref, vbuf_ref,
                       swa_ref, ck_out_ref, cv_out_ref):
    del kbuf_ref, vbuf_ref
    tb = ckt_ref.shape[0]
    tl = qkvg_ref.shape[0] // tb
    nt = (((1,), (1,)), ((), ()))
    pairs_per_kv = N_HEADS // KV_HEADS // 2
    low = _lane_mask_low()
    zero_t =jnp.zeros((HEAD_DIM, WINDOW), BF16)
    front = jnp.zeros((WINDOW - tl, KV_W), F32)
    is_new =lax.broadcasted_iota(jnp.int32, (1, WINDOW), 1) >= WINDOW - tl
    bias = bias_ref.at[0]

    def block_diag(t):
        return jnp.concatenate([jnp.concatenate([t, zero_t], axis=1), jnp.concatenate([zero_t, t], axis=1)], axis=0)

    wave1 = []
    for e in range(tb):
        rows = slice(e * tl, (e + 1) * tl)
        sq32 = qkvg_ref[rows, 4 * RET_W:5 * RET_W].astype(F32)
        k_new = kv_ref[rows, 0:KV_W]
        v_new = kv_ref[rows, KV_W:2 * KV_W]
        k_rot = pltpu.roll(k_new, HEAD_DIM, 1)
        v_rot = pltpu.roll(v_new, HEAD_DIM, 1)
        k_placed = jnp.concatenate([front, k_new], axis=0).T
        v_placed = jnp.concatenate([front, v_new], axis=0).T
        per_g = []
        for g in range(KV_HEADS):
            kt = ckt_ref[e, g]
            vt = cvt_ref[e, g]
            hd = slice(g * HEAD_DIM, (g + 1) * HEAD_DIM)
            ck_out_ref[e, g] = jnp.where(is_new, k_placed[hd, :], pltpu.roll(kt, WINDOW - tl, 1))
            cv_out_ref[e, g] = jnp.where(is_new, v_placed[hd, :], pltpu.roll(vt, WINDOW - tl, 1))
            lhs = jnp.concatenate([sq32[:, (g * pairs_per_kv + pp) * LANES:(g * pairs_per_kv + pp + 1) * LANES]
                                   for pp in range(pairs_per_kv)], axis=0).astype(BF16)
            kn_lo, kn_hi = (k_new, k_rot) if g == 0 else (k_rot, k_new)
            vn_lo, vn_hi = (v_new, v_rot) if g == 0 else (v_rot, v_new)
            kn = [jnp.where(low, kn_lo, 0.0).astype(BF16), jnp.where(low, 0.0, kn_hi).astype(BF16)]
            vn = [jnp.where(low, vn_lo, 0.0).astype(BF16), jnp.where(low, 0.0, vn_hi).astype(BF16)]
            lc = jnp.dot(lhs, block_diag(kt.astype(BF16)), preferred_element_type=F32)
            ln = [lax.dot_general(lhs, kn[s], nt, preferred_element_type=F32) for s in range(2)]
            per_g.append((lc, ln, vn, block_diag(vt.astype(BF16))))
        wave1.append(per_g)

    swa_rows = []
    for e in range(tb):
        outs = []
        for g in range(KV_HEADS):
            lc, ln, vn, vbd = wave1[e][g]
            pcs, pns, dens = [], [[], []], []
            for pp in range(pairs_per_kv):
                r = slice(pp * tl, (pp + 1) * tl)
                den_pair = []
                for s in range(2):
                    hh = 2 * (g * pairs_per_kv + pp) + s
                    lo_c = lc[r, s * WINDOW:(s + 1) * WINDOW] + bias[hh, 0:tl, 0:WINDOW]
                    lo_n = ln[s][r, :] + bias[hh, 0:tl, WINDOW:WINDOW + tl]
                    sink = sink_ref[hh]
                    m = jnp.maximum(jnp.maximum(jnp.max(lo_c, axis=-1, keepdims=True),
                                                jnp.max(lo_n, axis=-1, keepdims=True)), sink)
                    pc = jnp.exp(lo_c - m)
                    pn = jnp.exp(lo_n - m)
                    den_pair.append(jnp.sum(pc, axis=-1, keepdims=True) + jnp.sum(pn, axis=-1, keepdims=True)
                                    + jnp.exp(sink - m))
                    pcs.append((pp, pc))
                    pns[s].append(pn)
                dens.append(jnp.where(low, den_pair[0], den_pair[1]))
            pc_rows = [jnp.concatenate([x[1] for x in pcs if x[0] == pp], axis=1) for pp in range(pairs_per_kv)]
            pc_all = jnp.concatenate(pc_rows, axis=0).astype(BF16)
            oo = lax.dot_general(pc_all, vbd, nt, preferred_element_type=F32)
            for s in range(2):
                oo = oo + jnp.dot(jnp.concatenate(pns[s], axis=0).astype(BF16), vn[s], preferred_element_type=F32)
            oo = oo / jnp.concatenate(dens, axis=0)
            outs += [oo[pp * tl:(pp + 1) * tl, :] for pp in range(pairs_per_kv)]
        swa_rows.append(jnp.concatenate(outs, axis=1))
    swa_ref[...] = jnp.concatenate(swa_rows, axis=0).astype(BF16)


def _swa_sample_call(qkvg, kv, ckt, cvt, kbuf, vbuf, layer, bias, sinks, b, tl, tb):
    cache_blk = (None, tb, KV_HEADS, HEAD_DIM, WINDOW)
    layer5 = lambda i: (layer, i, 0, 0, 0)
    return pl.pallas_call(
        _swa_sample_kernel,
        grid=(b // tb,),
        in_specs=[pl.BlockSpec(memory_space=pltpu.SMEM),
                  pl.BlockSpec((tb * tl, QKVG_W), lambda i: (i, 0)),
                  pl.BlockSpec((tb * tl, 2 * KV_W), lambda i: (i, 0)),
                  pl.BlockSpec(cache_blk, layer5),
                  pl.BlockSpec(cache_blk, layer5),
                  pl.BlockSpec((1, N_HEADS, CHUNK, 2 * CHUNK), lambda i: (0, 0, 0, 0)),
                  pl.BlockSpec(memory_space=pl.ANY),
                  pl.BlockSpec(memory_space=pl.ANY)],
        out_specs=[pl.BlockSpec((tb * tl, RET_W), lambda i: (i, 0)),
                   pl.BlockSpec(cache_blk, layer5),
                   pl.BlockSpec(cache_blk, layer5)],
        out_shape=[jax.ShapeDtypeStruct((b * tl, RET_W), BF16),
                   jax.ShapeDtypeStruct(kbuf.shape, F32),
                   jax.ShapeDtypeStruct(vbuf.shape, F32)],
        input_output_aliases={6: 1, 7: 2},
        compiler_params=_cparams(1),
        name="swa_sample",
    )(sinks, qkvg, kv, ckt, cvt, bias, kbuf, vbuf)


def _layer_norm(x, g, b):
    mu = jnp.mean(x, axis=-1, keepdims=True)
    xc = x - mu
    var = jnp.mean(xc * xc, axis=-1, keepdims=True)
    return xc * lax.rsqrt(var + LN_EPS) * g + b


def _ffn_kernel(alpha, carry_rows, n_cast, n_alias, mix_ref, x_ref, g1_ref, sh2_ref, sc2_ref, g2_ref, wout_ref,
                ln1g_ref, ln1b_ref, wup_ref, cw_ref, cb_ref, prev_ref, wdn_ref, ln2g_ref, ln2b_ref, *rest):
    cast_in, rest = rest[:n_cast], rest[n_cast + n_alias:]
    (xo_ref, tail_ref), carry_scr = rest[0:2], rest[-1]
    zero_refs = rest[2:len(rest) - 1 - n_cast]
    for src_ref, dst_ref in zip(cast_in, rest[len(rest) - 1 - n_cast:-1]):
        dst_ref[...] = src_ref[...].astype(dst_ref.dtype)
    for z_ref in zero_refs:
        z_ref[...] = jnp.zeros(z_ref.shape, z_ref.dtype)
    tb, tl, _ = x_ref.shape
    tm = tb * tl
    sm = FF_ROWS
    n_sub = tm // sm
    sb, sl = (1, sm) if carry_rows else (tb // n_sub, tl)
    j = pl.program_id(1)

    def seq(ref, s):
        if carry_rows:
            return ref[:, s * sm:(s + 1) * sm, :] if ref.shape[1] == tl else ref[...]
        return ref[s * sb:(s + 1) * sb]

    if carry_rows:
        @pl.when(j == 0)
        def _():
            carry_scr[carry_rows - 2:carry_rows, :] = prev_ref[0]

    fix_rows = 8 if carry_rows else sm
    t_idx = lax.broadcasted_iota(jnp.int32, (fix_rows, 1), 0) % sl
    is_t0 = t_idx == 0
    is_t1 = t_idx == 1
    n_chunks = D_FF // FF_CHUNK

    def up_cols(c, half):
        return slice(half * D_FF + c * FF_CHUNK, half * D_FF + (c + 1) * FF_CHUNK)

    def pre(s):
        x1 = alpha * seq(x_ref, s) + seq(g1_ref, s) * y[s].reshape(sb, sl, D_MODEL)
        x1 = _layer_norm(x1, ln1g_ref[...], ln1b_ref[...])
        h2 = (x1 * (1.0 + seq(sc2_ref, s)) + seq(sh2_ref, s)).reshape(sm, D_MODEL).astype(BF16)
        return x1, h2

    def up_dots(h2, c):
        return [jnp.dot(h2, wup_ref[:, up_cols(c, half)], preferred_element_type=F32) for half in range(2)]

    def conv(s, up, cols):
        if carry_rows:
            src = carry_scr if s == 0 else tail_ref
            p0 = src[carry_rows - 2:carry_rows - 1, cols]
            p1 = src[carry_rows - 1:carry_rows, cols]
            tail_ref[:, cols] = up[sm - carry_rows:sm, :]
        else:
            up3 = up.reshape(sb, sl, FF_CHUNK)
            tail_ref[s * sb:(s + 1) * sb, :, cols] = up3[:, sl - 2:sl, :]
            prev = seq(prev_ref, s)
            p0, p1 = prev[:, 0:1, cols], prev[:, 1:2, cols]
            t3 = lax.broadcasted_iota(jnp.int32, (1, sl, 1), 1)
            s1 = jnp.where(t3 == 0, p1, pltpu.roll(up3, 1, 1))
            s2 = jnp.where(t3 == 0, p0, jnp.where(t3 == 1, p1, pltpu.roll(up3, 2, 1)))
            out = cb_ref[:, cols] + s2 * cw_ref[0:1, cols] + s1 * cw_ref[1:2, cols] + up3 * cw_ref[2:3, cols]
            return out.reshape(sm, FF_CHUNK)
        r1 = pltpu.roll(up, 1, 0)
        r2 = pltpu.roll(up, 2, 0)
        s1 = jnp.where(is_t0, p1, r1[0:fix_rows])
        s2 = jnp.where(is_t0, p0, jnp.where(is_t1, p1, r2[0:fix_rows]))
        if fix_rows < sm:
            s1 = jnp.concatenate([s1, r1[fix_rows:]], axis=0)
            s2 = jnp.concatenate([s2, r2[fix_rows:]], axis=0)
        return cb_ref[:, cols] + s2 * cw_ref[0:1, cols] + s1 * cw_ref[1:2, cols] + up * cw_ref[2:3, cols]

    y = [jnp.dot(mix_ref[s * sm:(s + 1) * sm, :], wout_ref[...], preferred_element_type=F32) for s in range(n_sub)]
    items = [(s, c) for s in range(n_sub) for c in range(n_chunks)]
    if n_sub == 2:
        half = n_chunks // 2
        items = [(0, c) for c in range(half)]
        for c in range(n_chunks - half):
            items += [(0, half + c), (1, c)]
        items += [(1, c) for c in range(n_chunks - half, n_chunks)]
    staged, ups, accs = {}, {}, {}

    def issue_up(k):
        s, c = items[k]
        if c == 0:
            staged[s] = pre(s)
        ups[k] = up_dots(staged[s][1], c)

    for k in range(min(FF_LOOKAHEAD, len(items))):
        issue_up(k)
    for k, (s, c) in enumerate(items):
        if k + FF_LOOKAHEAD < len(items):
            issue_up(k + FF_LOOKAHEAD)
        up_a, up_b = ups.pop(k)
        ua = conv(s, up_a, up_cols(c, 0))
        ub = conv(s, up_b, up_cols(c, 1))
        gated = (ua * jax.nn.sigmoid(ua) * ub).astype(BF16)
        down = jnp.dot(gated, wdn_ref[c * FF_CHUNK:(c + 1) * FF_CHUNK, :], preferred_element_type=F32)
        accs[s] = down if c == 0 else accs[s] + down
        if c == n_chunks - 1:
            x1, _ = staged.pop(s)
            x2 = alpha * x1 + seq(g2_ref, s) * accs.pop(s).reshape(sb, sl, D_MODEL)
            out = _layer_norm(x2, ln2g_ref[...], ln2b_ref[...])
            if carry_rows:
                xo_ref[:, s * sm:(s + 1) * sm, :] = out
            else:
                xo_ref[s * sb:(s + 1) * sb] = out

    if carry_rows:
        carry_scr[...] = tail_ref[...]


def _ffn_call(mix, x, mod, conv_prev, w_out, ln1_g, ln1_b, w_up, conv_w, conv_b, w_down, ln2_g, ln2_b,
              layer, alpha, tb, tl, zero_shapes=(), cast=(), tail_buf=None):
    prev_layer = layer if conv_prev.shape[0] > 1 else 0
    w_layer = layer if w_up.shape[0] > 1 else 0
    b, l, _ = x.shape
    nj = l // tl
    tm = tb * tl
    n_steps = (b // tb) * nj
    zero_specs = []
    for shape in zero_shapes:
        per0 = n_steps // shape[0]
        blk = (1, shape[1] // per0) + tuple(shape[2:])
        assert per0 * shape[0] == n_steps and blk[1] * per0 == shape[1]
        zero_specs.append(pl.BlockSpec(blk, lambda i, j, per0=per0, nd=len(shape):
                                       ((i * nj + j) // per0, (i * nj + j) % per0) + (0,) * (nd - 2)))
    cast_in, cast_out, cast_shape = [], [], []
    for w, w_src_layer in cast:
        _, rows, cols = w.shape
        blk = (None, rows // n_steps, cols)
        assert blk[1] * n_steps == rows and blk[1] % BF16_ROWS == 0
        cast_in.append(pl.BlockSpec(blk, lambda i, j, src=w_src_layer: (src, i * nj + j, 0)))
        cast_out.append(pl.BlockSpec(blk, lambda i, j: (0, i * nj + j, 0)))
        cast_shape.append(jax.ShapeDtypeStruct((1, rows, cols), BF16))
    carry_rows = 8 if tb == 1 else 0
    assert carry_rows or (nj == 1 and tl == 8)
    if carry_rows:
        tail_spec = pl.BlockSpec((8, UP_W), lambda i, j: (i, 0))
        tail_shape = jax.ShapeDtypeStruct((b * 8, UP_W), F32)
    else:
        tail_spec = pl.BlockSpec((None, tb, 2, UP_W), lambda i, j: (layer, i, 0, 0))
        tail_shape = jax.ShapeDtypeStruct(tail_buf.shape, F32)
    alias_in = [] if tail_buf is None else [tail_buf]
    operands = (mix, x, mod, mod, mod, mod, w_out, ln1_g, ln1_b, w_up, conv_w, conv_b, conv_prev, w_down, ln2_g, ln2_b,
                *[w for w, _ in cast])
    kern = functools.partial(_ffn_kernel, alpha, carry_rows, len(cast), len(alias_in))
    return pl.pallas_call(
        kern,
        grid=(b // tb, nj),
        in_specs=[pl.BlockSpec((tm, 2 * RET_W), lambda i, j: (i * nj + j, 0)),
                  pl.BlockSpec((tb, tl, D_MODEL), lambda i, j: (i, j, 0)),
                  _mod_spec(tb, layer, 2), _mod_spec(tb, layer, 3), _mod_spec(tb, layer, 4), _mod_spec(tb, layer, 5),
                  _layer_spec((2 * RET_W, D_MODEL), w_layer),
                  _layer_spec((1, D_MODEL), layer), _layer_spec((1, D_MODEL), layer),
                  _layer_spec((D_MODEL, UP_W), w_layer),
                  _layer_spec((3, UP_W), layer), _layer_spec((1, UP_W), layer),
                  pl.BlockSpec((None, tb, 2, UP_W), lambda i, j: (prev_layer, i, 0, 0)),
                  _layer_spec((D_FF, D_MODEL), w_layer),
                  _layer_spec((1, D_MODEL), layer), _layer_spec((1, D_MODEL), layer)] + cast_in
        + [pl.BlockSpec(memory_space=pl.ANY)] * len(alias_in),
        out_specs=[pl.BlockSpec((tb, tl, D_MODEL), lambda i, j: (i, j, 0)),
                   tail_spec] + zero_specs + cast_out,
        out_shape=[jax.ShapeDtypeStruct((b, l, D_MODEL), F32), tail_shape]
        + [jax.ShapeDtypeStruct(shape, F32) for shape in zero_shapes] + cast_shape,
        scratch_shapes=[pltpu.VMEM((8, UP_W), F32)],
        input_output_aliases={len(operands): 1} if alias_in else {},
        compiler_params=_cparams(2),
        name="ffn",
    )(*operands, *alias_in)


def kernel(x_prompt, x_sample, c_prompt, c_sample, state_ret, cache_swa_k, cache_swa_v, state_conv, rel_bias, w_ada, b_ada, w_in, swa_sinks, w_out, ln1_g, ln1_b, w_up, conv_w, conv_b, w_down, ln2_g, ln2_b):
    depth = w_ada.shape[0]
    bp, lp, _ = x_prompt.shape
    bs, ls, _ = x_sample.shape
    alpha = (2.0 * depth) ** 0.25
    tl_proj = 1024
    tl_p = 2 * FF_ROWS
    tb_s = FF_ROWS // ls
    tb_attn = 8
    blk_attn = 8

    c_all = jnp.concatenate([c_prompt, c_sample], axis=0)
    mod_p, mod_s = _ada_call(c_all, bp, w_ada, b_ada)
    bias = _bias_call(rel_bias)

    cos_p, sin_p = _rope_tables(jnp.arange(lp, dtype=jnp.int32))
    cos_s, sin_s = _rope_tables(PAST_LEN + jnp.arange(ls, dtype=jnp.int32))
    cos_s, sin_s = jnp.tile(cos_s, (tb_s, 1)), jnp.tile(sin_s, (tb_s, 1))
    tab_p = _prompt_tables()
    dec_s = _ret_sample_table(ls)
    conv0 = jnp.zeros((1, bp, 2, UP_W), F32)

    w_in_l = w_in[0:1].astype(BF16)
    vec = lambda a: a.reshape(depth, 1, a.shape[-1])
    ffn_vec = (vec(ln1_g), vec(ln1_b), conv_w, vec(conv_b), vec(ln2_g), vec(ln2_b))
    state5 = jnp.transpose(state_ret, (0, 2, 3, 4, 1))
    ckt = jnp.transpose(cache_swa_k, (0, 1, 3, 4, 2))
    cvt = jnp.transpose(cache_swa_v, (0, 1, 3, 4, 2))

    xp, xs = x_prompt, x_sample
    p_ret, p_k, p_v, p_conv = [], [], [], []
    for l in range(depth):
        qkvg, kv, wo_b, wu_b, wd_b = _proj_call(xp, mod_p, w_in_l, l, cos_p, sin_p, 1, tl_proj, (w_out, w_up, w_down))
        ffn_w = (wo_b, ffn_vec[0], ffn_vec[1], wu_b, ffn_vec[2], ffn_vec[3], wd_b, ffn_vec[4], ffn_vec[5])
        mix, r_p =_attn_prompt_call(qkvg, kv, bias, swa_sinks[l], tab_p, bp, lp, blk_attn)
        zero_shapes = (state5.shape, ckt.shape, cvt.shape, state_conv.shape) if l == 0 else ()
        cast_next = ((w_in, l + 1),) if l + 1 < depth else ()
        xp, tail, *extra = _ffn_call(mix, xp, mod_p, conv0, *ffn_w, l, alpha, 1, tl_p, zero_shapes, cast_next)
        if l == 0:
            s_buf, k_buf, v_buf, conv_buf = extra[0:4]
        w_in_next = extra[-1] if cast_next else None
        kv3 =kv.reshape(bp, lp, 2, KV_HEADS, HEAD_DIM)
        p_ret.append(r_p)
        p_k.append(kv3[:, lp - WINDOW:, 0])
        p_v.append(kv3[:, lp - WINDOW:, 1])
        p_conv.append(tail.reshape(bp, 8, UP_W)[:, 6:8])

        qkvg, kv = _proj_call(xs, mod_s, w_in_l, l, cos_s, sin_s, tb_s, ls)
        tq = jnp.transpose(qkvg.reshape(bs, ls, 5, RET_W)[:, :, 0:4], (2, 1, 3, 0))
        ret_t, s_buf = _ret_sample_call(tq, state5, s_buf, l, dec_s)
        ret = jnp.transpose(ret_t, (2, 0, 1)).reshape(bs * ls, RET_W)
        swa, k_buf, v_buf = _swa_sample_call(qkvg, kv, ckt, cvt, k_buf, v_buf, l, bias, swa_sinks[l], bs, ls, tb_attn)
        mix = jnp.concatenate([ret, swa], axis=1)
        xs, conv_buf = _ffn_call(mix, xs, mod_s, state_conv, *ffn_w, l, alpha, tb_s, ls, tail_buf=conv_buf)
        w_in_l = w_in_next

    return (xp, xs,
            jnp.stack(p_ret), jnp.stack(p_k), jnp.stack(p_v), jnp.stack(p_conv),
            jnp.transpose(s_buf, (0, 4, 1, 2, 3)), jnp.transpose(k_buf, (0, 1, 4, 2, 3)),
            jnp.transpose(v_buf, (0, 1, 4, 2, 3)), conv_buf)
```
